```python
import jax, jax.numpy as jnp
from jax import lax
import numpy as np

D_MODEL = 1024
BATCH = 2
SEQ = 8192
DEPTH = 1
DEC_BATCH = 32
DEC_SEQ = 16
PAST_LEN = 1024

CHUNK = 64
LEFT_CHUNKS = 8
HEAD_DIM = 64
H_A = 8
W_A = H_A * HEAD_DIM
REL_CLIP = 128
H_B = 4
DK_B = 64
DV_B = 128
W_BK = H_B * DK_B
W_BV = H_B * DV_B
GATE_RANK = 16
GATE_TAU = 16.0
GLA_BLOCK = 16
N_MEM = 256
H_M = 4
W_M = H_M * HEAD_DIM
EPS = 1e-6
IN_SIZES = (W_A, W_A, W_A, W_A, W_BK, W_BK, W_BV, GATE_RANK, W_BV, W_M, W_M, D_MODEL, D_MODEL, D_MODEL)
D_IN = sum(IN_SIZES)

kernel_name = "hybrid_streaming_encoder_step"


def rms_norm(x, g):
    xf = x.astype(jnp.float32)
    y = xf * lax.rsqrt(jnp.mean(xf * xf, axis=-1, keepdims=True) + EPS)
    return (y * g.astype(jnp.float32)).astype(x.dtype)


def split_heads(z, n):
    return z.reshape(*z.shape[:-1], n, z.shape[-1] // n)


def rel_bias_lookup(table, dist):
    return table[:, jnp.clip(dist, -REL_CLIP, REL_CLIP) + REL_CLIP].astype(jnp.float32)


def band_attention_prompt(q, k, v, table):
    b, t, h, hd = q.shape
    nc = t // CHUNK
    band = (LEFT_CHUNKS + 1) * CHUNK
    qc = q.reshape(b, nc, CHUNK, h, hd)
    pad = ((0, 0), (LEFT_CHUNKS * CHUNK, 0), (0, 0), (0, 0))
    idx = jnp.arange(nc)[:, None] + jnp.arange(LEFT_CHUNKS + 1)[None, :]
    kc = jnp.pad(k, pad).reshape(b, nc + LEFT_CHUNKS, CHUNK, h, hd)[:, idx].reshape(b, nc, band, h, hd)
    vc = jnp.pad(v, pad).reshape(b, nc + LEFT_CHUNKS, CHUNK, h, hd)[:, idx].reshape(b, nc, band, h, hd)
    dist = (LEFT_CHUNKS * CHUNK + jnp.arange(CHUNK))[:, None] - jnp.arange(band)[None, :]
    bias = rel_bias_lookup(table, dist)
    valid = jnp.repeat(idx >= LEFT_CHUNKS, CHUNK, axis=1)
    s = jnp.einsum('bnqhd,bnkhd->bnhqk', qc, kc).astype(jnp.float32) * (hd ** -0.5) + bias[None, None]
    s = jnp.where(valid[None, :, None, None, :], s, -jnp.inf)
    p = jax.nn.softmax(s, axis=-1).astype(v.dtype)
    o = jnp.einsum('bnhqk,bnkhd->bnqhd', p, vc)
    return o.reshape(b, t, h * hd)


def band_attention_step(q, k_new, v_new, k_past, v_past, table):
    b, s_len, h, hd = q.shape
    L = k_past.shape[1]
    kk = jnp.concatenate([k_past, k_new], axis=1)
    vv = jnp.concatenate([v_past, v_new], axis=1)
    dist = (L + jnp.arange(s_len))[:, None] - jnp.arange(L + s_len)[None, :]
    bias = rel_bias_lookup(table, dist)
    s = jnp.einsum('bqhd,bkhd->bhqk', q, kk).astype(jnp.float32) * (hd ** -0.5) + bias[None]
    p = jax.nn.softmax(s, axis=-1).astype(vv.dtype)
    o = jnp.einsum('bhqk,bkhd->bqhd', p, vv)
    return o.reshape(b, s_len, h * hd)


def memory_attention(q, mk, mv):
    b, t, h, hd = q.shape
    s = jnp.einsum('bqhd,bmhd->bhqm', q, mk).astype(jnp.float32) * (hd ** -0.5)
    p = jax.nn.softmax(s, axis=-1).astype(mv.dtype)
    return jnp.einsum('bhqm,bmhd->bqhd', p, mv).reshape(b, t, h * hd)


def gla_scan(q, k, v, log_a, s0):
    b, t, h, _ = q.shape
    pad = (-t) % GLA_BLOCK

    def blocks(z):
        z = jnp.pad(z.astype(jnp.float32), ((0, 0), (0, pad), (0, 0), (0, 0)))
        return z.reshape(b, -1, GLA_BLOCK, h, z.shape[-1]).transpose(1, 0, 3, 2, 4)

    causal = jnp.tril(jnp.ones((GLA_BLOCK, GLA_BLOCK), dtype=bool))

    def step(S, blk):
        qi, ki, vi, ai = blk
        cum = jnp.cumsum(ai, axis=-2)
        last = cum[..., -1:, :]
        diff = jnp.where(causal[:, :, None], cum[..., :, None, :] - cum[..., None, :, :], -jnp.inf)
        att = jnp.einsum('bhtd,bhsd,bhtsd->bhts', qi, ki, jnp.exp(diff))
        o = jnp.einsum('bhts,bhsv->bhtv', att, vi) + jnp.einsum('bhtd,bhdv->bhtv', qi * jnp.exp(cum), S)
        S = S * jnp.exp(last)[..., 0, :, None] + jnp.einsum('bhsd,bhsv->bhdv', ki * jnp.exp(last - cum), vi)
        return S, o

    S, o = lax.scan(step, s0.astype(jnp.float32), (blocks(q), blocks(k), blocks(v), blocks(log_a)))
    o = o.transpose(1, 0, 3, 2, 4).reshape(b, -1, h, v.shape[-1])[:, :t]
    return o.astype(v.dtype), S.astype(s0.dtype)


def memory_kv(mem, g_mem, w_mem_kv, g_km):
    mh = rms_norm(mem, g_mem)
    mk, mv = jnp.split(mh @ w_mem_kv, 2, axis=-1)
    return rms_norm(split_heads(mk, H_M), g_km), split_heads(mv, H_M)


def hybrid_layer(x, attend_a, mem_k, mem_v, s0, norm_in, w_in, g_qa, g_ka, rel_bias,
                 w_gate2, b_gate, g_gla_out, g_qm, w_up_a, w_up_b, w_up_m, w_out):
    b, t, _ = x.shape
    h = rms_norm(x, norm_in)
    proj = h @ w_in
    (qa, ka, va, za, qb, kb, vb, glr, zb, qm, zm, gate_a, gate_b, gate_m) = jnp.split(
        proj, np.cumsum(IN_SIZES)[:-1].tolist(), axis=-1)
    qa = rms_norm(split_heads(qa, H_A), g_qa)
    ka = rms_norm(split_heads(ka, H_A), g_ka)
    va = split_heads(va, H_A)
    out_a = attend_a(qa, ka, va, rel_bias)
    log_a = jax.nn.log_sigmoid((glr @ w_gate2 + b_gate).astype(jnp.float32)) / GATE_TAU
    o_b, s_new = gla_scan(split_heads(qb, H_B) * (DK_B ** -0.5), split_heads(kb, H_B),
                          split_heads(vb, H_B), split_heads(log_a, H_B), s0)
    out_b = rms_norm(o_b, g_gla_out).reshape(b, t, W_BV)
    out_m = memory_attention(rms_norm(split_heads(qm, H_M), g_qm), mem_k, mem_v)
    u = (jax.nn.sigmoid(gate_a) * ((out_a * jax.nn.silu(za)) @ w_up_a)
         + jax.nn.sigmoid(gate_b) * ((out_b * jax.nn.silu(zb)) @ w_up_b)
         + jax.nn.sigmoid(gate_m) * ((out_m * jax.nn.silu(zm)) @ w_up_m))
    return x + u @ w_out, ka, va, s_new


def setup_inputs(seed: int = 0) -> dict:
    key = jax.random.key(seed)
    ks = iter(jax.random.split(key, 32))

    def nrm(shape, scale):
        return jax.random.normal(next(ks), shape, jnp.float32) * scale

    a_keep = min(LEFT_CHUNKS * CHUNK, PAST_LEN)
    return {
        "x_prompt": nrm((BATCH, SEQ, D_MODEL), 1.0),
        "x_sample": nrm((DEC_BATCH, DEC_SEQ, D_MODEL), 1.0),
        "mem_prompt": nrm((BATCH, N_MEM, D_MODEL), 1.0),
        "cache_a_k": nrm((DEPTH, DEC_BATCH, a_keep, H_A, HEAD_DIM), 1.0),
        "cache_a_v": nrm((DEPTH, DEC_BATCH, a_keep, H_A, HEAD_DIM), 1.0),
        "state_gla": nrm((DEPTH, DEC_BATCH, H_B, DK_B, DV_B), 1.0),
        "cache_mem_k": nrm((DEPTH, DEC_BATCH, N_MEM, H_M, HEAD_DIM), 1.0),
        "cache_mem_v": nrm((DEPTH, DEC_BATCH, N_MEM, H_M, HEAD_DIM), 1.0),
        "norm_in": 1.0 + nrm((DEPTH, D_MODEL), 0.02),
        "w_in": nrm((DEPTH, D_MODEL, D_IN), D_MODEL ** -0.5),
        "g_qa": 1.0 + nrm((DEPTH, HEAD_DIM), 0.02),
        "g_ka": 1.0 + nrm((DEPTH, HEAD_DIM), 0.02),
        "rel_bias": nrm((DEPTH, H_A, 2 * REL_CLIP + 1), 0.5),
        "w_gate2": nrm((DEPTH, GATE_RANK, W_BK), GATE_RANK ** -0.5),
        "b_gate": nrm((DEPTH, W_BK), 0.1),
        "g_gla_out": 1.0 + nrm((DEPTH, DV_B), 0.02),
        "g_mem": 1.0 + nrm((DEPTH, D_MODEL), 0.02),
        "w_mem_kv": nrm((DEPTH, D_MODEL, 2 * W_M), D_MODEL ** -0.5),
        "g_qm": 1.0 + nrm((DEPTH, HEAD_DIM), 0.02),
        "g_km": 1.0 + nrm((DEPTH, HEAD_DIM), 0.02),
        "w_up_a": nrm((DEPTH, W_A, D_MODEL), W_A ** -0.5),
        "w_up_b": nrm((DEPTH, W_BV, D_MODEL), W_BV ** -0.5),
        "w_up_m": nrm((DEPTH, W_M, D_MODEL), W_M ** -0.5),
        "w_out": nrm((DEPTH, D_MODEL, D_MODEL), 0.5 * D_MODEL ** -0.5),
    }


def reference(x_prompt, x_sample, mem_prompt, cache_a_k, cache_a_v, state_gla, cache_mem_k,
              cache_mem_v, norm_in, w_in, g_qa, g_ka, rel_bias, w_gate2, b_gate, g_gla_out,
              g_mem, w_mem_kv, g_qm, g_km, w_up_a, w_up_b, w_up_m, w_out):
    xp, xs = x_prompt, x_sample
    keep_p = min(LEFT_CHUNKS * CHUNK, x_prompt.shape[1])
    akp, avp, sgp, mkp, mvp, aks, avs, sgs = [], [], [], [], [], [], [], []
    for l in range(DEPTH):
        weights = (norm_in[l], w_in[l], g_qa[l], g_ka[l], rel_bias[l], w_gate2[l], b_gate[l],
                   g_gla_out[l], g_qm[l], w_up_a[l], w_up_b[l], w_up_m[l], w_out[l])
        mk, mv = memory_kv(mem_prompt, g_mem[l], w_mem_kv[l], g_km[l])
        s0 = jnp.zeros((xp.shape[0], H_B, DK_B, DV_B), state_gla.dtype)
        xp, ka, va, sp = hybrid_layer(xp, band_attention_prompt, mk, mv, s0, *weights)
        akp.append(ka[:, -keep_p:])
        avp.append(va[:, -keep_p:])
        sgp.append(sp)
        mkp.append(mk)
        mvp.append(mv)
        past_k, past_v = cache_a_k[l], cache_a_v[l]
        attend_step = lambda q, k, v, tab: band_attention_step(q, k, v, past_k, past_v, tab)
        xs, ka_s, va_s, ss = hybrid_layer(xs, attend_step, cache_mem_k[l], cache_mem_v[l],
                                          state_gla[l], *weights)
        aks.append(ka_s)
        avs.append(va_s)
        sgs.append(ss)
    return (xp, xs, jnp.stack(akp), jnp.stack(avp), jnp.stack(sgp), jnp.stack(mkp), jnp.stack(mvp),
            jnp.stack(aks), jnp.stack(avs), jnp.stack(sgs))
```

```python
import functools

import jax
import jax.numpy as jnp
from jax import lax
from jax.experimental import pallas as pl
from jax.experimental.pallas import tpu as pltpu

F32 = jnp.float32
BF16 = jnp.bfloat16

D_MODEL = 1024
CHUNK = 64
LEFT_CHUNKS = 8
HEAD_DIM = 64
H_A = 8
W_A = H_A * HEAD_DIM
REL_CLIP = 128
H_B = 4
DK_B = 64
DV_B = 128
W_BK = H_B * DK_B
W_BV = H_B * DV_B
GATE_RANK = 16
GATE_TAU = 16.0
N_MEM = 256
H_M = 4
W_M = H_M * HEAD_DIM
EPS = 1e-6
IN_SIZES = (W_A, W_A, W_A, W_A, W_BK, W_BK, W_BV, GATE_RANK, W_BV, W_M, W_M, D_MODEL, D_MODEL, D_MODEL)

LANES = 128
SUBLANES = 8
VMEM_LIMIT = 56 * 1024 * 1024
NEG = -1e30

BAND = LEFT_CHUNKS * CHUNK
QBLK = 256
KSPAN = BAND + QBLK
GLA_C = 128
GLA_BASE = SUBLANES


def _dot(a, b):
    return jnp.dot(a, b, preferred_element_type=F32)


def _dot_nt(a, b):
    return lax.dot_general(a, b, (((1,), (1,)), ((), ())), preferred_element_type=F32)


def _split3(x):
    hi = x.astype(BF16)
    r = x - hi.astype(F32)
    mid = r.astype(BF16)
    lo = (r - mid.astype(F32)).astype(BF16)
    return hi, mid, lo


def _dot_exact_lhs(a01, x):
    hi, mid, lo = _split3(x)
    return _dot(a01, hi) + _dot(a01, mid) + _dot(a01, lo)


def _sigmoid(z):
    return 1.0 / (1.0 + jnp.exp(-z))


def _silu(z):
    return z * _sigmoid(z)


def _log_sigmoid(z):
    return jnp.minimum(z, 0.0) - jnp.log1p(jnp.exp(-jnp.abs(z)))


def _rms_rows(x, g):
    ms = jnp.mean(x * x, axis=-1, keepdims=True)
    return x * lax.rsqrt(ms + EPS) * g


def _head_mean_matrix(width, head):
    r = lax.broadcasted_iota(jnp.int32, (width, width), 0) // head
    c = lax.broadcasted_iota(jnp.int32, (width, width), 1) // head
    return jnp.where(r == c, 1.0 / head, 0.0).astype(BF16)


def _head_rms(x, g, head):
    x2 = x * x
    hi = x2.astype(BF16)
    lo = (x2 - hi.astype(F32)).astype(BF16)
    avg = _head_mean_matrix(x.shape[-1], head)
    ms = _dot(hi, avg) + _dot(lo, avg)
    return x * lax.rsqrt(ms + EPS) * g


def _full(shape):
    nd = len(shape)
    return pl.BlockSpec(shape, lambda *_: (0,) * nd, pipeline_mode=pl.Buffered(1))


def _params(*sem):
    return pltpu.CompilerParams(dimension_semantics=sem, vmem_limit_bytes=VMEM_LIMIT)


_MAIN_SIZES = (W_A, W_A, W_A, W_A, W_BK, W_BK, W_BV, W_BV, W_M, W_M, D_MODEL, D_MODEL, D_MODEL)
_MAIN_OFF = tuple(int(sum(_MAIN_SIZES[:i])) for i in range(len(_MAIN_SIZES) + 1))
W_MAIN = _MAIN_OFF[-1]


def _front_kernel(x_ref, nin_ref, w_ref, wglr_ref, wg2_ref, bg_ref, gqa_ref, gka_ref, gqm_ref,
                  qa_o, ka_o, va_o, sza_o, qb_o, kb_o, vb_o, la_o, szb_o, qm_o, szm_o,
                  ga_o, gb_o, gm_o):
    h = _rms_rows(x_ref[...], nin_ref[...]).astype(BF16)

    def proj(i):
        return _dot(h, w_ref[:, _MAIN_OFF[i]:_MAIN_OFF[i + 1]])

    scale = HEAD_DIM ** -0.5
    qa_o[...] = (_head_rms(proj(0), gqa_ref[...], HEAD_DIM) * scale).astype(BF16)
    ka_o[...] = _head_rms(proj(1), gka_ref[...], HEAD_DIM)
    va_o[...] = proj(2)
    sza_o[...] = _silu(proj(3)).astype(BF16)
    qb_o[...] = (proj(4) * (DK_B ** -0.5)).astype(BF16)
    kb_o[...] = proj(5).astype(BF16)
    vb_o[...] = proj(6).astype(BF16)
    szb_o[...] = _silu(proj(7)).astype(BF16)
    qm_o[...] = (_head_rms(proj(8), gqm_ref[...], HEAD_DIM) * scale).astype(BF16)
    szm_o[...] = _silu(proj(9)).astype(BF16)
    ga_o[...] = _sigmoid(proj(10)).astype(BF16)
    gb_o[...] = _sigmoid(proj(11)).astype(BF16)
    gm_o[...] = _sigmoid(proj(12)).astype(BF16)
    glr = _dot(h, wglr_ref[...]).astype(BF16)
    z = _dot(glr, wg2_ref[...]) + bg_ref[...]
    la_o[...] = _log_sigmoid(z) * (1.0 / GATE_TAU)


def _front(x2d, wts, tm):
    n = x2d.shape[0]
    tm = min(tm, n)
    row = lambda w: pl.BlockSpec((tm, w), lambda i: (i, 0))
    widths = (W_A, W_A, W_A, W_A, W_BK, W_BK, W_BV, W_BK, W_BV, W_M, W_M, D_MODEL, D_MODEL, D_MODEL)
    dtypes = (BF16, F32, F32, BF16, BF16, BF16, BF16, F32, BF16, BF16, BF16, BF16, BF16, BF16)
    return pl.pallas_call(
        _front_kernel,
        grid=(n // tm,),
        in_specs=[row(D_MODEL), _full((1, D_MODEL)), _full((D_MODEL, W_MAIN)), _full((D_MODEL, LANES)),
                  _full((LANES, W_BK)), _full((1, W_BK)), _full((1, W_A)), _full((1, W_A)), _full((1, W_M))],
        out_specs=[row(w) for w in widths],
        out_shape=[jax.ShapeDtypeStruct((n, w), d) for w, d in zip(widths, dtypes)],
        compiler_params=_params("parallel"),
        name="front",
    )(x2d, wts["norm_in"], wts["w_main"], wts["w_glr"], wts["w_gate2"], wts["b_gate"],
      wts["g_qa"], wts["g_ka"], wts["g_qm"])


def _back_kernel(x_ref, a_ref, b_ref, m_ref, ga_ref, gb_ref, gm_ref, wa_ref, wb_ref, wm_ref, wo_ref, y_ref):
    u = (ga_ref[...].astype(F32) * _dot(a_ref[...], wa_ref[...])
         + gb_ref[...].astype(F32) * _dot(b_ref[...], wb_ref[...])
         + gm_ref[...].astype(F32) * _dot(m_ref[...], wm_ref[...]))
    y_ref[...] = x_ref[...] + _dot(u.astype(BF16), wo_ref[...])


def _back(x2d, a, b, m, ga, gb, gm, wts, tm):
    n = x2d.shape[0]
    tm = min(tm, n)
    row = lambda w: pl.BlockSpec((tm, w), lambda i: (i, 0))
    return pl.pallas_call(
        _back_kernel,
        grid=(n // tm,),
        in_specs=[row(D_MODEL), row(W_A), row(W_BV), row(W_M), row(D_MODEL), row(D_MODEL), row(D_MODEL),
                  _full((W_A, D_MODEL)), _full((W_BV, D_MODEL)), _full((W_M, D_MODEL)),
                  _full((D_MODEL, D_MODEL))],
        out_specs=row(D_MODEL),
        out_shape=jax.ShapeDtypeStruct((n, D_MODEL), F32),
        compiler_params=_params("parallel"),
        name="back",
    )(x2d, a, b, m, ga, gb, gm, wts["w_up_a"], wts["w_up_b"], wts["w_up_m"], wts["w_out"])


def _bias_kernel(e_ref, raw_ref, band_ref):
    e = jnp.broadcast_to(e_ref[0], (QBLK, 2 * BAND))
    t = pltpu.roll(e, 0, 1, stride=1, stride_axis=0)[:, :KSPAN]
    raw_ref[0] = t
    qc = lax.broadcasted_iota(jnp.int32, (QBLK, KSPAN), 0) // CHUNK
    kc = lax.broadcasted_iota(jnp.int32, (QBLK, KSPAN), 1) // CHUNK - LEFT_CHUNKS
    band_ref[0] = jnp.where((kc <= qc) & (kc >= qc - LEFT_CHUNKS), t, NEG)


def _bias_tiles(rel_bias):
    j = jnp.arange(2 * BAND)
    dist = jnp.where(j < KSPAN, BAND - j, REL_CLIP)
    e = rel_bias[:, jnp.clip(dist, -REL_CLIP, REL_CLIP) + REL_CLIP].reshape(H_A, 1, 2 * BAND)
    tile = pl.BlockSpec((1, QBLK, KSPAN), lambda h: (h, 0, 0))
    return pl.pallas_call(
        _bias_kernel,
        grid=(H_A,),
        in_specs=[pl.BlockSpec((1, 1, 2 * BAND), lambda h: (h, 0, 0))],
        out_specs=[tile, tile],
        out_shape=[jax.ShapeDtypeStruct((H_A, QBLK, KSPAN), F32)] * 2,
        compiler_params=_params("parallel"),
        name="bias_tiles",
    )(e)


def _softmax_pv(s_list, v_list):
    m = functools.reduce(jnp.maximum, [jnp.max(s, axis=-1, keepdims=True) for s in s_list])
    ps = [jnp.exp(s - m) for s in s_list]
    l = functools.reduce(jnp.add, [jnp.sum(p, axis=-1, keepdims=True) for p in ps])
    o = functools.reduce(jnp.add, [_dot(p.astype(BF16), v) for p, v in zip(ps, v_list)])
    return o / l


def _attn_prompt_kernel(q_ref, k0_ref, k1_ref, k2_ref, v0_ref, v1_ref, v2_ref, sz_ref, bias_ref, o_ref):
    i = pl.program_id(1)
    q = q_ref[0]
    k = jnp.concatenate([k0_ref[0], k1_ref[0], k2_ref[0]], axis=0).astype(BF16)
    v = jnp.concatenate([v0_ref[0], v1_ref[0], v2_ref[0]], axis=0).astype(BF16)
    col = lax.broadcasted_iota(jnp.int32, (QBLK, KSPAN), 1)
    exists = col >= (BAND // QBLK - i) * QBLK
    outs = []
    for h in range(H_A):
        sl = slice(h * HEAD_DIM, (h + 1) * HEAD_DIM)
        s = jnp.where(exists, _dot_nt(q[:, sl], k[:, sl]) + bias_ref[h], NEG)
        outs.append(_softmax_pv([s], [v[:, sl]]))
    o = jnp.concatenate(outs, axis=-1)
    o_ref[0] = (o * sz_ref[0].astype(F32)).astype(BF16)


def _attn_prompt(q, k, v, sz, bias_band):
    b, t, _ = q.shape
    nb = BAND // QBLK
    cur = pl.BlockSpec((1, QBLK, W_A), lambda bi, i: (bi, i, 0))
    prev = lambda d: pl.BlockSpec((1, QBLK, W_A), lambda bi, i: (bi, jnp.maximum(i - d, 0), 0))
    kv_specs = [prev(nb - j) for j in range(nb)] + [cur]
    assert nb == 2
    return pl.pallas_call(
        _attn_prompt_kernel,
        grid=(b, t // QBLK),
        in_specs=[cur] + kv_specs + kv_specs + [cur, _full((H_A, QBLK, KSPAN))],
        out_specs=cur,
        out_shape=jax.ShapeDtypeStruct((b, t, W_A), BF16),
        compiler_params=_params("parallel", "parallel"),
        name="attn_prompt",
    )(q, k, k, k, v, v, v, sz, bias_band)


STEP_STREAMS = 4


def _attn_step_kernel(q_ref, kn_ref, vn_ref, kp_ref, vp_ref, sz_ref, bias_ref, o_ref):
    s_len = q_ref.shape[1]
    for b in range(STEP_STREAMS):
        q = q_ref[b]
        kn, vn = kn_ref[b].astype(BF16), vn_ref[b].astype(BF16)
        kp, vp = kp_ref[b].astype(BF16), vp_ref[b].astype(BF16)
        outs = []
        for h in range(H_A):
            sl = slice(h * HEAD_DIM, (h + 1) * HEAD_DIM)
            bias = bias_ref[h]
            sp = _dot_nt(q[:, sl], kp[:, sl]) + bias[:, :BAND]
            sn = _dot_nt(q[:, sl], kn[:, sl]) + bias[:, BAND:BAND + s_len]
            outs.append(_softmax_pv([sp, sn], [vp[:, sl], vn[:, sl]]))
        o = jnp.concatenate(outs, axis=-1)
        o_ref[b] = (o * sz_ref[b].astype(F32)).astype(BF16)


def _attn_step(q, k_new, v_new, k_past, v_past, sz, bias_raw):
    b, s_len, _ = q.shape
    assert k_past.shape[1] == BAND and s_len <= CHUNK and b % STEP_STREAMS == 0
    new = pl.BlockSpec((STEP_STREAMS, s_len, W_A), lambda i: (i, 0, 0))
    past = pl.BlockSpec((STEP_STREAMS, BAND, W_A), lambda i: (i, 0, 0))
    return pl.pallas_call(
        _attn_step_kernel,
        grid=(b // STEP_STREAMS,),
        in_specs=[new, new, new, past, past, new,
                  pl.BlockSpec((H_A, s_len, KSPAN), lambda i: (0, 0, 0))],
        out_specs=new,
        out_shape=jax.ShapeDtypeStruct((b, s_len, W_A), BF16),
        compiler_params=_params("parallel"),
        name="attn_step",
    )(q, k_new, v_new, k_past, v_past, sz, bias_raw)


def _mem_kv_kernel(mem_ref, gmem_ref, w_ref, gkm_ref, mk_ref, mv_ref):
    h = _rms_rows(mem_ref[0], gmem_ref[...]).astype(BF16)
    kv = _dot(h, w_ref[...])
    mk_ref[0] = _head_rms(kv[:, :W_M], gkm_ref[...], HEAD_DIM)
    mv_ref[0] = kv[:, W_M:]


def _mem_kv(mem, wts):
    b = mem.shape[0]
    out = pl.BlockSpec((1, N_MEM, W_M), lambda i: (i, 0, 0))
    return pl.pallas_call(
        _mem_kv_kernel,
        grid=(b,),
        in_specs=[pl.BlockSpec((1, N_MEM, D_MODEL), lambda i: (i, 0, 0)), _full((1, D_MODEL)),
                  _full((D_MODEL, 2 * W_M)), _full((1, W_M))],
        out_specs=[out, out],
        out_shape=[jax.ShapeDtypeStruct((b, N_MEM, W_M), F32)] * 2,
        compiler_params=_params("parallel"),
        name="mem_kv",
    )(mem, wts["g_mem"], wts["w_mem_kv"], wts["g_km"])


def _attn_mem_kernel(q_ref, mk_ref, mv_ref, sz_ref, o_ref):
    q = q_ref[0]
    mk, mv = mk_ref[0].astype(BF16), mv_ref[0].astype(BF16)
    outs = []
    for h in range(H_M):
        sl = slice(h * HEAD_DIM, (h + 1) * HEAD_DIM)
        outs.append(_softmax_pv([_dot_nt(q[:, sl], mk[:, sl])], [mv[:, sl]]))
    o = jnp.concatenate(outs, axis=-1)
    o_ref[0] = (o * sz_ref[0].astype(F32)).astype(BF16)


def _attn_mem(q, mk, mv, sz, tq):
    b, t, _ = q.shape
    rows = pl.BlockSpec((1, tq, W_M), lambda bi, i: (bi, i, 0))
    mem = pl.BlockSpec((1, N_MEM, W_M), lambda bi, i: (bi, 0, 0))
    return pl.pallas_call(
        _attn_mem_kernel,
        grid=(b, t // tq),
        in_specs=[rows, mem, mem, rows],
        out_specs=rows,
        out_shape=jax.ShapeDtypeStruct((b, t, W_M), BF16),
        compiler_params=_params("parallel", "parallel"),
        name="attn_mem",
    )(q, mk, mv, sz)


def _gla_kernel(q_ref, k_ref, v_ref, la_ref, sz_ref, g_ref, s0_ref, o_ref, sfin_ref, s_scr):
    t_idx = pl.program_id(1)
    rows = q_ref.shape[1]
    c = GLA_C

    @pl.when(t_idx == 0)
    def _():
        s_scr[...] = s0_ref[0].reshape(W_BK, DV_B)

    def load(ref, width):
        x = ref[0].astype(F32)
        if rows < c:
            x = jnp.concatenate([x, jnp.zeros((c - rows, width), F32)], axis=0)
        return x

    q, k, v, la = load(q_ref, W_BK), load(k_ref, W_BK), load(v_ref, W_BV), load(la_ref, W_BK)
    v16 = v.astype(BF16)

    r_cc = lax.broadcasted_iota(jnp.int32, (c, c), 0)
    c_cc = lax.broadcasted_iota(jnp.int32, (c, c), 1)
    cum = _dot_exact_lhs((c_cc <= r_cc).astype(BF16), la)
    cumx = cum - la
    r_ck = lax.broadcasted_iota(jnp.int32, (c, W_BK), 0)

    r_in = r_ck % GLA_BASE
    prods = [q * k]
    for j in range(1, GLA_BASE):
        decay = jnp.exp(jnp.where(r_in >= j, cum - pltpu.roll(cum, j, 0), NEG))
        prods.append(q * pltpu.roll(k, j, 0) * decay)
    head_of_k = lax.broadcasted_iota(jnp.int32, (W_BK, W_BV), 0) // DK_B
    head_of_v = lax.broadcasted_iota(jnp.int32, (W_BK, W_BV), 1) // DV_B
    spread = (head_of_k == head_of_v).astype(BF16)
    w = _dot(jnp.concatenate(prods, axis=0).astype(BF16), spread)
    o = w[:c] * v
    for j in range(1, GLA_BASE):
        o = o + w[j * c:(j + 1) * c] * pltpu.roll(v, j, 0)

    att = [jnp.zeros((c, c), F32) for _ in range(H_B)]
    half = GLA_BASE
    while half < c:
        nb = c // half
        cum3 = cum.reshape(nb, half, W_BK)
        q_exp = (cum3 - cumx.reshape(nb, half, W_BK)[:, 0:1, :]).reshape(c, W_BK)
        k_exp = (cum3[:, half - 1:half, :] - cum3).reshape(c, W_BK)
        odd = (r_ck // half) % 2 == 1
        qt = jnp.where(odd, q * jnp.exp(q_exp), 0.0).astype(BF16)
        kt = jnp.where(odd, 0.0, k * jnp.exp(k_exp)).astype(BF16)
        same = (r_cc // (2 * half)) == (c_cc // (2 * half))
        for h in range(H_B):
            sl = slice(h * DK_B, (h + 1) * DK_B)
            a = _dot_nt(qt[:, sl], kt[:, sl])
            att[h] = att[h] + (a if 2 * half == c else jnp.where(same, a, 0.0))
        half *= 2

    q_in = (q * jnp.exp(cum)).astype(BF16)
    k_out_t = (k * jnp.exp(cum[c - 1:c, :] - cum)).T.astype(BF16)
    keep = jnp.exp(cum.T[:, c - 1:c])
    s_old = s_scr[...]
    s_new = []
    outs = []
    for h in range(H_B):
        ks = slice(h * DK_B, (h + 1) * DK_B)
        vs = slice(h * DV_B, (h + 1) * DV_B)
        oh = o[:, vs] + _dot(att[h].astype(BF16), v16[:, vs]) + _dot(q_in[:, ks], s_old[ks].astype(BF16))
        outs.append(_rms_rows(oh, g_ref[:, vs]))
        s_new.append(s_old[ks] * keep[ks] + _dot(k_out_t[ks], v16[:, vs]))
    s_new = jnp.concatenate(s_new, axis=0)
    s_scr[...] = s_new
    out = jnp.concatenate(outs, axis=-1)[:rows]
    o_ref[0] = (out * sz_ref[0].astype(F32)).astype(BF16)

    @pl.when(t_idx == pl.num_programs(1) - 1)
    def _():
        sfin_ref[0] = s_new.reshape(H_B, DK_B, DV_B)


def _gla(q, k, v, la, sz, g, s0):
    b, t, _ = q.shape
    rows = min(t, GLA_C)
    assert t % rows == 0 and rows % SUBLANES == 0
    blk = lambda w: pl.BlockSpec((1, rows, w), lambda bi, i: (bi, i, 0))
    state = pl.BlockSpec((1, H_B, DK_B, DV_B), lambda bi, i: (bi, 0, 0, 0))
    return pl.pallas_call(
        _gla_kernel,
        grid=(b, t // rows),
        in_specs=[blk(W_BK), blk(W_BK), blk(W_BV), blk(W_BK), blk(W_BV),
                  pl.BlockSpec((1, W_BV), lambda bi, i: (0, 0)), state],
        out_specs=[blk(W_BV), state],
        out_shape=[jax.ShapeDtypeStruct((b, t, W_BV), BF16),
                   jax.ShapeDtypeStruct((b, H_B, DK_B, DV_B), F32)],
        scratch_shapes=[pltpu.VMEM((W_BK, DV_B), F32)],
        compiler_params=_params("parallel", "arbitrary"),
        name="gla",
    )(q, k, v, la, sz, g, s0)


def _prep_weights(l, norm_in, w_in, g_qa, g_ka, w_gate2, b_gate, g_gla_out, g_mem, w_mem_kv, g_qm, g_km,
                  w_up_a, w_up_b, w_up_m, w_out):
    off = [0]
    for s in IN_SIZES:
        off.append(off[-1] + s)
    w = w_in[l]
    cols = lambda i: w[:, off[i]:off[i + 1]]
    main = jnp.concatenate([cols(i) for i in range(len(IN_SIZES)) if i != 7], axis=1).astype(BF16)
    tile = lambda gain, n: jnp.tile(gain, n).reshape(1, -1)
    return {
        "norm_in": norm_in[l].reshape(1, D_MODEL),
        "w_main": main,
        "w_glr": jnp.pad(cols(7), ((0, 0), (0, LANES - GATE_RANK))).astype(BF16),
        "w_gate2": jnp.pad(w_gate2[l], ((0, LANES - GATE_RANK), (0, 0))).astype(BF16),
        "b_gate": b_gate[l].reshape(1, W_BK),
        "g_qa": tile(g_qa[l], H_A), "g_ka": tile(g_ka[l], H_A), "g_qm": tile(g_qm[l], H_M),
        "g_km": tile(g_km[l], H_M), "g_gla": tile(g_gla_out[l], H_B),
        "g_mem": g_mem[l].reshape(1, D_MODEL),
        "w_mem_kv": w_mem_kv[l].astype(BF16),
        "w_up_a": w_up_a[l].astype(BF16), "w_up_b": w_up_b[l].astype(BF16),
        "w_up_m": w_up_m[l].astype(BF16), "w_out": w_out[l].astype(BF16),
    }


def _layer(x, wts, attend_a, mem_k, mem_v, s0, tm, tq):
    b, t, _ = x.shape
    x2d = x.reshape(b * t, D_MODEL)
    (qa, ka, va, sza, qb, kb, vb, la, szb, qm, szm, ga, gb, gm) = _front(x2d, wts, tm)
    r3 = lambda z: z.reshape(b, t, z.shape[-1])
    ka3, va3 = r3(ka), r3(va)
    out_a = attend_a(r3(qa), ka3, va3, r3(sza))
    out_b, s_new = _gla(r3(qb), r3(kb), r3(vb), r3(la), r3(szb), wts["g_gla"], s0)
    out_m = _attn_mem(r3(qm), mem_k, mem_v, r3(szm), tq)
    flat = lambda z: z.reshape(b * t, z.shape[-1])
    y = _back(x2d, flat(out_a), flat(out_b), flat(out_m), ga, gb, gm, wts, tm)
    return y.reshape(b, t, D_MODEL), ka3, va3, s_new


def kernel(x_prompt, x_sample, mem_prompt, cache_a_k, cache_a_v, state_gla, cache_mem_k, cache_mem_v,
           norm_in, w_in, g_qa, g_ka, rel_bias, w_gate2, b_gate, g_gla_out, g_mem, w_mem_kv, g_qm, g_km,
           w_up_a, w_up_b, w_up_m, w_out):
    depth = w_in.shape[0]
    xp, xs = x_prompt, x_sample
    bp, tp, _ = xp.shape
    bs, ts, _ = xs.shape
    keep_p = min(BAND, tp)
    akp, avp, sgp, mkp, mvp, aks, avs, sgs = [], [], [], [], [], [], [], []
    for l in range(depth):
        wts = _prep_weights(l, norm_in, w_in, g_qa, g_ka, w_gate2, b_gate, g_gla_out, g_mem, w_mem_kv,
                            g_qm, g_km, w_up_a, w_up_b, w_up_m, w_out)
        bias_raw, bias_band = _bias_tiles(rel_bias[l])
        mk, mv = _mem_kv(mem_prompt, wts)
        attend_p = lambda q, k, v, sz: _attn_prompt(q, k, v, sz, bias_band)
        s0 = jnp.zeros((bp, H_B, DK_B, DV_B), state_gla.dtype)
        xp, ka, va, sp = _layer(xp, wts, attend_p, mk, mv, s0, 256, 512)
        akp.append(ka[:, -keep_p:].reshape(bp, keep_p, H_A, HEAD_DIM))
        avp.append(va[:, -keep_p:].reshape(bp, keep_p, H_A, HEAD_DIM))
        sgp.append(sp)
        mkp.append(mk.reshape(bp, N_MEM, H_M, HEAD_DIM))
        mvp.append(mv.reshape(bp, N_MEM, H_M, HEAD_DIM))
        past_k = cache_a_k[l].reshape(bs, -1, W_A)
        past_v = cache_a_v[l].reshape(bs, -1, W_A)
        attend_s = lambda q, k, v, sz: _attn_step(q, k, v, past_k, past_v, sz, bias_raw)
        xs, ka_s, va_s, ss = _layer(xs, wts, attend_s, cache_mem_k[l].reshape(bs, N_MEM, W_M),
                                    cache_mem_v[l].reshape(bs, N_MEM, W_M), state_gla[l], 256, ts)
        aks.append(ka_s.reshape(bs, ts, H_A, HEAD_DIM))
        avs.append(va_s.reshape(bs, ts, H_A, HEAD_DIM))
        sgs.append(ss)
    return (xp, xs, jnp.stack(akp), jnp.stack(avp), jnp.stack(sgp), jnp.stack(mkp), jnp.stack(mvp),
            jnp.stack(aks), jnp.stack(avs), jnp.stack(sgs))
```

```python
import functools

import jax
import jax.numpy as jnp
from jax import lax
from jax.experimental import pallas as pl
from jax.experimental.pallas import tpu as pltpu

F32 = jnp.float32
BF16 = jnp.bfloat16

D_MODEL = 1024
CHUNK = 64
LEFT_CHUNKS = 8
HEAD_DIM = 64
H_A = 8
W_A = H_A * HEAD_DIM
REL_CLIP = 128
H_B = 4
DK_B = 64
DV_B = 128
W_BK = H_B * DK_B
W_BV = H_B * DV_B
GATE_RANK = 16
GATE_TAU = 16.0
N_MEM = 256
H_M = 4
W_M = H_M * HEAD_DIM
EPS = 1e-6
IN_SIZES = (W_A, W_A, W_A, W_A, W_BK, W_BK, W_BV, GATE_RANK, W_BV, W_M, W_M, D_MODEL, D_MODEL, D_MODEL)

LANES = 128
SUBLANES = 8
VMEM_LIMIT = 56 * 1024 * 1024
NEG = -1e30

BAND = LEFT_CHUNKS * CHUNK
QBLK = 256
KSPAN = BAND + QBLK
WINDOW = BAND + CHUNK
BIAS_W = BAND + 2 * CHUNK
GLA_C = 128
GLA_BASE = SUBLANES


def _dot(a, b):
    return jnp.dot(a, b, preferred_element_type=F32)


def _dot_nt(a, b):
    return lax.dot_general(a, b, (((1,), (1,)), ((), ())), preferred_element_type=F32)


def _split3(x):
    hi = x.astype(BF16)
    r = x - hi.astype(F32)
    mid = r.astype(BF16)
    lo = (r - mid.astype(F32)).astype(BF16)
    return hi, mid, lo


def _dot_exact_lhs(a01, x):
    hi, mid, lo = _split3(x)
    return _dot(a01, hi) + _dot(a01, mid) + _dot(a01, lo)


def _sigmoid(z):
    return 1.0 / (1.0 + jnp.exp(-z))


def _silu(z):
    return z * _sigmoid(z)


def _log_sigmoid(z):
    return jnp.minimum(z, 0.0) - jnp.log1p(jnp.exp(-jnp.abs(z)))


def _rms_rows(x, g):
    ms = jnp.mean(x * x, axis=-1, keepdims=True)
    return x * lax.rsqrt(ms + EPS) * g


def _head_mean_matrix(width, head):
    r = lax.broadcasted_iota(jnp.int32, (width, width), 0) // head
    c = lax.broadcasted_iota(jnp.int32, (width, width), 1) // head
    return jnp.where(r == c, 1.0 / head, 0.0).astype(BF16)


def _head_rms(x, g, head):
    x2 = x * x
    hi = x2.astype(BF16)
    lo = (x2 - hi.astype(F32)).astype(BF16)
    avg = _head_mean_matrix(x.shape[-1], head)
    ms = _dot(hi, avg) + _dot(lo, avg)
    return x * lax.rsqrt(ms + EPS) * g


def _full(shape):
    nd = len(shape)
    return pl.BlockSpec(shape, lambda *_: (0,) * nd, pipeline_mode=pl.Buffered(1))


def _params(*sem):
    return pltpu.CompilerParams(dimension_semantics=sem, vmem_limit_bytes=VMEM_LIMIT)


_MAIN_SIZES = (W_A, W_A, W_A, W_A, W_BK, W_BK, W_BV, W_BV, W_M, W_M, D_MODEL, D_MODEL, D_MODEL)
_MAIN_OFF = tuple(int(sum(_MAIN_SIZES[:i])) for i in range(len(_MAIN_SIZES) + 1))
W_MAIN = _MAIN_OFF[-1]


def _front_kernel(x_ref, nin_ref, w_ref, wglr_ref, wg2_ref, bg_ref, gqa_ref, gka_ref, gqm_ref,
                  qa_o, ka_o, va_o, ka16_o, va16_o, sza_o, qb_o, kb_o, vb_o, la_o, szb_o, qm_o, szm_o,
                  ga_o, gb_o, gm_o):
    h = _rms_rows(x_ref[...], nin_ref[...]).astype(BF16)

    def proj(i):
        return _dot(h, w_ref[:, _MAIN_OFF[i]:_MAIN_OFF[i + 1]])

    scale = HEAD_DIM ** -0.5
    qa_o[...] = (_head_rms(proj(0), gqa_ref[...], HEAD_DIM) * scale).astype(BF16)
    ka = _head_rms(proj(1), gka_ref[...], HEAD_DIM)
    va = proj(2)
    ka_o[...] = ka
    va_o[...] = va
    ka16_o[...] = ka.astype(BF16)
    va16_o[...] = va.astype(BF16)
    sza_o[...] = _silu(proj(3)).astype(BF16)
    qb_o[...] = (proj(4) * (DK_B ** -0.5)).astype(BF16)
    kb_o[...] = proj(5).astype(BF16)
    vb_o[...] = proj(6).astype(BF16)
    szb_o[...] = _silu(proj(7)).astype(BF16)
    qm_o[...] = (_head_rms(proj(8), gqm_ref[...], HEAD_DIM) * scale).astype(BF16)
    szm_o[...] = _silu(proj(9)).astype(BF16)
    ga_o[...] = _sigmoid(proj(10)).astype(BF16)
    gb_o[...] = _sigmoid(proj(11)).astype(BF16)
    gm_o[...] = _sigmoid(proj(12)).astype(BF16)
    glr = _dot(h, wglr_ref[...]).astype(BF16)
    z = _dot(glr, wg2_ref[...]) + bg_ref[...]
    la_o[...] = _log_sigmoid(z) * (1.0 / GATE_TAU)


def _front(x2d, wts, tm, t, keep):
    n = x2d.shape[0]
    tm = min(tm, n)
    row = lambda w: pl.BlockSpec((tm, w), lambda i: (i, 0))
    if keep == t:
        kv_rows, kv = n, row(W_A)
    else:
        assert t % tm == 0 and keep % tm == 0
        per_stream, kept = t // tm, keep // tm
        kv_rows = (n // t) * keep
        kv = pl.BlockSpec((tm, W_A), lambda i: (
            (i // per_stream) * kept + jnp.maximum(i % per_stream - (per_stream - kept), 0), 0))
    widths = (W_A, W_A, W_A, W_A, W_A, W_A, W_BK, W_BK, W_BV, W_BK, W_BV, W_M, W_M, D_MODEL, D_MODEL, D_MODEL)
    dtypes = (BF16, F32, F32, BF16, BF16, BF16, BF16, BF16, BF16, F32, BF16, BF16, BF16, BF16, BF16, BF16)
    rows = (n, kv_rows, kv_rows) + (n,) * 13
    specs = [row(W_A), kv, kv] + [row(w) for w in widths[3:]]
    return pl.pallas_call(
        _front_kernel,
        grid=(n // tm,),
        in_specs=[row(D_MODEL), _full((1, D_MODEL)), _full((D_MODEL, W_MAIN)), _full((D_MODEL, LANES)),
                  _full((LANES, W_BK)), _full((1, W_BK)), _full((1, W_A)), _full((1, W_A)), _full((1, W_M))],
        out_specs=specs,
        out_shape=[jax.ShapeDtypeStruct((r, w), d) for r, w, d in zip(rows, widths, dtypes)],
        compiler_params=_params("arbitrary"),
        name="front",
    )(x2d, wts["norm_in"], wts["w_main"], wts["w_glr"], wts["w_gate2"], wts["b_gate"],
      wts["g_qa"], wts["g_ka"], wts["g_qm"])


def _back_kernel(x_ref, a_ref, b_ref, m_ref, ga_ref, gb_ref, gm_ref, wa_ref, wb_ref, wm_ref, wo_ref, y_ref):
    u = (ga_ref[...].astype(F32) * _dot(a_ref[...], wa_ref[...])
         + gb_ref[...].astype(F32) * _dot(b_ref[...], wb_ref[...])
         + gm_ref[...].astype(F32) * _dot(m_ref[...], wm_ref[...]))
    y_ref[...] = x_ref[...] + _dot(u.astype(BF16), wo_ref[...])


def _back(x2d, a, b, m, ga, gb, gm, wts, tm):
    n = x2d.shape[0]
    tm = min(tm, n)
    row = lambda w: pl.BlockSpec((tm, w), lambda i: (i, 0))
    return pl.pallas_call(
        _back_kernel,
        grid=(n // tm,),
        in_specs=[row(D_MODEL), row(W_A), row(W_BV), row(W_M), row(D_MODEL), row(D_MODEL), row(D_MODEL),
                  _full((W_A, D_MODEL)), _full((W_BV, D_MODEL)), _full((W_M, D_MODEL)),
                  _full((D_MODEL, D_MODEL))],
        out_specs=row(D_MODEL),
        out_shape=jax.ShapeDtypeStruct((n, D_MODEL), F32),
        compiler_params=_params("parallel"),
        name="back",
    )(x2d, a, b, m, ga, gb, gm, wts["w_up_a"], wts["w_up_b"], wts["w_up_m"], wts["w_out"])


def _bias_kernel(e_ref, tile_ref):
    e = jnp.broadcast_to(e_ref[0], (CHUNK, 2 * BAND))
    tile_ref[0] = pltpu.roll(e, 0, 1, stride=1, stride_axis=0)[:, :BIAS_W]


def _bias_tile(rel_bias):
    j = jnp.arange(2 * BAND)
    dist = jnp.where(j < BIAS_W, BAND - j, REL_CLIP)
    e = rel_bias[:, jnp.clip(dist, -REL_CLIP, REL_CLIP) + REL_CLIP].reshape(H_A, 1, 2 * BAND)
    return pl.pallas_call(
        _bias_kernel,
        grid=(H_A,),
        in_specs=[pl.BlockSpec((1, 1, 2 * BAND), lambda h: (h, 0, 0))],
        out_specs=pl.BlockSpec((1, CHUNK, BIAS_W), lambda h: (h, 0, 0)),
        out_shape=jax.ShapeDtypeStruct((H_A, CHUNK, BIAS_W), F32),
        compiler_params=_params("parallel"),
        name="bias_tile",
    )(e)


def _softmax_pv(s_list, v_list):
    m = functools.reduce(jnp.maximum, [jnp.max(s, axis=-1, keepdims=True) for s in s_list])
    ps = [jnp.exp(s - m) for s in s_list]
    l = functools.reduce(jnp.add, [jnp.sum(p, axis=-1, keepdims=True) for p in ps])
    o = functools.reduce(jnp.add, [_dot(p.astype(BF16), v) for p, v in zip(ps, v_list)])
    return o / l


def _head_pairs_attention(q_pairs, k_lists, v_lists, bias_fn):
    m_rows = q_pairs[0].shape[0]
    lane = lax.broadcasted_iota(jnp.int32, (m_rows, LANES), 1)
    first = lane < HEAD_DIM
    scores = []
    for p, qp in enumerate(q_pairs):
        zero = jnp.zeros_like(qp)
        q2 = jnp.concatenate([jnp.where(first, qp, zero), jnp.where(first, zero, qp)], axis=0)
        s_list = []
        for i, k in enumerate(k_lists[p]):
            s = _dot_nt(q2, k)
            b0, b1 = bias_fn(p, 0, i), bias_fn(p, 1, i)
            s_list.append(s if b0 is None else s + jnp.concatenate([b0, b1], axis=0))
        scores.append(s_list)
    tops = [functools.reduce(jnp.maximum, [jnp.max(s, axis=-1, keepdims=True) for s in s_list])
            for s_list in scores]
    probs = [[jnp.exp(s - m) for s in s_list] for s_list, m in zip(scores, tops)]
    sums = [functools.reduce(jnp.add, [jnp.sum(x, axis=-1, keepdims=True) for x in p_list]) for p_list in probs]
    outs = [functools.reduce(jnp.add, [_dot(x.astype(BF16), v) for x, v in zip(p_list, v_lists[p])]) / l
            for p, (p_list, l) in enumerate(zip(probs, sums))]
    return [jnp.where(first, o[:m_rows], o[m_rows:]) for o in outs]


def _attn_prompt_kernel(q_ref, k0_ref, k1_ref, k2_ref, v0_ref, v1_ref, v2_ref, sz_ref, bias_ref, o_ref,
                        k_buf, v_buf):
    i = pl.program_id(1)
    for j, (k_ref, v_ref) in enumerate(((k0_ref, v0_ref), (k1_ref, v1_ref), (k2_ref, v2_ref))):
        k_buf[j * QBLK:(j + 1) * QBLK, :] = k_ref[0]
        v_buf[j * QBLK:(j + 1) * QBLK, :] = v_ref[0]
    col = lax.broadcasted_iota(jnp.int32, (1, WINDOW), 1)
    for c in range(QBLK // CHUNK):
        rows = slice(c * CHUNK, (c + 1) * CHUNK)
        win = slice(c * CHUNK, c * CHUNK + WINDOW)
        before_start = jnp.where(col + c * CHUNK < (BAND // QBLK - i) * QBLK, NEG, 0.0)
        pairs = [slice(p * LANES, (p + 1) * LANES) for p in range(H_A // 2)]
        outs = _head_pairs_attention(
            [q_ref[0, rows, lanes] for lanes in pairs],
            [[k_buf[win, lanes]] for lanes in pairs], [[v_buf[win, lanes]] for lanes in pairs],
            lambda p, e, _: bias_ref[2 * p + e, :, :WINDOW] + before_start)
        for lanes, o in zip(pairs, outs):
            o_ref[0, rows, lanes] = (o * sz_ref[0, rows, lanes].astype(F32)).astype(BF16)


def _attn_prompt(q, k, v, sz, bias):
    b, t, _ = q.shape
    nb = BAND // QBLK
    cur = pl.BlockSpec((1, QBLK, W_A), lambda bi, i: (bi, i, 0))
    prev = lambda d: pl.BlockSpec((1, QBLK, W_A), lambda bi, i: (bi, jnp.maximum(i - d, 0), 0))
    kv_specs = [prev(nb - j) for j in range(nb)] + [cur]
    assert nb == 2
    return pl.pallas_call(
        _attn_prompt_kernel,
        grid=(b, t // QBLK),
        in_specs=[cur] + kv_specs + kv_specs + [cur, _full((H_A, CHUNK, BIAS_W))],
        out_specs=cur,
        out_shape=jax.ShapeDtypeStruct((b, t, W_A), BF16),
        scratch_shapes=[pltpu.VMEM((KSPAN, W_A), BF16), pltpu.VMEM((KSPAN, W_A), BF16)],
        compiler_params=_params("parallel", "parallel"),
        name="attn_prompt",
    )(q, k, k, k, v, v, v, sz, bias)


STEP_STREAMS = 4


def _attn_step_kernel(q_ref, kn_ref, vn_ref, kp_ref, vp_ref, sz_ref, bias_ref, o_ref):
    s_len = q_ref.shape[1]
    pairs = [slice(p * LANES, (p + 1) * LANES) for p in range(H_A // 2)]
    cols = (slice(0, BAND), slice(BAND, BAND + s_len))
    for b in range(STEP_STREAMS):
        outs = _head_pairs_attention(
            [q_ref[b, :, lanes] for lanes in pairs],
            [[kp_ref[b, :, lanes].astype(BF16), kn_ref[b, :, lanes]] for lanes in pairs],
            [[vp_ref[b, :, lanes].astype(BF16), vn_ref[b, :, lanes]] for lanes in pairs],
            lambda p, e, i: bias_ref[2 * p + e, :s_len, cols[i]])
        for lanes, o in zip(pairs, outs):
            o_ref[b, :, lanes] = (o * sz_ref[b, :, lanes].astype(F32)).astype(BF16)


def _attn_step(q, k_new, v_new, k_past, v_past, sz, bias):
    b, s_len, _ = q.shape
    assert k_past.shape[1] == BAND and s_len <= CHUNK and b % STEP_STREAMS == 0
    new = pl.BlockSpec((STEP_STREAMS, s_len, W_A), lambda i: (i, 0, 0))
    past = pl.BlockSpec((STEP_STREAMS, BAND, W_A), lambda i: (i, 0, 0))
    return pl.pallas_call(
        _attn_step_kernel,
        grid=(b // STEP_STREAMS,),
        in_specs=[new, new, new, past, past, new, _full((H_A, CHUNK, BIAS_W))],
        out_specs=new,
        out_shape=jax.ShapeDtypeStruct((b, s_len, W_A), BF16),
        compiler_params=_params("parallel"),
        name="attn_step",
    )(q, k_new, v_new, k_past, v_past, sz, bias)


def _mem_kv_kernel(mem_ref, gmem_ref, w_ref, gkm_ref, mk_ref, mv_ref):
    h = _rms_rows(mem_ref[0], gmem_ref[...]).astype(BF16)
    kv = _dot(h, w_ref[...])
    mk_ref[0] = _head_rms(kv[:, :W_M], gkm_ref[...], HEAD_DIM)
    mv_ref[0] = kv[:, W_M:]


def _mem_kv(mem, wts):
    b = mem.shape[0]
    out = pl.BlockSpec((1, N_MEM, W_M), lambda i: (i, 0, 0))
    return pl.pallas_call(
        _mem_kv_kernel,
        grid=(b,),
        in_specs=[pl.BlockSpec((1, N_MEM, D_MODEL), lambda i: (i, 0, 0)), _full((1, D_MODEL)),
                  _full((D_MODEL, 2 * W_M)), _full((1, W_M))],
        out_specs=[out, out],
        out_shape=[jax.ShapeDtypeStruct((b, N_MEM, W_M), F32)] * 2,
        compiler_params=_params("parallel"),
        name="mem_kv",
    )(mem, wts["g_mem"], wts["w_mem_kv"], wts["g_km"])


MEM_ROWS = 128


def _attn_mem_kernel(q_ref, mk_ref, mv_ref, sz_ref, o_ref):
    tq = q_ref.shape[1]
    step = min(tq, MEM_ROWS)
    pairs = [slice(p * LANES, (p + 1) * LANES) for p in range(H_M // 2)]
    mk = [[mk_ref[0, :, lanes].astype(BF16)] for lanes in pairs]
    mv = [[mv_ref[0, :, lanes].astype(BF16)] for lanes in pairs]
    for r in range(0, tq, step):
        rows = slice(r, r + step)
        outs = _head_pairs_attention([q_ref[0, rows, lanes] for lanes in pairs], mk, mv, lambda p, e, i: None)
        for lanes, o in zip(pairs, outs):
            o_ref[0, rows, lanes] = (o * sz_ref[0, rows, lanes].astype(F32)).astype(BF16)


def _attn_mem(q, mk, mv, sz, tq):
    b, t, _ = q.shape
    rows = pl.BlockSpec((1, tq, W_M), lambda bi, i: (bi, i, 0))
    mem = pl.BlockSpec((1, N_MEM, W_M), lambda bi, i: (bi, 0, 0))
    return pl.pallas_call(
        _attn_mem_kernel,
        grid=(b, t // tq),
        in_specs=[rows, mem, mem, rows],
        out_specs=rows,
        out_shape=jax.ShapeDtypeStruct((b, t, W_M), BF16),
        compiler_params=_params("parallel", "parallel"),
        name="attn_mem",
    )(q, mk, mv, sz)


def _gla_kernel(q_ref, k_ref, v_ref, la_ref, sz_ref, g_ref, s0_ref, o_ref, sfin_ref, s_scr):
    t_idx = pl.program_id(1)
    rows = q_ref.shape[1]
    c = GLA_C

    @pl.when(t_idx == 0)
    def _():
        s_scr[...] = s0_ref[0].reshape(W_BK, DV_B)

    def load(ref, width):
        x = ref[0].astype(F32)
        if rows < c:
            x = jnp.concatenate([x, jnp.zeros((c - rows, width), F32)], axis=0)
        return x

    q, k, v, la = load(q_ref, W_BK), load(k_ref, W_BK), load(v_ref, W_BV), load(la_ref, W_BK)
    v16 = v.astype(BF16)

    r_cc = lax.broadcasted_iota(jnp.int32, (c, c), 0)
    c_cc = lax.broadcasted_iota(jnp.int32, (c, c), 1)
    cum = _dot_exact_lhs((c_cc <= r_cc).astype(BF16), la)
    cumx = cum - la
    r_ck = lax.broadcasted_iota(jnp.int32, (c, W_BK), 0)

    def back(x, j):
        n, w = x.shape
        return pltpu.roll(x.reshape(n // GLA_BASE, GLA_BASE, w), j, 1).reshape(n, w)

    r_in = r_ck % GLA_BASE
    prods = [q * k]
    for j in range(1, GLA_BASE):
        decay = jnp.exp(jnp.where(r_in >= j, cum - back(cum, j), NEG))
        prods.append(q * back(k, j) * decay)
    head_of_k = lax.broadcasted_iota(jnp.int32, (W_BK, W_BV), 0) // DK_B
    head_of_v = lax.broadcasted_iota(jnp.int32, (W_BK, W_BV), 1) // DV_B
    spread = (head_of_k == head_of_v).astype(BF16)
    w = _dot(jnp.concatenate(prods, axis=0).astype(BF16), spread)
    o = w[:c] * v
    for j in range(1, GLA_BASE):
        o = o + w[j * c:(j + 1) * c] * back(v, j)

    att = [jnp.zeros((c, c), F32) for _ in range(H_B)]
    half = GLA_BASE
    while half < c:
        nb = c // half
        cum3 = cum.reshape(nb, half, W_BK)
        q_exp = (cum3 - cumx.reshape(nb, half, W_BK)[:, 0:1, :]).reshape(c, W_BK)
        k_exp = (cum3[:, half - 1:half, :] - cum3).reshape(c, W_BK)
        odd = (r_ck // half) % 2 == 1
        qt = jnp.where(odd, q * jnp.exp(q_exp), 0.0).astype(BF16)
        kt = jnp.where(odd, 0.0, k * jnp.exp(k_exp)).astype(BF16)
        same = (r_cc // (2 * half)) == (c_cc // (2 * half))
        for h in range(H_B):
            sl = slice(h * DK_B, (h + 1) * DK_B)
            a = _dot_nt(qt[:, sl], kt[:, sl])
            att[h] = att[h] + (a if 2 * half == c else jnp.where(same, a, 0.0))
        half *= 2

    q_in = (q * jnp.exp(cum)).astype(BF16)
    k_out_t = (k * jnp.exp(cum[c - 1:c, :] - cum)).T.astype(BF16)
    keep = jnp.exp(cum.T[:, c - 1:c])
    s_old = s_scr[...]
    s_new = []
    outs = []
    for h in range(H_B):
        ks = slice(h * DK_B, (h + 1) * DK_B)
        vs = slice(h * DV_B, (h + 1) * DV_B)
        oh = o[:, vs] + _dot(att[h].astype(BF16), v16[:, vs]) + _dot(q_in[:, ks], s_old[ks].astype(BF16))
        outs.append(_rms_rows(oh, g_ref[:, vs]))
        s_new.append(s_old[ks] * keep[ks] + _dot(k_out_t[ks], v16[:, vs]))
    s_new = jnp.concatenate(s_new, axis=0)
    s_scr[...] = s_new
    out = jnp.concatenate(outs, axis=-1)[:rows]
    o_ref[0] = (out * sz_ref[0].astype(F32)).astype(BF16)

    @pl.when(t_idx == pl.num_programs(1) - 1)
    def _():
        sfin_ref[0] = s_new.reshape(H_B, DK_B, DV_B)


def _gla(q, k, v, la, sz, g, s0):
    b, t, _ = q.shape
    rows = min(t, GLA_C)
    assert t % rows == 0 and rows % SUBLANES == 0
    blk = lambda w: pl.BlockSpec((1, rows, w), lambda bi, i: (bi, i, 0))
    state = pl.BlockSpec((1, H_B, DK_B, DV_B), lambda bi, i: (bi, 0, 0, 0))
    return pl.pallas_call(
        _gla_kernel,
        grid=(b, t // rows),
        in_specs=[blk(W_BK), blk(W_BK), blk(W_BV), blk(W_BK), blk(W_BV),
                  pl.BlockSpec((1, W_BV), lambda bi, i: (0, 0)), state],
        out_specs=[blk(W_BV), state],
        out_shape=[jax.ShapeDtypeStruct((b, t, W_BV), BF16),
                   jax.ShapeDtypeStruct((b, H_B, DK_B, DV_B), F32)],
        scratch_shapes=[pltpu.VMEM((W_BK, DV_B), F32)],
        compiler_params=_params("parallel", "arbitrary"),
        name="gla",
    )(q, k, v, la, sz, g, s0)


def _prep_weights(l, norm_in, w_in, g_qa, g_ka, w_gate2, b_gate, g_gla_out, g_mem, w_mem_kv, g_qm, g_km,
                  w_up_a, w_up_b, w_up_m, w_out):
    off = [0]
    for s in IN_SIZES:
        off.append(off[-1] + s)
    w = w_in[l]
    cols = lambda i: w[:, off[i]:off[i + 1]]
    main = jnp.concatenate([cols(i) for i in range(len(IN_SIZES)) if i != 7], axis=1).astype(BF16)
    tile = lambda gain, n: jnp.tile(gain, n).reshape(1, -1)
    return {
        "norm_in": norm_in[l].reshape(1, D_MODEL),
        "w_main": main,
        "w_glr": jnp.pad(cols(7), ((0, 0), (0, LANES - GATE_RANK))).astype(BF16),
        "w_gate2": jnp.pad(w_gate2[l], ((0, LANES - GATE_RANK), (0, 0))).astype(BF16),
        "b_gate": b_gate[l].reshape(1, W_BK),
        "g_qa": tile(g_qa[l], H_A), "g_ka": tile(g_ka[l], H_A), "g_qm": tile(g_qm[l], H_M),
        "g_km": tile(g_km[l], H_M), "g_gla": tile(g_gla_out[l], H_B),
        "g_mem": g_mem[l].reshape(1, D_MODEL),
        "w_mem_kv": w_mem_kv[l].astype(BF16),
        "w_up_a": w_up_a[l].astype(BF16), "w_up_b": w_up_b[l].astype(BF16),
        "w_up_m": w_up_m[l].astype(BF16), "w_out": w_out[l].astype(BF16),
    }


def _layer(x, wts, attend_a, mem_k, mem_v, s0, tm, tq):
    b, t, _ = x.shape
    keep = min(BAND, t)
    x2d = x.reshape(b * t, D_MODEL)
    (qa, ka, va, ka16, va16, sza, qb, kb, vb, la, szb, qm, szm, ga, gb, gm) = _front(x2d, wts, tm, t, keep)
    r3 = lambda z: z.reshape(b, t, z.shape[-1])
    out_a = attend_a(r3(qa), r3(ka16), r3(va16), r3(sza))
    out_b, s_new = _gla(r3(qb), r3(kb), r3(vb), r3(la), r3(szb), wts["g_gla"], s0)
    out_m = _attn_mem(r3(qm), mem_k, mem_v, r3(szm), tq)
    flat = lambda z: z.reshape(b * t, z.shape[-1])
    y = _back(x2d, flat(out_a), flat(out_b), flat(out_m), ga, gb, gm, wts, tm)
    heads = lambda z: z.reshape(b, keep, H_A, HEAD_DIM)
    return y.reshape(b, t, D_MODEL), heads(ka), heads(va), s_new


def kernel(x_prompt, x_sample, mem_prompt, cache_a_k, cache_a_v, state_gla, cache_mem_k, cache_mem_v,
           norm_in, w_in, g_qa, g_ka, rel_bias, w_gate2, b_gate, g_gla_out, g_mem, w_mem_kv, g_qm, g_km,
           w_up_a, w_up_b, w_up_m, w_out):
    depth = w_in.shape[0]
    xp, xs = x_prompt, x_sample
    bp, tp, _ = xp.shape
    bs, ts, _ = xs.shape
    akp, avp, sgp, mkp, mvp, aks, avs, sgs = [], [], [], [], [], [], [], []
    for l in range(depth):
        wts = _prep_weights(l, norm_in, w_in, g_qa, g_ka, w_gate2, b_gate, g_gla_out, g_mem, w_mem_kv,
                            g_qm, g_km, w_up_a, w_up_b, w_up_m, w_out)
        bias = _bias_tile(rel_bias[l])
        mk, mv = _mem_kv(mem_prompt, wts)
        attend_p = lambda q, k, v, sz: _attn_prompt(q, k, v, sz, bias)
        s0 = jnp.zeros((bp, H_B, DK_B, DV_B), state_gla.dtype)
        xp, ka, va, sp = _layer(xp, wts, attend_p, mk, mv, s0, 256, 512)
        akp.append(ka)
        avp.append(va)
        sgp.append(sp)
        mkp.append(mk.reshape(bp, N_MEM, H_M, HEAD_DIM))
        mvp.append(mv.reshape(bp, N_MEM, H_M, HEAD_DIM))
        past_k = cache_a_k[l].reshape(bs, -1, W_A)
        past_v = cache_a_v[l].reshape(bs, -1, W_A)
        attend_s = lambda q, k, v, sz: _attn_step(q, k, v, past_k, past_v, sz, bias)
        xs, ka_s, va_s, ss = _layer(xs, wts, attend_s, cache_mem_k[l].reshape(bs, N_MEM, W_M),
                                    cache_mem_v[l].reshape(bs, N_MEM, W_M), state_gla[l], 256, ts)
        aks.append(ka_s)
        avs.append(va_s)
        sgs.append(ss)
    return (xp, xs, jnp.stack(akp), jnp.stack(avp), jnp.stack(sgp), jnp.stack(mkp), jnp.stack(mvp),
            jnp.stack(aks), jnp.stack(avs), jnp.stack(sgs))
```

```python
import functools

import jax
import jax.numpy as jnp
from jax import lax
from jax.experimental import pallas as pl
from jax.experimental.pallas import tpu as pltpu

F32 = jnp.float32
BF16 = jnp.bfloat16

D_MODEL = 1024
CHUNK = 64
LEFT_CHUNKS = 8
HEAD_DIM = 64
H_A = 8
W_A = H_A * HEAD_DIM
REL_CLIP = 128
H_B = 4
DK_B = 64
DV_B = 128
W_BK = H_B * DK_B
W_BV = H_B * DV_B
GATE_RANK = 16
GATE_TAU = 16.0
N_MEM = 256
H_M = 4
W_M = H_M * HEAD_DIM
EPS = 1e-6
IN_SIZES = (W_A, W_A, W_A, W_A, W_BK, W_BK, W_BV, GATE_RANK, W_BV, W_M, W_M, D_MODEL, D_MODEL, D_MODEL)

LANES = 128
SUBLANES = 8
VMEM_LIMIT = 56 * 1024 * 1024
NEG = -1e30

BAND = LEFT_CHUNKS * CHUNK
QBLK = 256
KSPAN = BAND + QBLK
WINDOW = BAND + CHUNK
BIAS_W = BAND + 2 * CHUNK
GLA_C = 128
GLA_BASE = SUBLANES


def _dot(a, b):
    return jnp.dot(a, b, preferred_element_type=F32)


def _dot_nt(a, b):
    return lax.dot_general(a, b, (((1,), (1,)), ((), ())), preferred_element_type=F32)


def _split3(x):
    hi = x.astype(BF16)
    r = x - hi.astype(F32)
    mid = r.astype(BF16)
    lo = (r - mid.astype(F32)).astype(BF16)
    return hi, mid, lo


def _dot_exact_lhs(a01, x):
    hi, mid, lo = _split3(x)
    return _dot(a01, hi) + _dot(a01, mid) + _dot(a01, lo)


def _sigmoid(z):
    return 1.0 / (1.0 + jnp.exp(-z))


def _silu(z):
    return z * _sigmoid(z)


def _log_sigmoid(z):
    return jnp.minimum(z, 0.0) - jnp.log1p(jnp.exp(-jnp.abs(z)))


def _rms_rows(x, g):
    ms = jnp.mean(x * x, axis=-1, keepdims=True)
    return x * lax.rsqrt(ms + EPS) * g


def _head_mean_matrix(width, head):
    r = lax.broadcasted_iota(jnp.int32, (width, width), 0) // head
    c = lax.broadcasted_iota(jnp.int32, (width, width), 1) // head
    return jnp.where(r == c, 1.0 / head, 0.0).astype(BF16)


def _head_rms(x, g, head):
    x2 = x * x
    hi = x2.astype(BF16)
    lo = (x2 - hi.astype(F32)).astype(BF16)
    avg = _head_mean_matrix(x.shape[-1], head)
    ms = _dot(hi, avg) + _dot(lo, avg)
    return x * lax.rsqrt(ms + EPS) * g


def _full(shape):
    nd = len(shape)
    return pl.BlockSpec(shape, lambda *_: (0,) * nd, pipeline_mode=pl.Buffered(1))


def _params(*sem):
    return pltpu.CompilerParams(dimension_semantics=sem, vmem_limit_bytes=VMEM_LIMIT)


_MAIN_SIZES = (W_A, W_A, W_A, W_A, W_BK, W_BK, W_BV, W_BV, W_M, W_M, D_MODEL, D_MODEL, D_MODEL)
_MAIN_OFF = tuple(int(sum(_MAIN_SIZES[:i])) for i in range(len(_MAIN_SIZES) + 1))
W_MAIN = _MAIN_OFF[-1]


def _front_kernel(x_ref, nin_ref, w_ref, wglr_ref, wg2_ref, bg_ref, gqa_ref, gka_ref, gqm_ref,
                  qa_o, ka_o, va_o, ka16_o, va16_o, sza_o, qb_o, kb_o, vb_o, la_o, szb_o, qm_o, szm_o,
                  ga_o, gb_o, gm_o):
    h = _rms_rows(x_ref[...], nin_ref[...]).astype(BF16)

    def proj(i):
        return _dot(h, w_ref[:, _MAIN_OFF[i]:_MAIN_OFF[i + 1]])

    scale = HEAD_DIM ** -0.5
    qa_o[...] = (_head_rms(proj(0), gqa_ref[...], HEAD_DIM) * scale).astype(BF16)
    ka = _head_rms(proj(1), gka_ref[...], HEAD_DIM)
    va = proj(2)
    ka_o[...] = ka
    va_o[...] = va
    ka16_o[...] = ka.astype(BF16)
    va16_o[...] = va.astype(BF16)
    sza_o[...] = _silu(proj(3)).astype(BF16)
    qb_o[...] = (proj(4) * (DK_B ** -0.5)).astype(BF16)
    kb_o[...] = proj(5).astype(BF16)
    vb_o[...] = proj(6).astype(BF16)
    szb_o[...] = _silu(proj(7)).astype(BF16)
    qm_o[...] = (_head_rms(proj(8), gqm_ref[...], HEAD_DIM) * scale).astype(BF16)
    szm_o[...] = _silu(proj(9)).astype(BF16)
    ga_o[...] = _sigmoid(proj(10)).astype(BF16)
    gb_o[...] = _sigmoid(proj(11)).astype(BF16)
    gm_o[...] = _sigmoid(proj(12)).astype(BF16)
    glr = _dot(h, wglr_ref[...]).astype(BF16)
    z = _dot(glr, wg2_ref[...]) + bg_ref[...]
    la_o[...] = _log_sigmoid(z) * (1.0 / GATE_TAU)


def _front(x2d, wts, tm, t, keep):
    n = x2d.shape[0]
    tm = min(tm, n)
    row = lambda w: pl.BlockSpec((tm, w), lambda i: (i, 0))
    if keep == t:
        kv_rows, kv = n, row(W_A)
    else:
        assert t % tm == 0 and keep % tm == 0
        per_stream, kept = t // tm, keep // tm
        kv_rows = (n // t) * keep
        kv = pl.BlockSpec((tm, W_A), lambda i: (
            (i // per_stream) * kept + jnp.maximum(i % per_stream - (per_stream - kept), 0), 0))
    widths = (W_A, W_A, W_A, W_A, W_A, W_A, W_BK, W_BK, W_BV, W_BK, W_BV, W_M, W_M, D_MODEL, D_MODEL, D_MODEL)
    dtypes = (BF16, F32, F32, BF16, BF16, BF16, BF16, BF16, BF16, F32, BF16, BF16, BF16, BF16, BF16, BF16)
    rows = (n, kv_rows, kv_rows) + (n,) * 13
    specs = [row(W_A), kv, kv] + [row(w) for w in widths[3:]]
    return pl.pallas_call(
        _front_kernel,
        grid=(n // tm,),
        in_specs=[row(D_MODEL), _full((1, D_MODEL)), _full((D_MODEL, W_MAIN)), _full((D_MODEL, LANES)),
                  _full((LANES, W_BK)), _full((1, W_BK)), _full((1, W_A)), _full((1, W_A)), _full((1, W_M))],
        out_specs=specs,
        out_shape=[jax.ShapeDtypeStruct((r, w), d) for r, w, d in zip(rows, widths, dtypes)],
        compiler_params=_params("arbitrary"),
        name="front",
    )(x2d, wts["norm_in"], wts["w_main"], wts["w_glr"], wts["w_gate2"], wts["b_gate"],
      wts["g_qa"], wts["g_ka"], wts["g_qm"])


def _back_kernel(x_ref, a_ref, b_ref, m_ref, ga_ref, gb_ref, gm_ref, wa_ref, wb_ref, wm_ref, wo_ref, y_ref):
    u = (ga_ref[...].astype(F32) * _dot(a_ref[...], wa_ref[...])
         + gb_ref[...].astype(F32) * _dot(b_ref[...], wb_ref[...])
         + gm_ref[...].astype(F32) * _dot(m_ref[...], wm_ref[...]))
    y_ref[...] = x_ref[...] + _dot(u.astype(BF16), wo_ref[...])


def _back(x2d, a, b, m, ga, gb, gm, wts, tm):
    n = x2d.shape[0]
    tm = min(tm, n)
    row = lambda w: pl.BlockSpec((tm, w), lambda i: (i, 0))
    return pl.pallas_call(
        _back_kernel,
        grid=(n // tm,),
        in_specs=[row(D_MODEL), row(W_A), row(W_BV), row(W_M), row(D_MODEL), row(D_MODEL), row(D_MODEL),
                  _full((W_A, D_MODEL)), _full((W_BV, D_MODEL)), _full((W_M, D_MODEL)),
                  _full((D_MODEL, D_MODEL))],
        out_specs=row(D_MODEL),
        out_shape=jax.ShapeDtypeStruct((n, D_MODEL), F32),
        compiler_params=_params("parallel"),
        name="back",
    )(x2d, a, b, m, ga, gb, gm, wts["w_up_a"], wts["w_up_b"], wts["w_up_m"], wts["w_out"])


def _bias_kernel(e_ref, tile_ref):
    e = jnp.broadcast_to(e_ref[0], (CHUNK, 2 * BAND))
    tile_ref[0] = pltpu.roll(e, 0, 1, stride=1, stride_axis=0)[:, :BIAS_W]


def _bias_tile(rel_bias):
    j = jnp.arange(2 * BAND)
    dist = jnp.where(j < BIAS_W, BAND - j, REL_CLIP)
    e = rel_bias[:, jnp.clip(dist, -REL_CLIP, REL_CLIP) + REL_CLIP].reshape(H_A, 1, 2 * BAND)
    return pl.pallas_call(
        _bias_kernel,
        grid=(H_A,),
        in_specs=[pl.BlockSpec((1, 1, 2 * BAND), lambda h: (h, 0, 0))],
        out_specs=pl.BlockSpec((1, CHUNK, BIAS_W), lambda h: (h, 0, 0)),
        out_shape=jax.ShapeDtypeStruct((H_A, CHUNK, BIAS_W), F32),
        compiler_params=_params("parallel"),
        name="bias_tile",
    )(e)


def _softmax_pv_staged(scores, values):
    tops = [functools.reduce(jnp.maximum, [jnp.max(s, axis=-1, keepdims=True) for s in s_list])
            for s_list in scores]
    probs = [[jnp.exp(s - m) for s in s_list] for s_list, m in zip(scores, tops)]
    sums = [functools.reduce(jnp.add, [jnp.sum(x, axis=-1, keepdims=True) for x in p_list]) for p_list in probs]
    return [functools.reduce(jnp.add, [_dot(x.astype(BF16), v) for x, v in zip(p_list, v_list)]) / l
            for p_list, v_list, l in zip(probs, values, sums)]


def _head_pairs_attention(q_pairs, k_lists, v_lists, bias_fn):
    m_rows = q_pairs[0].shape[0]
    lane = lax.broadcasted_iota(jnp.int32, (m_rows, LANES), 1)
    first = lane < HEAD_DIM
    scores = []
    for p, qp in enumerate(q_pairs):
        zero = jnp.zeros_like(qp)
        q2 = jnp.concatenate([jnp.where(first, qp, zero), jnp.where(first, zero, qp)], axis=0)
        s_list = []
        for i, k in enumerate(k_lists[p]):
            s = _dot_nt(q2, k)
            b0, b1 = bias_fn(p, 0, i), bias_fn(p, 1, i)
            s_list.append(s if b0 is None else s + jnp.concatenate([b0, b1], axis=0))
        scores.append(s_list)
    outs = _softmax_pv_staged(scores, v_lists)
    return [jnp.where(first, o[:m_rows], o[m_rows:]) for o in outs]


def _attn_prompt_kernel(q_ref, k0_ref, k1_ref, k2_ref, v0_ref, v1_ref, v2_ref, sz_ref, bias_ref, o_ref,
                        k_buf, v_buf):
    i = pl.program_id(1)
    for j, (k_ref, v_ref) in enumerate(((k0_ref, v0_ref), (k1_ref, v1_ref), (k2_ref, v2_ref))):
        k_buf[j * QBLK:(j + 1) * QBLK, :] = k_ref[0]
        v_buf[j * QBLK:(j + 1) * QBLK, :] = v_ref[0]
    col = lax.broadcasted_iota(jnp.int32, (1, WINDOW), 1)
    for c in range(QBLK // CHUNK):
        rows = slice(c * CHUNK, (c + 1) * CHUNK)
        win = slice(c * CHUNK, c * CHUNK + WINDOW)
        before_start = jnp.where(col + c * CHUNK < (BAND // QBLK - i) * QBLK, NEG, 0.0)
        pairs = [slice(p * LANES, (p + 1) * LANES) for p in range(H_A // 2)]
        outs = _head_pairs_attention(
            [q_ref[0, rows, lanes] for lanes in pairs],
            [[k_buf[win, lanes]] for lanes in pairs], [[v_buf[win, lanes]] for lanes in pairs],
            lambda p, e, _: bias_ref[2 * p + e, :, :WINDOW] + before_start)
        for lanes, o in zip(pairs, outs):
            o_ref[0, rows, lanes] = (o * sz_ref[0, rows, lanes].astype(F32)).astype(BF16)


def _attn_prompt(q, k, v, sz, bias):
    b, t, _ = q.shape
    nb = BAND // QBLK
    cur = pl.BlockSpec((1, QBLK, W_A), lambda bi, i: (bi, i, 0))
    prev = lambda d: pl.BlockSpec((1, QBLK, W_A), lambda bi, i: (bi, jnp.maximum(i - d, 0), 0))
    kv_specs = [prev(nb - j) for j in range(nb)] + [cur]
    assert nb == 2
    return pl.pallas_call(
        _attn_prompt_kernel,
        grid=(b, t // QBLK),
        in_specs=[cur] + kv_specs + kv_specs + [cur, _full((H_A, CHUNK, BIAS_W))],
        out_specs=cur,
        out_shape=jax.ShapeDtypeStruct((b, t, W_A), BF16),
        scratch_shapes=[pltpu.VMEM((KSPAN, W_A), BF16), pltpu.VMEM((KSPAN, W_A), BF16)],
        compiler_params=_params("parallel", "parallel"),
        name="attn_prompt",
    )(q, k, k, k, v, v, v, sz, bias)


STEP_STREAMS = 4


def _attn_step_kernel(q_ref, kn_ref, vn_ref, kp_ref, vp_ref, sz_ref, bias_ref, o_ref):
    s_len = q_ref.shape[1]
    pairs = [slice(p * LANES, (p + 1) * LANES) for p in range(H_A // 2)]
    cols = (slice(0, BAND), slice(BAND, BAND + s_len))
    for b in range(STEP_STREAMS):
        outs = _head_pairs_attention(
            [q_ref[b, :, lanes] for lanes in pairs],
            [[kp_ref[b, :, lanes], kn_ref[b, :, lanes]] for lanes in pairs],
            [[vp_ref[b, :, lanes], vn_ref[b, :, lanes]] for lanes in pairs],
            lambda p, e, i: bias_ref[2 * p + e, :s_len, cols[i]])
        for lanes, o in zip(pairs, outs):
            o_ref[b, :, lanes] = (o * sz_ref[b, :, lanes].astype(F32)).astype(BF16)


def _attn_step(q, k_new, v_new, k_past, v_past, sz, bias):
    b, s_len, _ = q.shape
    assert k_past.shape[1:] == (BAND, W_A) and s_len <= CHUNK and b % STEP_STREAMS == 0
    new = pl.BlockSpec((STEP_STREAMS, s_len, W_A), lambda i: (i, 0, 0))
    past = pl.BlockSpec((STEP_STREAMS, BAND, W_A), lambda i: (i, 0, 0))
    return pl.pallas_call(
        _attn_step_kernel,
        grid=(b // STEP_STREAMS,),
        in_specs=[new, new, new, past, past, new, _full((H_A, CHUNK, BIAS_W))],
        out_specs=new,
        out_shape=jax.ShapeDtypeStruct((b, s_len, W_A), BF16),
        compiler_params=_params("parallel"),
        name="attn_step",
    )(q, k_new, v_new, k_past, v_past, sz, bias)


def _mem_kv_kernel(mem_ref, gmem_ref, w_ref, gkm_ref, mk_ref, mv_ref):
    h = _rms_rows(mem_ref[0], gmem_ref[...]).astype(BF16)
    kv = _dot(h, w_ref[...])
    mk_ref[0] = _head_rms(kv[:, :W_M], gkm_ref[...], HEAD_DIM)
    mv_ref[0] = kv[:, W_M:]


def _mem_kv(mem, wts):
    b = mem.shape[0]
    out = pl.BlockSpec((1, N_MEM, W_M), lambda i: (i, 0, 0))
    return pl.pallas_call(
        _mem_kv_kernel,
        grid=(b,),
        in_specs=[pl.BlockSpec((1, N_MEM, D_MODEL), lambda i: (i, 0, 0)), _full((1, D_MODEL)),
                  _full((D_MODEL, 2 * W_M)), _full((1, W_M))],
        out_specs=[out, out],
        out_shape=[jax.ShapeDtypeStruct((b, N_MEM, W_M), F32)] * 2,
        compiler_params=_params("parallel"),
        name="mem_kv",
    )(mem, wts["g_mem"], wts["w_mem_kv"], wts["g_km"])


MEM_ROWS = 128
MEM_STREAMS = 8


def _attn_mem_kernel(q_ref, mk_ref, mv_ref, sz_ref, o_ref):
    nb, tq, _ = q_ref.shape
    step = min(tq, MEM_ROWS)
    pairs = [slice(p * LANES, (p + 1) * LANES) for p in range(H_M // 2)]
    for b in range(nb):
        mk = [[mk_ref[b, :, lanes].astype(BF16)] for lanes in pairs]
        mv = [[mv_ref[b, :, lanes].astype(BF16)] for lanes in pairs]
        for r in range(0, tq, step):
            rows = slice(r, r + step)
            outs = _head_pairs_attention([q_ref[b, rows, lanes] for lanes in pairs], mk, mv,
                                         lambda p, e, i: None)
            for lanes, o in zip(pairs, outs):
                o_ref[b, rows, lanes] = (o * sz_ref[b, rows, lanes].astype(F32)).astype(BF16)


def _attn_mem(q, mk, mv, sz, tq, nb):
    b, t, _ = q.shape
    assert b % nb == 0 and t % tq == 0
    rows = pl.BlockSpec((nb, tq, W_M), lambda bi, i: (bi, i, 0))
    mem = pl.BlockSpec((nb, N_MEM, W_M), lambda bi, i: (bi, 0, 0))
    return pl.pallas_call(
        _attn_mem_kernel,
        grid=(b // nb, t // tq),
        in_specs=[rows, mem, mem, rows],
        out_specs=rows,
        out_shape=jax.ShapeDtypeStruct((b, t, W_M), BF16),
        compiler_params=_params("parallel", "parallel"),
        name="attn_mem",
    )(q, mk, mv, sz)


def _gla_intra(q, k, v, v16, la, group):
    c = q.shape[0]
    r_cc = lax.broadcasted_iota(jnp.int32, (c, c), 0)
    c_cc = lax.broadcasted_iota(jnp.int32, (c, c), 1)
    causal = c_cc <= r_cc
    if group < c:
        causal = causal & (r_cc // group == c_cc // group)
    cum = _dot_exact_lhs(causal.astype(BF16), la)
    cumx = cum - la
    r_ck = lax.broadcasted_iota(jnp.int32, (c, W_BK), 0)

    def back(x, j):
        n, w = x.shape
        return pltpu.roll(x.reshape(n // GLA_BASE, GLA_BASE, w), j, 1).reshape(n, w)

    r_in = r_ck % GLA_BASE
    prods = [q * k]
    for j in range(1, GLA_BASE):
        decay = jnp.exp(jnp.where(r_in >= j, cum - back(cum, j), NEG))
        prods.append(q * back(k, j) * decay)
    head_of_k = lax.broadcasted_iota(jnp.int32, (W_BK, W_BV), 0) // DK_B
    head_of_v = lax.broadcasted_iota(jnp.int32, (W_BK, W_BV), 1) // DV_B
    spread = (head_of_k == head_of_v).astype(BF16)
    w = _dot(jnp.concatenate(prods, axis=0).astype(BF16), spread)
    o = w[:c] * v
    for j in range(1, GLA_BASE):
        o = o + w[j * c:(j + 1) * c] * back(v, j)

    att = [jnp.zeros((c, c), F32) for _ in range(H_B)]
    half = GLA_BASE
    while half < group:
        nb = c // half
        cum3 = cum.reshape(nb, half, W_BK)
        q_exp = (cum3 - cumx.reshape(nb, half, W_BK)[:, 0:1, :]).reshape(c, W_BK)
        k_exp = (cum3[:, half - 1:half, :] - cum3).reshape(c, W_BK)
        odd = (r_ck // half) % 2 == 1
        qt = jnp.where(odd, q * jnp.exp(q_exp), 0.0).astype(BF16)
        kt = jnp.where(odd, 0.0, k * jnp.exp(k_exp)).astype(BF16)
        same = (r_cc // (2 * half)) == (c_cc // (2 * half))
        for h in range(H_B):
            sl = slice(h * DK_B, (h + 1) * DK_B)
            a = _dot_nt(qt[:, sl], kt[:, sl])
            att[h] = att[h] + (a if 2 * half == c else jnp.where(same, a, 0.0))
        half *= 2
    if group > GLA_BASE:
        o = o + jnp.concatenate(
            [_dot(att[h].astype(BF16), v16[:, h * DV_B:(h + 1) * DV_B]) for h in range(H_B)], axis=-1)
    return o, cum


def _gla_finish(o, g_ref, sz):
    outs = [_rms_rows(o[:, h * DV_B:(h + 1) * DV_B], g_ref[:, h * DV_B:(h + 1) * DV_B]) for h in range(H_B)]
    return (jnp.concatenate(outs, axis=-1) * sz.astype(F32)).astype(BF16)


def _gla_kernel(q_ref, k_ref, v_ref, la_ref, sz_ref, g_ref, s0_ref, o_ref, sfin_ref, s_scr):
    t_idx = pl.program_id(1)
    c = GLA_C

    @pl.when(t_idx == 0)
    def _():
        s_scr[...] = s0_ref[0].reshape(W_BK, DV_B)

    q, k, la = q_ref[0].astype(F32), k_ref[0].astype(F32), la_ref[0]
    v16 = v_ref[0]
    o, cum = _gla_intra(q, k, v16.astype(F32), v16, la, c)

    q_in = (q * jnp.exp(cum)).astype(BF16)
    k_out_t = (k * jnp.exp(cum[c - 1:c, :] - cum)).T.astype(BF16)
    keep = jnp.exp(cum.T[:, c - 1:c])
    s_old = s_scr[...]
    s_new = []
    inter = []
    for h in range(H_B):
        ks = slice(h * DK_B, (h + 1) * DK_B)
        vs = slice(h * DV_B, (h + 1) * DV_B)
        inter.append(_dot(q_in[:, ks], s_old[ks].astype(BF16)))
        s_new.append(s_old[ks] * keep[ks] + _dot(k_out_t[ks], v16[:, vs]))
    s_new = jnp.concatenate(s_new, axis=0)
    s_scr[...] = s_new
    o_ref[0] = _gla_finish(o + jnp.concatenate(inter, axis=-1), g_ref, sz_ref[0])

    @pl.when(t_idx == pl.num_programs(1) - 1)
    def _():
        sfin_ref[0] = s_new.reshape(H_B, DK_B, DV_B)


def _gla(q, k, v, la, sz, g, s0):
    b, t, _ = q.shape
    assert t % GLA_C == 0
    blk = lambda w: pl.BlockSpec((1, GLA_C, w), lambda bi, i: (bi, i, 0))
    state = pl.BlockSpec((1, H_B, DK_B, DV_B), lambda bi, i: (bi, 0, 0, 0))
    return pl.pallas_call(
        _gla_kernel,
        grid=(b, t // GLA_C),
        in_specs=[blk(W_BK), blk(W_BK), blk(W_BV), blk(W_BK), blk(W_BV),
                  pl.BlockSpec((1, W_BV), lambda bi, i: (0, 0)), state],
        out_specs=[blk(W_BV), state],
        out_shape=[jax.ShapeDtypeStruct((b, t, W_BV), BF16),
                   jax.ShapeDtypeStruct((b, H_B, DK_B, DV_B), F32)],
        scratch_shapes=[pltpu.VMEM((W_BK, DV_B), F32)],
        compiler_params=_params("parallel", "arbitrary"),
        name="gla",
    )(q, k, v, la, sz, g, s0)


def _gla_step_kernel(q_ref, k_ref, v_ref, la_ref, sz_ref, g_ref, s0_ref, o_ref, s_ref, *, t):
    c = GLA_C
    n = c // t
    q, k, la = q_ref[...].astype(F32), k_ref[...].astype(F32), la_ref[...]
    v16 = v_ref[...]
    o, cum = _gla_intra(q, k, v16.astype(F32), v16, la, t)

    q_in = (q * jnp.exp(cum)).astype(BF16)
    cum_t, k_t = cum.T, k.T
    stream_of_col = lax.broadcasted_iota(jnp.int32, (W_BK, c), 1) // t
    inter = []
    for s in range(n):
        rows = slice(s * t, (s + 1) * t)
        last = cum_t[:, (s + 1) * t - 1:(s + 1) * t]
        k_out_t = (k_t * jnp.exp(jnp.where(stream_of_col == s, last - cum_t, NEG))).astype(BF16)
        keep = jnp.exp(last)
        o_s = []
        for h in range(H_B):
            ks = slice(h * DK_B, (h + 1) * DK_B)
            vs = slice(h * DV_B, (h + 1) * DV_B)
            s_old = s0_ref[s, h]
            o_s.append(_dot(q_in[rows, ks], s_old.astype(BF16)))
            s_ref[s, h] = s_old * keep[ks] + _dot(k_out_t[ks], v16[:, vs])
        inter.append(jnp.concatenate(o_s, axis=-1))
    o_ref[...] = _gla_finish(o + jnp.concatenate(inter, axis=0), g_ref, sz_ref[...])


def _gla_step(q, k, v, la, sz, g, s0):
    b, t, _ = q.shape
    n = GLA_C // t
    assert GLA_C % t == 0 and t % (2 * GLA_BASE) == 0 and b % n == 0
    flat = lambda z: z.reshape(b * t, z.shape[-1])
    blk = lambda w: pl.BlockSpec((GLA_C, w), lambda i: (i, 0))
    state = pl.BlockSpec((n, H_B, DK_B, DV_B), lambda i: (i, 0, 0, 0))
    o, s_new = pl.pallas_call(
        functools.partial(_gla_step_kernel, t=t),
        grid=(b // n,),
        in_specs=[blk(W_BK), blk(W_BK), blk(W_BV), blk(W_BK), blk(W_BV),
                  pl.BlockSpec((1, W_BV), lambda i: (0, 0)), state],
        out_specs=[blk(W_BV), state],
        out_shape=[jax.ShapeDtypeStruct((b * t, W_BV), BF16),
                   jax.ShapeDtypeStruct((b, H_B, DK_B, DV_B), F32)],
        compiler_params=_params("parallel"),
        name="gla_step",
    )(flat(q), flat(k), flat(v), flat(la), flat(sz), g, s0)
    return o.reshape(b, t, W_BV), s_new


REPACK_ROWS = 128
_GLR_OFF = int(sum(IN_SIZES[:7]))


def _repack_kernel(w_ref, main_ref, glr_ref):
    main_ref[:, :_GLR_OFF] = w_ref[:, :_GLR_OFF].astype(BF16)
    main_ref[:, _GLR_OFF:] = w_ref[:, _GLR_OFF + GATE_RANK:].astype(BF16)
    pad = jnp.zeros((REPACK_ROWS, LANES - GATE_RANK), BF16)
    glr_ref[...] = jnp.concatenate([w_ref[:, _GLR_OFF:_GLR_OFF + GATE_RANK].astype(BF16), pad], axis=1)


def _repack_w_in(w):
    d_in = w.shape[1]
    return pl.pallas_call(
        _repack_kernel,
        grid=(D_MODEL // REPACK_ROWS,),
        in_specs=[pl.BlockSpec((REPACK_ROWS, d_in), lambda i: (i, 0))],
        out_specs=[pl.BlockSpec((REPACK_ROWS, W_MAIN), lambda i: (i, 0)),
                   pl.BlockSpec((REPACK_ROWS, LANES), lambda i: (i, 0))],
        out_shape=[jax.ShapeDtypeStruct((D_MODEL, W_MAIN), BF16), jax.ShapeDtypeStruct((D_MODEL, LANES), BF16)],
        compiler_params=_params("parallel"),
        name="repack_w_in",
    )(w)


def _prep_weights(l, norm_in, w_in, g_qa, g_ka, w_gate2, b_gate, g_gla_out, g_mem, w_mem_kv, g_qm, g_km,
                  w_up_a, w_up_b, w_up_m, w_out):
    main, glr = _repack_w_in(w_in[l])
    tile = lambda gain, n: jnp.tile(gain, n).reshape(1, -1)
    return {
        "norm_in": norm_in[l].reshape(1, D_MODEL),
        "w_main": main,
        "w_glr": glr,
        "w_gate2": jnp.pad(w_gate2[l], ((0, LANES - GATE_RANK), (0, 0))).astype(BF16),
        "b_gate": b_gate[l].reshape(1, W_BK),
        "g_qa": tile(g_qa[l], H_A), "g_ka": tile(g_ka[l], H_A), "g_qm": tile(g_qm[l], H_M),
        "g_km": tile(g_km[l], H_M), "g_gla": tile(g_gla_out[l], H_B),
        "g_mem": g_mem[l].reshape(1, D_MODEL),
        "w_mem_kv": w_mem_kv[l].astype(BF16),
        "w_up_a": w_up_a[l].astype(BF16), "w_up_b": w_up_b[l].astype(BF16),
        "w_up_m": w_up_m[l].astype(BF16), "w_out": w_out[l].astype(BF16),
    }


def _layer(x, wts, attend_a, mem_k, mem_v, s0, tm, tq, nb):
    b, t, _ = x.shape
    keep = min(BAND, t)
    x2d = x.reshape(b * t, D_MODEL)
    (qa, ka, va, ka16, va16, sza, qb, kb, vb, la, szb, qm, szm, ga, gb, gm) = _front(x2d, wts, tm, t, keep)
    r3 = lambda z: z.reshape(b, t, z.shape[-1])
    out_a = attend_a(r3(qa), r3(ka16), r3(va16), r3(sza))
    gla = _gla if t >= GLA_C else _gla_step
    out_b, s_new = gla(r3(qb), r3(kb), r3(vb), r3(la), r3(szb), wts["g_gla"], s0)
    out_m = _attn_mem(r3(qm), mem_k, mem_v, r3(szm), tq, nb)
    flat = lambda z: z.reshape(b * t, z.shape[-1])
    y = _back(x2d, flat(out_a), flat(out_b), flat(out_m), ga, gb, gm, wts, tm)
    heads = lambda z: z.reshape(b, keep, H_A, HEAD_DIM)
    return y.reshape(b, t, D_MODEL), heads(ka), heads(va), s_new


def kernel(x_prompt, x_sample, mem_prompt, cache_a_k, cache_a_v, state_gla, cache_mem_k, cache_mem_v,
           norm_in, w_in, g_qa, g_ka, rel_bias, w_gate2, b_gate, g_gla_out, g_mem, w_mem_kv, g_qm, g_km,
           w_up_a, w_up_b, w_up_m, w_out):
    depth = w_in.shape[0]
    xp, xs = x_prompt, x_sample
    bp, tp, _ = xp.shape
    bs, ts, _ = xs.shape
    akp, avp, sgp, mkp, mvp, aks, avs, sgs = [], [], [], [], [], [], [], []
    for l in range(depth):
        wts = _prep_weights(l, norm_in, w_in, g_qa, g_ka, w_gate2, b_gate, g_gla_out, g_mem, w_mem_kv,
                            g_qm, g_km, w_up_a, w_up_b, w_up_m, w_out)
        bias = _bias_tile(rel_bias[l])
        mk, mv = _mem_kv(mem_prompt, wts)
        attend_p = lambda q, k, v, sz: _attn_prompt(q, k, v, sz, bias)
        s0 = jnp.zeros((bp, H_B, DK_B, DV_B), state_gla.dtype)
        xp, ka, va, sp = _layer(xp, wts, attend_p, mk, mv, s0, 256, 512, 1)
        akp.append(ka)
        avp.append(va)
        sgp.append(sp)
        mkp.append(mk.reshape(bp, N_MEM, H_M, HEAD_DIM))
        mvp.append(mv.reshape(bp, N_MEM, H_M, HEAD_DIM))
        past_k = cache_a_k[l].reshape(bs, -1, W_A).astype(BF16)
        past_v = cache_a_v[l].reshape(bs, -1, W_A).astype(BF16)
        attend_s = lambda q, k, v, sz: _attn_step(q, k, v, past_k, past_v, sz, bias)
        xs, ka_s, va_s, ss = _layer(xs, wts, attend_s, cache_mem_k[l].reshape(bs, N_MEM, W_M).astype(BF16),
                                    cache_mem_v[l].reshape(bs, N_MEM, W_M).astype(BF16), state_gla[l],
                                    256, ts, MEM_STREAMS)
        aks.append(ka_s)
        avs.append(va_s)
        sgs.append(ss)
    return (xp, xs, jnp.stack(akp), jnp.stack(avp), jnp.stack(sgp), jnp.stack(mkp), jnp.stack(mvp),
            jnp.stack(aks), jnp.stack(avs), jnp.stack(sgs))
```

```python
import functools

import jax
import jax.numpy as jnp
from jax import lax
from jax.experimental import pallas as pl
from jax.experimental.pallas import tpu as pltpu

F32 = jnp.float32
BF16 = jnp.bfloat16

D_MODEL = 1024
CHUNK = 64
LEFT_CHUNKS = 8
HEAD_DIM = 64
H_A = 8
W_A = H_A * HEAD_DIM
REL_CLIP = 128
H_B = 4
DK_B = 64
DV_B = 128
W_BK = H_B * DK_B
W_BV = H_B * DV_B
GATE_RANK = 16
GATE_TAU = 16.0
N_MEM = 256
H_M = 4
W_M = H_M * HEAD_DIM
EPS = 1e-6
IN_SIZES = (W_A, W_A, W_A, W_A, W_BK, W_BK, W_BV, GATE_RANK, W_BV, W_M, W_M, D_MODEL, D_MODEL, D_MODEL)

LANES = 128
SUBLANES = 8
VMEM_LIMIT = 56 * 1024 * 1024
NEG = -1e30

BAND = LEFT_CHUNKS * CHUNK
QBLK = 256
KSPAN = BAND + QBLK
WINDOW = BAND + CHUNK
BIAS_W = BAND + 2 * CHUNK
GLA_C = 128
GLA_BASE = SUBLANES


def _dot(a, b):
    return jnp.dot(a, b, preferred_element_type=F32)


def _dot_nt(a, b):
    return lax.dot_general(a, b, (((1,), (1,)), ((), ())), preferred_element_type=F32)


def _split3(x):
    hi = x.astype(BF16)
    r = x - hi.astype(F32)
    mid = r.astype(BF16)
    lo = (r - mid.astype(F32)).astype(BF16)
    return hi, mid, lo


def _dot_exact_lhs(a01, x):
    hi, mid, lo = _split3(x)
    return _dot(a01, hi) + _dot(a01, mid) + _dot(a01, lo)


def _sigmoid(z):
    return 1.0 / (1.0 + jnp.exp(-z))


def _silu(z):
    return z * _sigmoid(z)


def _log_sigmoid(z):
    return jnp.minimum(z, 0.0) - jnp.log1p(jnp.exp(-jnp.abs(z)))


def _rms_rows(x, g):
    ms = jnp.mean(x * x, axis=-1, keepdims=True)
    return x * lax.rsqrt(ms + EPS) * g


def _head_mean_matrix(width, head):
    r = lax.broadcasted_iota(jnp.int32, (width, width), 0) // head
    c = lax.broadcasted_iota(jnp.int32, (width, width), 1) // head
    return jnp.where(r == c, 1.0 / head, 0.0).astype(BF16)


def _head_rms(x, g, head):
    x2 = x * x
    hi = x2.astype(BF16)
    lo = (x2 - hi.astype(F32)).astype(BF16)
    avg = _head_mean_matrix(x.shape[-1], head)
    ms = _dot(hi, avg) + _dot(lo, avg)
    return x * lax.rsqrt(ms + EPS) * g


def _full(shape):
    nd = len(shape)
    return pl.BlockSpec(shape, lambda *_: (0,) * nd, pipeline_mode=pl.Buffered(1))


def _params(*sem):
    return pltpu.CompilerParams(dimension_semantics=sem, vmem_limit_bytes=VMEM_LIMIT)


_IN_OFF = tuple(int(sum(IN_SIZES[:i])) for i in range(len(IN_SIZES) + 1))
D_IN = _IN_OFF[-1]


def _front_kernel(x_ref, nin_ref, w_ref, wg2_ref, bg_ref, gqa_ref, gka_ref, gqm_ref,
                  qa_o, ka_o, va_o, ka16_o, va16_o, sza_o, qb_o, kb_o, vb_o, la_o, szb_o, qm_o, szm_o,
                  ga_o, gb_o, gm_o):
    h = _rms_rows(x_ref[...], nin_ref[...]).astype(BF16)

    def proj(i):
        return _dot_nt(h, w_ref[_IN_OFF[i]:_IN_OFF[i + 1], :])

    scale = HEAD_DIM ** -0.5
    qa_o[...] = (_head_rms(proj(0), gqa_ref[...], HEAD_DIM) * scale).astype(BF16)
    ka = _head_rms(proj(1), gka_ref[...], HEAD_DIM)
    va = proj(2)
    ka_o[...] = ka
    va_o[...] = va
    ka16_o[...] = ka.astype(BF16)
    va16_o[...] = va.astype(BF16)
    sza_o[...] = _silu(proj(3)).astype(BF16)
    qb_o[...] = (proj(4) * (DK_B ** -0.5)).astype(BF16)
    kb_o[...] = proj(5).astype(BF16)
    vb_o[...] = proj(6).astype(BF16)
    z = _dot(proj(7).astype(BF16), wg2_ref[...]) + bg_ref[...]
    la_o[...] = _log_sigmoid(z) * (1.0 / GATE_TAU)
    szb_o[...] = _silu(proj(8)).astype(BF16)
    qm_o[...] = (_head_rms(proj(9), gqm_ref[...], HEAD_DIM) * scale).astype(BF16)
    szm_o[...] = _silu(proj(10)).astype(BF16)
    ga_o[...] = _sigmoid(proj(11)).astype(BF16)
    gb_o[...] = _sigmoid(proj(12)).astype(BF16)
    gm_o[...] = _sigmoid(proj(13)).astype(BF16)


def _front(x2d, wts, tm, t, keep):
    n = x2d.shape[0]
    tm = min(tm, n)
    row = lambda w: pl.BlockSpec((tm, w), lambda i: (i, 0))
    if keep == t:
        kv_rows, kv = n, row(W_A)
    else:
        assert t % tm == 0 and keep % tm == 0
        per_stream, kept = t // tm, keep // tm
        kv_rows = (n // t) * keep
        kv = pl.BlockSpec((tm, W_A), lambda i: (
            (i // per_stream) * kept + jnp.maximum(i % per_stream - (per_stream - kept), 0), 0))
    widths = (W_A, W_A, W_A, W_A, W_A, W_A, W_BK, W_BK, W_BV, W_BK, W_BV, W_M, W_M, D_MODEL, D_MODEL, D_MODEL)
    dtypes = (BF16, F32, F32, BF16, BF16, BF16, BF16, BF16, BF16, F32, BF16, BF16, BF16, BF16, BF16, BF16)
    rows = (n, kv_rows, kv_rows) + (n,) * 13
    specs = [row(W_A), kv, kv] + [row(w) for w in widths[3:]]
    return pl.pallas_call(
        _front_kernel,
        grid=(n // tm,),
        in_specs=[row(D_MODEL), _full((1, D_MODEL)), _full((D_IN, D_MODEL)),
                  _full((GATE_RANK, W_BK)), _full((1, W_BK)), _full((1, W_A)), _full((1, W_A)), _full((1, W_M))],
        out_specs=specs,
        out_shape=[jax.ShapeDtypeStruct((r, w), d) for r, w, d in zip(rows, widths, dtypes)],
        compiler_params=_params("arbitrary"),
        name="front",
    )(x2d, wts["norm_in"], wts["w_in_t"], wts["w_gate2"], wts["b_gate"],
      wts["g_qa"], wts["g_ka"], wts["g_qm"])


def _back_kernel(x_ref, a_ref, b_ref, m_ref, ga_ref, gb_ref, gm_ref, wa_ref, wb_ref, wm_ref, wo_ref, y_ref):
    u = (ga_ref[...].astype(F32) * _dot(a_ref[...], wa_ref[...])
         + gb_ref[...].astype(F32) * _dot(b_ref[...], wb_ref[...])
         + gm_ref[...].astype(F32) * _dot(m_ref[...], wm_ref[...]))
    y_ref[...] = x_ref[...] + _dot(u.astype(BF16), wo_ref[...])


def _back(x2d, a, b, m, ga, gb, gm, wts, tm):
    n = x2d.shape[0]
    tm = min(tm, n)
    row = lambda w: pl.BlockSpec((tm, w), lambda i: (i, 0))
    return pl.pallas_call(
        _back_kernel,
        grid=(n // tm,),
        in_specs=[row(D_MODEL), row(W_A), row(W_BV), row(W_M), row(D_MODEL), row(D_MODEL), row(D_MODEL),
                  _full((W_A, D_MODEL)), _full((W_BV, D_MODEL)), _full((W_M, D_MODEL)),
                  _full((D_MODEL, D_MODEL))],
        out_specs=row(D_MODEL),
        out_shape=jax.ShapeDtypeStruct((n, D_MODEL), F32),
        compiler_params=_params("parallel"),
        name="back",
    )(x2d, a, b, m, ga, gb, gm, wts["w_up_a"], wts["w_up_b"], wts["w_up_m"], wts["w_out"])


def _bias_kernel(e_ref, tile_ref):
    e = jnp.broadcast_to(e_ref[0], (CHUNK, 2 * BAND))
    tile_ref[0] = pltpu.roll(e, 0, 1, stride=1, stride_axis=0)[:, :BIAS_W]


def _bias_tile(rel_bias):
    j = jnp.arange(2 * BAND)
    dist = jnp.where(j < BIAS_W, BAND - j, REL_CLIP)
    e = rel_bias[:, jnp.clip(dist, -REL_CLIP, REL_CLIP) + REL_CLIP].reshape(H_A, 1, 2 * BAND)
    return pl.pallas_call(
        _bias_kernel,
        grid=(H_A,),
        in_specs=[pl.BlockSpec((1, 1, 2 * BAND), lambda h: (h, 0, 0))],
        out_specs=pl.BlockSpec((1, CHUNK, BIAS_W), lambda h: (h, 0, 0)),
        out_shape=jax.ShapeDtypeStruct((H_A, CHUNK, BIAS_W), F32),
        compiler_params=_params("parallel"),
        name="bias_tile",
    )(e)


def _softmax_pv_staged(scores, values):
    tops = [functools.reduce(jnp.maximum, [jnp.max(s, axis=-1, keepdims=True) for s in s_list])
            for s_list in scores]
    probs = [[jnp.exp(s - m) for s in s_list] for s_list, m in zip(scores, tops)]
    sums = [functools.reduce(jnp.add, [jnp.sum(x, axis=-1, keepdims=True) for x in p_list]) for p_list in probs]
    return [functools.reduce(jnp.add, [(v(x.astype(BF16)) if callable(v) else _dot(x.astype(BF16), v))
                                       for x, v in zip(p_list, v_list)]) / l
            for p_list, v_list, l in zip(probs, values, sums)]


def _head_pairs_attention(q_pairs, k_lists, v_lists, bias_fn):
    m_rows = q_pairs[0].shape[0]
    lane = lax.broadcasted_iota(jnp.int32, (m_rows, LANES), 1)
    first = lane < HEAD_DIM
    scores = []
    for p, qp in enumerate(q_pairs):
        zero = jnp.zeros_like(qp)
        q2 = jnp.concatenate([jnp.where(first, qp, zero), jnp.where(first, zero, qp)], axis=0)
        s_list = []
        for i, k in enumerate(k_lists[p]):
            s = _dot_nt(q2, k)
            b0, b1 = bias_fn(p, 0, i), bias_fn(p, 1, i)
            s_list.append(s if b0 is None else s + jnp.concatenate([b0, b1], axis=0))
        scores.append(s_list)
    outs = _softmax_pv_staged(scores, v_lists)
    return [jnp.where(first, o[:m_rows], o[m_rows:]) for o in outs]


def _attn_prompt_kernel(q_ref, k0_ref, k1_ref, k2_ref, v0_ref, v1_ref, v2_ref, sz_ref, bias_ref, o_ref,
                        k_buf, v_buf):
    i = pl.program_id(1)
    for j, (k_ref, v_ref) in enumerate(((k0_ref, v0_ref), (k1_ref, v1_ref), (k2_ref, v2_ref))):
        k_buf[j * QBLK:(j + 1) * QBLK, :] = k_ref[0]
        v_buf[j * QBLK:(j + 1) * QBLK, :] = v_ref[0]
    col = lax.broadcasted_iota(jnp.int32, (1, WINDOW), 1)
    for c in range(QBLK // CHUNK):
        rows = slice(c * CHUNK, (c + 1) * CHUNK)
        win = slice(c * CHUNK, c * CHUNK + WINDOW)
        before_start = jnp.where(col + c * CHUNK < (BAND // QBLK - i) * QBLK, NEG, 0.0)
        pairs = [slice(p * LANES, (p + 1) * LANES) for p in range(H_A // 2)]
        outs = _head_pairs_attention(
            [q_ref[0, rows, lanes] for lanes in pairs],
            [[k_buf[win, lanes]] for lanes in pairs], [[v_buf[win, lanes]] for lanes in pairs],
            lambda p, e, _: bias_ref[2 * p + e, :, :WINDOW] + before_start)
        for lanes, o in zip(pairs, outs):
            o_ref[0, rows, lanes] = (o * sz_ref[0, rows, lanes].astype(F32)).astype(BF16)


def _attn_prompt(q, k, v, sz, bias):
    b, t, _ = q.shape
    nb = BAND // QBLK
    cur = pl.BlockSpec((1, QBLK, W_A), lambda bi, i: (bi, i, 0))
    prev = lambda d: pl.BlockSpec((1, QBLK, W_A), lambda bi, i: (bi, jnp.maximum(i - d, 0), 0))
    kv_specs = [prev(nb - j) for j in range(nb)] + [cur]
    assert nb == 2
    return pl.pallas_call(
        _attn_prompt_kernel,
        grid=(b, t // QBLK),
        in_specs=[cur] + kv_specs + kv_specs + [cur, _full((H_A, CHUNK, BIAS_W))],
        out_specs=cur,
        out_shape=jax.ShapeDtypeStruct((b, t, W_A), BF16),
        scratch_shapes=[pltpu.VMEM((KSPAN, W_A), BF16), pltpu.VMEM((KSPAN, W_A), BF16)],
        compiler_params=_params("parallel", "parallel"),
        name="attn_prompt",
    )(q, k, k, k, v, v, v, sz, bias)


STEP_STREAMS = 4


def _attn_step_kernel(q_ref, kn_ref, vn_ref, kp_ref, vp_ref, sz_ref, bias_ref, o_ref):
    s_len = q_ref.shape[1]
    for b in range(STEP_STREAMS):
        scores, values = [], []
        for h in range(H_A):
            lanes = slice(h * HEAD_DIM, (h + 1) * HEAD_DIM)
            q = q_ref[b, :, lanes]
            scores.append([_dot(q, kp_ref[b, h].astype(BF16)) + bias_ref[h, :s_len, :BAND],
                           _dot_nt(q, kn_ref[b, :, lanes]) + bias_ref[h, :s_len, BAND:BAND + s_len]])
            values.append([functools.partial(_dot_nt, b=vp_ref[b, h].astype(BF16)), vn_ref[b, :, lanes]])
        o = jnp.concatenate(_softmax_pv_staged(scores, values), axis=-1)
        o_ref[b] = (o * sz_ref[b].astype(F32)).astype(BF16)


def _attn_step(q, k_new, v_new, k_past_t, v_past_t, sz, bias):
    b, s_len, _ = q.shape
    assert k_past_t.shape[1:] == (H_A, HEAD_DIM, BAND) and s_len <= CHUNK and b % STEP_STREAMS == 0
    new = pl.BlockSpec((STEP_STREAMS, s_len, W_A), lambda i: (i, 0, 0))
    past = pl.BlockSpec((STEP_STREAMS, H_A, HEAD_DIM, BAND), lambda i: (i, 0, 0, 0))
    return pl.pallas_call(
        _attn_step_kernel,
        grid=(b // STEP_STREAMS,),
        in_specs=[new, new, new, past, past, new, _full((H_A, CHUNK, BIAS_W))],
        out_specs=new,
        out_shape=jax.ShapeDtypeStruct((b, s_len, W_A), BF16),
        compiler_params=_params("parallel"),
        name="attn_step",
    )(q, k_new, v_new, k_past_t, v_past_t, sz, bias)


def _mem_kv_kernel(mem_ref, gmem_ref, w_ref, gkm_ref, mk_ref, mv_ref):
    h = _rms_rows(mem_ref[0], gmem_ref[...]).astype(BF16)
    kv = _dot(h, w_ref[...])
    mk_ref[0] = _head_rms(kv[:, :W_M], gkm_ref[...], HEAD_DIM)
    mv_ref[0] = kv[:, W_M:]


def _mem_kv(mem, wts):
    b = mem.shape[0]
    out = pl.BlockSpec((1, N_MEM, W_M), lambda i: (i, 0, 0))
    return pl.pallas_call(
        _mem_kv_kernel,
        grid=(b,),
        in_specs=[pl.BlockSpec((1, N_MEM, D_MODEL), lambda i: (i, 0, 0)), _full((1, D_MODEL)),
                  _full((D_MODEL, 2 * W_M)), _full((1, W_M))],
        out_specs=[out, out],
        out_shape=[jax.ShapeDtypeStruct((b, N_MEM, W_M), F32)] * 2,
        compiler_params=_params("parallel"),
        name="mem_kv",
    )(mem, wts["g_mem"], wts["w_mem_kv"], wts["g_km"])


MEM_ROWS = 128
MEM_STREAMS = 8


def _attn_mem_kernel(q_ref, mk_ref, mv_ref, sz_ref, o_ref):
    nb, tq, _ = q_ref.shape
    step = min(tq, MEM_ROWS)
    pairs = [slice(p * LANES, (p + 1) * LANES) for p in range(H_M // 2)]
    for b in range(nb):
        mk = [[mk_ref[b, :, lanes].astype(BF16)] for lanes in pairs]
        mv = [[mv_ref[b, :, lanes].astype(BF16)] for lanes in pairs]
        for r in range(0, tq, step):
            rows = slice(r, r + step)
            outs = _head_pairs_attention([q_ref[b, rows, lanes] for lanes in pairs], mk, mv,
                                         lambda p, e, i: None)
            for lanes, o in zip(pairs, outs):
                o_ref[b, rows, lanes] = (o * sz_ref[b, rows, lanes].astype(F32)).astype(BF16)


def _attn_mem(q, mk, mv, sz, tq, nb):
    b, t, _ = q.shape
    assert b % nb == 0 and t % tq == 0
    rows = pl.BlockSpec((nb, tq, W_M), lambda bi, i: (bi, i, 0))
    mem = pl.BlockSpec((nb, N_MEM, W_M), lambda bi, i: (bi, 0, 0))
    return pl.pallas_call(
        _attn_mem_kernel,
        grid=(b // nb, t // tq),
        in_specs=[rows, mem, mem, rows],
        out_specs=rows,
        out_shape=jax.ShapeDtypeStruct((b, t, W_M), BF16),
        compiler_params=_params("parallel", "parallel"),
        name="attn_mem",
    )(q, mk, mv, sz)


def _attn_mem_step_kernel(q_ref, mk_ref, mv_ref, sz_ref, o_ref):
    for b in range(q_ref.shape[0]):
        scores, values = [], []
        for h in range(H_M):
            lanes = slice(h * HEAD_DIM, (h + 1) * HEAD_DIM)
            scores.append([_dot(q_ref[b, :, lanes], mk_ref[b, h].astype(BF16))])
            values.append([functools.partial(_dot_nt, b=mv_ref[b, h].astype(BF16))])
        o = jnp.concatenate(_softmax_pv_staged(scores, values), axis=-1)
        o_ref[b] = (o * sz_ref[b].astype(F32)).astype(BF16)


def _attn_mem_step(q, mk_t, mv_t, sz):
    b, t, _ = q.shape
    nb = min(b, MEM_STREAMS)
    assert b % nb == 0 and mk_t.shape[1:] == (H_M, HEAD_DIM, N_MEM)
    rows = pl.BlockSpec((nb, t, W_M), lambda i: (i, 0, 0))
    mem = pl.BlockSpec((nb, H_M, HEAD_DIM, N_MEM), lambda i: (i, 0, 0, 0))
    return pl.pallas_call(
        _attn_mem_step_kernel,
        grid=(b // nb,),
        in_specs=[rows, mem, mem, rows],
        out_specs=rows,
        out_shape=jax.ShapeDtypeStruct((b, t, W_M), BF16),
        compiler_params=_params("parallel"),
        name="attn_mem_step",
    )(q, mk_t, mv_t, sz)


def _gla_intra(q, k, v, v16, la, group):
    c = q.shape[0]
    r_cc = lax.broadcasted_iota(jnp.int32, (c, c), 0)
    c_cc = lax.broadcasted_iota(jnp.int32, (c, c), 1)
    causal = c_cc <= r_cc
    if group < c:
        causal = causal & (r_cc // group == c_cc // group)
    cum = _dot_exact_lhs(causal.astype(BF16), la)
    cumx = cum - la
    r_ck = lax.broadcasted_iota(jnp.int32, (c, W_BK), 0)

    def back(x, j):
        n, w = x.shape
        return pltpu.roll(x.reshape(n // GLA_BASE, GLA_BASE, w), j, 1).reshape(n, w)

    r_in = r_ck % GLA_BASE
    prods = [q * k]
    for j in range(1, GLA_BASE):
        decay = jnp.exp(jnp.where(r_in >= j, cum - back(cum, j), NEG))
        prods.append(q * back(k, j) * decay)
    head_of_k = lax.broadcasted_iota(jnp.int32, (W_BK, W_BV), 0) // DK_B
    head_of_v = lax.broadcasted_iota(jnp.int32, (W_BK, W_BV), 1) // DV_B
    spread = (head_of_k == head_of_v).astype(BF16)
    w = _dot(jnp.concatenate(prods, axis=0).astype(BF16), spread)
    o = w[:c] * v
    for j in range(1, GLA_BASE):
        o = o + w[j * c:(j + 1) * c] * back(v, j)

    att = [jnp.zeros((c, c), F32) for _ in range(H_B)]
    half = GLA_BASE
    while half < group:
        nb = c // half
        cum3 = cum.reshape(nb, half, W_BK)
        q_exp = (cum3 - cumx.reshape(nb, half, W_BK)[:, 0:1, :]).reshape(c, W_BK)
        k_exp = (cum3[:, half - 1:half, :] - cum3).reshape(c, W_BK)
        odd = (r_ck // half) % 2 == 1
        qt = jnp.where(odd, q * jnp.exp(q_exp), 0.0).astype(BF16)
        kt = jnp.where(odd, 0.0, k * jnp.exp(k_exp)).astype(BF16)
        same = (r_cc // (2 * half)) == (c_cc // (2 * half))
        for h in range(H_B):
            sl = slice(h * DK_B, (h + 1) * DK_B)
            a = _dot_nt(qt[:, sl], kt[:, sl])
            att[h] = att[h] + (a if 2 * half == c else jnp.where(same, a, 0.0))
        half *= 2
    if group > GLA_BASE:
        o = o + jnp.concatenate(
            [_dot(att[h].astype(BF16), v16[:, h * DV_B:(h + 1) * DV_B]) for h in range(H_B)], axis=-1)
    return o, cum


def _gla_finish(o, g_ref, sz):
    outs = [_rms_rows(o[:, h * DV_B:(h + 1) * DV_B], g_ref[:, h * DV_B:(h + 1) * DV_B]) for h in range(H_B)]
    return (jnp.concatenate(outs, axis=-1) * sz.astype(F32)).astype(BF16)


def _gla_kernel(q_ref, k_ref, v_ref, la_ref, sz_ref, g_ref, s0_ref, o_ref, sfin_ref, s_scr):
    t_idx = pl.program_id(1)
    c = GLA_C

    @pl.when(t_idx == 0)
    def _():
        s_scr[...] = s0_ref[0].reshape(W_BK, DV_B)

    q, k, la = q_ref[0].astype(F32), k_ref[0].astype(F32), la_ref[0]
    v16 = v_ref[0]
    o, cum = _gla_intra(q, k, v16.astype(F32), v16, la, c)

    q_in = (q * jnp.exp(cum)).astype(BF16)
    k_out_t = (k * jnp.exp(cum[c - 1:c, :] - cum)).T.astype(BF16)
    keep = jnp.exp(cum.T[:, c - 1:c])
    s_old = s_scr[...]
    s_new = []
    inter = []
    for h in range(H_B):
        ks = slice(h * DK_B, (h + 1) * DK_B)
        vs = slice(h * DV_B, (h + 1) * DV_B)
        inter.append(_dot(q_in[:, ks], s_old[ks].astype(BF16)))
        s_new.append(s_old[ks] * keep[ks] + _dot(k_out_t[ks], v16[:, vs]))
    s_new = jnp.concatenate(s_new, axis=0)
    s_scr[...] = s_new
    o_ref[0] = _gla_finish(o + jnp.concatenate(inter, axis=-1), g_ref, sz_ref[0])

    @pl.when(t_idx == pl.num_programs(1) - 1)
    def _():
        sfin_ref[0] = s_new.reshape(H_B, DK_B, DV_B)


def _gla(q, k, v, la, sz, g, s0):
    b, t, _ = q.shape
    assert t % GLA_C == 0
    blk = lambda w: pl.BlockSpec((1, GLA_C, w), lambda bi, i: (bi, i, 0))
    state = pl.BlockSpec((1, H_B, DK_B, DV_B), lambda bi, i: (bi, 0, 0, 0))
    return pl.pallas_call(
        _gla_kernel,
        grid=(b, t // GLA_C),
        in_specs=[blk(W_BK), blk(W_BK), blk(W_BV), blk(W_BK), blk(W_BV),
                  pl.BlockSpec((1, W_BV), lambda bi, i: (0, 0)), state],
        out_specs=[blk(W_BV), state],
        out_shape=[jax.ShapeDtypeStruct((b, t, W_BV), BF16),
                   jax.ShapeDtypeStruct((b, H_B, DK_B, DV_B), F32)],
        scratch_shapes=[pltpu.VMEM((W_BK, DV_B), F32)],
        compiler_params=_params("parallel", "arbitrary"),
        name="gla",
    )(q, k, v, la, sz, g, s0)


def _gla_step_kernel(q_ref, k_ref, v_ref, la_ref, sz_ref, g_ref, s0_ref, o_ref, s_ref, *, t):
    c = GLA_C
    n = c // t
    q, k, la = q_ref[...].astype(F32), k_ref[...].astype(F32), la_ref[...]
    v16 = v_ref[...]
    o, cum = _gla_intra(q, k, v16.astype(F32), v16, la, t)

    q_in = (q * jnp.exp(cum)).astype(BF16)
    cum_t, k_t = cum.T, k.T
    stream_of_col = lax.broadcasted_iota(jnp.int32, (W_BK, c), 1) // t
    inter = []
    for s in range(n):
        rows = slice(s * t, (s + 1) * t)
        last = cum_t[:, (s + 1) * t - 1:(s + 1) * t]
        k_out_t = (k_t * jnp.exp(jnp.where(stream_of_col == s, last - cum_t, NEG))).astype(BF16)
        keep = jnp.exp(last)
        o_s = []
        for h in range(H_B):
            ks = slice(h * DK_B, (h + 1) * DK_B)
            vs = slice(h * DV_B, (h + 1) * DV_B)
            s_old = s0_ref[s, h]
            o_s.append(_dot(q_in[rows, ks], s_old.astype(BF16)))
            s_ref[s, h] = s_old * keep[ks] + _dot(k_out_t[ks], v16[:, vs])
        inter.append(jnp.concatenate(o_s, axis=-1))
    o_ref[...] = _gla_finish(o + jnp.concatenate(inter, axis=0), g_ref, sz_ref[...])


def _gla_step(q, k, v, la, sz, g, s0):
    b, t, _ = q.shape
    n = GLA_C // t
    assert GLA_C % t == 0 and t % (2 * GLA_BASE) == 0 and b % n == 0
    flat = lambda z: z.reshape(b * t, z.shape[-1])
    blk = lambda w: pl.BlockSpec((GLA_C, w), lambda i: (i, 0))
    state = pl.BlockSpec((n, H_B, DK_B, DV_B), lambda i: (i, 0, 0, 0))
    o, s_new = pl.pallas_call(
        functools.partial(_gla_step_kernel, t=t),
        grid=(b // n,),
        in_specs=[blk(W_BK), blk(W_BK), blk(W_BV), blk(W_BK), blk(W_BV),
                  pl.BlockSpec((1, W_BV), lambda i: (0, 0)), state],
        out_specs=[blk(W_BV), state],
        out_shape=[jax.ShapeDtypeStruct((b * t, W_BV), BF16),
                   jax.ShapeDtypeStruct((b, H_B, DK_B, DV_B), F32)],
        compiler_params=_params("parallel"),
        name="gla_step",
    )(flat(q), flat(k), flat(v), flat(la), flat(sz), g, s0)
    return o.reshape(b, t, W_BV), s_new


def _prep_weights(l, norm_in, w_in, g_qa, g_ka, w_gate2, b_gate, g_gla_out, g_mem, w_mem_kv, g_qm, g_km,
                  w_up_a, w_up_b, w_up_m, w_out):
    tile = lambda gain, n: jnp.tile(gain, n).reshape(1, -1)
    return {
        "norm_in": norm_in[l].reshape(1, D_MODEL),
        "w_in_t": w_in[l].T.astype(BF16),
        "w_gate2": w_gate2[l].astype(BF16),
        "b_gate": b_gate[l].reshape(1, W_BK),
        "g_qa": tile(g_qa[l], H_A), "g_ka": tile(g_ka[l], H_A), "g_qm": tile(g_qm[l], H_M),
        "g_km": tile(g_km[l], H_M), "g_gla": tile(g_gla_out[l], H_B),
        "g_mem": g_mem[l].reshape(1, D_MODEL),
        "w_mem_kv": w_mem_kv[l].astype(BF16),
        "w_up_a": w_up_a[l].astype(BF16), "w_up_b": w_up_b[l].astype(BF16),
        "w_up_m": w_up_m[l].astype(BF16), "w_out": w_out[l].astype(BF16),
    }


def _layer(x, wts, attend_a, attend_m, s0, tm):
    b, t, _ = x.shape
    keep = min(BAND, t)
    x2d = x.reshape(b * t, D_MODEL)
    (qa, ka, va, ka16, va16, sza, qb, kb, vb, la, szb, qm, szm, ga, gb, gm) = _front(x2d, wts, tm, t, keep)
    r3 = lambda z: z.reshape(b, t, z.shape[-1])
    out_a = attend_a(r3(qa), r3(ka16), r3(va16), r3(sza))
    gla = _gla if t >= GLA_C else _gla_step
    out_b, s_new = gla(r3(qb), r3(kb), r3(vb), r3(la), r3(szb), wts["g_gla"], s0)
    out_m = attend_m(r3(qm), r3(szm))
    flat = lambda z: z.reshape(b * t, z.shape[-1])
    y = _back(x2d, flat(out_a), flat(out_b), flat(out_m), ga, gb, gm, wts, tm)
    heads = lambda z: z.reshape(b, keep, H_A, HEAD_DIM)
    return y.reshape(b, t, D_MODEL), heads(ka), heads(va), s_new


def kernel(x_prompt, x_sample, mem_prompt, cache_a_k, cache_a_v, state_gla, cache_mem_k, cache_mem_v,
           norm_in, w_in, g_qa, g_ka, rel_bias, w_gate2, b_gate, g_gla_out, g_mem, w_mem_kv, g_qm, g_km,
           w_up_a, w_up_b, w_up_m, w_out):
    depth = w_in.shape[0]
    xp, xs = x_prompt, x_sample
    bp, tp, _ = xp.shape
    bs, ts, _ = xs.shape
    akp, avp, sgp, mkp, mvp, aks, avs, sgs = [], [], [], [], [], [], [], []
    for l in range(depth):
        wts = _prep_weights(l, norm_in, w_in, g_qa, g_ka, w_gate2, b_gate, g_gla_out, g_mem, w_mem_kv,
                            g_qm, g_km, w_up_a, w_up_b, w_up_m, w_out)
        bias = _bias_tile(rel_bias[l])
        mk, mv = _mem_kv(mem_prompt, wts)
        attend_p = lambda q, k, v, sz: _attn_prompt(q, k, v, sz, bias)
        s0 = jnp.zeros((bp, H_B, DK_B, DV_B), state_gla.dtype)
        attend_mp = lambda q, sz: _attn_mem(q, mk, mv, sz, min(tp, 512), 1)
        xp, ka, va, sp = _layer(xp, wts, attend_p, attend_mp, s0, 256)
        akp.append(ka)
        avp.append(va)
        sgp.append(sp)
        mkp.append(mk.reshape(bp, N_MEM, H_M, HEAD_DIM))
        mvp.append(mv.reshape(bp, N_MEM, H_M, HEAD_DIM))
        rows_last = lambda z: jnp.transpose(z, (0, 2, 3, 1))
        past_k, past_v = rows_last(cache_a_k[l]), rows_last(cache_a_v[l])
        mem_k, mem_v = rows_last(cache_mem_k[l]), rows_last(cache_mem_v[l])
        attend_s = lambda q, k, v, sz: _attn_step(q, k, v, past_k, past_v, sz, bias)
        attend_ms = lambda q, sz: _attn_mem_step(q, mem_k, mem_v, sz)
        xs, ka_s, va_s, ss = _layer(xs, wts, attend_s, attend_ms, state_gla[l], 256)
        aks.append(ka_s)
        avs.append(va_s)
        sgs.append(ss)
    return (xp, xs, jnp.stack(akp), jnp.stack(avp), jnp.stack(sgp), jnp.stack(mkp), jnp.stack(mvp),
            jnp.stack(aks), jnp.stack(avs), jnp.stack(sgs))
```

```python
import functools

import jax
import jax.numpy as jnp
from jax import lax
from jax.experimental import pallas as pl
from jax.experimental.pallas import tpu as pltpu

F32 = jnp.float32
BF16 = jnp.bfloat16

D_MODEL = 1024
CHUNK = 64
LEFT_CHUNKS = 8
HEAD_DIM = 64
H_A = 8
W_A = H_A * HEAD_DIM
REL_CLIP = 128
H_B = 4
DK_B = 64
DV_B = 128
W_BK = H_B * DK_B
W_BV = H_B * DV_B
GATE_RANK = 16
GATE_TAU = 16.0
N_MEM = 256
H_M = 4
W_M = H_M * HEAD_DIM
EPS = 1e-6
IN_SIZES = (W_A, W_A, W_A, W_A, W_BK, W_BK, W_BV, GATE_RANK, W_BV, W_M, W_M, D_MODEL, D_MODEL, D_MODEL)

LANES = 128
SUBLANES = 8
VMEM_LIMIT = 56 * 1024 * 1024
NEG = -1e30
LOG2E = 1.4426950408889634

BAND = LEFT_CHUNKS * CHUNK
QBLK = 256
KSPAN = BAND + QBLK
WINDOW = BAND + CHUNK
BIAS_W = BAND + 2 * CHUNK
GLA_C = 128
GLA_BASE = SUBLANES


def _dot(a, b):
    return jnp.dot(a, b, preferred_element_type=F32)


def _dot_nt(a, b):
    return lax.dot_general(a, b, (((1,), (1,)), ((), ())), preferred_element_type=F32)


def _split3(x):
    hi = x.astype(BF16)
    r = x - hi.astype(F32)
    mid = r.astype(BF16)
    lo = (r - mid.astype(F32)).astype(BF16)
    return hi, mid, lo


def _dot_exact_lhs(a01, x):
    hi, mid, lo = _split3(x)
    return _dot(a01, hi) + _dot(a01, mid) + _dot(a01, lo)


def _sigmoid(z):
    return 1.0 / (1.0 + jnp.exp(-z))


def _silu(z):
    return z * _sigmoid(z)


def _log_sigmoid(z):
    return jnp.minimum(z, 0.0) - jnp.log1p(jnp.exp(-jnp.abs(z)))


def _rms_rows(x, g):
    ms = jnp.mean(x * x, axis=-1, keepdims=True)
    return x * lax.rsqrt(ms + EPS) * g


def _head_mean_matrix(width, head):
    r = lax.broadcasted_iota(jnp.int32, (width, width), 0) // head
    c = lax.broadcasted_iota(jnp.int32, (width, width), 1) // head
    return jnp.where(r == c, 1.0 / head, 0.0).astype(BF16)


def _head_rms(x, g, head):
    avg = _head_mean_matrix(x.shape[-1], head)
    ms = _dot((x * x).astype(BF16), avg)
    return x * lax.rsqrt(ms + EPS) * g


def _full(shape):
    nd = len(shape)
    return pl.BlockSpec(shape, lambda *_: (0,) * nd, pipeline_mode=pl.Buffered(1))


def _params(*sem):
    return pltpu.CompilerParams(dimension_semantics=sem, vmem_limit_bytes=VMEM_LIMIT)


_IN_OFF = tuple(int(sum(IN_SIZES[:i])) for i in range(len(IN_SIZES) + 1))
D_IN = _IN_OFF[-1]


def _front_kernel(x_ref, nin_ref, w_ref, wg2_ref, bg_ref, gqa_ref, gka_ref, gqm_ref,
                  qa_o, ka_o, va_o, ka16_o, va16_o, sza_o, qb_o, kb_o, vb_o, la_o, szb_o, qm_o, szm_o,
                  ga_o, gb_o, gm_o):
    h = _rms_rows(x_ref[...], nin_ref[...]).astype(BF16)

    def proj(i):
        return _dot_nt(h, w_ref[_IN_OFF[i]:_IN_OFF[i + 1], :])

    scale = HEAD_DIM ** -0.5 * LOG2E
    qa_o[...] = (_head_rms(proj(0), gqa_ref[...], HEAD_DIM) * scale).astype(BF16)
    ka = _head_rms(proj(1), gka_ref[...], HEAD_DIM)
    va = proj(2)
    ka_o[...] = ka
    va_o[...] = va
    ka16_o[...] = ka.astype(BF16)
    va16_o[...] = va.astype(BF16)
    sza_o[...] = _silu(proj(3)).astype(BF16)
    qb_o[...] = (proj(4) * (DK_B ** -0.5)).astype(BF16)
    kb_o[...] = proj(5).astype(BF16)
    vb_o[...] = proj(6).astype(BF16)
    z = _dot(proj(7).astype(BF16), wg2_ref[...]) + bg_ref[...]
    la_o[...] = _log_sigmoid(z) * (1.0 / GATE_TAU)
    szb_o[...] = _silu(proj(8)).astype(BF16)
    qm_o[...] = (_head_rms(proj(9), gqm_ref[...], HEAD_DIM) * scale).astype(BF16)
    szm_o[...] = _silu(proj(10)).astype(BF16)
    ga_o[...] = _sigmoid(proj(11)).astype(BF16)
    gb_o[...] = _sigmoid(proj(12)).astype(BF16)
    gm_o[...] = _sigmoid(proj(13)).astype(BF16)


def _front(x2d, wts, tm, t, keep):
    n = x2d.shape[0]
    tm = min(tm, n)
    row = lambda w: pl.BlockSpec((tm, w), lambda i: (i, 0))
    if keep == t:
        kv_rows, kv = n, row(W_A)
    else:
        assert t % tm == 0 and keep % tm == 0
        per_stream, kept = t // tm, keep // tm
        kv_rows = (n // t) * keep
        kv = pl.BlockSpec((tm, W_A), lambda i: (
            (i // per_stream) * kept + jnp.maximum(i % per_stream - (per_stream - kept), 0), 0))
    widths = (W_A, W_A, W_A, W_A, W_A, W_A, W_BK, W_BK, W_BV, W_BK, W_BV, W_M, W_M, D_MODEL, D_MODEL, D_MODEL)
    dtypes = (BF16, F32, F32, BF16, BF16, BF16, BF16, BF16, BF16, F32, BF16, BF16, BF16, BF16, BF16, BF16)
    rows = (n, kv_rows, kv_rows) + (n,) * 13
    specs = [row(W_A), kv, kv] + [row(w) for w in widths[3:]]
    return pl.pallas_call(
        _front_kernel,
        grid=(n // tm,),
        in_specs=[row(D_MODEL), _full((1, D_MODEL)), _full((D_IN, D_MODEL)),
                  _full((GATE_RANK, W_BK)), _full((1, W_BK)), _full((1, W_A)), _full((1, W_A)), _full((1, W_M))],
        out_specs=specs,
        out_shape=[jax.ShapeDtypeStruct((r, w), d) for r, w, d in zip(rows, widths, dtypes)],
        compiler_params=_params("arbitrary"),
        name="front",
    )(x2d, wts["norm_in"], wts["w_in_t"], wts["w_gate2"], wts["b_gate"],
      wts["g_qa"], wts["g_ka"], wts["g_qm"])


def _back_kernel(x_ref, a_ref, b_ref, m_ref, ga_ref, gb_ref, gm_ref, wa_ref, wb_ref, wm_ref, wo_ref, y_ref):
    u = (ga_ref[...].astype(F32) * _dot(a_ref[...], wa_ref[...])
         + gb_ref[...].astype(F32) * _dot(b_ref[...], wb_ref[...])
         + gm_ref[...].astype(F32) * _dot(m_ref[...], wm_ref[...]))
    y_ref[...] = x_ref[...] + _dot(u.astype(BF16), wo_ref[...])


def _back(x2d, a, b, m, ga, gb, gm, wts, tm):
    n = x2d.shape[0]
    tm = min(tm, n)
    row = lambda w: pl.BlockSpec((tm, w), lambda i: (i, 0))
    return pl.pallas_call(
        _back_kernel,
        grid=(n // tm,),
        in_specs=[row(D_MODEL), row(W_A), row(W_BV), row(W_M), row(D_MODEL), row(D_MODEL), row(D_MODEL),
                  _full((W_A, D_MODEL)), _full((W_BV, D_MODEL)), _full((W_M, D_MODEL)),
                  _full((D_MODEL, D_MODEL))],
        out_specs=row(D_MODEL),
        out_shape=jax.ShapeDtypeStruct((n, D_MODEL), F32),
        compiler_params=_params("parallel"),
        name="back",
    )(x2d, a, b, m, ga, gb, gm, wts["w_up_a"], wts["w_up_b"], wts["w_up_m"], wts["w_out"])


def _bias_kernel(e_ref, tile_ref):
    e = jnp.broadcast_to(e_ref[0] * LOG2E, (CHUNK, 2 * BAND))
    tile_ref[0] = pltpu.roll(e, 0, 1, stride=1, stride_axis=0)[:, :BIAS_W]


def _bias_tile(rel_bias):
    j = jnp.arange(2 * BAND)
    dist = jnp.where(j < BIAS_W, BAND - j, REL_CLIP)
    e = rel_bias[:, jnp.clip(dist, -REL_CLIP, REL_CLIP) + REL_CLIP].reshape(H_A, 1, 2 * BAND)
    return pl.pallas_call(
        _bias_kernel,
        grid=(H_A,),
        in_specs=[pl.BlockSpec((1, 1, 2 * BAND), lambda h: (h, 0, 0))],
        out_specs=pl.BlockSpec((1, CHUNK, BIAS_W), lambda h: (h, 0, 0)),
        out_shape=jax.ShapeDtypeStruct((H_A, CHUNK, BIAS_W), F32),
        compiler_params=_params("parallel"),
        name="bias_tile",
    )(e)


def _softmax_pv_staged(scores, values):
    tops = [functools.reduce(jnp.maximum, [jnp.max(s, axis=-1, keepdims=True) for s in s_list])
            for s_list in scores]
    probs = [[jnp.exp2(s - m) for s in s_list] for s_list, m in zip(scores, tops)]
    sums = [functools.reduce(jnp.add, [jnp.sum(x, axis=-1, keepdims=True) for x in p_list]) for p_list in probs]
    return [functools.reduce(jnp.add, [(v(x.astype(BF16)) if callable(v) else _dot(x.astype(BF16), v))
                                       for x, v in zip(p_list, v_list)]) / l
            for p_list, v_list, l in zip(probs, values, sums)]


def _head_pairs_attention(q_pairs, k_lists, v_lists, bias_fn):
    m_rows = q_pairs[0].shape[0]
    lane = lax.broadcasted_iota(jnp.int32, (m_rows, LANES), 1)
    first = lane < HEAD_DIM
    scores = []
    for p, qp in enumerate(q_pairs):
        zero = jnp.zeros_like(qp)
        q2 = jnp.concatenate([jnp.where(first, qp, zero), jnp.where(first, zero, qp)], axis=0)
        s_list = []
        for i, k in enumerate(k_lists[p]):
            s = _dot_nt(q2, k)
            b0, b1 = bias_fn(p, 0, i), bias_fn(p, 1, i)
            s_list.append(s if b0 is None else s + jnp.concatenate([b0, b1], axis=0))
        scores.append(s_list)
    outs = _softmax_pv_staged(scores, v_lists)
    return [jnp.where(first, o[:m_rows], o[m_rows:]) for o in outs]


def _attn_prompt_kernel(q_ref, k0_ref, k1_ref, k2_ref, v0_ref, v1_ref, v2_ref, sz_ref, bias_ref, o_ref,
                        k_buf, v_buf, bias_buf):
    i = pl.program_id(1)
    for j, (k_ref, v_ref) in enumerate(((k0_ref, v0_ref), (k1_ref, v1_ref), (k2_ref, v2_ref))):
        k_buf[j * QBLK:(j + 1) * QBLK, :] = k_ref[0]
        v_buf[j * QBLK:(j + 1) * QBLK, :] = v_ref[0]

    @pl.when(i <= BAND // QBLK)
    def _():
        col = lax.broadcasted_iota(jnp.int32, (1, BIAS_W), 1)
        for c in range(QBLK // CHUNK):
            before_start = jnp.where(col + c * CHUNK < (BAND // QBLK - i) * QBLK, NEG, 0.0)
            for h in range(H_A):
                bias_buf[c, h] = bias_ref[h] + before_start

    for c in range(QBLK // CHUNK):
        rows = slice(c * CHUNK, (c + 1) * CHUNK)
        win = slice(c * CHUNK, c * CHUNK + WINDOW)
        pairs = [slice(p * LANES, (p + 1) * LANES) for p in range(H_A // 2)]
        outs = _head_pairs_attention(
            [q_ref[0, rows, lanes] for lanes in pairs],
            [[k_buf[win, lanes]] for lanes in pairs], [[v_buf[win, lanes]] for lanes in pairs],
            lambda p, e, _: bias_buf[c, 2 * p + e, :, :WINDOW])
        for lanes, o in zip(pairs, outs):
            o_ref[0, rows, lanes] = (o * sz_ref[0, rows, lanes].astype(F32)).astype(BF16)


def _attn_prompt(q, k, v, sz, bias):
    b, t, _ = q.shape
    nb = BAND // QBLK
    cur = pl.BlockSpec((1, QBLK, W_A), lambda bi, i: (bi, i, 0))
    prev = lambda d: pl.BlockSpec((1, QBLK, W_A), lambda bi, i: (bi, jnp.maximum(i - d, 0), 0))
    kv_specs = [prev(nb - j) for j in range(nb)] + [cur]
    assert nb == 2
    return pl.pallas_call(
        _attn_prompt_kernel,
        grid=(b, t // QBLK),
        in_specs=[cur] + kv_specs + kv_specs + [cur, _full((H_A, CHUNK, BIAS_W))],
        out_specs=cur,
        out_shape=jax.ShapeDtypeStruct((b, t, W_A), BF16),
        scratch_shapes=[pltpu.VMEM((KSPAN, W_A), BF16), pltpu.VMEM((KSPAN, W_A), BF16),
                        pltpu.VMEM((QBLK // CHUNK, H_A, CHUNK, BIAS_W), F32)],
        compiler_params=_params("parallel", "arbitrary"),
        name="attn_prompt",
    )(q, k, k, k, v, v, v, sz, bias)


STEP_STREAMS = 4


def _attn_step_kernel(q_ref, kn_ref, vn_ref, kp_ref, vp_ref, sz_ref, bias_ref, o_ref):
    s_len = q_ref.shape[1]
    for b in range(STEP_STREAMS):
        scores, values = [], []
        for h in range(H_A):
            lanes = slice(h * HEAD_DIM, (h + 1) * HEAD_DIM)
            q = q_ref[b, :, lanes]
            scores.append([_dot(q, kp_ref[b, h].astype(BF16)) + bias_ref[h, :s_len, :BAND],
                           _dot_nt(q, kn_ref[b, :, lanes]) + bias_ref[h, :s_len, BAND:BAND + s_len]])
            values.append([functools.partial(_dot_nt, b=vp_ref[b, h].astype(BF16)), vn_ref[b, :, lanes]])
        o = jnp.concatenate(_softmax_pv_staged(scores, values), axis=-1)
        o_ref[b] = (o * sz_ref[b].astype(F32)).astype(BF16)


def _attn_step(q, k_new, v_new, k_past_t, v_past_t, sz, bias):
    b, s_len, _ = q.shape
    assert k_past_t.shape[1:] == (H_A, HEAD_DIM, BAND) and s_len <= CHUNK and b % STEP_STREAMS == 0
    new = pl.BlockSpec((STEP_STREAMS, s_len, W_A), lambda i: (i, 0, 0))
    past = pl.BlockSpec((STEP_STREAMS, H_A, HEAD_DIM, BAND), lambda i: (i, 0, 0, 0))
    return pl.pallas_call(
        _attn_step_kernel,
        grid=(b // STEP_STREAMS,),
        in_specs=[new, new, new, past, past, new, _full((H_A, CHUNK, BIAS_W))],
        out_specs=new,
        out_shape=jax.ShapeDtypeStruct((b, s_len, W_A), BF16),
        compiler_params=_params("parallel"),
        name="attn_step",
    )(q, k_new, v_new, k_past_t, v_past_t, sz, bias)


def _mem_kv_kernel(mem_ref, gmem_ref, w_ref, gkm_ref, mk_ref, mv_ref):
    h = _rms_rows(mem_ref[0], gmem_ref[...]).astype(BF16)
    kv = _dot(h, w_ref[...])
    mk_ref[0] = _head_rms(kv[:, :W_M], gkm_ref[...], HEAD_DIM)
    mv_ref[0] = kv[:, W_M:]


def _mem_kv(mem, wts):
    b = mem.shape[0]
    out = pl.BlockSpec((1, N_MEM, W_M), lambda i: (i, 0, 0))
    return pl.pallas_call(
        _mem_kv_kernel,
        grid=(b,),
        in_specs=[pl.BlockSpec((1, N_MEM, D_MODEL), lambda i: (i, 0, 0)), _full((1, D_MODEL)),
                  _full((D_MODEL, 2 * W_M)), _full((1, W_M))],
        out_specs=[out, out],
        out_shape=[jax.ShapeDtypeStruct((b, N_MEM, W_M), F32)] * 2,
        compiler_params=_params("parallel"),
        name="mem_kv",
    )(mem, wts["g_mem"], wts["w_mem_kv"], wts["g_km"])


MEM_ROWS = 128
MEM_STREAMS = 8


def _attn_mem_kernel(q_ref, mk_ref, mv_ref, sz_ref, o_ref):
    nb, tq, _ = q_ref.shape
    step = min(tq, MEM_ROWS)
    pairs = [slice(p * LANES, (p + 1) * LANES) for p in range(H_M // 2)]
    for b in range(nb):
        mk = [[mk_ref[b, :, lanes].astype(BF16)] for lanes in pairs]
        mv = [[mv_ref[b, :, lanes].astype(BF16)] for lanes in pairs]
        for r in range(0, tq, step):
            rows = slice(r, r + step)
            outs = _head_pairs_attention([q_ref[b, rows, lanes] for lanes in pairs], mk, mv,
                                         lambda p, e, i: None)
            for lanes, o in zip(pairs, outs):
                o_ref[b, rows, lanes] = (o * sz_ref[b, rows, lanes].astype(F32)).astype(BF16)


def _attn_mem(q, mk, mv, sz, tq, nb):
    b, t, _ = q.shape
    assert b % nb == 0 and t % tq == 0
    rows = pl.BlockSpec((nb, tq, W_M), lambda bi, i: (bi, i, 0))
    mem = pl.BlockSpec((nb, N_MEM, W_M), lambda bi, i: (bi, 0, 0))
    return pl.pallas_call(
        _attn_mem_kernel,
        grid=(b // nb, t // tq),
        in_specs=[rows, mem, mem, rows],
        out_specs=rows,
        out_shape=jax.ShapeDtypeStruct((b, t, W_M), BF16),
        compiler_params=_params("parallel", "parallel"),
        name="attn_mem",
    )(q, mk, mv, sz)


def _attn_mem_step_kernel(q_ref, mk_ref, mv_ref, sz_ref, o_ref):
    for b in range(q_ref.shape[0]):
        scores, values = [], []
        for h in range(H_M):
            lanes = slice(h * HEAD_DIM, (h + 1) * HEAD_DIM)
            scores.append([_dot(q_ref[b, :, lanes], mk_ref[b, h].astype(BF16))])
            values.append([functools.partial(_dot_nt, b=mv_ref[b, h].astype(BF16))])
        o = jnp.concatenate(_softmax_pv_staged(scores, values), axis=-1)
        o_ref[b] = (o * sz_ref[b].astype(F32)).astype(BF16)


def _attn_mem_step(q, mk_t, mv_t, sz):
    b, t, _ = q.shape
    nb = min(b, MEM_STREAMS)
    assert b % nb == 0 and mk_t.shape[1:] == (H_M, HEAD_DIM, N_MEM)
    rows = pl.BlockSpec((nb, t, W_M), lambda i: (i, 0, 0))
    mem = pl.BlockSpec((nb, H_M, HEAD_DIM, N_MEM), lambda i: (i, 0, 0, 0))
    return pl.pallas_call(
        _attn_mem_step_kernel,
        grid=(b // nb,),
        in_specs=[rows, mem, mem, rows],
        out_specs=rows,
        out_shape=jax.ShapeDtypeStruct((b, t, W_M), BF16),
        compiler_params=_params("parallel"),
        name="attn_mem_step",
    )(q, mk_t, mv_t, sz)


def _gla_intra(q, k, v, v16, la, group):
    c = q.shape[0]
    la = la * LOG2E
    r_cc = lax.broadcasted_iota(jnp.int32, (c, c), 0)
    c_cc = lax.broadcasted_iota(jnp.int32, (c, c), 1)
    causal = c_cc <= r_cc
    if group < c:
        causal = causal & (r_cc // group == c_cc // group)
    cum = _dot_exact_lhs(causal.astype(BF16), la)
    cumx = cum - la
    r_ck = lax.broadcasted_iota(jnp.int32, (c, W_BK), 0)

    def back(x, j):
        n, w = x.shape
        return pltpu.roll(x.reshape(n // GLA_BASE, GLA_BASE, w), j, 1).reshape(n, w)

    r_in = r_ck % GLA_BASE
    prods = [q * k]
    for j in range(1, GLA_BASE):
        decay = jnp.exp2(jnp.where(r_in >= j, cum - back(cum, j), NEG))
        prods.append(q * back(k, j) * decay)
    head_of_k = lax.broadcasted_iota(jnp.int32, (W_BK, W_BV), 0) // DK_B
    head_of_v = lax.broadcasted_iota(jnp.int32, (W_BK, W_BV), 1) // DV_B
    spread = (head_of_k == head_of_v).astype(BF16)
    w = _dot(jnp.concatenate(prods, axis=0).astype(BF16), spread)
    o = w[:c] * v
    for j in range(1, GLA_BASE):
        o = o + w[j * c:(j + 1) * c] * back(v, j)

    att = [jnp.zeros((c, c), F32) for _ in range(H_B)]
    half = GLA_BASE
    while half < group:
        nb = c // half
        cum3 = cum.reshape(nb, half, W_BK)
        q_exp = (cum3 - cumx.reshape(nb, half, W_BK)[:, 0:1, :]).reshape(c, W_BK)
        k_exp = (cum3[:, half - 1:half, :] - cum3).reshape(c, W_BK)
        odd = (r_ck // half) % 2 == 1
        qt = jnp.where(odd, q * jnp.exp2(q_exp), 0.0).astype(BF16)
        kt = jnp.where(odd, 0.0, k * jnp.exp2(k_exp)).astype(BF16)
        same = (r_cc // (2 * half)) == (c_cc // (2 * half))
        for h in range(H_B):
            sl = slice(h * DK_B, (h + 1) * DK_B)
            a = _dot_nt(qt[:, sl], kt[:, sl])
            att[h] = att[h] + (a if 2 * half == c else jnp.where(same, a, 0.0))
        half *= 2
    if group > GLA_BASE:
        o = o + jnp.concatenate(
            [_dot(att[h].astype(BF16), v16[:, h * DV_B:(h + 1) * DV_B]) for h in range(H_B)], axis=-1)
    return o, cum


def _gla_finish(o, g_ref, sz):
    outs = [_rms_rows(o[:, h * DV_B:(h + 1) * DV_B], g_ref[:, h * DV_B:(h + 1) * DV_B]) for h in range(H_B)]
    return (jnp.concatenate(outs, axis=-1) * sz.astype(F32)).astype(BF16)


def _gla_kernel(q_ref, k_ref, v_ref, la_ref, sz_ref, g_ref, s0_ref, o_ref, sfin_ref, s_scr):
    t_idx = pl.program_id(1)
    c = GLA_C

    @pl.when(t_idx == 0)
    def _():
        s_scr[...] = s0_ref[0].reshape(W_BK, DV_B)

    q, k, la = q_ref[0].astype(F32), k_ref[0].astype(F32), la_ref[0]
    v16 = v_ref[0]
    o, cum = _gla_intra(q, k, v16.astype(F32), v16, la, c)

    q_in = (q * jnp.exp2(cum)).astype(BF16)
    k_out_t = (k * jnp.exp2(cum[c - 1:c, :] - cum)).T.astype(BF16)
    keep = jnp.exp2(cum.T[:, c - 1:c])
    s_old = s_scr[...]
    s_new = []
    inter = []
    for h in range(H_B):
        ks = slice(h * DK_B, (h + 1) * DK_B)
        vs = slice(h * DV_B, (h + 1) * DV_B)
        inter.append(_dot(q_in[:, ks], s_old[ks].astype(BF16)))
        s_new.append(s_old[ks] * keep[ks] + _dot(k_out_t[ks], v16[:, vs]))
    s_new = jnp.concatenate(s_new, axis=0)
    s_scr[...] = s_new
    o_ref[0] = _gla_finish(o + jnp.concatenate(inter, axis=-1), g_ref, sz_ref[0])

    @pl.when(t_idx == pl.num_programs(1) - 1)
    def _():
        sfin_ref[0] = s_new.reshape(H_B, DK_B, DV_B)


def _gla(q, k, v, la, sz, g, s0):
    b, t, _ = q.shape
    assert t % GLA_C == 0
    blk = lambda w: pl.BlockSpec((1, GLA_C, w), lambda bi, i: (bi, i, 0))
    state = pl.BlockSpec((1, H_B, DK_B, DV_B), lambda bi, i: (bi, 0, 0, 0))
    return pl.pallas_call(
        _gla_kernel,
        grid=(b, t // GLA_C),
        in_specs=[blk(W_BK), blk(W_BK), blk(W_BV), blk(W_BK), blk(W_BV),
                  pl.BlockSpec((1, W_BV), lambda bi, i: (0, 0)), state],
        out_specs=[blk(W_BV), state],
        out_shape=[jax.ShapeDtypeStruct((b, t, W_BV), BF16),
                   jax.ShapeDtypeStruct((b, H_B, DK_B, DV_B), F32)],
        scratch_shapes=[pltpu.VMEM((W_BK, DV_B), F32)],
        compiler_params=_params("parallel", "arbitrary"),
        name="gla",
    )(q, k, v, la, sz, g, s0)


def _gla_step_kernel(q_ref, k_ref, v_ref, la_ref, sz_ref, g_ref, s0_ref, o_ref, s_ref, *, t):
    c = GLA_C
    n = c // t
    q, k, la = q_ref[...].astype(F32), k_ref[...].astype(F32), la_ref[...]
    v16 = v_ref[...]
    o, cum = _gla_intra(q, k, v16.astype(F32), v16, la, t)

    q_in = (q * jnp.exp2(cum)).astype(BF16)
    cum_t, k_t = cum.T, k.T
    stream_of_col = lax.broadcasted_iota(jnp.int32, (W_BK, c), 1) // t
    inter = []
    for s in range(n):
        rows = slice(s * t, (s + 1) * t)
        last = cum_t[:, (s + 1) * t - 1:(s + 1) * t]
        k_out_t = (k_t * jnp.exp2(jnp.where(stream_of_col == s, last - cum_t, NEG))).astype(BF16)
        keep = jnp.exp2(last)
        o_s = []
        for h in range(H_B):
            ks = slice(h * DK_B, (h + 1) * DK_B)
            vs = slice(h * DV_B, (h + 1) * DV_B)
            s_old = s0_ref[s, h]
            o_s.append(_dot(q_in[rows, ks], s_old.astype(BF16)))
            s_ref[s, h] = s_old * keep[ks] + _dot(k_out_t[ks], v16[:, vs])
        inter.append(jnp.concatenate(o_s, axis=-1))
    o_ref[...] = _gla_finish(o + jnp.concatenate(inter, axis=0), g_ref, sz_ref[...])


def _gla_step(q, k, v, la, sz, g, s0):
    b, t, _ = q.shape
    n = GLA_C // t
    assert GLA_C % t == 0 and t % (2 * GLA_BASE) == 0 and b % n == 0
    flat = lambda z: z.reshape(b * t, z.shape[-1])
    blk = lambda w: pl.BlockSpec((GLA_C, w), lambda i: (i, 0))
    state = pl.BlockSpec((n, H_B, DK_B, DV_B), lambda i: (i, 0, 0, 0))
    o, s_new = pl.pallas_call(
        functools.partial(_gla_step_kernel, t=t),
        grid=(b // n,),
        in_specs=[blk(W_BK), blk(W_BK), blk(W_BV), blk(W_BK), blk(W_BV),
                  pl.BlockSpec((1, W_BV), lambda i: (0, 0)), state],
        out_specs=[blk(W_BV), state],
        out_shape=[jax.ShapeDtypeStruct((b * t, W_BV), BF16),
                   jax.ShapeDtypeStruct((b, H_B, DK_B, DV_B), F32)],
        compiler_params=_params("parallel"),
        name="gla_step",
    )(flat(q), flat(k), flat(v), flat(la), flat(sz), g, s0)
    return o.reshape(b, t, W_BV), s_new


def _prep_weights(l, norm_in, w_in, g_qa, g_ka, w_gate2, b_gate, g_gla_out, g_mem, w_mem_kv, g_qm, g_km,
                  w_up_a, w_up_b, w_up_m, w_out):
    tile = lambda gain, n: jnp.tile(gain, n).reshape(1, -1)
    return {
        "norm_in": norm_in[l].reshape(1, D_MODEL),
        "w_in_t": w_in[l].T.astype(BF16),
        "w_gate2": w_gate2[l].astype(BF16),
        "b_gate": b_gate[l].reshape(1, W_BK),
        "g_qa": tile(g_qa[l], H_A), "g_ka": tile(g_ka[l], H_A), "g_qm": tile(g_qm[l], H_M),
        "g_km": tile(g_km[l], H_M), "g_gla": tile(g_gla_out[l], H_B),
        "g_mem": g_mem[l].reshape(1, D_MODEL),
        "w_mem_kv": w_mem_kv[l].astype(BF16),
        "w_up_a": w_up_a[l].astype(BF16), "w_up_b": w_up_b[l].astype(BF16),
        "w_up_m": w_up_m[l].astype(BF16), "w_out": w_out[l].astype(BF16),
    }


def _layer(x, wts, attend_a, attend_m, s0, tm):
    b, t, _ = x.shape
    keep = min(BAND, t)
    x2d = x.reshape(b * t, D_MODEL)
    (qa, ka, va, ka16, va16, sza, qb, kb, vb, la, szb, qm, szm, ga, gb, gm) = _front(x2d, wts, tm, t, keep)
    r3 = lambda z: z.reshape(b, t, z.shape[-1])
    out_a = attend_a(r3(qa), r3(ka16), r3(va16), r3(sza))
    gla = _gla if t >= GLA_C else _gla_step
    out_b, s_new = gla(r3(qb), r3(kb), r3(vb), r3(la), r3(szb), wts["g_gla"], s0)
    out_m = attend_m(r3(qm), r3(szm))
    flat = lambda z: z.reshape(b * t, z.shape[-1])
    y = _back(x2d, flat(out_a), flat(out_b), flat(out_m), ga, gb, gm, wts, tm)
    heads = lambda z: z.reshape(b, keep, H_A, HEAD_DIM)
    return y.reshape(b, t, D_MODEL), heads(ka), heads(va), s_new


def kernel(x_prompt, x_sample, mem_prompt, cache_a_k, cache_a_v, state_gla, cache_mem_k, cache_mem_v,
           norm_in, w_in, g_qa, g_ka, rel_bias, w_gate2, b_gate, g_gla_out, g_mem, w_mem_kv, g_qm, g_km,
           w_up_a, w_up_b, w_up_m, w_out):
    depth = w_in.shape[0]
    xp, xs = x_prompt, x_sample
    bp, tp, _ = xp.shape
    bs, ts, _ = xs.shape
    akp, avp, sgp, mkp, mvp, aks, avs, sgs = [], [], [], [], [], [], [], []
    for l in range(depth):
        wts = _prep_weights(l, norm_in, w_in, g_qa, g_ka, w_gate2, b_gate, g_gla_out, g_mem, w_mem_kv,
                            g_qm, g_km, w_up_a, w_up_b, w_up_m, w_out)
        bias = _bias_tile(rel_bias[l])
        mk, mv = _mem_kv(mem_prompt, wts)
        attend_p = lambda q, k, v, sz: _attn_prompt(q, k, v, sz, bias)
        s0 = jnp.zeros((bp, H_B, DK_B, DV_B), state_gla.dtype)
        attend_mp = lambda q, sz: _attn_mem(q, mk, mv, sz, min(tp, 512), 1)
        xp, ka, va, sp = _layer(xp, wts, attend_p, attend_mp, s0, 512)
        akp.append(ka)
        avp.append(va)
        sgp.append(sp)
        mkp.append(mk.reshape(bp, N_MEM, H_M, HEAD_DIM))
        mvp.append(mv.reshape(bp, N_MEM, H_M, HEAD_DIM))
        rows_last = lambda z: jnp.transpose(z, (0, 2, 3, 1))
        past_k, past_v = rows_last(cache_a_k[l]), rows_last(cache_a_v[l])
        mem_k, mem_v = rows_last(cache_mem_k[l]), rows_last(cache_mem_v[l])
        attend_s = lambda q, k, v, sz: _attn_step(q, k, v, past_k, past_v, sz, bias)
        attend_ms = lambda q, sz: _attn_mem_step(q, mem_k, mem_v, sz)
        xs, ka_s, va_s, ss = _layer(xs, wts, attend_s, attend_ms, state_gla[l], 256)
        aks.append(ka_s)
        avs.append(va_s)
        sgs.append(ss)
    return (xp, xs, jnp.stack(akp), jnp.stack(avp), jnp.stack(sgp), jnp.stack(mkp), jnp.stack(mvp),
            jnp.stack(aks), jnp.stack(avs), jnp.stack(sgs))
```

```python
import functools

import jax
import jax.numpy as jnp
from jax import lax
from jax.experimental import pallas as pl
from jax.experimental.pallas import tpu as pltpu

F32 = jnp.float32
BF16 = jnp.bfloat16

D_MODEL = 1024
CHUNK = 64
LEFT_CHUNKS = 8
HEAD_DIM = 64
H_A = 8
W_A = H_A * HEAD_DIM
REL_CLIP = 128
H_B = 4
DK_B = 64
DV_B = 128
W_BK = H_B * DK_B
W_BV = H_B * DV_B
GATE_RANK = 16
GATE_TAU = 16.0
N_MEM = 256
H_M = 4
W_M = H_M * HEAD_DIM
EPS = 1e-6
IN_SIZES = (W_A, W_A, W_A, W_A, W_BK, W_BK, W_BV, GATE_RANK, W_BV, W_M, W_M, D_MODEL, D_MODEL, D_MODEL)

LANES = 128
SUBLANES = 8
VMEM_LIMIT = 56 * 1024 * 1024
NEG = -1e30
LOG2E = 1.4426950408889634

BAND = LEFT_CHUNKS * CHUNK
QBLK = 256
KSPAN = BAND + QBLK
WINDOW = BAND + CHUNK
BIAS_W = BAND + 2 * CHUNK
ATTN_CHUNKS = 1
GLA_C = 128
GLA_BASE = SUBLANES


def _dot(a, b):
    return jnp.dot(a, b, preferred_element_type=F32)


def _dot_nt(a, b):
    return lax.dot_general(a, b, (((1,), (1,)), ((), ())), preferred_element_type=F32)


def _split3(x):
    hi = x.astype(BF16)
    r = x - hi.astype(F32)
    mid = r.astype(BF16)
    lo = (r - mid.astype(F32)).astype(BF16)
    return hi, mid, lo


def _dot_exact_lhs(a01, x):
    hi, mid, lo = _split3(x)
    return _dot(a01, hi) + _dot(a01, mid) + _dot(a01, lo)


def _sigmoid(z):
    return 1.0 / (1.0 + jnp.exp(-z))


def _silu(z):
    return z * _sigmoid(z)


def _log_sigmoid(z):
    return jnp.minimum(z, 0.0) - jnp.log1p(jnp.exp(-jnp.abs(z)))


def _rms_rows(x, g):
    ms = jnp.mean(x * x, axis=-1, keepdims=True)
    return x * lax.rsqrt(ms + EPS) * g


def _head_mean_matrix(width, head):
    r = lax.broadcasted_iota(jnp.int32, (width, width), 0) // head
    c = lax.broadcasted_iota(jnp.int32, (width, width), 1) // head
    return jnp.where(r == c, 1.0 / head, 0.0).astype(BF16)


def _head_rms(x, g, head):
    avg = _head_mean_matrix(x.shape[-1], head)
    ms = _dot((x * x).astype(BF16), avg)
    return x * lax.rsqrt(ms + EPS) * g


def _full(shape):
    nd = len(shape)
    return pl.BlockSpec(shape, lambda *_: (0,) * nd, pipeline_mode=pl.Buffered(1))


def _params(*sem):
    return pltpu.CompilerParams(dimension_semantics=sem, vmem_limit_bytes=VMEM_LIMIT)


_IN_OFF = tuple(int(sum(IN_SIZES[:i])) for i in range(len(IN_SIZES) + 1))
D_IN = _IN_OFF[-1]


def _front_kernel(x_ref, nin_ref, w_ref, wg2_ref, bg_ref, gqa_ref, gka_ref, gqm_ref,
                  qa_o, ka_o, va_o, ka16_o, va16_o, sza_o, qb_o, kb_o, vb_o, la_o, szb_o, qm_o, szm_o,
                  ga_o, gb_o, gm_o):
    h = _rms_rows(x_ref[...], nin_ref[...]).astype(BF16)

    def proj(i):
        return _dot_nt(h, w_ref[_IN_OFF[i]:_IN_OFF[i + 1], :])

    scale = HEAD_DIM ** -0.5 * LOG2E
    qa_o[...] = (_head_rms(proj(0), gqa_ref[...], HEAD_DIM) * scale).astype(BF16)
    ka = _head_rms(proj(1), gka_ref[...], HEAD_DIM)
    va = proj(2)
    ka_o[...] = ka
    va_o[...] = va
    ka16_o[...] = ka.astype(BF16)
    va16_o[...] = va.astype(BF16)
    sza_o[...] = _silu(proj(3)).astype(BF16)
    qb_o[...] = (proj(4) * (DK_B ** -0.5)).astype(BF16)
    kb_o[...] = proj(5).astype(BF16)
    vb_o[...] = proj(6).astype(BF16)
    z = _dot(proj(7).astype(BF16), wg2_ref[...]) + bg_ref[...]
    la_o[...] = _log_sigmoid(z) * (1.0 / GATE_TAU)
    szb_o[...] = _silu(proj(8)).astype(BF16)
    qm_o[...] = (_head_rms(proj(9), gqm_ref[...], HEAD_DIM) * scale).astype(BF16)
    szm_o[...] = _silu(proj(10)).astype(BF16)
    ga_o[...] = _sigmoid(proj(11)).astype(BF16)
    gb_o[...] = _sigmoid(proj(12)).astype(BF16)
    gm_o[...] = _sigmoid(proj(13)).astype(BF16)


def _front(x2d, wts, tm, t, keep):
    n = x2d.shape[0]
    tm = min(tm, n)
    row = lambda w: pl.BlockSpec((tm, w), lambda i: (i, 0))
    if keep == t:
        kv_rows, kv = n, row(W_A)
    else:
        assert t % tm == 0 and keep % tm == 0
        per_stream, kept = t // tm, keep // tm
        kv_rows = (n // t) * keep
        kv = pl.BlockSpec((tm, W_A), lambda i: (
            (i // per_stream) * kept + jnp.maximum(i % per_stream - (per_stream - kept), 0), 0))
    widths = (W_A, W_A, W_A, W_A, W_A, W_A, W_BK, W_BK, W_BV, W_BK, W_BV, W_M, W_M, D_MODEL, D_MODEL, D_MODEL)
    dtypes = (BF16, F32, F32, BF16, BF16, BF16, BF16, BF16, BF16, F32, BF16, BF16, BF16, BF16, BF16, BF16)
    rows = (n, kv_rows, kv_rows) + (n,) * 13
    specs = [row(W_A), kv, kv] + [row(w) for w in widths[3:]]
    return pl.pallas_call(
        _front_kernel,
        grid=(n // tm,),
        in_specs=[row(D_MODEL), _full((1, D_MODEL)), _full((D_IN, D_MODEL)),
                  _full((GATE_RANK, W_BK)), _full((1, W_BK)), _full((1, W_A)), _full((1, W_A)), _full((1, W_M))],
        out_specs=specs,
        out_shape=[jax.ShapeDtypeStruct((r, w), d) for r, w, d in zip(rows, widths, dtypes)],
        compiler_params=_params("arbitrary"),
        name="front",
    )(x2d, wts["norm_in"], wts["w_in_t"], wts["w_gate2"], wts["b_gate"],
      wts["g_qa"], wts["g_ka"], wts["g_qm"])


def _back_kernel(x_ref, a_ref, b_ref, m_ref, ga_ref, gb_ref, gm_ref, wa_ref, wb_ref, wm_ref, wo_ref, y_ref):
    u = (ga_ref[...].astype(F32) * _dot(a_ref[...], wa_ref[...])
         + gb_ref[...].astype(F32) * _dot(b_ref[...], wb_ref[...])
         + gm_ref[...].astype(F32) * _dot(m_ref[...], wm_ref[...]))
    y_ref[...] = x_ref[...] + _dot(u.astype(BF16), wo_ref[...])


def _back(x2d, a, b, m, ga, gb, gm, wts, tm):
    n = x2d.shape[0]
    tm = min(tm, n)
    row = lambda w: pl.BlockSpec((tm, w), lambda i: (i, 0))
    return pl.pallas_call(
        _back_kernel,
        grid=(n // tm,),
        in_specs=[row(D_MODEL), row(W_A), row(W_BV), row(W_M), row(D_MODEL), row(D_MODEL), row(D_MODEL),
                  _full((W_A, D_MODEL)), _full((W_BV, D_MODEL)), _full((W_M, D_MODEL)),
                  _full((D_MODEL, D_MODEL))],
        out_specs=row(D_MODEL),
        out_shape=jax.ShapeDtypeStruct((n, D_MODEL), F32),
        compiler_params=_params("parallel"),
        name="back",
    )(x2d, a, b, m, ga, gb, gm, wts["w_up_a"], wts["w_up_b"], wts["w_up_m"], wts["w_out"])


def _bias_kernel(e_ref, tile_ref):
    e = jnp.broadcast_to(e_ref[0] * LOG2E, (CHUNK, 2 * BAND))
    tile_ref[0] = pltpu.roll(e, 0, 1, stride=1, stride_axis=0)[:, :BIAS_W]


def _bias_tile(rel_bias):
    j = jnp.arange(2 * BAND)
    dist = jnp.where(j < BIAS_W, BAND - j, REL_CLIP)
    e = rel_bias[:, jnp.clip(dist, -REL_CLIP, REL_CLIP) + REL_CLIP].reshape(H_A, 1, 2 * BAND)
    return pl.pallas_call(
        _bias_kernel,
        grid=(H_A,),
        in_specs=[pl.BlockSpec((1, 1, 2 * BAND), lambda h: (h, 0, 0))],
        out_specs=pl.BlockSpec((1, CHUNK, BIAS_W), lambda h: (h, 0, 0)),
        out_shape=jax.ShapeDtypeStruct((H_A, CHUNK, BIAS_W), F32),
        compiler_params=_params("parallel"),
        name="bias_tile",
    )(e)


def _softmax_pv_staged(scores, values):
    biased = [[s if b is None else s + b for s, b in s_list] for s_list in scores]
    tops = [functools.reduce(jnp.maximum, [jnp.max(s, axis=-1, keepdims=True) for s in s_list])
            for s_list in biased]
    probs = [[jnp.exp2(s - m) for s in s_list] for s_list, m in zip(biased, tops)]
    sums = [functools.reduce(jnp.add, [jnp.sum(x, axis=-1, keepdims=True) for x in p_list]) for p_list in probs]
    return [functools.reduce(jnp.add, [(v(x.astype(BF16)) if callable(v) else _dot(x.astype(BF16), v))
                                       for x, v in zip(p_list, v_list)]) / l
            for p_list, v_list, l in zip(probs, values, sums)]


def _head_pairs_attention(q_pairs, k_lists, v_lists, bias_fn):
    m_rows = q_pairs[0].shape[0]
    lane = lax.broadcasted_iota(jnp.int32, (m_rows, LANES), 1)
    first = lane < HEAD_DIM
    scores = []
    for p, qp in enumerate(q_pairs):
        zero = jnp.zeros_like(qp)
        q2 = jnp.concatenate([jnp.where(first, qp, zero), jnp.where(first, zero, qp)], axis=0)
        s_list = []
        for i, k in enumerate(k_lists[p]):
            b0, b1 = bias_fn(p, 0, i), bias_fn(p, 1, i)
            s_list.append((_dot_nt(q2, k), None if b0 is None else jnp.concatenate([b0, b1], axis=0)))
        scores.append(s_list)
    outs = _softmax_pv_staged(scores, v_lists)
    return [jnp.where(first, o[:m_rows], o[m_rows:]) for o in outs]


def _attn_prompt_kernel(q_ref, k0_ref, k1_ref, k2_ref, v0_ref, v1_ref, v2_ref, sz_ref, bias_ref, o_ref,
                        k_buf, v_buf, bias_buf):
    i = pl.program_id(1)
    for j, (k_ref, v_ref) in enumerate(((k0_ref, v0_ref), (k1_ref, v1_ref), (k2_ref, v2_ref))):
        k_buf[j * QBLK:(j + 1) * QBLK, :] = k_ref[0]
        v_buf[j * QBLK:(j + 1) * QBLK, :] = v_ref[0]

    @pl.when(i <= BAND // QBLK)
    def _():
        col = lax.broadcasted_iota(jnp.int32, (1, BIAS_W), 1)
        for c in range(QBLK // CHUNK):
            before_start = jnp.where(col + c * CHUNK < (BAND // QBLK - i) * QBLK, NEG, 0.0)
            for h in range(H_A):
                bias_buf[c, h] = bias_ref[h] + before_start

    pairs = [slice(p * LANES, (p + 1) * LANES) for p in range(H_A // 2)]
    for c0 in range(0, QBLK // CHUNK, ATTN_CHUNKS):
        blocks = [(c, p) for c in range(c0, c0 + ATTN_CHUNKS) for p in range(len(pairs))]
        rows = lambda c: slice(c * CHUNK, (c + 1) * CHUNK)
        win = lambda c: slice(c * CHUNK, c * CHUNK + WINDOW)
        outs = _head_pairs_attention(
            [q_ref[0, rows(c), pairs[p]] for c, p in blocks],
            [[k_buf[win(c), pairs[p]]] for c, p in blocks], [[v_buf[win(c), pairs[p]]] for c, p in blocks],
            lambda n, e, _: bias_buf[blocks[n][0], 2 * blocks[n][1] + e, :, :WINDOW])
        for (c, p), o in zip(blocks, outs):
            o_ref[0, rows(c), pairs[p]] = (o * sz_ref[0, rows(c), pairs[p]].astype(F32)).astype(BF16)


def _attn_prompt(q, k, v, sz, bias):
    b, t, _ = q.shape
    nb = BAND // QBLK
    cur = pl.BlockSpec((1, QBLK, W_A), lambda bi, i: (bi, i, 0))
    prev = lambda d: pl.BlockSpec((1, QBLK, W_A), lambda bi, i: (bi, jnp.maximum(i - d, 0), 0))
    kv_specs = [prev(nb - j) for j in range(nb)] + [cur]
    assert nb == 2
    return pl.pallas_call(
        _attn_prompt_kernel,
        grid=(b, t // QBLK),
        in_specs=[cur] + kv_specs + kv_specs + [cur, _full((H_A, CHUNK, BIAS_W))],
        out_specs=cur,
        out_shape=jax.ShapeDtypeStruct((b, t, W_A), BF16),
        scratch_shapes=[pltpu.VMEM((KSPAN, W_A), BF16), pltpu.VMEM((KSPAN, W_A), BF16),
                        pltpu.VMEM((QBLK // CHUNK, H_A, CHUNK, BIAS_W), F32)],
        compiler_params=_params("parallel", "arbitrary"),
        name="attn_prompt",
    )(q, k, k, k, v, v, v, sz, bias)


STEP_STREAMS = 4


def _attn_step_kernel(q_ref, kn_ref, vn_ref, kp_ref, vp_ref, sz_ref, bias_ref, o_ref):
    s_len = q_ref.shape[1]
    for b in range(STEP_STREAMS):
        scores, values = [], []
        for h in range(H_A):
            lanes = slice(h * HEAD_DIM, (h + 1) * HEAD_DIM)
            q = q_ref[b, :, lanes]
            scores.append([(_dot(q, kp_ref[b, h].astype(BF16)), bias_ref[h, :s_len, :BAND]),
                           (_dot_nt(q, kn_ref[b, :, lanes]), bias_ref[h, :s_len, BAND:BAND + s_len])])
            values.append([functools.partial(_dot_nt, b=vp_ref[b, h].astype(BF16)), vn_ref[b, :, lanes]])
        o = jnp.concatenate(_softmax_pv_staged(scores, values), axis=-1)
        o_ref[b] = (o * sz_ref[b].astype(F32)).astype(BF16)


def _attn_step(q, k_new, v_new, k_past_t, v_past_t, sz, bias):
    b, s_len, _ = q.shape
    assert k_past_t.shape[1:] == (H_A, HEAD_DIM, BAND) and s_len <= CHUNK and b % STEP_STREAMS == 0
    new = pl.BlockSpec((STEP_STREAMS, s_len, W_A), lambda i: (i, 0, 0))
    past = pl.BlockSpec((STEP_STREAMS, H_A, HEAD_DIM, BAND), lambda i: (i, 0, 0, 0))
    return pl.pallas_call(
        _attn_step_kernel,
        grid=(b // STEP_STREAMS,),
        in_specs=[new, new, new, past, past, new, _full((H_A, CHUNK, BIAS_W))],
        out_specs=new,
        out_shape=jax.ShapeDtypeStruct((b, s_len, W_A), BF16),
        compiler_params=_params("parallel"),
        name="attn_step",
    )(q, k_new, v_new, k_past_t, v_past_t, sz, bias)


def _mem_kv_kernel(mem_ref, gmem_ref, w_ref, gkm_ref, mk_ref, mv_ref):
    h = _rms_rows(mem_ref[0], gmem_ref[...]).astype(BF16)
    kv = _dot(h, w_ref[...])
    mk_ref[0] = _head_rms(kv[:, :W_M], gkm_ref[...], HEAD_DIM)
    mv_ref[0] = kv[:, W_M:]


def _mem_kv(mem, wts):
    b = mem.shape[0]
    out = pl.BlockSpec((1, N_MEM, W_M), lambda i: (i, 0, 0))
    return pl.pallas_call(
        _mem_kv_kernel,
        grid=(b,),
        in_specs=[pl.BlockSpec((1, N_MEM, D_MODEL), lambda i: (i, 0, 0)), _full((1, D_MODEL)),
                  _full((D_MODEL, 2 * W_M)), _full((1, W_M))],
        out_specs=[out, out],
        out_shape=[jax.ShapeDtypeStruct((b, N_MEM, W_M), F32)] * 2,
        compiler_params=_params("parallel"),
        name="mem_kv",
    )(mem, wts["g_mem"], wts["w_mem_kv"], wts["g_km"])


MEM_ROWS = 128
MEM_STREAMS = 8


def _attn_mem_kernel(q_ref, mk_ref, mv_ref, sz_ref, o_ref):
    nb, tq, _ = q_ref.shape
    step = min(tq, MEM_ROWS)
    pairs = [slice(p * LANES, (p + 1) * LANES) for p in range(H_M // 2)]
    for b in range(nb):
        mk = [[mk_ref[b, :, lanes].astype(BF16)] for lanes in pairs]
        mv = [[mv_ref[b, :, lanes].astype(BF16)] for lanes in pairs]
        blocks = [(slice(r, r + step), p) for r in range(0, tq, step) for p in range(len(pairs))]
        outs = _head_pairs_attention([q_ref[b, rows, pairs[p]] for rows, p in blocks],
                                     [mk[p] for _, p in blocks], [mv[p] for _, p in blocks],
                                     lambda n, e, i: None)
        for (rows, p), o in zip(blocks, outs):
            o_ref[b, rows, pairs[p]] = (o * sz_ref[b, rows, pairs[p]].astype(F32)).astype(BF16)


def _attn_mem(q, mk, mv, sz, tq, nb):
    b, t, _ = q.shape
    assert b % nb == 0 and t % tq == 0
    rows = pl.BlockSpec((nb, tq, W_M), lambda bi, i: (bi, i, 0))
    mem = pl.BlockSpec((nb, N_MEM, W_M), lambda bi, i: (bi, 0, 0))
    return pl.pallas_call(
        _attn_mem_kernel,
        grid=(b // nb, t // tq),
        in_specs=[rows, mem, mem, rows],
        out_specs=rows,
        out_shape=jax.ShapeDtypeStruct((b, t, W_M), BF16),
        compiler_params=_params("parallel", "parallel"),
        name="attn_mem",
    )(q, mk, mv, sz)


def _attn_mem_step_kernel(q_ref, mk_ref, mv_ref, sz_ref, o_ref):
    for b in range(q_ref.shape[0]):
        scores, values = [], []
        for h in range(H_M):
            lanes = slice(h * HEAD_DIM, (h + 1) * HEAD_DIM)
            scores.append([(_dot(q_ref[b, :, lanes], mk_ref[b, h].astype(BF16)), None)])
            values.append([functools.partial(_dot_nt, b=mv_ref[b, h].astype(BF16))])
        o = jnp.concatenate(_softmax_pv_staged(scores, values), axis=-1)
        o_ref[b] = (o * sz_ref[b].astype(F32)).astype(BF16)


def _attn_mem_step(q, mk_t, mv_t, sz):
    b, t, _ = q.shape
    nb = min(b, MEM_STREAMS)
    assert b % nb == 0 and mk_t.shape[1:] == (H_M, HEAD_DIM, N_MEM)
    rows = pl.BlockSpec((nb, t, W_M), lambda i: (i, 0, 0))
    mem = pl.BlockSpec((nb, H_M, HEAD_DIM, N_MEM), lambda i: (i, 0, 0, 0))
    return pl.pallas_call(
        _attn_mem_step_kernel,
        grid=(b // nb,),
        in_specs=[rows, mem, mem, rows],
        out_specs=rows,
        out_shape=jax.ShapeDtypeStruct((b, t, W_M), BF16),
        compiler_params=_params("parallel"),
        name="attn_mem_step",
    )(q, mk_t, mv_t, sz)


def _gla_intra(items, group):
    n = len(items)
    qs, ks, vs, v16s = ([it[j] for it in items] for j in range(4))
    c = qs[0].shape[0]
    las = [it[4] * LOG2E for it in items]
    r_cc = lax.broadcasted_iota(jnp.int32, (c, c), 0)
    c_cc = lax.broadcasted_iota(jnp.int32, (c, c), 1)
    causal = c_cc <= r_cc
    if group < c:
        causal = causal & (r_cc // group == c_cc // group)
    causal16 = causal.astype(BF16)
    cums = [_dot_exact_lhs(causal16, la) for la in las]
    cumxs = [cum - la for cum, la in zip(cums, las)]
    r_ck = lax.broadcasted_iota(jnp.int32, (c, W_BK), 0)

    head_of_k = lax.broadcasted_iota(jnp.int32, (W_BK, W_BV), 0) // DK_B
    head_of_v = lax.broadcasted_iota(jnp.int32, (W_BK, W_BV), 1) // DV_B
    spread = (head_of_k == head_of_v).astype(BF16)
    os = [_dot((q * k).astype(BF16), spread) * v for q, k, v in zip(qs, ks, vs)]

    def shifted(x, j):
        return pltpu.roll(x.reshape(c // SUBLANES, SUBLANES, W_BK), j % SUBLANES, 1).reshape(c, W_BK)

    def block_edges(cum, cumx, half):
        if half >= SUBLANES:
            nb = c // half
            first = jnp.broadcast_to(cumx.reshape(nb, half, W_BK)[:, 0:1, :], (nb, half, W_BK))
            last = jnp.broadcast_to(cum.reshape(nb, half, W_BK)[:, half - 1:half, :], (nb, half, W_BK))
            return first.reshape(c, W_BK), last.reshape(c, W_BK)
        r_in = r_ck % half
        first, last = cumx, cum
        for j in range(1, half):
            first = jnp.where(r_in == j, shifted(cumx, j), first)
            last = jnp.where(r_in == half - 1 - j, shifted(cum, -j), last)
        return first, last

    head_of_lane = lax.broadcasted_iota(jnp.int32, (c, W_BK), 1) // DK_B
    head_lanes = [jnp.where(head_of_lane == h, 1.0, 0.0).astype(BF16) for h in range(H_B)]
    r_att = lax.broadcasted_iota(jnp.int32, (c, H_B * c), 0)
    s_att = lax.broadcasted_iota(jnp.int32, (c, H_B * c), 1) % c
    atts = [jnp.zeros((c, H_B * c), F32) for _ in range(n)]
    half = 1
    while half < group:
        edges = [block_edges(cum, cumx, half) for cum, cumx in zip(cums, cumxs)]
        odd = (r_ck // half) % 2 == 1
        qts = [jnp.where(odd, q * jnp.exp2(cum - first), 0.0).astype(BF16)
               for q, cum, (first, _) in zip(qs, cums, edges)]
        kts = [jnp.where(odd, 0.0, k * jnp.exp2(last - cum)).astype(BF16)
               for k, cum, (_, last) in zip(ks, cums, edges)]
        prods = [_dot_nt(qt, jnp.concatenate([kt * m for m in head_lanes], axis=0)) for qt, kt in zip(qts, kts)]
        if 2 * half < c:
            same = r_att // (2 * half) == s_att // (2 * half)
            prods = [jnp.where(same, a, 0.0) for a in prods]
        atts = [att + a for att, a in zip(atts, prods)]
        half *= 2
    if group > 1:
        att16s = [att.astype(BF16) for att in atts]
        os = [o + jnp.concatenate([_dot(att16[:, h * c:(h + 1) * c], v16[:, h * DV_B:(h + 1) * DV_B])
                                   for h in range(H_B)], axis=-1)
              for o, att16, v16 in zip(os, att16s, v16s)]
    return list(zip(os, cums))


def _gla_finish(o, g_ref, sz):
    outs = [_rms_rows(o[:, h * DV_B:(h + 1) * DV_B], g_ref[:, h * DV_B:(h + 1) * DV_B]) for h in range(H_B)]
    return (jnp.concatenate(outs, axis=-1) * sz.astype(F32)).astype(BF16)


def _gla_kernel(q_ref, k_ref, v_ref, la_ref, sz_ref, g_ref, s0_ref, o_ref, sfin_ref, s_scr):
    t_idx = pl.program_id(1)
    c = GLA_C
    streams = range(q_ref.shape[0])

    @pl.when(t_idx == 0)
    def _():
        for b in streams:
            s_scr[b] = s0_ref[b].reshape(W_BK, DV_B)

    qs = [q_ref[b].astype(F32) for b in streams]
    ks = [k_ref[b].astype(F32) for b in streams]
    v16s = [v_ref[b] for b in streams]
    intra = _gla_intra([(qs[b], ks[b], v16s[b].astype(F32), v16s[b], la_ref[b]) for b in streams], c)

    q_ins = [(qs[b] * jnp.exp2(intra[b][1])).astype(BF16) for b in streams]
    k_out_ts = [(ks[b] * jnp.exp2(intra[b][1][c - 1:c, :] - intra[b][1])).T.astype(BF16) for b in streams]
    keeps = [jnp.exp2(intra[b][1].T[:, c - 1:c]) for b in streams]
    s_olds = [s_scr[b] for b in streams]
    inters, s_news = [], []
    for b in streams:
        kd = lambda h: slice(h * DK_B, (h + 1) * DK_B)
        vd = lambda h: slice(h * DV_B, (h + 1) * DV_B)
        inters.append(jnp.concatenate(
            [_dot(q_ins[b][:, kd(h)], s_olds[b][kd(h)].astype(BF16)) for h in range(H_B)], axis=-1))
        s_news.append(jnp.concatenate(
            [s_olds[b][kd(h)] * keeps[b][kd(h)] + _dot(k_out_ts[b][kd(h)], v16s[b][:, vd(h)])
             for h in range(H_B)], axis=0))
    for b in streams:
        s_scr[b] = s_news[b]
        o_ref[b] = _gla_finish(intra[b][0] + inters[b], g_ref, sz_ref[b])

    @pl.when(t_idx == pl.num_programs(1) - 1)
    def _():
        for b in streams:
            sfin_ref[b] = s_news[b].reshape(H_B, DK_B, DV_B)


GLA_STREAMS = 2


def _gla(q, k, v, la, sz, g, s0):
    b, t, _ = q.shape
    nb = GLA_STREAMS if b % GLA_STREAMS == 0 else 1
    assert t % GLA_C == 0
    blk = lambda w: pl.BlockSpec((nb, GLA_C, w), lambda bi, i: (bi, i, 0))
    state = pl.BlockSpec((nb, H_B, DK_B, DV_B), lambda bi, i: (bi, 0, 0, 0))
    return pl.pallas_call(
        _gla_kernel,
        grid=(b // nb, t // GLA_C),
        in_specs=[blk(W_BK), blk(W_BK), blk(W_BV), blk(W_BK), blk(W_BV),
                  pl.BlockSpec((1, W_BV), lambda bi, i: (0, 0)), state],
        out_specs=[blk(W_BV), state],
        out_shape=[jax.ShapeDtypeStruct((b, t, W_BV), BF16),
                   jax.ShapeDtypeStruct((b, H_B, DK_B, DV_B), F32)],
        scratch_shapes=[pltpu.VMEM((nb, W_BK, DV_B), F32)],
        compiler_params=_params("parallel", "arbitrary"),
        name="gla",
    )(q, k, v, la, sz, g, s0)


def _gla_step_kernel(q_ref, k_ref, v_ref, la_ref, sz_ref, g_ref, s0_ref, o_ref, s_ref, *, t):
    c = GLA_C
    n = c // t
    q, k, la = q_ref[...].astype(F32), k_ref[...].astype(F32), la_ref[...]
    v16 = v_ref[...]
    (o, cum), = _gla_intra([(q, k, v16.astype(F32), v16, la)], t)

    q_in = (q * jnp.exp2(cum)).astype(BF16)
    cum_t, k_t = cum.T, k.T
    stream_of_col = lax.broadcasted_iota(jnp.int32, (W_BK, c), 1) // t
    inter = []
    for s in range(n):
        rows = slice(s * t, (s + 1) * t)
        last = cum_t[:, (s + 1) * t - 1:(s + 1) * t]
        k_out_t = (k_t * jnp.exp2(jnp.where(stream_of_col == s, last - cum_t, NEG))).astype(BF16)
        keep = jnp.exp2(last)
        o_s = []
        for h in range(H_B):
            ks = slice(h * DK_B, (h + 1) * DK_B)
            vs = slice(h * DV_B, (h + 1) * DV_B)
            s_old = s0_ref[s, h]
            o_s.append(_dot(q_in[rows, ks], s_old.astype(BF16)))
            s_ref[s, h] = s_old * keep[ks] + _dot(k_out_t[ks], v16[:, vs])
        inter.append(jnp.concatenate(o_s, axis=-1))
    o_ref[...] = _gla_finish(o + jnp.concatenate(inter, axis=0), g_ref, sz_ref[...])


def _gla_step(q, k, v, la, sz, g, s0):
    b, t, _ = q.shape
    n = GLA_C // t
    assert GLA_C % t == 0 and t % (2 * GLA_BASE) == 0 and b % n == 0
    flat = lambda z: z.reshape(b * t, z.shape[-1])
    blk = lambda w: pl.BlockSpec((GLA_C, w), lambda i: (i, 0))
    state = pl.BlockSpec((n, H_B, DK_B, DV_B), lambda i: (i, 0, 0, 0))
    o, s_new = pl.pallas_call(
        functools.partial(_gla_step_kernel, t=t),
        grid=(b // n,),
        in_specs=[blk(W_BK), blk(W_BK), blk(W_BV), blk(W_BK), blk(W_BV),
                  pl.BlockSpec((1, W_BV), lambda i: (0, 0)), state],
        out_specs=[blk(W_BV), state],
        out_shape=[jax.ShapeDtypeStruct((b * t, W_BV), BF16),
                   jax.ShapeDtypeStruct((b, H_B, DK_B, DV_B), F32)],
        compiler_params=_params("parallel"),
        name="gla_step",
    )(flat(q), flat(k), flat(v), flat(la), flat(sz), g, s0)
    return o.reshape(b, t, W_BV), s_new


def _prep_weights(l, norm_in, w_in, g_qa, g_ka, w_gate2, b_gate, g_gla_out, g_mem, w_mem_kv, g_qm, g_km,
                  w_up_a, w_up_b, w_up_m, w_out):
    tile = lambda gain, n: jnp.tile(gain, n).reshape(1, -1)
    return {
        "norm_in": norm_in[l].reshape(1, D_MODEL),
        "w_in_t": w_in[l].T.astype(BF16),
        "w_gate2": w_gate2[l].astype(BF16),
        "b_gate": b_gate[l].reshape(1, W_BK),
        "g_qa": tile(g_qa[l], H_A), "g_ka": tile(g_ka[l], H_A), "g_qm": tile(g_qm[l], H_M),
        "g_km": tile(g_km[l], H_M), "g_gla": tile(g_gla_out[l], H_B),
        "g_mem": g_mem[l].reshape(1, D_MODEL),
        "w_mem_kv": w_mem_kv[l].astype(BF16),
        "w_up_a": w_up_a[l].astype(BF16), "w_up_b": w_up_b[l].astype(BF16),
        "w_up_m": w_up_m[l].astype(BF16), "w_out": w_out[l].astype(BF16),
    }


def _layer(x, wts, attend_a, attend_m, s0, tm):
    b, t, _ = x.shape
    keep = min(BAND, t)
    x2d = x.reshape(b * t, D_MODEL)
    (qa, ka, va, ka16, va16, sza, qb, kb, vb, la, szb, qm, szm, ga, gb, gm) = _front(x2d, wts, tm, t, keep)
    r3 = lambda z: z.reshape(b, t, z.shape[-1])
    out_a = attend_a(r3(qa), r3(ka16), r3(va16), r3(sza))
    gla = _gla if t >= GLA_C else _gla_step
    out_b, s_new = gla(r3(qb), r3(kb), r3(vb), r3(la), r3(szb), wts["g_gla"], s0)
    out_m = attend_m(r3(qm), r3(szm))
    flat = lambda z: z.reshape(b * t, z.shape[-1])
    y = _back(x2d, flat(out_a), flat(out_b), flat(out_m), ga, gb, gm, wts, tm)
    heads = lambda z: z.reshape(b, keep, H_A, HEAD_DIM)
    return y.reshape(b, t, D_MODEL), heads(ka), heads(va), s_new


def kernel(x_prompt, x_sample, mem_prompt, cache_a_k, cache_a_v, state_gla, cache_mem_k, cache_mem_v,
           norm_in, w_in, g_qa, g_ka, rel_bias, w_gate2, b_gate, g_gla_out, g_mem, w_mem_kv, g_qm, g_km,
           w_up_a, w_up_b, w_up_m, w_out):
    depth = w_in.shape[0]
    xp, xs = x_prompt, x_sample
    bp, tp, _ = xp.shape
    bs, ts, _ = xs.shape
    akp, avp, sgp, mkp, mvp, aks, avs, sgs = [], [], [], [], [], [], [], []
    for l in range(depth):
        wts = _prep_weights(l, norm_in, w_in, g_qa, g_ka, w_gate2, b_gate, g_gla_out, g_mem, w_mem_kv,
                            g_qm, g_km, w_up_a, w_up_b, w_up_m, w_out)
        bias = _bias_tile(rel_bias[l])
        mk, mv = _mem_kv(mem_prompt, wts)
        attend_p = lambda q, k, v, sz: _attn_prompt(q, k, v, sz, bias)
        s0 = jnp.zeros((bp, H_B, DK_B, DV_B), state_gla.dtype)
        attend_mp = lambda q, sz: _attn_mem(q, mk, mv, sz, min(tp, 512), 1)
        xp, ka, va, sp = _layer(xp, wts, attend_p, attend_mp, s0, 512)
        akp.append(ka)
        avp.append(va)
        sgp.append(sp)
        mkp.append(mk.reshape(bp, N_MEM, H_M, HEAD_DIM))
        mvp.append(mv.reshape(bp, N_MEM, H_M, HEAD_DIM))
        rows_last = lambda z: jnp.transpose(z, (0, 2, 3, 1))
        past_k, past_v = rows_last(cache_a_k[l]), rows_last(cache_a_v[l])
        mem_k, mem_v = rows_last(cache_mem_k[l]), rows_last(cache_mem_v[l])
        attend_s = lambda q, k, v, sz: _attn_step(q, k, v, past_k, past_v, sz, bias)
        attend_ms = lambda q, sz: _attn_mem_step(q, mem_k, mem_v, sz)
        xs, ka_s, va_s, ss = _layer(xs, wts, attend_s, attend_ms, state_gla[l], 256)
        aks.append(ka_s)
        avs.append(va_s)
        sgs.append(ss)
    return (xp, xs, jnp.stack(akp), jnp.stack(avp), jnp.stack(sgp), jnp.stack(mkp), jnp.stack(mvp),
            jnp.stack(aks), jnp.stack(avs), jnp.stack(sgs))
```

```python
import functools

import jax
import jax.numpy as jnp
from jax import lax
from jax.experimental import pallas as pl
from jax.experimental.pallas import tpu as pltpu

F32 = jnp.float32
BF16 = jnp.bfloat16

D_MODEL = 1024
CHUNK = 64
LEFT_CHUNKS = 8
HEAD_DIM = 64
H_A = 8
W_A = H_A * HEAD_DIM
REL_CLIP = 128
H_B = 4
DK_B = 64
DV_B = 128
W_BK = H_B * DK_B
W_BV = H_B * DV_B
GATE_RANK = 16
GATE_TAU = 16.0
N_MEM = 256
H_M = 4
W_M = H_M * HEAD_DIM
EPS = 1e-6
IN_SIZES = (W_A, W_A, W_A, W_A, W_BK, W_BK, W_BV, GATE_RANK, W_BV, W_M, W_M, D_MODEL, D_MODEL, D_MODEL)

LANES = 128
SUBLANES = 8
VMEM_LIMIT = 56 * 1024 * 1024
NEG = -1e30
LOG2E = 1.4426950408889634

BAND = LEFT_CHUNKS * CHUNK
QBLK = 256
KSPAN = BAND + QBLK
WINDOW = BAND + CHUNK
BIAS_W = BAND + 2 * CHUNK
ATTN_CHUNKS = 1
GLA_C = 128
GLA_BASE = SUBLANES


def _dot(a, b):
    return jnp.dot(a, b, preferred_element_type=F32)


def _dot_nt(a, b):
    return lax.dot_general(a, b, (((1,), (1,)), ((), ())), preferred_element_type=F32)


def _split3(x):
    hi = x.astype(BF16)
    r = x - hi.astype(F32)
    mid = r.astype(BF16)
    lo = (r - mid.astype(F32)).astype(BF16)
    return hi, mid, lo


def _dot_exact_lhs(a01, x):
    hi, mid, lo = _split3(x)
    return _dot(a01, hi) + _dot(a01, mid) + _dot(a01, lo)


def _sigmoid(z):
    return 1.0 / (1.0 + jnp.exp(-z))


def _silu(z):
    return z * _sigmoid(z)


def _log_sigmoid(z):
    return jnp.minimum(z, 0.0) - jnp.log1p(jnp.exp(-jnp.abs(z)))


def _rms_rows(x, g):
    ms = jnp.mean(x * x, axis=-1, keepdims=True)
    return x * lax.rsqrt(ms + EPS) * g


def _head_mean_matrix(width, head):
    r = lax.broadcasted_iota(jnp.int32, (width, width), 0) // head
    c = lax.broadcasted_iota(jnp.int32, (width, width), 1) // head
    return jnp.where(r == c, 1.0 / head, 0.0).astype(BF16)


def _head_rms(x, g, head):
    avg = _head_mean_matrix(x.shape[-1], head)
    ms = _dot((x * x).astype(BF16), avg)
    return x * lax.rsqrt(ms + EPS) * g


def _full(shape):
    nd = len(shape)
    return pl.BlockSpec(shape, lambda *_: (0,) * nd, pipeline_mode=pl.Buffered(1))


def _params(*sem):
    return pltpu.CompilerParams(dimension_semantics=sem, vmem_limit_bytes=VMEM_LIMIT)


_IN_OFF = tuple(int(sum(IN_SIZES[:i])) for i in range(len(IN_SIZES) + 1))
D_IN = _IN_OFF[-1]


def _front_kernel(x_ref, nin_ref, w_ref, wg2_ref, bg_ref, gqa_ref, gka_ref, gqm_ref, *rest, per_stream=None):
    fused = per_stream is not None
    if fused:
        (ggla_ref, s0_ref, qa_o, ka_o, va_o, ka16_o, va16_o, sza_o, outb_o, sfin_o, qm_o, szm_o,
         ga_o, gb_o, gm_o, s_scr) = rest
    else:
        (qa_o, ka_o, va_o, ka16_o, va16_o, sza_o, qb_o, kb_o, vb_o, la_o, szb_o, qm_o, szm_o,
         ga_o, gb_o, gm_o) = rest
    if fused:
        @pl.when(pl.program_id(0) % per_stream == 0)
        def _():
            s_scr[...] = s0_ref[pl.program_id(0) // per_stream].reshape(W_BK, DV_B)

    h = _rms_rows(x_ref[...], nin_ref[...]).astype(BF16)

    def proj(i):
        return _dot_nt(h, w_ref[_IN_OFF[i]:_IN_OFF[i + 1], :])

    qb = (proj(4) * (DK_B ** -0.5)).astype(BF16)
    kb = proj(5).astype(BF16)
    vb = proj(6).astype(BF16)
    z = _dot(proj(7).astype(BF16), wg2_ref[...]) + bg_ref[...]
    la = _log_sigmoid(z) * (1.0 / GATE_TAU)
    szb = _silu(proj(8)).astype(BF16)

    scale = HEAD_DIM ** -0.5 * LOG2E

    def emit_qa():
        qa_o[...] = (_head_rms(proj(0), gqa_ref[...], HEAD_DIM) * scale).astype(BF16)

    def emit_ka():
        ka = _head_rms(proj(1), gka_ref[...], HEAD_DIM)
        ka_o[...] = ka
        ka16_o[...] = ka.astype(BF16)

    def emit_va():
        va = proj(2)
        va_o[...] = va
        va16_o[...] = va.astype(BF16)

    def emit_sza():
        sza_o[...] = _silu(proj(3)).astype(BF16)

    def emit_qm():
        qm_o[...] = (_head_rms(proj(9), gqm_ref[...], HEAD_DIM) * scale).astype(BF16)

    def emit_szm():
        szm_o[...] = _silu(proj(10)).astype(BF16)

    def emit_gate(i, ref):
        ref[...] = _sigmoid(proj(i)).astype(BF16)

    pending = [emit_qa, emit_ka, emit_va, emit_sza, emit_qm, emit_szm,
               functools.partial(emit_gate, 11, ga_o), functools.partial(emit_gate, 12, gb_o),
               functools.partial(emit_gate, 13, gm_o)]

    def emit_next():
        if pending:
            pending.pop(0)()

    if fused:
        chunks = [slice(r, r + GLA_C) for r in range(0, x_ref.shape[0], GLA_C)]
        items = [(qb[rows].astype(F32), kb[rows].astype(F32), vb[rows].astype(F32), vb[rows], la[rows])
                 for rows in chunks]
        state = s_scr[...]
        outs = []
        for rows, item, (o, cum) in zip(chunks, items, _gla_intra(items, GLA_C, emit_next)):
            inter, state = _gla_state_step(item[0], item[1], item[3], cum, state)
            outs.append(_gla_finish(o + inter, ggla_ref, szb[rows]))
            emit_next()
        s_scr[...] = state
        outb_o[...] = jnp.concatenate(outs, axis=0)
    else:
        qb_o[...] = qb
        kb_o[...] = kb
        vb_o[...] = vb
        la_o[...] = la
        szb_o[...] = szb
    while pending:
        emit_next()

    if fused:
        @pl.when(pl.program_id(0) % per_stream == per_stream - 1)
        def _():
            sfin_o[0] = s_scr[...].reshape(H_B, DK_B, DV_B)


def _front(x2d, wts, tm, t, keep, s0=None):
    n = x2d.shape[0]
    tm = min(tm, n)
    row = lambda w: pl.BlockSpec((tm, w), lambda i: (i, 0))
    if keep == t:
        kv_rows, kv = n, row(W_A)
    else:
        assert t % tm == 0 and keep % tm == 0
        per_stream, kept = t // tm, keep // tm
        kv_rows = (n // t) * keep
        kv = pl.BlockSpec((tm, W_A), lambda i: (
            (i // per_stream) * kept + jnp.maximum(i % per_stream - (per_stream - kept), 0), 0))
    out = lambda rows_, w, d, spec: (jax.ShapeDtypeStruct((rows_, w), d), spec)
    head = [out(n, W_A, BF16, row(W_A)), out(kv_rows, W_A, F32, kv), out(kv_rows, W_A, F32, kv),
            out(n, W_A, BF16, row(W_A)), out(n, W_A, BF16, row(W_A)), out(n, W_A, BF16, row(W_A))]
    tail = [out(n, W_M, BF16, row(W_M)), out(n, W_M, BF16, row(W_M))] + [out(n, D_MODEL, BF16, row(D_MODEL))] * 3
    in_specs = [row(D_MODEL), _full((1, D_MODEL)), _full((D_IN, D_MODEL)),
                _full((GATE_RANK, W_BK)), _full((1, W_BK)), _full((1, W_A)), _full((1, W_A)), _full((1, W_M))]
    args = [x2d, wts["norm_in"], wts["w_in_t"], wts["w_gate2"], wts["b_gate"], wts["g_qa"], wts["g_ka"], wts["g_qm"]]
    if s0 is None:
        mid = [out(n, W_BK, BF16, row(W_BK)), out(n, W_BK, BF16, row(W_BK)), out(n, W_BV, BF16, row(W_BV)),
               out(n, W_BK, F32, row(W_BK)), out(n, W_BV, BF16, row(W_BV))]
        body, scratch = _front_kernel, []
    else:
        assert t % tm == 0 and tm % GLA_C == 0
        streams = n // t
        state = (streams, H_B, DK_B, DV_B)
        mid = [out(n, W_BV, BF16, row(W_BV)),
               (jax.ShapeDtypeStruct(state, F32),
                pl.BlockSpec((1,) + state[1:], lambda i: (i // (t // tm), 0, 0, 0)))]
        in_specs += [_full((1, W_BV)), _full(state)]
        args += [wts["g_gla"], s0]
        body, scratch = functools.partial(_front_kernel, per_stream=t // tm), [pltpu.VMEM((W_BK, DV_B), F32)]
    outs = head + mid + tail
    return pl.pallas_call(
        body,
        grid=(n // tm,),
        in_specs=in_specs,
        out_specs=[spec for _, spec in outs],
        out_shape=[shape for shape, _ in outs],
        scratch_shapes=scratch,
        compiler_params=_params("arbitrary"),
        name="front",
    )(*args)


def _back_kernel(x_ref, a_ref, b_ref, m_ref, ga_ref, gb_ref, gm_ref, wa_ref, wb_ref, wm_ref, wo_ref, y_ref):
    u = (ga_ref[...].astype(F32) * _dot(a_ref[...], wa_ref[...])
         + gb_ref[...].astype(F32) * _dot(b_ref[...], wb_ref[...])
         + gm_ref[...].astype(F32) * _dot(m_ref[...], wm_ref[...]))
    y_ref[...] = x_ref[...] + _dot(u.astype(BF16), wo_ref[...])


def _back(x2d, a, b, m, ga, gb, gm, wts, tm):
    n = x2d.shape[0]
    tm = min(tm, n)
    row = lambda w: pl.BlockSpec((tm, w), lambda i: (i, 0))
    return pl.pallas_call(
        _back_kernel,
        grid=(n // tm,),
        in_specs=[row(D_MODEL), row(W_A), row(W_BV), row(W_M), row(D_MODEL), row(D_MODEL), row(D_MODEL),
                  _full((W_A, D_MODEL)), _full((W_BV, D_MODEL)), _full((W_M, D_MODEL)),
                  _full((D_MODEL, D_MODEL))],
        out_specs=row(D_MODEL),
        out_shape=jax.ShapeDtypeStruct((n, D_MODEL), F32),
        compiler_params=_params("parallel"),
        name="back",
    )(x2d, a, b, m, ga, gb, gm, wts["w_up_a"], wts["w_up_b"], wts["w_up_m"], wts["w_out"])


def _bias_kernel(e_ref, tile_ref):
    e = jnp.broadcast_to(e_ref[0] * LOG2E, (CHUNK, 2 * BAND))
    tile_ref[0] = pltpu.roll(e, 0, 1, stride=1, stride_axis=0)[:, :BIAS_W]


def _bias_tile(rel_bias):
    j = jnp.arange(2 * BAND)
    dist = jnp.where(j < BIAS_W, BAND - j, REL_CLIP)
    e = rel_bias[:, jnp.clip(dist, -REL_CLIP, REL_CLIP) + REL_CLIP].reshape(H_A, 1, 2 * BAND)
    return pl.pallas_call(
        _bias_kernel,
        grid=(H_A,),
        in_specs=[pl.BlockSpec((1, 1, 2 * BAND), lambda h: (h, 0, 0))],
        out_specs=pl.BlockSpec((1, CHUNK, BIAS_W), lambda h: (h, 0, 0)),
        out_shape=jax.ShapeDtypeStruct((H_A, CHUNK, BIAS_W), F32),
        compiler_params=_params("parallel"),
        name="bias_tile",
    )(e)


def _softmax_pv_staged(scores, values):
    biased = [[s if b is None else s + b for s, b in s_list] for s_list in scores]
    tops = [functools.reduce(jnp.maximum, [jnp.max(s, axis=-1, keepdims=True) for s in s_list])
            for s_list in biased]
    probs = [[jnp.exp2(s - m) for s in s_list] for s_list, m in zip(biased, tops)]
    sums = [functools.reduce(jnp.add, [jnp.sum(x, axis=-1, keepdims=True) for x in p_list]) for p_list in probs]
    return [functools.reduce(jnp.add, [(v(x.astype(BF16)) if callable(v) else _dot(x.astype(BF16), v))
                                       for x, v in zip(p_list, v_list)]) / l
            for p_list, v_list, l in zip(probs, values, sums)]


def _head_pairs_attention(q_pairs, k_lists, v_lists, bias_fn):
    m_rows = q_pairs[0].shape[0]
    lane = lax.broadcasted_iota(jnp.int32, (m_rows, LANES), 1)
    first = lane < HEAD_DIM
    scores = []
    for p, qp in enumerate(q_pairs):
        zero = jnp.zeros_like(qp)
        q2 = jnp.concatenate([jnp.where(first, qp, zero), jnp.where(first, zero, qp)], axis=0)
        s_list = []
        for i, k in enumerate(k_lists[p]):
            b0, b1 = bias_fn(p, 0, i), bias_fn(p, 1, i)
            s_list.append((_dot_nt(q2, k), None if b0 is None else jnp.concatenate([b0, b1], axis=0)))
        scores.append(s_list)
    outs = _softmax_pv_staged(scores, v_lists)
    return [jnp.where(first, o[:m_rows], o[m_rows:]) for o in outs]


def _attn_prompt_kernel(q_ref, k0_ref, k1_ref, k2_ref, v0_ref, v1_ref, v2_ref, sz_ref, bias_ref, o_ref,
                        k_buf, v_buf, bias_buf):
    i = pl.program_id(1)
    for j, (k_ref, v_ref) in enumerate(((k0_ref, v0_ref), (k1_ref, v1_ref), (k2_ref, v2_ref))):
        k_buf[j * QBLK:(j + 1) * QBLK, :] = k_ref[0]
        v_buf[j * QBLK:(j + 1) * QBLK, :] = v_ref[0]

    @pl.when(i <= BAND // QBLK)
    def _():
        col = lax.broadcasted_iota(jnp.int32, (1, BIAS_W), 1)
        for c in range(QBLK // CHUNK):
            before_start = jnp.where(col + c * CHUNK < (BAND // QBLK - i) * QBLK, NEG, 0.0)
            for h in range(H_A):
                bias_buf[c, h] = bias_ref[h] + before_start

    pairs = [slice(p * LANES, (p + 1) * LANES) for p in range(H_A // 2)]
    for c0 in range(0, QBLK // CHUNK, ATTN_CHUNKS):
        blocks = [(c, p) for c in range(c0, c0 + ATTN_CHUNKS) for p in range(len(pairs))]
        rows = lambda c: slice(c * CHUNK, (c + 1) * CHUNK)
        win = lambda c: slice(c * CHUNK, c * CHUNK + WINDOW)
        outs = _head_pairs_attention(
            [q_ref[0, rows(c), pairs[p]] for c, p in blocks],
            [[k_buf[win(c), pairs[p]]] for c, p in blocks], [[v_buf[win(c), pairs[p]]] for c, p in blocks],
            lambda n, e, _: bias_buf[blocks[n][0], 2 * blocks[n][1] + e, :, :WINDOW])
        for (c, p), o in zip(blocks, outs):
            o_ref[0, rows(c), pairs[p]] = (o * sz_ref[0, rows(c), pairs[p]].astype(F32)).astype(BF16)


def _attn_prompt(q, k, v, sz, bias):
    b, t, _ = q.shape
    nb = BAND // QBLK
    cur = pl.BlockSpec((1, QBLK, W_A), lambda bi, i: (bi, i, 0))
    prev = lambda d: pl.BlockSpec((1, QBLK, W_A), lambda bi, i: (bi, jnp.maximum(i - d, 0), 0))
    kv_specs = [prev(nb - j) for j in range(nb)] + [cur]
    assert nb == 2
    return pl.pallas_call(
        _attn_prompt_kernel,
        grid=(b, t // QBLK),
        in_specs=[cur] + kv_specs + kv_specs + [cur, _full((H_A, CHUNK, BIAS_W))],
        out_specs=cur,
        out_shape=jax.ShapeDtypeStruct((b, t, W_A), BF16),
        scratch_shapes=[pltpu.VMEM((KSPAN, W_A), BF16), pltpu.VMEM((KSPAN, W_A), BF16),
                        pltpu.VMEM((QBLK // CHUNK, H_A, CHUNK, BIAS_W), F32)],
        compiler_params=_params("parallel", "arbitrary"),
        name="attn_prompt",
    )(q, k, k, k, v, v, v, sz, bias)


STEP_STREAMS = 4


def _attn_step_kernel(q_ref, kn_ref, vn_ref, kp_ref, vp_ref, sz_ref, bias_ref, o_ref):
    s_len = q_ref.shape[1]
    for b in range(STEP_STREAMS):
        scores, values = [], []
        for h in range(H_A):
            lanes = slice(h * HEAD_DIM, (h + 1) * HEAD_DIM)
            q = q_ref[b, :, lanes]
            scores.append([(_dot(q, kp_ref[b, h].astype(BF16)), bias_ref[h, :s_len, :BAND]),
                           (_dot_nt(q, kn_ref[b, :, lanes]), bias_ref[h, :s_len, BAND:BAND + s_len])])
            values.append([functools.partial(_dot_nt, b=vp_ref[b, h].astype(BF16)), vn_ref[b, :, lanes]])
        o = jnp.concatenate(_softmax_pv_staged(scores, values), axis=-1)
        o_ref[b] = (o * sz_ref[b].astype(F32)).astype(BF16)


def _attn_step(q, k_new, v_new, k_past_t, v_past_t, sz, bias):
    b, s_len, _ = q.shape
    assert k_past_t.shape[1:] == (H_A, HEAD_DIM, BAND) and s_len <= CHUNK and b % STEP_STREAMS == 0
    new = pl.BlockSpec((STEP_STREAMS, s_len, W_A), lambda i: (i, 0, 0))
    past = pl.BlockSpec((STEP_STREAMS, H_A, HEAD_DIM, BAND), lambda i: (i, 0, 0, 0))
    return pl.pallas_call(
        _attn_step_kernel,
        grid=(b // STEP_STREAMS,),
        in_specs=[new, new, new, past, past, new, _full((H_A, CHUNK, BIAS_W))],
        out_specs=new,
        out_shape=jax.ShapeDtypeStruct((b, s_len, W_A), BF16),
        compiler_params=_params("parallel"),
        name="attn_step",
    )(q, k_new, v_new, k_past_t, v_past_t, sz, bias)


def _mem_kv_kernel(mem_ref, gmem_ref, w_ref, gkm_ref, mk_ref, mv_ref):
    h = _rms_rows(mem_ref[0], gmem_ref[...]).astype(BF16)
    kv = _dot(h, w_ref[...])
    mk_ref[0] = _head_rms(kv[:, :W_M], gkm_ref[...], HEAD_DIM)
    mv_ref[0] = kv[:, W_M:]


def _mem_kv(mem, wts):
    b = mem.shape[0]
    out = pl.BlockSpec((1, N_MEM, W_M), lambda i: (i, 0, 0))
    return pl.pallas_call(
        _mem_kv_kernel,
        grid=(b,),
        in_specs=[pl.BlockSpec((1, N_MEM, D_MODEL), lambda i: (i, 0, 0)), _full((1, D_MODEL)),
                  _full((D_MODEL, 2 * W_M)), _full((1, W_M))],
        out_specs=[out, out],
        out_shape=[jax.ShapeDtypeStruct((b, N_MEM, W_M), F32)] * 2,
        compiler_params=_params("parallel"),
        name="mem_kv",
    )(mem, wts["g_mem"], wts["w_mem_kv"], wts["g_km"])


MEM_ROWS = 128
MEM_STREAMS = 8


def _attn_mem_kernel(q_ref, mk_ref, mv_ref, sz_ref, o_ref):
    nb, tq, _ = q_ref.shape
    step = min(tq, MEM_ROWS)
    pairs = [slice(p * LANES, (p + 1) * LANES) for p in range(H_M // 2)]
    for b in range(nb):
        mk = [[mk_ref[b, :, lanes].astype(BF16)] for lanes in pairs]
        mv = [[mv_ref[b, :, lanes].astype(BF16)] for lanes in pairs]
        blocks = [(slice(r, r + step), p) for r in range(0, tq, step) for p in range(len(pairs))]
        outs = _head_pairs_attention([q_ref[b, rows, pairs[p]] for rows, p in blocks],
                                     [mk[p] for _, p in blocks], [mv[p] for _, p in blocks],
                                     lambda n, e, i: None)
        for (rows, p), o in zip(blocks, outs):
            o_ref[b, rows, pairs[p]] = (o * sz_ref[b, rows, pairs[p]].astype(F32)).astype(BF16)


def _attn_mem(q, mk, mv, sz, tq, nb):
    b, t, _ = q.shape
    assert b % nb == 0 and t % tq == 0
    rows = pl.BlockSpec((nb, tq, W_M), lambda bi, i: (bi, i, 0))
    mem = pl.BlockSpec((nb, N_MEM, W_M), lambda bi, i: (bi, 0, 0))
    return pl.pallas_call(
        _attn_mem_kernel,
        grid=(b // nb, t // tq),
        in_specs=[rows, mem, mem, rows],
        out_specs=rows,
        out_shape=jax.ShapeDtypeStruct((b, t, W_M), BF16),
        compiler_params=_params("parallel", "parallel"),
        name="attn_mem",
    )(q, mk, mv, sz)


def _attn_mem_step_kernel(q_ref, mk_ref, mv_ref, sz_ref, o_ref):
    for b in range(q_ref.shape[0]):
        scores, values = [], []
        for h in range(H_M):
            lanes = slice(h * HEAD_DIM, (h + 1) * HEAD_DIM)
            scores.append([(_dot(q_ref[b, :, lanes], mk_ref[b, h].astype(BF16)), None)])
            values.append([functools.partial(_dot_nt, b=mv_ref[b, h].astype(BF16))])
        o = jnp.concatenate(_softmax_pv_staged(scores, values), axis=-1)
        o_ref[b] = (o * sz_ref[b].astype(F32)).astype(BF16)


def _attn_mem_step(q, mk_t, mv_t, sz):
    b, t, _ = q.shape
    nb = min(b, MEM_STREAMS)
    assert b % nb == 0 and mk_t.shape[1:] == (H_M, HEAD_DIM, N_MEM)
    rows = pl.BlockSpec((nb, t, W_M), lambda i: (i, 0, 0))
    mem = pl.BlockSpec((nb, H_M, HEAD_DIM, N_MEM), lambda i: (i, 0, 0, 0))
    return pl.pallas_call(
        _attn_mem_step_kernel,
        grid=(b // nb,),
        in_specs=[rows, mem, mem, rows],
        out_specs=rows,
        out_shape=jax.ShapeDtypeStruct((b, t, W_M), BF16),
        compiler_params=_params("parallel"),
        name="attn_mem_step",
    )(q, mk_t, mv_t, sz)


def _gla_intra(items, group, between=lambda: None):
    n = len(items)
    qs, ks, vs, v16s = ([it[j] for it in items] for j in range(4))
    c = qs[0].shape[0]
    las = [it[4] * LOG2E for it in items]
    r_cc = lax.broadcasted_iota(jnp.int32, (c, c), 0)
    c_cc = lax.broadcasted_iota(jnp.int32, (c, c), 1)
    causal = c_cc <= r_cc
    if group < c:
        causal = causal & (r_cc // group == c_cc // group)
    causal16 = causal.astype(BF16)
    cums = [_dot_exact_lhs(causal16, la) for la in las]
    cumxs = [cum - la for cum, la in zip(cums, las)]
    r_ck = lax.broadcasted_iota(jnp.int32, (c, W_BK), 0)

    head_of_k = lax.broadcasted_iota(jnp.int32, (W_BK, W_BV), 0) // DK_B
    head_of_v = lax.broadcasted_iota(jnp.int32, (W_BK, W_BV), 1) // DV_B
    spread = (head_of_k == head_of_v).astype(BF16)
    os = [_dot((q * k).astype(BF16), spread) * v for q, k, v in zip(qs, ks, vs)]

    def shifted(x, j):
        return pltpu.roll(x.reshape(c // SUBLANES, SUBLANES, W_BK), j % SUBLANES, 1).reshape(c, W_BK)

    def block_edges(cum, cumx, half):
        if half >= SUBLANES:
            nb = c // half
            first = jnp.broadcast_to(cumx.reshape(nb, half, W_BK)[:, 0:1, :], (nb, half, W_BK))
            last = jnp.broadcast_to(cum.reshape(nb, half, W_BK)[:, half - 1:half, :], (nb, half, W_BK))
            return first.reshape(c, W_BK), last.reshape(c, W_BK)
        r_in = r_ck % half
        first, last = cumx, cum
        for j in range(1, half):
            first = jnp.where(r_in == j, shifted(cumx, j), first)
            last = jnp.where(r_in == half - 1 - j, shifted(cum, -j), last)
        return first, last

    head_of_row = lax.broadcasted_iota(jnp.int32, (W_BK, c), 0) // DK_B
    head_rows = [jnp.where(head_of_row == h, 1.0, 0.0).astype(BF16) for h in range(H_B)]
    r_att = lax.broadcasted_iota(jnp.int32, (c, H_B * c), 0)
    s_att = lax.broadcasted_iota(jnp.int32, (c, H_B * c), 1) % c
    atts = [jnp.zeros((c, H_B * c), F32) for _ in range(n)]
    half = 1
    while half < group:
        edges = [block_edges(cum, cumx, half) for cum, cumx in zip(cums, cumxs)]
        odd = (r_ck // half) % 2 == 1
        qts = [jnp.where(odd, q * jnp.exp2(cum - first), 0.0).astype(BF16)
               for q, cum, (first, _) in zip(qs, cums, edges)]
        kts = [jnp.where(odd, 0.0, k * jnp.exp2(last - cum)).T.astype(BF16)
               for k, cum, (_, last) in zip(ks, cums, edges)]
        prods = [_dot(qt, jnp.concatenate([kt * m for m in head_rows], axis=1)) for qt, kt in zip(qts, kts)]
        if 2 * half < c:
            same = r_att // (2 * half) == s_att // (2 * half)
            prods = [jnp.where(same, a, 0.0) for a in prods]
        atts = [att + a for att, a in zip(atts, prods)]
        between()
        half *= 2
    if group > 1:
        att16s = [att.astype(BF16) for att in atts]
        os = [o + jnp.concatenate([_dot(att16[:, h * c:(h + 1) * c], v16[:, h * DV_B:(h + 1) * DV_B])
                                   for h in range(H_B)], axis=-1)
              for o, att16, v16 in zip(os, att16s, v16s)]
    return list(zip(os, cums))


def _gla_state_step(q, k, v16, cum, s_old):
    c = q.shape[0]
    q_in = (q * jnp.exp2(cum)).astype(BF16)
    k_out_t = (k * jnp.exp2(cum[c - 1:c, :] - cum)).T.astype(BF16)
    keep = jnp.exp2(cum.T[:, c - 1:c])
    kd = lambda h: slice(h * DK_B, (h + 1) * DK_B)
    vd = lambda h: slice(h * DV_B, (h + 1) * DV_B)
    o = jnp.concatenate([_dot(q_in[:, kd(h)], s_old[kd(h)].astype(BF16)) for h in range(H_B)], axis=-1)
    s_new = jnp.concatenate([s_old[kd(h)] * keep[kd(h)] + _dot(k_out_t[kd(h)], v16[:, vd(h)])
                             for h in range(H_B)], axis=0)
    return o, s_new


def _gla_finish(o, g_ref, sz):
    outs = [_rms_rows(o[:, h * DV_B:(h + 1) * DV_B], g_ref[:, h * DV_B:(h + 1) * DV_B]) for h in range(H_B)]
    return (jnp.concatenate(outs, axis=-1) * sz.astype(F32)).astype(BF16)


def _gla_kernel(q_ref, k_ref, v_ref, la_ref, sz_ref, g_ref, s0_ref, o_ref, sfin_ref, s_scr):
    t_idx = pl.program_id(1)
    c = GLA_C
    streams = range(q_ref.shape[0])

    @pl.when(t_idx == 0)
    def _():
        for b in streams:
            s_scr[b] = s0_ref[b].reshape(W_BK, DV_B)

    qs = [q_ref[b].astype(F32) for b in streams]
    ks = [k_ref[b].astype(F32) for b in streams]
    v16s = [v_ref[b] for b in streams]
    intra = _gla_intra([(qs[b], ks[b], v16s[b].astype(F32), v16s[b], la_ref[b]) for b in streams], c)

    q_ins = [(qs[b] * jnp.exp2(intra[b][1])).astype(BF16) for b in streams]
    k_out_ts = [(ks[b] * jnp.exp2(intra[b][1][c - 1:c, :] - intra[b][1])).T.astype(BF16) for b in streams]
    keeps = [jnp.exp2(intra[b][1].T[:, c - 1:c]) for b in streams]
    s_olds = [s_scr[b] for b in streams]
    inters, s_news = [], []
    for b in streams:
        kd = lambda h: slice(h * DK_B, (h + 1) * DK_B)
        vd = lambda h: slice(h * DV_B, (h + 1) * DV_B)
        inters.append(jnp.concatenate(
            [_dot(q_ins[b][:, kd(h)], s_olds[b][kd(h)].astype(BF16)) for h in range(H_B)], axis=-1))
        s_news.append(jnp.concatenate(
            [s_olds[b][kd(h)] * keeps[b][kd(h)] + _dot(k_out_ts[b][kd(h)], v16s[b][:, vd(h)])
             for h in range(H_B)], axis=0))
    for b in streams:
        s_scr[b] = s_news[b]
        o_ref[b] = _gla_finish(intra[b][0] + inters[b], g_ref, sz_ref[b])

    @pl.when(t_idx == pl.num_programs(1) - 1)
    def _():
        for b in streams:
            sfin_ref[b] = s_news[b].reshape(H_B, DK_B, DV_B)


GLA_STREAMS = 2


def _gla(q, k, v, la, sz, g, s0):
    b, t, _ = q.shape
    nb = GLA_STREAMS if b % GLA_STREAMS == 0 else 1
    assert t % GLA_C == 0
    blk = lambda w: pl.BlockSpec((nb, GLA_C, w), lambda bi, i: (bi, i, 0))
    state = pl.BlockSpec((nb, H_B, DK_B, DV_B), lambda bi, i: (bi, 0, 0, 0))
    return pl.pallas_call(
        _gla_kernel,
        grid=(b // nb, t // GLA_C),
        in_specs=[blk(W_BK), blk(W_BK), blk(W_BV), blk(W_BK), blk(W_BV),
                  pl.BlockSpec((1, W_BV), lambda bi, i: (0, 0)), state],
        out_specs=[blk(W_BV), state],
        out_shape=[jax.ShapeDtypeStruct((b, t, W_BV), BF16),
                   jax.ShapeDtypeStruct((b, H_B, DK_B, DV_B), F32)],
        scratch_shapes=[pltpu.VMEM((nb, W_BK, DV_B), F32)],
        compiler_params=_params("parallel", "arbitrary"),
        name="gla",
    )(q, k, v, la, sz, g, s0)


def _gla_step_kernel(q_ref, k_ref, v_ref, la_ref, sz_ref, g_ref, s0_ref, o_ref, s_ref, *, t):
    c = GLA_C
    n = c // t
    q, k, la = q_ref[...].astype(F32), k_ref[...].astype(F32), la_ref[...]
    v16 = v_ref[...]
    (o, cum), = _gla_intra([(q, k, v16.astype(F32), v16, la)], t)

    q_in = (q * jnp.exp2(cum)).astype(BF16)
    cum_t, k_t = cum.T, k.T
    stream_of_col = lax.broadcasted_iota(jnp.int32, (W_BK, c), 1) // t
    inter = []
    for s in range(n):
        rows = slice(s * t, (s + 1) * t)
        last = cum_t[:, (s + 1) * t - 1:(s + 1) * t]
        k_out_t = (k_t * jnp.exp2(jnp.where(stream_of_col == s, last - cum_t, NEG))).astype(BF16)
        keep = jnp.exp2(last)
        o_s = []
        for h in range(H_B):
            ks = slice(h * DK_B, (h + 1) * DK_B)
            vs = slice(h * DV_B, (h + 1) * DV_B)
            s_old = s0_ref[s, h]
            o_s.append(_dot(q_in[rows, ks], s_old.astype(BF16)))
            s_ref[s, h] = s_old * keep[ks] + _dot(k_out_t[ks], v16[:, vs])
        inter.append(jnp.concatenate(o_s, axis=-1))
    o_ref[...] = _gla_finish(o + jnp.concatenate(inter, axis=0), g_ref, sz_ref[...])


def _gla_step(q, k, v, la, sz, g, s0):
    b, t, _ = q.shape
    n = GLA_C // t
    assert GLA_C % t == 0 and t % (2 * GLA_BASE) == 0 and b % n == 0
    flat = lambda z: z.reshape(b * t, z.shape[-1])
    blk = lambda w: pl.BlockSpec((GLA_C, w), lambda i: (i, 0))
    state = pl.BlockSpec((n, H_B, DK_B, DV_B), lambda i: (i, 0, 0, 0))
    o, s_new = pl.pallas_call(
        functools.partial(_gla_step_kernel, t=t),
        grid=(b // n,),
        in_specs=[blk(W_BK), blk(W_BK), blk(W_BV), blk(W_BK), blk(W_BV),
                  pl.BlockSpec((1, W_BV), lambda i: (0, 0)), state],
        out_specs=[blk(W_BV), state],
        out_shape=[jax.ShapeDtypeStruct((b * t, W_BV), BF16),
                   jax.ShapeDtypeStruct((b, H_B, DK_B, DV_B), F32)],
        compiler_params=_params("parallel"),
        name="gla_step",
    )(flat(q), flat(k), flat(v), flat(la), flat(sz), g, s0)
    return o.reshape(b, t, W_BV), s_new


def _prep_weights(l, norm_in, w_in, g_qa, g_ka, w_gate2, b_gate, g_gla_out, g_mem, w_mem_kv, g_qm, g_km,
                  w_up_a, w_up_b, w_up_m, w_out):
    tile = lambda gain, n: jnp.tile(gain, n).reshape(1, -1)
    return {
        "norm_in": norm_in[l].reshape(1, D_MODEL),
        "w_in_t": w_in[l].T.astype(BF16),
        "w_gate2": w_gate2[l].astype(BF16),
        "b_gate": b_gate[l].reshape(1, W_BK),
        "g_qa": tile(g_qa[l], H_A), "g_ka": tile(g_ka[l], H_A), "g_qm": tile(g_qm[l], H_M),
        "g_km": tile(g_km[l], H_M), "g_gla": tile(g_gla_out[l], H_B),
        "g_mem": g_mem[l].reshape(1, D_MODEL),
        "w_mem_kv": w_mem_kv[l].astype(BF16),
        "w_up_a": w_up_a[l].astype(BF16), "w_up_b": w_up_b[l].astype(BF16),
        "w_up_m": w_up_m[l].astype(BF16), "w_out": w_out[l].astype(BF16),
    }


def _layer(x, wts, attend_a, attend_m, s0, tm):
    b, t, _ = x.shape
    keep = min(BAND, t)
    x2d = x.reshape(b * t, D_MODEL)
    r3 = lambda z: z.reshape(b, t, z.shape[-1])
    flat = lambda z: z.reshape(b * t, z.shape[-1])
    if t >= GLA_C:
        (qa, ka, va, ka16, va16, sza, out_b, s_new, qm, szm, ga, gb, gm) = _front(x2d, wts, tm, t, keep, s0)
    else:
        (qa, ka, va, ka16, va16, sza, qb, kb, vb, la, szb, qm, szm, ga, gb, gm) = _front(x2d, wts, tm, t, keep)
        out_b, s_new = _gla_step(r3(qb), r3(kb), r3(vb), r3(la), r3(szb), wts["g_gla"], s0)
        out_b = flat(out_b)
    out_a = attend_a(r3(qa), r3(ka16), r3(va16), r3(sza))
    out_m = attend_m(r3(qm), r3(szm))
    y = _back(x2d, flat(out_a), out_b, flat(out_m), ga, gb, gm, wts, tm)
    heads = lambda z: z.reshape(b, keep, H_A, HEAD_DIM)
    return y.reshape(b, t, D_MODEL), heads(ka), heads(va), s_new


def kernel(x_prompt, x_sample, mem_prompt, cache_a_k, cache_a_v, state_gla, cache_mem_k, cache_mem_v,
           norm_in, w_in, g_qa, g_ka, rel_bias, w_gate2, b_gate, g_gla_out, g_mem, w_mem_kv, g_qm, g_km,
           w_up_a, w_up_b, w_up_m, w_out):
    depth = w_in.shape[0]
    xp, xs = x_prompt, x_sample
    bp, tp, _ = xp.shape
    bs, ts, _ = xs.shape
    akp, avp, sgp, mkp, mvp, aks, avs, sgs = [], [], [], [], [], [], [], []
    for l in range(depth):
        wts = _prep_weights(l, norm_in, w_in, g_qa, g_ka, w_gate2, b_gate, g_gla_out, g_mem, w_mem_kv,
                            g_qm, g_km, w_up_a, w_up_b, w_up_m, w_out)
        bias = _bias_tile(rel_bias[l])
        mk, mv = _mem_kv(mem_prompt, wts)
        attend_p = lambda q, k, v, sz: _attn_prompt(q, k, v, sz, bias)
        s0 = jnp.zeros((bp, H_B, DK_B, DV_B), state_gla.dtype)
        attend_mp = lambda q, sz: _attn_mem(q, mk, mv, sz, min(tp, 512), 1)
        xp, ka, va, sp = _layer(xp, wts, attend_p, attend_mp, s0, 512)
        akp.append(ka)
        avp.append(va)
        sgp.append(sp)
        mkp.append(mk.reshape(bp, N_MEM, H_M, HEAD_DIM))
        mvp.append(mv.reshape(bp, N_MEM, H_M, HEAD_DIM))
        rows_last = lambda z: jnp.transpose(z, (0, 2, 3, 1))
        past_k, past_v = rows_last(cache_a_k[l]), rows_last(cache_a_v[l])
        mem_k, mem_v = rows_last(cache_mem_k[l]), rows_last(cache_mem_v[l])
        attend_s = lambda q, k, v, sz: _attn_step(q, k, v, past_k, past_v, sz, bias)
        attend_ms = lambda q, sz: _attn_mem_step(q, mem_k, mem_v, sz)
        xs, ka_s, va_s, ss = _layer(xs, wts, attend_s, attend_ms, state_gla[l], 256)
        aks.append(ka_s)
        avs.append(va_s)
        sgs.append(ss)
    return (xp, xs, jnp.stack(akp), jnp.stack(avp), jnp.stack(sgp), jnp.stack(mkp), jnp.stack(mvp),
            jnp.stack(aks), jnp.stack(avs), jnp.stack(sgs))
```

```python
import functools

import jax
import jax.numpy as jnp
from jax import lax
from jax.experimental import pallas as pl
from jax.experimental.pallas import tpu as pltpu

F32 = jnp.float32
BF16 = jnp.bfloat16

D_MODEL = 1024
CHUNK = 64
LEFT_CHUNKS = 8
HEAD_DIM = 64
H_A = 8
W_A = H_A * HEAD_DIM
REL_CLIP = 128
H_B = 4
DK_B = 64
DV_B = 128
W_BK = H_B * DK_B
W_BV = H_B * DV_B
GATE_RANK = 16
GATE_TAU = 16.0
N_MEM = 256
H_M = 4
W_M = H_M * HEAD_DIM
EPS = 1e-6
IN_SIZES = (W_A, W_A, W_A, W_A, W_BK, W_BK, W_BV, GATE_RANK, W_BV, W_M, W_M, D_MODEL, D_MODEL, D_MODEL)

LANES = 128
SUBLANES = 8
VMEM_LIMIT = 56 * 1024 * 1024
NEG = -1e30
LOG2E = 1.4426950408889634

BAND = LEFT_CHUNKS * CHUNK
QBLK = 256
KSPAN = BAND + QBLK
WINDOW = BAND + CHUNK
BIAS_W = BAND + 2 * CHUNK
ATTN_CHUNKS = 1
GLA_C = 128
GLA_BASE = SUBLANES


def _dot(a, b):
    return jnp.dot(a, b, preferred_element_type=F32)


def _dot_nt(a, b):
    return lax.dot_general(a, b, (((1,), (1,)), ((), ())), preferred_element_type=F32)


def _split3(x):
    hi = x.astype(BF16)
    r = x - hi.astype(F32)
    mid = r.astype(BF16)
    lo = (r - mid.astype(F32)).astype(BF16)
    return hi, mid, lo


def _dot_exact_lhs(a01, x):
    hi, mid, lo = _split3(x)
    return _dot(a01, hi) + _dot(a01, mid) + _dot(a01, lo)


def _sigmoid(z):
    return 1.0 / (1.0 + jnp.exp(-z))


def _silu(z):
    return z * _sigmoid(z)


def _log_sigmoid(z):
    return jnp.minimum(z, 0.0) - jnp.log1p(jnp.exp(-jnp.abs(z)))


def _rms_rows(x, g):
    ms = jnp.mean(x * x, axis=-1, keepdims=True)
    return x * lax.rsqrt(ms + EPS) * g


def _head_mean_matrix(width, head):
    r = lax.broadcasted_iota(jnp.int32, (width, width), 0) // head
    c = lax.broadcasted_iota(jnp.int32, (width, width), 1) // head
    return jnp.where(r == c, 1.0 / head, 0.0).astype(BF16)


def _head_rms(x, g, head):
    avg = _head_mean_matrix(x.shape[-1], head)
    ms = _dot((x * x).astype(BF16), avg)
    return x * lax.rsqrt(ms + EPS) * g


def _full(shape):
    nd = len(shape)
    return pl.BlockSpec(shape, lambda *_: (0,) * nd, pipeline_mode=pl.Buffered(1))


def _params(*sem):
    return pltpu.CompilerParams(dimension_semantics=sem, vmem_limit_bytes=VMEM_LIMIT)


_IN_OFF = tuple(int(sum(IN_SIZES[:i])) for i in range(len(IN_SIZES) + 1))
D_IN = _IN_OFF[-1]


def _front_kernel(x_ref, nin_ref, w_ref, wg2_ref, bg_ref, gqa_ref, gka_ref, gqm_ref, *rest, per_stream=None):
    fused = per_stream is not None
    if fused:
        (ggla_ref, s0_ref, qa_o, ka_o, va_o, ka16_o, va16_o, sza_o, outb_o, sfin_o, qm_o, szm_o,
         ga_o, gb_o, gm_o, s_scr) = rest
    else:
        (qa_o, ka_o, va_o, ka16_o, va16_o, sza_o, qb_o, kb_o, vb_o, la_o, szb_o, qm_o, szm_o,
         ga_o, gb_o, gm_o) = rest
    if fused:
        @pl.when(pl.program_id(0) % per_stream == 0)
        def _():
            s_scr[...] = s0_ref[pl.program_id(0) // per_stream].reshape(W_BK, DV_B)

    h = _rms_rows(x_ref[...], nin_ref[...]).astype(BF16)

    def proj(i):
        return _dot_nt(h, w_ref[_IN_OFF[i]:_IN_OFF[i + 1], :])

    qb = (proj(4) * (DK_B ** -0.5)).astype(BF16)
    kb = proj(5).astype(BF16)
    vb = proj(6).astype(BF16)
    z = _dot(proj(7).astype(BF16), wg2_ref[...]) + bg_ref[...]
    la = _log_sigmoid(z) * (1.0 / GATE_TAU)
    szb = _silu(proj(8)).astype(BF16)

    scale = HEAD_DIM ** -0.5 * LOG2E

    def emit_qa():
        qa_o[...] = (_head_rms(proj(0), gqa_ref[...], HEAD_DIM) * scale).astype(BF16)

    def emit_ka():
        ka = _head_rms(proj(1), gka_ref[...], HEAD_DIM)
        ka_o[...] = ka
        ka16_o[...] = ka.astype(BF16)

    def emit_va():
        va = proj(2)
        va_o[...] = va
        va16_o[...] = va.astype(BF16)

    def emit_sza():
        sza_o[...] = _silu(proj(3)).astype(BF16)

    def emit_qm():
        qm_o[...] = (_head_rms(proj(9), gqm_ref[...], HEAD_DIM) * scale).astype(BF16)

    def emit_szm():
        szm_o[...] = _silu(proj(10)).astype(BF16)

    def emit_gate(i, ref):
        ref[...] = _sigmoid(proj(i)).astype(BF16)

    pending = [emit_qa, emit_ka, emit_va, emit_sza, emit_qm, emit_szm,
               functools.partial(emit_gate, 11, ga_o), functools.partial(emit_gate, 12, gb_o),
               functools.partial(emit_gate, 13, gm_o)]

    def emit_next():
        if pending:
            pending.pop(0)()

    if fused:
        chunks = [slice(r, r + GLA_C) for r in range(0, x_ref.shape[0], GLA_C)]
        items = [(qb[rows].astype(F32), kb[rows].astype(F32), vb[rows].astype(F32), vb[rows], la[rows])
                 for rows in chunks]
        state = s_scr[...]
        outs = []
        for rows, item, (o, cum) in zip(chunks, items, _gla_intra(items, GLA_C, emit_next)):
            inter, state = _gla_state_step(item[0], item[1], item[3], cum, state)
            outs.append(_gla_finish(o + inter, ggla_ref, szb[rows]))
            emit_next()
        s_scr[...] = state
        outb_o[...] = jnp.concatenate(outs, axis=0)
    else:
        qb_o[...] = qb
        kb_o[...] = kb
        vb_o[...] = vb
        la_o[...] = la
        szb_o[...] = szb
    while pending:
        emit_next()

    if fused:
        @pl.when(pl.program_id(0) % per_stream == per_stream - 1)
        def _():
            sfin_o[0] = s_scr[...].reshape(H_B, DK_B, DV_B)


def _front(x2d, wts, tm, t, keep, s0=None):
    n = x2d.shape[0]
    tm = min(tm, n)
    row = lambda w: pl.BlockSpec((tm, w), lambda i: (i, 0))
    if keep == t:
        kv_rows, kv = n, row(W_A)
    else:
        assert t % tm == 0 and keep % tm == 0
        per_stream, kept = t // tm, keep // tm
        kv_rows = (n // t) * keep
        kv = pl.BlockSpec((tm, W_A), lambda i: (
            (i // per_stream) * kept + jnp.maximum(i % per_stream - (per_stream - kept), 0), 0))
    out = lambda rows_, w, d, spec: (jax.ShapeDtypeStruct((rows_, w), d), spec)
    head = [out(n, W_A, BF16, row(W_A)), out(kv_rows, W_A, F32, kv), out(kv_rows, W_A, F32, kv),
            out(n, W_A, BF16, row(W_A)), out(n, W_A, BF16, row(W_A)), out(n, W_A, BF16, row(W_A))]
    tail = [out(n, W_M, BF16, row(W_M)), out(n, W_M, BF16, row(W_M))] + [out(n, D_MODEL, BF16, row(D_MODEL))] * 3
    in_specs = [row(D_MODEL), _full((1, D_MODEL)), _full((D_IN, D_MODEL)),
                _full((GATE_RANK, W_BK)), _full((1, W_BK)), _full((1, W_A)), _full((1, W_A)), _full((1, W_M))]
    args = [x2d, wts["norm_in"], wts["w_in_t"], wts["w_gate2"], wts["b_gate"], wts["g_qa"], wts["g_ka"], wts["g_qm"]]
    if s0 is None:
        mid = [out(n, W_BK, BF16, row(W_BK)), out(n, W_BK, BF16, row(W_BK)), out(n, W_BV, BF16, row(W_BV)),
               out(n, W_BK, F32, row(W_BK)), out(n, W_BV, BF16, row(W_BV))]
        body, scratch = _front_kernel, []
    else:
        assert t % tm == 0 and tm % GLA_C == 0
        streams = n // t
        state = (streams, H_B, DK_B, DV_B)
        mid = [out(n, W_BV, BF16, row(W_BV)),
               (jax.ShapeDtypeStruct(state, F32),
                pl.BlockSpec((1,) + state[1:], lambda i: (i // (t // tm), 0, 0, 0)))]
        in_specs += [_full((1, W_BV)), _full(state)]
        args += [wts["g_gla"], s0]
        body, scratch = functools.partial(_front_kernel, per_stream=t // tm), [pltpu.VMEM((W_BK, DV_B), F32)]
    outs = head + mid + tail
    return pl.pallas_call(
        body,
        grid=(n // tm,),
        in_specs=in_specs,
        out_specs=[spec for _, spec in outs],
        out_shape=[shape for shape, _ in outs],
        scratch_shapes=scratch,
        compiler_params=_params("arbitrary"),
        name="front",
    )(*args)


def _back_kernel(x_ref, a_ref, b_ref, m_ref, ga_ref, gb_ref, gm_ref, wa_ref, wb_ref, wm_ref, wo_ref, y_ref):
    u = (ga_ref[...].astype(F32) * _dot(a_ref[...], wa_ref[...])
         + gb_ref[...].astype(F32) * _dot(b_ref[...], wb_ref[...])
         + gm_ref[...].astype(F32) * _dot(m_ref[...], wm_ref[...]))
    y_ref[...] = x_ref[...] + _dot(u.astype(BF16), wo_ref[...])


def _back(x2d, a, b, m, ga, gb, gm, wts, tm):
    n = x2d.shape[0]
    tm = min(tm, n)
    row = lambda w: pl.BlockSpec((tm, w), lambda i: (i, 0))
    return pl.pallas_call(
        _back_kernel,
        grid=(n // tm,),
        in_specs=[row(D_MODEL), row(W_A), row(W_BV), row(W_M), row(D_MODEL), row(D_MODEL), row(D_MODEL),
                  _full((W_A, D_MODEL)), _full((W_BV, D_MODEL)), _full((W_M, D_MODEL)),
                  _full((D_MODEL, D_MODEL))],
        out_specs=row(D_MODEL),
        out_shape=jax.ShapeDtypeStruct((n, D_MODEL), F32),
        compiler_params=_params("parallel"),
        name="back",
    )(x2d, a, b, m, ga, gb, gm, wts["w_up_a"], wts["w_up_b"], wts["w_up_m"], wts["w_out"])


def _bias_kernel(e_ref, tile_ref):
    e = jnp.broadcast_to(e_ref[0] * LOG2E, (CHUNK, 2 * BAND))
    tile_ref[0] = pltpu.roll(e, 0, 1, stride=1, stride_axis=0)[:, :BIAS_W]


def _bias_tile(rel_bias):
    j = jnp.arange(2 * BAND)
    dist = jnp.where(j < BIAS_W, BAND - j, REL_CLIP)
    e = rel_bias[:, jnp.clip(dist, -REL_CLIP, REL_CLIP) + REL_CLIP].reshape(H_A, 1, 2 * BAND)
    return pl.pallas_call(
        _bias_kernel,
        grid=(H_A,),
        in_specs=[pl.BlockSpec((1, 1, 2 * BAND), lambda h: (h, 0, 0))],
        out_specs=pl.BlockSpec((1, CHUNK, BIAS_W), lambda h: (h, 0, 0)),
        out_shape=jax.ShapeDtypeStruct((H_A, CHUNK, BIAS_W), F32),
        compiler_params=_params("parallel"),
        name="bias_tile",
    )(e)


def _softmax_pv_staged(scores, values):
    biased = [[s if b is None else s + b for s, b in s_list] for s_list in scores]
    tops = [functools.reduce(jnp.maximum, [jnp.max(s, axis=-1, keepdims=True) for s in s_list])
            for s_list in biased]
    probs = [[jnp.exp2(s - m) for s in s_list] for s_list, m in zip(biased, tops)]
    sums = [functools.reduce(jnp.add, [jnp.sum(x, axis=-1, keepdims=True) for x in p_list]) for p_list in probs]
    return [functools.reduce(jnp.add, [(v(x.astype(BF16)) if callable(v) else _dot(x.astype(BF16), v))
                                       for x, v in zip(p_list, v_list)]) / l
            for p_list, v_list, l in zip(probs, values, sums)]


def _head_pairs_attention(q_pairs, k_lists, v_lists, bias_fn):
    m_rows = q_pairs[0].shape[0]
    lane = lax.broadcasted_iota(jnp.int32, (m_rows, LANES), 1)
    first = lane < HEAD_DIM
    scores = []
    for p, qp in enumerate(q_pairs):
        zero = jnp.zeros_like(qp)
        q2 = jnp.concatenate([jnp.where(first, qp, zero), jnp.where(first, zero, qp)], axis=0)
        s_list = []
        for i, k in enumerate(k_lists[p]):
            b0, b1 = bias_fn(p, 0, i), bias_fn(p, 1, i)
            s_list.append((_dot_nt(q2, k), None if b0 is None else jnp.concatenate([b0, b1], axis=0)))
        scores.append(s_list)
    outs = _softmax_pv_staged(scores, v_lists)
    return [jnp.where(first, o[:m_rows], o[m_rows:]) for o in outs]


def _attn_back_kernel(q_ref, k0_ref, k1_ref, k2_ref, v0_ref, v1_ref, v2_ref, sz_ref, bias_ref,
                      x_ref, b_ref, m_ref, ga_ref, gb_ref, gm_ref, wa_ref, wb_ref, wm_ref, wo_ref, y_ref,
                      k_buf, v_buf, bias_buf, a_buf):
    i = pl.program_id(1)
    blk = jnp.minimum(i, pl.num_programs(1) - 2)
    slot = i % 2

    @pl.when(i == 0)
    def _():
        a_buf[...] = jnp.zeros_like(a_buf)

    for j, (k_ref, v_ref) in enumerate(((k0_ref, v0_ref), (k1_ref, v1_ref), (k2_ref, v2_ref))):
        k_buf[j * QBLK:(j + 1) * QBLK, :] = k_ref[0]
        v_buf[j * QBLK:(j + 1) * QBLK, :] = v_ref[0]

    @pl.when(blk <= BAND // QBLK)
    def _():
        col = lax.broadcasted_iota(jnp.int32, (1, BIAS_W), 1)
        for c in range(QBLK // CHUNK):
            before_start = jnp.where(col + c * CHUNK < (BAND // QBLK - blk) * QBLK, NEG, 0.0)
            for h in range(H_A):
                bias_buf[c, h] = bias_ref[h] + before_start

    back = {}

    def back_up():
        a_prev = a_buf[1 - slot]
        back["u"] = (ga_ref[0].astype(F32) * _dot(a_prev, wa_ref[...])
                     + gb_ref[0].astype(F32) * _dot(b_ref[0], wb_ref[...])
                     + gm_ref[0].astype(F32) * _dot(m_ref[0], wm_ref[...])).astype(BF16)

    def back_out():
        y_ref[0] = x_ref[0] + _dot(back["u"], wo_ref[...])

    pending = {0: back_up, 2: back_out}
    pairs = [slice(p * LANES, (p + 1) * LANES) for p in range(H_A // 2)]
    for c in range(QBLK // CHUNK):
        rows = slice(c * CHUNK, (c + 1) * CHUNK)
        win = slice(c * CHUNK, c * CHUNK + WINDOW)
        outs = _head_pairs_attention(
            [q_ref[0, rows, lanes] for lanes in pairs],
            [[k_buf[win, lanes]] for lanes in pairs], [[v_buf[win, lanes]] for lanes in pairs],
            lambda p, e, _: bias_buf[c, 2 * p + e, :, :WINDOW])
        for lanes, o in zip(pairs, outs):
            a_buf[slot, rows, lanes] = (o * sz_ref[0, rows, lanes].astype(F32)).astype(BF16)
        if c in pending:
            pending[c]()


def _attn_back(q, k, v, sz, bias, x, out_b, out_m, ga, gb, gm, wts):
    b, t, _ = q.shape
    nb = BAND // QBLK
    n = t // QBLK
    assert nb == 2 and t % QBLK == 0
    at = lambda d: (lambda bi, i: (bi, jnp.maximum(jnp.minimum(i, n - 1) - d, 0), 0))
    a_blk = lambda d: pl.BlockSpec((1, QBLK, W_A), at(d))
    kv_specs = [a_blk(nb - j) for j in range(nb + 1)]
    behind = lambda w: pl.BlockSpec((1, QBLK, w), lambda bi, i: (bi, jnp.maximum(i - 1, 0), 0))
    return pl.pallas_call(
        _attn_back_kernel,
        grid=(b, n + 1),
        in_specs=[a_blk(0)] + kv_specs + kv_specs + [a_blk(0), _full((H_A, CHUNK, BIAS_W)),
                  behind(D_MODEL), behind(W_BV), behind(W_M), behind(D_MODEL), behind(D_MODEL), behind(D_MODEL),
                  _full((W_A, D_MODEL)), _full((W_BV, D_MODEL)), _full((W_M, D_MODEL)),
                  _full((D_MODEL, D_MODEL))],
        out_specs=behind(D_MODEL),
        out_shape=jax.ShapeDtypeStruct((b, t, D_MODEL), F32),
        scratch_shapes=[pltpu.VMEM((KSPAN, W_A), BF16), pltpu.VMEM((KSPAN, W_A), BF16),
                        pltpu.VMEM((QBLK // CHUNK, H_A, CHUNK, BIAS_W), F32),
                        pltpu.VMEM((2, QBLK, W_A), BF16)],
        compiler_params=_params("parallel", "arbitrary"),
        name="attn_back",
    )(q, k, k, k, v, v, v, sz, bias, x, out_b, out_m, ga, gb, gm,
      wts["w_up_a"], wts["w_up_b"], wts["w_up_m"], wts["w_out"])


STEP_STREAMS = 4


def _attn_step_kernel(q_ref, kn_ref, vn_ref, kp_ref, vp_ref, sz_ref, bias_ref, o_ref):
    s_len = q_ref.shape[1]
    for b in range(STEP_STREAMS):
        scores, values = [], []
        for h in range(H_A):
            lanes = slice(h * HEAD_DIM, (h + 1) * HEAD_DIM)
            q = q_ref[b, :, lanes]
            scores.append([(_dot(q, kp_ref[b, h].astype(BF16)), bias_ref[h, :s_len, :BAND]),
                           (_dot_nt(q, kn_ref[b, :, lanes]), bias_ref[h, :s_len, BAND:BAND + s_len])])
            values.append([functools.partial(_dot_nt, b=vp_ref[b, h].astype(BF16)), vn_ref[b, :, lanes]])
        o = jnp.concatenate(_softmax_pv_staged(scores, values), axis=-1)
        o_ref[b] = (o * sz_ref[b].astype(F32)).astype(BF16)


def _attn_step(q, k_new, v_new, k_past_t, v_past_t, sz, bias):
    b, s_len, _ = q.shape
    assert k_past_t.shape[1:] == (H_A, HEAD_DIM, BAND) and s_len <= CHUNK and b % STEP_STREAMS == 0
    new = pl.BlockSpec((STEP_STREAMS, s_len, W_A), lambda i: (i, 0, 0))
    past = pl.BlockSpec((STEP_STREAMS, H_A, HEAD_DIM, BAND), lambda i: (i, 0, 0, 0))
    return pl.pallas_call(
        _attn_step_kernel,
        grid=(b // STEP_STREAMS,),
        in_specs=[new, new, new, past, past, new, _full((H_A, CHUNK, BIAS_W))],
        out_specs=new,
        out_shape=jax.ShapeDtypeStruct((b, s_len, W_A), BF16),
        compiler_params=_params("parallel"),
        name="attn_step",
    )(q, k_new, v_new, k_past_t, v_past_t, sz, bias)


def _mem_kv_kernel(mem_ref, gmem_ref, w_ref, gkm_ref, mk_ref, mv_ref):
    h = _rms_rows(mem_ref[0], gmem_ref[...]).astype(BF16)
    kv = _dot(h, w_ref[...])
    mk_ref[0] = _head_rms(kv[:, :W_M], gkm_ref[...], HEAD_DIM)
    mv_ref[0] = kv[:, W_M:]


def _mem_kv(mem, wts):
    b = mem.shape[0]
    out = pl.BlockSpec((1, N_MEM, W_M), lambda i: (i, 0, 0))
    return pl.pallas_call(
        _mem_kv_kernel,
        grid=(b,),
        in_specs=[pl.BlockSpec((1, N_MEM, D_MODEL), lambda i: (i, 0, 0)), _full((1, D_MODEL)),
                  _full((D_MODEL, 2 * W_M)), _full((1, W_M))],
        out_specs=[out, out],
        out_shape=[jax.ShapeDtypeStruct((b, N_MEM, W_M), F32)] * 2,
        compiler_params=_params("parallel"),
        name="mem_kv",
    )(mem, wts["g_mem"], wts["w_mem_kv"], wts["g_km"])


MEM_ROWS = 128
MEM_STREAMS = 8


def _attn_mem_kernel(q_ref, mk_ref, mv_ref, sz_ref, o_ref):
    nb, tq, _ = q_ref.shape
    step = min(tq, MEM_ROWS)
    pairs = [slice(p * LANES, (p + 1) * LANES) for p in range(H_M // 2)]
    for b in range(nb):
        mk = [[mk_ref[b, :, lanes].astype(BF16)] for lanes in pairs]
        mv = [[mv_ref[b, :, lanes].astype(BF16)] for lanes in pairs]
        blocks = [(slice(r, r + step), p) for r in range(0, tq, step) for p in range(len(pairs))]
        outs = _head_pairs_attention([q_ref[b, rows, pairs[p]] for rows, p in blocks],
                                     [mk[p] for _, p in blocks], [mv[p] for _, p in blocks],
                                     lambda n, e, i: None)
        for (rows, p), o in zip(blocks, outs):
            o_ref[b, rows, pairs[p]] = (o * sz_ref[b, rows, pairs[p]].astype(F32)).astype(BF16)


def _attn_mem(q, mk, mv, sz, tq, nb):
    b, t, _ = q.shape
    assert b % nb == 0 and t % tq == 0
    rows = pl.BlockSpec((nb, tq, W_M), lambda bi, i: (bi, i, 0))
    mem = pl.BlockSpec((nb, N_MEM, W_M), lambda bi, i: (bi, 0, 0))
    return pl.pallas_call(
        _attn_mem_kernel,
        grid=(b // nb, t // tq),
        in_specs=[rows, mem, mem, rows],
        out_specs=rows,
        out_shape=jax.ShapeDtypeStruct((b, t, W_M), BF16),
        compiler_params=_params("parallel", "parallel"),
        name="attn_mem",
    )(q, mk, mv, sz)


def _attn_mem_step_kernel(q_ref, mk_ref, mv_ref, sz_ref, o_ref):
    for b in range(q_ref.shape[0]):
        scores, values = [], []
        for h in range(H_M):
            lanes = slice(h * HEAD_DIM, (h + 1) * HEAD_DIM)
            scores.append([(_dot(q_ref[b, :, lanes], mk_ref[b, h].astype(BF16)), None)])
            values.append([functools.partial(_dot_nt, b=mv_ref[b, h].astype(BF16))])
        o = jnp.concatenate(_softmax_pv_staged(scores, values), axis=-1)
        o_ref[b] = (o * sz_ref[b].astype(F32)).astype(BF16)


def _attn_mem_step(q, mk_t, mv_t, sz):
    b, t, _ = q.shape
    nb = min(b, MEM_STREAMS)
    assert b % nb == 0 and mk_t.shape[1:] == (H_M, HEAD_DIM, N_MEM)
    rows = pl.BlockSpec((nb, t, W_M), lambda i: (i, 0, 0))
    mem = pl.BlockSpec((nb, H_M, HEAD_DIM, N_MEM), lambda i: (i, 0, 0, 0))
    return pl.pallas_call(
        _attn_mem_step_kernel,
        grid=(b // nb,),
        in_specs=[rows, mem, mem, rows],
        out_specs=rows,
        out_shape=jax.ShapeDtypeStruct((b, t, W_M), BF16),
        compiler_params=_params("parallel"),
        name="attn_mem_step",
    )(q, mk_t, mv_t, sz)


def _gla_intra(items, group, between=lambda: None):
    n = len(items)
    qs, ks, vs, v16s = ([it[j] for it in items] for j in range(4))
    c = qs[0].shape[0]
    las = [it[4] * LOG2E for it in items]
    r_cc = lax.broadcasted_iota(jnp.int32, (c, c), 0)
    c_cc = lax.broadcasted_iota(jnp.int32, (c, c), 1)
    causal = c_cc <= r_cc
    if group < c:
        causal = causal & (r_cc // group == c_cc // group)
    causal16 = causal.astype(BF16)
    cums = [_dot_exact_lhs(causal16, la) for la in las]
    cumxs = [cum - la for cum, la in zip(cums, las)]
    r_ck = lax.broadcasted_iota(jnp.int32, (c, W_BK), 0)

    head_of_k = lax.broadcasted_iota(jnp.int32, (W_BK, W_BV), 0) // DK_B
    head_of_v = lax.broadcasted_iota(jnp.int32, (W_BK, W_BV), 1) // DV_B
    spread = (head_of_k == head_of_v).astype(BF16)
    os = [_dot((q * k).astype(BF16), spread) * v for q, k, v in zip(qs, ks, vs)]

    def shifted(x, j):
        return pltpu.roll(x.reshape(c // SUBLANES, SUBLANES, W_BK), j % SUBLANES, 1).reshape(c, W_BK)

    def block_edges(cum, cumx, half):
        if half >= SUBLANES:
            nb = c // half
            first = jnp.broadcast_to(cumx.reshape(nb, half, W_BK)[:, 0:1, :], (nb, half, W_BK))
            last = jnp.broadcast_to(cum.reshape(nb, half, W_BK)[:, half - 1:half, :], (nb, half, W_BK))
            return first.reshape(c, W_BK), last.reshape(c, W_BK)
        r_in = r_ck % half
        first, last = cumx, cum
        for j in range(1, half):
            first = jnp.where(r_in == j, shifted(cumx, j), first)
            last = jnp.where(r_in == half - 1 - j, shifted(cum, -j), last)
        return first, last

    head_of_row = lax.broadcasted_iota(jnp.int32, (W_BK, c), 0) // DK_B
    head_rows = [jnp.where(head_of_row == h, 1.0, 0.0).astype(BF16) for h in range(H_B)]
    r_att = lax.broadcasted_iota(jnp.int32, (c, H_B * c), 0)
    s_att = lax.broadcasted_iota(jnp.int32, (c, H_B * c), 1) % c
    atts = [jnp.zeros((c, H_B * c), F32) for _ in range(n)]
    half = 1
    while half < group:
        edges = [block_edges(cum, cumx, half) for cum, cumx in zip(cums, cumxs)]
        odd = (r_ck // half) % 2 == 1
        qts = [jnp.where(odd, q * jnp.exp2(cum - first), 0.0).astype(BF16)
               for q, cum, (first, _) in zip(qs, cums, edges)]
        kts = [jnp.where(odd, 0.0, k * jnp.exp2(last - cum)).T.astype(BF16)
               for k, cum, (_, last) in zip(ks, cums, edges)]
        prods = [_dot(qt, jnp.concatenate([kt * m for m in head_rows], axis=1)) for qt, kt in zip(qts, kts)]
        if 2 * half < c:
            same = r_att // (2 * half) == s_att // (2 * half)
            prods = [jnp.where(same, a, 0.0) for a in prods]
        atts = [att + a for att, a in zip(atts, prods)]
        between()
        half *= 2
    if group > 1:
        att16s = [att.astype(BF16) for att in atts]
        os = [o + jnp.concatenate([_dot(att16[:, h * c:(h + 1) * c], v16[:, h * DV_B:(h + 1) * DV_B])
                                   for h in range(H_B)], axis=-1)
              for o, att16, v16 in zip(os, att16s, v16s)]
    return list(zip(os, cums))


def _gla_state_step(q, k, v16, cum, s_old):
    c = q.shape[0]
    q_in = (q * jnp.exp2(cum)).astype(BF16)
    k_out_t = (k * jnp.exp2(cum[c - 1:c, :] - cum)).T.astype(BF16)
    keep = jnp.exp2(cum.T[:, c - 1:c])
    kd = lambda h: slice(h * DK_B, (h + 1) * DK_B)
    vd = lambda h: slice(h * DV_B, (h + 1) * DV_B)
    o = jnp.concatenate([_dot(q_in[:, kd(h)], s_old[kd(h)].astype(BF16)) for h in range(H_B)], axis=-1)
    s_new = jnp.concatenate([s_old[kd(h)] * keep[kd(h)] + _dot(k_out_t[kd(h)], v16[:, vd(h)])
                             for h in range(H_B)], axis=0)
    return o, s_new


def _gla_finish(o, g_ref, sz):
    outs = [_rms_rows(o[:, h * DV_B:(h + 1) * DV_B], g_ref[:, h * DV_B:(h + 1) * DV_B]) for h in range(H_B)]
    return (jnp.concatenate(outs, axis=-1) * sz.astype(F32)).astype(BF16)


def _gla_step_kernel(q_ref, k_ref, v_ref, la_ref, sz_ref, g_ref, s0_ref, o_ref, s_ref, *, t):
    c = GLA_C
    n = c // t
    q, k, la = q_ref[...].astype(F32), k_ref[...].astype(F32), la_ref[...]
    v16 = v_ref[...]
    (o, cum), = _gla_intra([(q, k, v16.astype(F32), v16, la)], t)

    q_in = (q * jnp.exp2(cum)).astype(BF16)
    cum_t, k_t = cum.T, k.T
    stream_of_col = lax.broadcasted_iota(jnp.int32, (W_BK, c), 1) // t
    inter = []
    for s in range(n):
        rows = slice(s * t, (s + 1) * t)
        last = cum_t[:, (s + 1) * t - 1:(s + 1) * t]
        k_out_t = (k_t * jnp.exp2(jnp.where(stream_of_col == s, last - cum_t, NEG))).astype(BF16)
        keep = jnp.exp2(last)
        o_s = []
        for h in range(H_B):
            ks = slice(h * DK_B, (h + 1) * DK_B)
            vs = slice(h * DV_B, (h + 1) * DV_B)
            s_old = s0_ref[s, h]
            o_s.append(_dot(q_in[rows, ks], s_old.astype(BF16)))
            s_ref[s, h] = s_old * keep[ks] + _dot(k_out_t[ks], v16[:, vs])
        inter.append(jnp.concatenate(o_s, axis=-1))
    o_ref[...] = _gla_finish(o + jnp.concatenate(inter, axis=0), g_ref, sz_ref[...])


def _gla_step(q, k, v, la, sz, g, s0):
    b, t, _ = q.shape
    n = GLA_C // t
    assert GLA_C % t == 0 and t % (2 * GLA_BASE) == 0 and b % n == 0
    flat = lambda z: z.reshape(b * t, z.shape[-1])
    blk = lambda w: pl.BlockSpec((GLA_C, w), lambda i: (i, 0))
    state = pl.BlockSpec((n, H_B, DK_B, DV_B), lambda i: (i, 0, 0, 0))
    o, s_new = pl.pallas_call(
        functools.partial(_gla_step_kernel, t=t),
        grid=(b // n,),
        in_specs=[blk(W_BK), blk(W_BK), blk(W_BV), blk(W_BK), blk(W_BV),
                  pl.BlockSpec((1, W_BV), lambda i: (0, 0)), state],
        out_specs=[blk(W_BV), state],
        out_shape=[jax.ShapeDtypeStruct((b * t, W_BV), BF16),
                   jax.ShapeDtypeStruct((b, H_B, DK_B, DV_B), F32)],
        compiler_params=_params("parallel"),
        name="gla_step",
    )(flat(q), flat(k), flat(v), flat(la), flat(sz), g, s0)
    return o.reshape(b, t, W_BV), s_new


def _prep_weights(l, norm_in, w_in, g_qa, g_ka, w_gate2, b_gate, g_gla_out, g_mem, w_mem_kv, g_qm, g_km,
                  w_up_a, w_up_b, w_up_m, w_out):
    tile = lambda gain, n: jnp.tile(gain, n).reshape(1, -1)
    return {
        "norm_in": norm_in[l].reshape(1, D_MODEL),
        "w_in_t": w_in[l].T.astype(BF16),
        "w_gate2": w_gate2[l].astype(BF16),
        "b_gate": b_gate[l].reshape(1, W_BK),
        "g_qa": tile(g_qa[l], H_A), "g_ka": tile(g_ka[l], H_A), "g_qm": tile(g_qm[l], H_M),
        "g_km": tile(g_km[l], H_M), "g_gla": tile(g_gla_out[l], H_B),
        "g_mem": g_mem[l].reshape(1, D_MODEL),
        "w_mem_kv": w_mem_kv[l].astype(BF16),
        "w_up_a": w_up_a[l].astype(BF16), "w_up_b": w_up_b[l].astype(BF16),
        "w_up_m": w_up_m[l].astype(BF16), "w_out": w_out[l].astype(BF16),
    }


def _layer(x, wts, attend_a, attend_m, s0, tm):
    b, t, _ = x.shape
    keep = min(BAND, t)
    x2d = x.reshape(b * t, D_MODEL)
    r3 = lambda z: z.reshape(b, t, z.shape[-1])
    flat = lambda z: z.reshape(b * t, z.shape[-1])
    heads = lambda z: z.reshape(b, keep, H_A, HEAD_DIM)
    if t >= GLA_C:
        (qa, ka, va, ka16, va16, sza, out_b, s_new, qm, szm, ga, gb, gm) = _front(x2d, wts, tm, t, keep, s0)
        out_m = attend_m(r3(qm), r3(szm))
        y = _attn_back(r3(qa), r3(ka16), r3(va16), r3(sza), attend_a, x, r3(out_b), out_m,
                       r3(ga), r3(gb), r3(gm), wts)
        return y, heads(ka), heads(va), s_new
    (qa, ka, va, ka16, va16, sza, qb, kb, vb, la, szb, qm, szm, ga, gb, gm) = _front(x2d, wts, tm, t, keep)
    out_b, s_new = _gla_step(r3(qb), r3(kb), r3(vb), r3(la), r3(szb), wts["g_gla"], s0)
    out_a = attend_a(r3(qa), r3(ka16), r3(va16), r3(sza))
    out_m = attend_m(r3(qm), r3(szm))
    y = _back(x2d, flat(out_a), flat(out_b), flat(out_m), ga, gb, gm, wts, tm)
    return y.reshape(b, t, D_MODEL), heads(ka), heads(va), s_new


def kernel(x_prompt, x_sample, mem_prompt, cache_a_k, cache_a_v, state_gla, cache_mem_k, cache_mem_v,
           norm_in, w_in, g_qa, g_ka, rel_bias, w_gate2, b_gate, g_gla_out, g_mem, w_mem_kv, g_qm, g_km,
           w_up_a, w_up_b, w_up_m, w_out):
    depth = w_in.shape[0]
    xp, xs = x_prompt, x_sample
    bp, tp, _ = xp.shape
    bs, ts, _ = xs.shape
    akp, avp, sgp, mkp, mvp, aks, avs, sgs = [], [], [], [], [], [], [], []
    for l in range(depth):
        wts = _prep_weights(l, norm_in, w_in, g_qa, g_ka, w_gate2, b_gate, g_gla_out, g_mem, w_mem_kv,
                            g_qm, g_km, w_up_a, w_up_b, w_up_m, w_out)
        bias = _bias_tile(rel_bias[l])
        mk, mv = _mem_kv(mem_prompt, wts)
        s0 = jnp.zeros((bp, H_B, DK_B, DV_B), state_gla.dtype)
        attend_mp = lambda q, sz: _attn_mem(q, mk, mv, sz, min(tp, 512), 1)
        xp, ka, va, sp = _layer(xp, wts, bias, attend_mp, s0, 512)
        akp.append(ka)
        avp.append(va)
        sgp.append(sp)
        mkp.append(mk.reshape(bp, N_MEM, H_M, HEAD_DIM))
        mvp.append(mv.reshape(bp, N_MEM, H_M, HEAD_DIM))
        rows_last = lambda z: jnp.transpose(z, (0, 2, 3, 1))
        past_k, past_v = rows_last(cache_a_k[l]), rows_last(cache_a_v[l])
        mem_k, mem_v = rows_last(cache_mem_k[l]), rows_last(cache_mem_v[l])
        attend_s = lambda q, k, v, sz: _attn_step(q, k, v, past_k, past_v, sz, bias)
        attend_ms = lambda q, sz: _attn_mem_step(q, mem_k, mem_v, sz)
        xs, ka_s, va_s, ss = _layer(xs, wts, attend_s, attend_ms, state_gla[l], 256)
        aks.append(ka_s)
        avs.append(va_s)
        sgs.append(ss)
    return (xp, xs, jnp.stack(akp), jnp.stack(avp), jnp.stack(sgp), jnp.stack(mkp), jnp.stack(mvp),
            jnp.stack(aks), jnp.stack(avs), jnp.stack(sgs))
```

```python
import functools

import jax
import jax.numpy as jnp
from jax import lax
from jax.experimental import pallas as pl
from jax.experimental.pallas import tpu as pltpu

F32 = jnp.float32
BF16 = jnp.bfloat16

D_MODEL = 1024
CHUNK = 64
LEFT_CHUNKS = 8
HEAD_DIM = 64
H_A = 8
W_A = H_A * HEAD_DIM
REL_CLIP = 128
H_B = 4
DK_B = 64
DV_B = 128
W_BK = H_B * DK_B
W_BV = H_B * DV_B
GATE_RANK = 16
GATE_TAU = 16.0
N_MEM = 256
H_M = 4
W_M = H_M * HEAD_DIM
EPS = 1e-6
IN_SIZES = (W_A, W_A, W_A, W_A, W_BK, W_BK, W_BV, GATE_RANK, W_BV, W_M, W_M, D_MODEL, D_MODEL, D_MODEL)

LANES = 128
SUBLANES = 8
VMEM_LIMIT = 56 * 1024 * 1024
NEG = -1e30
LOG2E = 1.4426950408889634

BAND = LEFT_CHUNKS * CHUNK
QBLK = 256
KSPAN = BAND + QBLK
WINDOW = BAND + CHUNK
BIAS_W = BAND + 2 * CHUNK
GLA_C = 128
GLA_BASE = SUBLANES


def _dot(a, b):
    return jnp.dot(a, b, preferred_element_type=F32)


def _dot_nt(a, b):
    return lax.dot_general(a, b, (((1,), (1,)), ((), ())), preferred_element_type=F32)


def _split3(x):
    hi = x.astype(BF16)
    r = x - hi.astype(F32)
    mid = r.astype(BF16)
    lo = (r - mid.astype(F32)).astype(BF16)
    return hi, mid, lo


def _dot_exact_lhs(a01, x):
    hi, mid, lo = _split3(x)
    return _dot(a01, hi) + _dot(a01, mid) + _dot(a01, lo)


def _sigmoid(z):
    return 1.0 / (1.0 + jnp.exp(-z))


def _silu(z):
    return z * _sigmoid(z)


def _log_sigmoid(z):
    return jnp.minimum(z, 0.0) - jnp.log1p(jnp.exp(-jnp.abs(z)))


def _rms_rows(x, g):
    ms = jnp.mean(x * x, axis=-1, keepdims=True)
    return x * lax.rsqrt(ms + EPS) * g


def _head_mean_matrix(width, head):
    r = lax.broadcasted_iota(jnp.int32, (width, width), 0) // head
    c = lax.broadcasted_iota(jnp.int32, (width, width), 1) // head
    return jnp.where(r == c, 1.0 / head, 0.0).astype(BF16)


def _head_rms(x, g, head):
    avg = _head_mean_matrix(x.shape[-1], head)
    ms = _dot((x * x).astype(BF16), avg)
    return x * lax.rsqrt(ms + EPS) * g


def _full(shape):
    nd = len(shape)
    return pl.BlockSpec(shape, lambda *_: (0,) * nd, pipeline_mode=pl.Buffered(1))


def _params(*sem):
    return pltpu.CompilerParams(dimension_semantics=sem, vmem_limit_bytes=VMEM_LIMIT)


_IN_OFF = tuple(int(sum(IN_SIZES[:i])) for i in range(len(IN_SIZES) + 1))
D_IN = _IN_OFF[-1]


def _front_kernel(x_ref, nin_ref, w_ref, wg2_ref, bg_ref, gqa_ref, gka_ref, gqm_ref, *rest, per_stream=None):
    fused = per_stream is not None
    if fused:
        (ggla_ref, s0_ref, qa_o, ka_o, va_o, ka16_o, va16_o, sza_o, outb_o, sfin_o, qm_o, szm_o,
         ga_o, gb_o, gm_o, s_scr) = rest
    else:
        (qa_o, ka_o, va_o, ka16_o, va16_o, sza_o, qb_o, kb_o, vb_o, la_o, szb_o, qm_o, szm_o,
         ga_o, gb_o, gm_o) = rest
    if fused:
        @pl.when(pl.program_id(0) % per_stream == 0)
        def _():
            s_scr[...] = s0_ref[pl.program_id(0) // per_stream].reshape(W_BK, DV_B)

    h = _rms_rows(x_ref[...], nin_ref[...]).astype(BF16)

    def proj(i):
        return _dot_nt(h, w_ref[_IN_OFF[i]:_IN_OFF[i + 1], :])

    qb = (proj(4) * (DK_B ** -0.5)).astype(BF16)
    kb = proj(5).astype(BF16)
    vb = proj(6).astype(BF16)
    z = _dot(proj(7).astype(BF16), wg2_ref[...]) + bg_ref[...]
    la = _log_sigmoid(z) * (1.0 / GATE_TAU)
    szb = _silu(proj(8)).astype(BF16)

    scale = HEAD_DIM ** -0.5 * LOG2E

    def emit_qa():
        qa_o[...] = (_head_rms(proj(0), gqa_ref[...], HEAD_DIM) * scale).astype(BF16)

    def emit_ka():
        ka = _head_rms(proj(1), gka_ref[...], HEAD_DIM)
        ka_o[...] = ka
        ka16_o[...] = ka.astype(BF16)

    def emit_va():
        va = proj(2)
        va_o[...] = va
        va16_o[...] = va.astype(BF16)

    def emit_sza():
        sza_o[...] = _silu(proj(3)).astype(BF16)

    def emit_qm():
        qm_o[...] = (_head_rms(proj(9), gqm_ref[...], HEAD_DIM) * scale).astype(BF16)

    def emit_szm():
        szm_o[...] = _silu(proj(10)).astype(BF16)

    def emit_gate(i, ref):
        ref[...] = _sigmoid(proj(i)).astype(BF16)

    pending = [emit_qa, emit_ka, emit_va, emit_sza, emit_qm, emit_szm,
               functools.partial(emit_gate, 11, ga_o), functools.partial(emit_gate, 12, gb_o),
               functools.partial(emit_gate, 13, gm_o)]

    def emit_next():
        if pending:
            pending.pop(0)()

    if fused:
        chunks = [slice(r, r + GLA_C) for r in range(0, x_ref.shape[0], GLA_C)]
        items = [(qb[rows].astype(F32), kb[rows].astype(F32), vb[rows].astype(F32), vb[rows], la[rows])
                 for rows in chunks]
        state = s_scr[...]
        outs = []
        for rows, item, (o, cum) in zip(chunks, items, _gla_intra(items, GLA_C, emit_next)):
            inter, state = _gla_state_step(item[0], item[1], item[3], cum, state)
            outs.append(_gla_finish(o + inter, ggla_ref, szb[rows]))
            emit_next()
        s_scr[...] = state
        outb_o[...] = jnp.concatenate(outs, axis=0)
    else:
        qb_o[...] = qb
        kb_o[...] = kb
        vb_o[...] = vb
        la_o[...] = la
        szb_o[...] = szb
    while pending:
        emit_next()

    if fused:
        @pl.when(pl.program_id(0) % per_stream == per_stream - 1)
        def _():
            sfin_o[0] = s_scr[...].reshape(H_B, DK_B, DV_B)


def _front(x2d, wts, tm, t, keep, s0=None):
    n = x2d.shape[0]
    tm = min(tm, n)
    row = lambda w: pl.BlockSpec((tm, w), lambda i: (i, 0))
    if keep == t:
        kv_rows, kv = n, row(W_A)
    else:
        assert t % tm == 0 and keep % tm == 0
        per_stream, kept = t // tm, keep // tm
        kv_rows = (n // t) * keep
        kv = pl.BlockSpec((tm, W_A), lambda i: (
            (i // per_stream) * kept + jnp.maximum(i % per_stream - (per_stream - kept), 0), 0))
    out = lambda rows_, w, d, spec: (jax.ShapeDtypeStruct((rows_, w), d), spec)
    head = [out(n, W_A, BF16, row(W_A)), out(kv_rows, W_A, F32, kv), out(kv_rows, W_A, F32, kv),
            out(n, W_A, BF16, row(W_A)), out(n, W_A, BF16, row(W_A)), out(n, W_A, BF16, row(W_A))]
    tail = [out(n, W_M, BF16, row(W_M)), out(n, W_M, BF16, row(W_M))] + [out(n, D_MODEL, BF16, row(D_MODEL))] * 3
    in_specs = [row(D_MODEL), _full((1, D_MODEL)), _full((D_IN, D_MODEL)),
                _full((GATE_RANK, W_BK)), _full((1, W_BK)), _full((1, W_A)), _full((1, W_A)), _full((1, W_M))]
    args = [x2d, wts["norm_in"], wts["w_in_t"], wts["w_gate2"], wts["b_gate"], wts["g_qa"], wts["g_ka"], wts["g_qm"]]
    if s0 is None:
        mid = [out(n, W_BK, BF16, row(W_BK)), out(n, W_BK, BF16, row(W_BK)), out(n, W_BV, BF16, row(W_BV)),
               out(n, W_BK, F32, row(W_BK)), out(n, W_BV, BF16, row(W_BV))]
        body, scratch = _front_kernel, []
    else:
        assert t % tm == 0 and tm % GLA_C == 0
        streams = n // t
        state = (streams, H_B, DK_B, DV_B)
        mid = [out(n, W_BV, BF16, row(W_BV)),
               (jax.ShapeDtypeStruct(state, F32),
                pl.BlockSpec((1,) + state[1:], lambda i: (i // (t // tm), 0, 0, 0)))]
        in_specs += [_full((1, W_BV)), _full(state)]
        args += [wts["g_gla"], s0]
        body, scratch = functools.partial(_front_kernel, per_stream=t // tm), [pltpu.VMEM((W_BK, DV_B), F32)]
    outs = head + mid + tail
    return pl.pallas_call(
        body,
        grid=(n // tm,),
        in_specs=in_specs,
        out_specs=[spec for _, spec in outs],
        out_shape=[shape for shape, _ in outs],
        scratch_shapes=scratch,
        compiler_params=_params("arbitrary"),
        name="front",
    )(*args)


def _back_kernel(x_ref, a_ref, b_ref, m_ref, ga_ref, gb_ref, gm_ref, wa_ref, wb_ref, wm_ref, wo_ref, y_ref):
    u = (ga_ref[...].astype(F32) * _dot(a_ref[...], wa_ref[...])
         + gb_ref[...].astype(F32) * _dot(b_ref[...], wb_ref[...])
         + gm_ref[...].astype(F32) * _dot(m_ref[...], wm_ref[...]))
    y_ref[...] = x_ref[...] + _dot(u.astype(BF16), wo_ref[...])


def _back(x2d, a, b, m, ga, gb, gm, wts, tm):
    n = x2d.shape[0]
    tm = min(tm, n)
    row = lambda w: pl.BlockSpec((tm, w), lambda i: (i, 0))
    return pl.pallas_call(
        _back_kernel,
        grid=(n // tm,),
        in_specs=[row(D_MODEL), row(W_A), row(W_BV), row(W_M), row(D_MODEL), row(D_MODEL), row(D_MODEL),
                  _full((W_A, D_MODEL)), _full((W_BV, D_MODEL)), _full((W_M, D_MODEL)),
                  _full((D_MODEL, D_MODEL))],
        out_specs=row(D_MODEL),
        out_shape=jax.ShapeDtypeStruct((n, D_MODEL), F32),
        compiler_params=_params("parallel"),
        name="back",
    )(x2d, a, b, m, ga, gb, gm, wts["w_up_a"], wts["w_up_b"], wts["w_up_m"], wts["w_out"])


def _bias_kernel(e_ref, tile_ref):
    e = jnp.broadcast_to(e_ref[0] * LOG2E, (CHUNK, 2 * BAND))
    tile_ref[0] = pltpu.roll(e, 0, 1, stride=1, stride_axis=0)[:, :BIAS_W]


def _bias_tile(rel_bias):
    j = jnp.arange(2 * BAND)
    dist = jnp.where(j < BIAS_W, BAND - j, REL_CLIP)
    e = rel_bias[:, jnp.clip(dist, -REL_CLIP, REL_CLIP) + REL_CLIP].reshape(H_A, 1, 2 * BAND)
    return pl.pallas_call(
        _bias_kernel,
        grid=(H_A,),
        in_specs=[pl.BlockSpec((1, 1, 2 * BAND), lambda h: (h, 0, 0))],
        out_specs=pl.BlockSpec((1, CHUNK, BIAS_W), lambda h: (h, 0, 0)),
        out_shape=jax.ShapeDtypeStruct((H_A, CHUNK, BIAS_W), F32),
        compiler_params=_params("parallel"),
        name="bias_tile",
    )(e)


def _softmax_pv_staged(scores, values):
    biased = [[s if b is None else s + b for s, b in s_list] for s_list in scores]
    tops = [functools.reduce(jnp.maximum, [jnp.max(s, axis=-1, keepdims=True) for s in s_list])
            for s_list in biased]
    probs = [[jnp.exp2(s - m) for s in s_list] for s_list, m in zip(biased, tops)]
    sums = [functools.reduce(jnp.add, [jnp.sum(x, axis=-1, keepdims=True) for x in p_list]) for p_list in probs]
    return [functools.reduce(jnp.add, [(v(x.astype(BF16)) if callable(v) else _dot(x.astype(BF16), v))
                                       for x, v in zip(p_list, v_list)]) / l
            for p_list, v_list, l in zip(probs, values, sums)]


def _head_pairs_attention(q_pairs, k_lists, v_lists, bias_fn):
    m_rows = q_pairs[0].shape[0]
    lane = lax.broadcasted_iota(jnp.int32, (m_rows, LANES), 1)
    first = lane < HEAD_DIM
    scores = []
    for p, qp in enumerate(q_pairs):
        zero = jnp.zeros_like(qp)
        q2 = jnp.concatenate([jnp.where(first, qp, zero), jnp.where(first, zero, qp)], axis=0)
        s_list = []
        for i, k in enumerate(k_lists[p]):
            b0, b1 = bias_fn(p, 0, i), bias_fn(p, 1, i)
            s_list.append((_dot_nt(q2, k), None if b0 is None else jnp.concatenate([b0, b1], axis=0)))
        scores.append(s_list)
    outs = _softmax_pv_staged(scores, v_lists)
    return [jnp.where(first, o[:m_rows], o[m_rows:]) for o in outs]


def _attn_back_kernel(q_ref, k0_ref, k1_ref, k2_ref, v0_ref, v1_ref, v2_ref, sz_ref, bias_ref,
                      qm_ref, mk_ref, mv_ref, szm_ref,
                      x_ref, b_ref, ga_ref, gb_ref, gm_ref, wa_ref, wb_ref, wm_ref, wo_ref, y_ref,
                      k_buf, v_buf, bias_buf, a_buf, m_buf):
    i = pl.program_id(1)
    blk = jnp.minimum(i, pl.num_programs(1) - 2)
    slot = i % 2

    @pl.when(i == 0)
    def _():
        a_buf[...] = jnp.zeros_like(a_buf)
        m_buf[...] = jnp.zeros_like(m_buf)

    for j, (k_ref, v_ref) in enumerate(((k0_ref, v0_ref), (k1_ref, v1_ref), (k2_ref, v2_ref))):
        k_buf[j * QBLK:(j + 1) * QBLK, :] = k_ref[0]
        v_buf[j * QBLK:(j + 1) * QBLK, :] = v_ref[0]

    @pl.when(blk <= BAND // QBLK)
    def _():
        col = lax.broadcasted_iota(jnp.int32, (1, BIAS_W), 1)
        for c in range(QBLK // CHUNK):
            before_start = jnp.where(col + c * CHUNK < (BAND // QBLK - blk) * QBLK, NEG, 0.0)
            for h in range(H_A):
                bias_buf[c, h] = bias_ref[h] + before_start

    back = {}

    def back_up():
        back["u"] = (ga_ref[0].astype(F32) * _dot(a_buf[1 - slot], wa_ref[...])
                     + gb_ref[0].astype(F32) * _dot(b_ref[0], wb_ref[...])
                     + gm_ref[0].astype(F32) * _dot(m_buf[1 - slot], wm_ref[...])).astype(BF16)

    def back_out():
        y_ref[0] = x_ref[0] + _dot(back["u"], wo_ref[...])

    def mixer_m():
        lanes_of = [slice(p * LANES, (p + 1) * LANES) for p in range(H_M // 2)]
        mk = [[mk_ref[0, :, lanes].astype(BF16)] for lanes in lanes_of]
        mv = [[mv_ref[0, :, lanes].astype(BF16)] for lanes in lanes_of]
        blocks = [(slice(r, r + MEM_ROWS), p) for r in range(0, QBLK, MEM_ROWS) for p in range(len(lanes_of))]
        outs = _head_pairs_attention([qm_ref[0, rows, lanes_of[p]] for rows, p in blocks],
                                     [mk[p] for _, p in blocks], [mv[p] for _, p in blocks],
                                     lambda n, e, _: None)
        for (rows, p), o in zip(blocks, outs):
            m_buf[slot, rows, lanes_of[p]] = (o * szm_ref[0, rows, lanes_of[p]].astype(F32)).astype(BF16)

    pending = {0: back_up, 1: mixer_m, 2: back_out}
    pairs = [slice(p * LANES, (p + 1) * LANES) for p in range(H_A // 2)]
    for c in range(QBLK // CHUNK):
        rows = slice(c * CHUNK, (c + 1) * CHUNK)
        win = slice(c * CHUNK, c * CHUNK + WINDOW)
        outs = _head_pairs_attention(
            [q_ref[0, rows, lanes] for lanes in pairs],
            [[k_buf[win, lanes]] for lanes in pairs], [[v_buf[win, lanes]] for lanes in pairs],
            lambda p, e, _: bias_buf[c, 2 * p + e, :, :WINDOW])
        for lanes, o in zip(pairs, outs):
            a_buf[slot, rows, lanes] = (o * sz_ref[0, rows, lanes].astype(F32)).astype(BF16)
        if c in pending:
            pending[c]()


def _attn_back(q, k, v, sz, bias, qm, mk, mv, szm, x, out_b, ga, gb, gm, wts):
    b, t, _ = q.shape
    nb = BAND // QBLK
    n = t // QBLK
    assert nb == 2 and t % QBLK == 0 and QBLK % MEM_ROWS == 0
    at = lambda d: (lambda bi, i: (bi, jnp.maximum(jnp.minimum(i, n - 1) - d, 0), 0))
    ahead = lambda w, d=0: pl.BlockSpec((1, QBLK, w), at(d))
    kv_specs = [ahead(W_A, nb - j) for j in range(nb + 1)]
    behind = lambda w: pl.BlockSpec((1, QBLK, w), lambda bi, i: (bi, jnp.maximum(i - 1, 0), 0))
    mem = pl.BlockSpec((1, N_MEM, W_M), lambda bi, i: (bi, 0, 0))
    return pl.pallas_call(
        _attn_back_kernel,
        grid=(b, n + 1),
        in_specs=[ahead(W_A)] + kv_specs + kv_specs + [ahead(W_A), _full((H_A, CHUNK, BIAS_W)),
                  ahead(W_M), mem, mem, ahead(W_M),
                  behind(D_MODEL), behind(W_BV), behind(D_MODEL), behind(D_MODEL), behind(D_MODEL),
                  _full((W_A, D_MODEL)), _full((W_BV, D_MODEL)), _full((W_M, D_MODEL)),
                  _full((D_MODEL, D_MODEL))],
        out_specs=behind(D_MODEL),
        out_shape=jax.ShapeDtypeStruct((b, t, D_MODEL), F32),
        scratch_shapes=[pltpu.VMEM((KSPAN, W_A), BF16), pltpu.VMEM((KSPAN, W_A), BF16),
                        pltpu.VMEM((QBLK // CHUNK, H_A, CHUNK, BIAS_W), F32),
                        pltpu.VMEM((2, QBLK, W_A), BF16), pltpu.VMEM((2, QBLK, W_M), BF16)],
        compiler_params=_params("parallel", "arbitrary"),
        name="attn_back",
    )(q, k, k, k, v, v, v, sz, bias, qm, mk, mv, szm, x, out_b, ga, gb, gm,
      wts["w_up_a"], wts["w_up_b"], wts["w_up_m"], wts["w_out"])


STEP_STREAMS = 4


def _attn_step_kernel(q_ref, kn_ref, vn_ref, kp_ref, vp_ref, sz_ref, bias_ref, o_ref):
    s_len = q_ref.shape[1]
    for b in range(STEP_STREAMS):
        scores, values = [], []
        for h in range(H_A):
            lanes = slice(h * HEAD_DIM, (h + 1) * HEAD_DIM)
            q = q_ref[b, :, lanes]
            scores.append([(_dot(q, kp_ref[b, h].astype(BF16)), bias_ref[h, :s_len, :BAND]),
                           (_dot_nt(q, kn_ref[b, :, lanes]), bias_ref[h, :s_len, BAND:BAND + s_len])])
            values.append([functools.partial(_dot_nt, b=vp_ref[b, h].astype(BF16)), vn_ref[b, :, lanes]])
        o = jnp.concatenate(_softmax_pv_staged(scores, values), axis=-1)
        o_ref[b] = (o * sz_ref[b].astype(F32)).astype(BF16)


def _attn_step(q, k_new, v_new, k_past_t, v_past_t, sz, bias):
    b, s_len, _ = q.shape
    assert k_past_t.shape[1:] == (H_A, HEAD_DIM, BAND) and s_len <= CHUNK and b % STEP_STREAMS == 0
    new = pl.BlockSpec((STEP_STREAMS, s_len, W_A), lambda i: (i, 0, 0))
    past = pl.BlockSpec((STEP_STREAMS, H_A, HEAD_DIM, BAND), lambda i: (i, 0, 0, 0))
    return pl.pallas_call(
        _attn_step_kernel,
        grid=(b // STEP_STREAMS,),
        in_specs=[new, new, new, past, past, new, _full((H_A, CHUNK, BIAS_W))],
        out_specs=new,
        out_shape=jax.ShapeDtypeStruct((b, s_len, W_A), BF16),
        compiler_params=_params("parallel"),
        name="attn_step",
    )(q, k_new, v_new, k_past_t, v_past_t, sz, bias)


def _mem_kv_kernel(mem_ref, gmem_ref, w_ref, gkm_ref, mk_ref, mv_ref):
    h = _rms_rows(mem_ref[0], gmem_ref[...]).astype(BF16)
    kv = _dot(h, w_ref[...])
    mk_ref[0] = _head_rms(kv[:, :W_M], gkm_ref[...], HEAD_DIM)
    mv_ref[0] = kv[:, W_M:]


def _mem_kv(mem, wts):
    b = mem.shape[0]
    out = pl.BlockSpec((1, N_MEM, W_M), lambda i: (i, 0, 0))
    return pl.pallas_call(
        _mem_kv_kernel,
        grid=(b,),
        in_specs=[pl.BlockSpec((1, N_MEM, D_MODEL), lambda i: (i, 0, 0)), _full((1, D_MODEL)),
                  _full((D_MODEL, 2 * W_M)), _full((1, W_M))],
        out_specs=[out, out],
        out_shape=[jax.ShapeDtypeStruct((b, N_MEM, W_M), F32)] * 2,
        compiler_params=_params("parallel"),
        name="mem_kv",
    )(mem, wts["g_mem"], wts["w_mem_kv"], wts["g_km"])


MEM_ROWS = 128
MEM_STREAMS = 8


def _attn_mem_step_kernel(q_ref, mk_ref, mv_ref, sz_ref, o_ref):
    for b in range(q_ref.shape[0]):
        scores, values = [], []
        for h in range(H_M):
            lanes = slice(h * HEAD_DIM, (h + 1) * HEAD_DIM)
            scores.append([(_dot(q_ref[b, :, lanes], mk_ref[b, h].astype(BF16)), None)])
            values.append([functools.partial(_dot_nt, b=mv_ref[b, h].astype(BF16))])
        o = jnp.concatenate(_softmax_pv_staged(scores, values), axis=-1)
        o_ref[b] = (o * sz_ref[b].astype(F32)).astype(BF16)


def _attn_mem_step(q, mk_t, mv_t, sz):
    b, t, _ = q.shape
    nb = min(b, MEM_STREAMS)
    assert b % nb == 0 and mk_t.shape[1:] == (H_M, HEAD_DIM, N_MEM)
    rows = pl.BlockSpec((nb, t, W_M), lambda i: (i, 0, 0))
    mem = pl.BlockSpec((nb, H_M, HEAD_DIM, N_MEM), lambda i: (i, 0, 0, 0))
    return pl.pallas_call(
        _attn_mem_step_kernel,
        grid=(b // nb,),
        in_specs=[rows, mem, mem, rows],
        out_specs=rows,
        out_shape=jax.ShapeDtypeStruct((b, t, W_M), BF16),
        compiler_params=_params("parallel"),
        name="attn_mem_step",
    )(q, mk_t, mv_t, sz)


def _gla_intra(items, group, between=lambda: None):
    n = len(items)
    qs, ks, vs, v16s = ([it[j] for it in items] for j in range(4))
    c = qs[0].shape[0]
    las = [it[4] * LOG2E for it in items]
    r_cc = lax.broadcasted_iota(jnp.int32, (c, c), 0)
    c_cc = lax.broadcasted_iota(jnp.int32, (c, c), 1)
    causal = c_cc <= r_cc
    if group < c:
        causal = causal & (r_cc // group == c_cc // group)
    causal16 = causal.astype(BF16)
    cums = [_dot_exact_lhs(causal16, la) for la in las]
    cumxs = [cum - la for cum, la in zip(cums, las)]
    r_ck = lax.broadcasted_iota(jnp.int32, (c, W_BK), 0)

    head_of_k = lax.broadcasted_iota(jnp.int32, (W_BK, W_BV), 0) // DK_B
    head_of_v = lax.broadcasted_iota(jnp.int32, (W_BK, W_BV), 1) // DV_B
    spread = (head_of_k == head_of_v).astype(BF16)
    os = [_dot((q * k).astype(BF16), spread) * v for q, k, v in zip(qs, ks, vs)]

    def shifted(x, j):
        return pltpu.roll(x.reshape(c // SUBLANES, SUBLANES, W_BK), j % SUBLANES, 1).reshape(c, W_BK)

    def block_edges(cum, cumx, half):
        if half >= SUBLANES:
            nb = c // half
            first = jnp.broadcast_to(cumx.reshape(nb, half, W_BK)[:, 0:1, :], (nb, half, W_BK))
            last = jnp.broadcast_to(cum.reshape(nb, half, W_BK)[:, half - 1:half, :], (nb, half, W_BK))
            return first.reshape(c, W_BK), last.reshape(c, W_BK)
        r_in = r_ck % half
        first, last = cumx, cum
        for j in range(1, half):
            first = jnp.where(r_in == j, shifted(cumx, j), first)
            last = jnp.where(r_in == half - 1 - j, shifted(cum, -j), last)
        return first, last

    head_of_row = lax.broadcasted_iota(jnp.int32, (W_BK, c), 0) // DK_B
    head_rows = [jnp.where(head_of_row == h, 1.0, 0.0).astype(BF16) for h in range(H_B)]
    r_att = lax.broadcasted_iota(jnp.int32, (c, H_B * c), 0)
    s_att = lax.broadcasted_iota(jnp.int32, (c, H_B * c), 1) % c
    atts = [jnp.zeros((c, H_B * c), F32) for _ in range(n)]
    half = 1
    while half < group:
        edges = [block_edges(cum, cumx, half) for cum, cumx in zip(cums, cumxs)]
        odd = (r_ck // half) % 2 == 1
        qts = [jnp.where(odd, q * jnp.exp2(cum - first), 0.0).astype(BF16)
               for q, cum, (first, _) in zip(qs, cums, edges)]
        kts = [jnp.where(odd, 0.0, k * jnp.exp2(last - cum)).T.astype(BF16)
               for k, cum, (_, last) in zip(ks, cums, edges)]
        prods = [_dot(qt, jnp.concatenate([kt * m for m in head_rows], axis=1)) for qt, kt in zip(qts, kts)]
        if 2 * half < c:
            same = r_att // (2 * half) == s_att // (2 * half)
            prods = [jnp.where(same, a, 0.0) for a in prods]
        atts = [att + a for att, a in zip(atts, prods)]
        between()
        half *= 2
    if group > 1:
        att16s = [att.astype(BF16) for att in atts]
        os = [o + jnp.concatenate([_dot(att16[:, h * c:(h + 1) * c], v16[:, h * DV_B:(h + 1) * DV_B])
                                   for h in range(H_B)], axis=-1)
              for o, att16, v16 in zip(os, att16s, v16s)]
    return list(zip(os, cums))


def _gla_state_step(q, k, v16, cum, s_old):
    c = q.shape[0]
    q_in = (q * jnp.exp2(cum)).astype(BF16)
    k_out_t = (k * jnp.exp2(cum[c - 1:c, :] - cum)).T.astype(BF16)
    keep = jnp.exp2(cum.T[:, c - 1:c])
    kd = lambda h: slice(h * DK_B, (h + 1) * DK_B)
    vd = lambda h: slice(h * DV_B, (h + 1) * DV_B)
    o = jnp.concatenate([_dot(q_in[:, kd(h)], s_old[kd(h)].astype(BF16)) for h in range(H_B)], axis=-1)
    s_new = jnp.concatenate([s_old[kd(h)] * keep[kd(h)] + _dot(k_out_t[kd(h)], v16[:, vd(h)])
                             for h in range(H_B)], axis=0)
    return o, s_new


def _gla_finish(o, g_ref, sz):
    outs = [_rms_rows(o[:, h * DV_B:(h + 1) * DV_B], g_ref[:, h * DV_B:(h + 1) * DV_B]) for h in range(H_B)]
    return (jnp.concatenate(outs, axis=-1) * sz.astype(F32)).astype(BF16)


def _gla_step_kernel(q_ref, k_ref, v_ref, la_ref, sz_ref, g_ref, s0_ref, o_ref, s_ref, *, t):
    c = GLA_C
    n = c // t
    q, k, la = q_ref[...].astype(F32), k_ref[...].astype(F32), la_ref[...]
    v16 = v_ref[...]
    (o, cum), = _gla_intra([(q, k, v16.astype(F32), v16, la)], t)

    q_in = (q * jnp.exp2(cum)).astype(BF16)
    cum_t, k_t = cum.T, k.T
    stream_of_col = lax.broadcasted_iota(jnp.int32, (W_BK, c), 1) // t
    inter = []
    for s in range(n):
        rows = slice(s * t, (s + 1) * t)
        last = cum_t[:, (s + 1) * t - 1:(s + 1) * t]
        k_out_t = (k_t * jnp.exp2(jnp.where(stream_of_col == s, last - cum_t, NEG))).astype(BF16)
        keep = jnp.exp2(last)
        o_s = []
        for h in range(H_B):
            ks = slice(h * DK_B, (h + 1) * DK_B)
            vs = slice(h * DV_B, (h + 1) * DV_B)
            s_old = s0_ref[s, h]
            o_s.append(_dot(q_in[rows, ks], s_old.astype(BF16)))
            s_ref[s, h] = s_old * keep[ks] + _dot(k_out_t[ks], v16[:, vs])
        inter.append(jnp.concatenate(o_s, axis=-1))
    o_ref[...] = _gla_finish(o + jnp.concatenate(inter, axis=0), g_ref, sz_ref[...])


def _gla_step(q, k, v, la, sz, g, s0):
    b, t, _ = q.shape
    n = GLA_C // t
    assert GLA_C % t == 0 and t % (2 * GLA_BASE) == 0 and b % n == 0
    flat = lambda z: z.reshape(b * t, z.shape[-1])
    blk = lambda w: pl.BlockSpec((GLA_C, w), lambda i: (i, 0))
    state = pl.BlockSpec((n, H_B, DK_B, DV_B), lambda i: (i, 0, 0, 0))
    o, s_new = pl.pallas_call(
        functools.partial(_gla_step_kernel, t=t),
        grid=(b // n,),
        in_specs=[blk(W_BK), blk(W_BK), blk(W_BV), blk(W_BK), blk(W_BV),
                  pl.BlockSpec((1, W_BV), lambda i: (0, 0)), state],
        out_specs=[blk(W_BV), state],
        out_shape=[jax.ShapeDtypeStruct((b * t, W_BV), BF16),
                   jax.ShapeDtypeStruct((b, H_B, DK_B, DV_B), F32)],
        compiler_params=_params("parallel"),
        name="gla_step",
    )(flat(q), flat(k), flat(v), flat(la), flat(sz), g, s0)
    return o.reshape(b, t, W_BV), s_new


def _prep_weights(l, norm_in, w_in, g_qa, g_ka, w_gate2, b_gate, g_gla_out, g_mem, w_mem_kv, g_qm, g_km,
                  w_up_a, w_up_b, w_up_m, w_out):
    tile = lambda gain, n: jnp.tile(gain, n).reshape(1, -1)
    return {
        "norm_in": norm_in[l].reshape(1, D_MODEL),
        "w_in_t": w_in[l].T.astype(BF16),
        "w_gate2": w_gate2[l].astype(BF16),
        "b_gate": b_gate[l].reshape(1, W_BK),
        "g_qa": tile(g_qa[l], H_A), "g_ka": tile(g_ka[l], H_A), "g_qm": tile(g_qm[l], H_M),
        "g_km": tile(g_km[l], H_M), "g_gla": tile(g_gla_out[l], H_B),
        "g_mem": g_mem[l].reshape(1, D_MODEL),
        "w_mem_kv": w_mem_kv[l].astype(BF16),
        "w_up_a": w_up_a[l].astype(BF16), "w_up_b": w_up_b[l].astype(BF16),
        "w_up_m": w_up_m[l].astype(BF16), "w_out": w_out[l].astype(BF16),
    }


def _layer_long(x, wts, bias, mk, mv, s0, tm):
    b, t, _ = x.shape
    keep = min(BAND, t)
    r3 = lambda z: z.reshape(b, t, z.shape[-1])
    heads = lambda z: z.reshape(b, keep, H_A, HEAD_DIM)
    (qa, ka, va, ka16, va16, sza, out_b, s_new, qm, szm, ga, gb, gm) = _front(
        x.reshape(b * t, D_MODEL), wts, tm, t, keep, s0)
    y = _attn_back(r3(qa), r3(ka16), r3(va16), r3(sza), bias, r3(qm), mk, mv, r3(szm), x, r3(out_b),
                   r3(ga), r3(gb), r3(gm), wts)
    return y, heads(ka), heads(va), s_new


def _layer_short(x, wts, attend_a, attend_m, s0, tm):
    b, t, _ = x.shape
    keep = min(BAND, t)
    x2d = x.reshape(b * t, D_MODEL)
    r3 = lambda z: z.reshape(b, t, z.shape[-1])
    flat = lambda z: z.reshape(b * t, z.shape[-1])
    heads = lambda z: z.reshape(b, keep, H_A, HEAD_DIM)
    (qa, ka, va, ka16, va16, sza, qb, kb, vb, la, szb, qm, szm, ga, gb, gm) = _front(x2d, wts, tm, t, keep)
    out_b, s_new = _gla_step(r3(qb), r3(kb), r3(vb), r3(la), r3(szb), wts["g_gla"], s0)
    out_a = attend_a(r3(qa), r3(ka16), r3(va16), r3(sza))
    out_m = attend_m(r3(qm), r3(szm))
    y = _back(x2d, flat(out_a), flat(out_b), flat(out_m), ga, gb, gm, wts, tm)
    return y.reshape(b, t, D_MODEL), heads(ka), heads(va), s_new


def kernel(x_prompt, x_sample, mem_prompt, cache_a_k, cache_a_v, state_gla, cache_mem_k, cache_mem_v,
           norm_in, w_in, g_qa, g_ka, rel_bias, w_gate2, b_gate, g_gla_out, g_mem, w_mem_kv, g_qm, g_km,
           w_up_a, w_up_b, w_up_m, w_out):
    depth = w_in.shape[0]
    xp, xs = x_prompt, x_sample
    bp, tp, _ = xp.shape
    bs, ts, _ = xs.shape
    akp, avp, sgp, mkp, mvp, aks, avs, sgs = [], [], [], [], [], [], [], []
    for l in range(depth):
        wts = _prep_weights(l, norm_in, w_in, g_qa, g_ka, w_gate2, b_gate, g_gla_out, g_mem, w_mem_kv,
                            g_qm, g_km, w_up_a, w_up_b, w_up_m, w_out)
        bias = _bias_tile(rel_bias[l])
        mk, mv = _mem_kv(mem_prompt, wts)
        s0 = jnp.zeros((bp, H_B, DK_B, DV_B), state_gla.dtype)
        xp, ka, va, sp = _layer_long(xp, wts, bias, mk, mv, s0, 512)
        akp.append(ka)
        avp.append(va)
        sgp.append(sp)
        mkp.append(mk.reshape(bp, N_MEM, H_M, HEAD_DIM))
        mvp.append(mv.reshape(bp, N_MEM, H_M, HEAD_DIM))
        rows_last = lambda z: jnp.transpose(z, (0, 2, 3, 1))
        past_k, past_v = rows_last(cache_a_k[l]), rows_last(cache_a_v[l])
        mem_k, mem_v = rows_last(cache_mem_k[l]), rows_last(cache_mem_v[l])
        attend_s = lambda q, k, v, sz: _attn_step(q, k, v, past_k, past_v, sz, bias)
        attend_ms = lambda q, sz: _attn_mem_step(q, mem_k, mem_v, sz)
        xs, ka_s, va_s, ss = _layer_short(xs, wts, attend_s, attend_ms, state_gla[l], 512)
        aks.append(ka_s)
        avs.append(va_s)
        sgs.append(ss)
    return (xp, xs, jnp.stack(akp), jnp.stack(avp), jnp.stack(sgp), jnp.stack(mkp), jnp.stack(mvp),
            jnp.stack(aks), jnp.stack(avs), jnp.stack(sgs))
```

```python
import functools

import jax
import jax.numpy as jnp
from jax import lax
from jax.experimental import pallas as pl
from jax.experimental.pallas import tpu as pltpu

F32 = jnp.float32
BF16 = jnp.bfloat16

D_MODEL = 1024
CHUNK = 64
LEFT_CHUNKS = 8
HEAD_DIM = 64
H_A = 8
W_A = H_A * HEAD_DIM
REL_CLIP = 128
H_B = 4
DK_B = 64
DV_B = 128
W_BK = H_B * DK_B
W_BV = H_B * DV_B
GATE_RANK = 16
GATE_TAU = 16.0
N_MEM = 256
H_M = 4
W_M = H_M * HEAD_DIM
EPS = 1e-6
IN_SIZES = (W_A, W_A, W_A, W_A, W_BK, W_BK, W_BV, GATE_RANK, W_BV, W_M, W_M, D_MODEL, D_MODEL, D_MODEL)

LANES = 128
SUBLANES = 8
VMEM_LIMIT = 56 * 1024 * 1024
NEG = -1e30
LOG2E = 1.4426950408889634

BAND = LEFT_CHUNKS * CHUNK
QBLK = 256
KSPAN = BAND + QBLK
WINDOW = BAND + CHUNK
BIAS_W = BAND + 2 * CHUNK
GLA_C = 128
GLA_BASE = SUBLANES


def _dot(a, b):
    return jnp.dot(a, b, preferred_element_type=F32)


def _dot_nt(a, b):
    return lax.dot_general(a, b, (((1,), (1,)), ((), ())), preferred_element_type=F32)


def _split3(x):
    hi = x.astype(BF16)
    r = x - hi.astype(F32)
    mid = r.astype(BF16)
    lo = (r - mid.astype(F32)).astype(BF16)
    return hi, mid, lo


def _dot_exact_lhs(a01, x):
    hi, mid, lo = _split3(x)
    return _dot(a01, hi) + _dot(a01, mid) + _dot(a01, lo)


def _sigmoid(z):
    return 1.0 / (1.0 + jnp.exp(-z))


def _silu(z):
    return z * _sigmoid(z)


def _log_sigmoid(z):
    return jnp.minimum(z, 0.0) - jnp.log1p(jnp.exp(-jnp.abs(z)))


def _rms_rows(x, g):
    ms = jnp.mean(x * x, axis=-1, keepdims=True)
    return x * lax.rsqrt(ms + EPS) * g


def _head_mean_matrix(width, head):
    r = lax.broadcasted_iota(jnp.int32, (width, width), 0) // head
    c = lax.broadcasted_iota(jnp.int32, (width, width), 1) // head
    return jnp.where(r == c, 1.0 / head, 0.0).astype(BF16)


def _head_rms(x, g, head):
    avg = _head_mean_matrix(x.shape[-1], head)
    ms = _dot((x * x).astype(BF16), avg)
    return x * lax.rsqrt(ms + EPS) * g


def _full(shape):
    nd = len(shape)
    return pl.BlockSpec(shape, lambda *_: (0,) * nd, pipeline_mode=pl.Buffered(1))


def _params(*sem):
    return pltpu.CompilerParams(dimension_semantics=sem, vmem_limit_bytes=VMEM_LIMIT)


_IN_OFF = tuple(int(sum(IN_SIZES[:i])) for i in range(len(IN_SIZES) + 1))
D_IN = _IN_OFF[-1]


def _front_kernel(x_ref, nin_ref, w_ref, wg2_ref, bg_ref, gqa_ref, gka_ref, gqm_ref, *rest, per_stream=None):
    fused = per_stream is not None
    if fused:
        (ggla_ref, s0_ref, qa_o, ka_o, va_o, ka16_o, va16_o, sza_o, outb_o, sfin_o, qm_o, szm_o,
         ga_o, gb_o, gm_o, s_scr) = rest
    else:
        (qa_o, ka_o, va_o, ka16_o, va16_o, sza_o, qb_o, kb_o, vb_o, la_o, szb_o, qm_o, szm_o,
         ga_o, gb_o, gm_o) = rest
    if fused:
        @pl.when(pl.program_id(0) % per_stream == 0)
        def _():
            s_scr[...] = s0_ref[pl.program_id(0) // per_stream].reshape(W_BK, DV_B)

    h = _rms_rows(x_ref[...], nin_ref[...]).astype(BF16)

    def proj(i):
        return _dot_nt(h, w_ref[_IN_OFF[i]:_IN_OFF[i + 1], :])

    qb = (proj(4) * (DK_B ** -0.5)).astype(BF16)
    kb = proj(5).astype(BF16)
    vb = proj(6).astype(BF16)
    z = _dot(proj(7).astype(BF16), wg2_ref[...]) + bg_ref[...]
    la = _log_sigmoid(z) * (1.0 / GATE_TAU)
    szb = _silu(proj(8)).astype(BF16)

    scale = HEAD_DIM ** -0.5 * LOG2E

    def emit_qa():
        qa_o[...] = (_head_rms(proj(0), gqa_ref[...], HEAD_DIM) * scale).astype(BF16)

    def emit_ka():
        ka = _head_rms(proj(1), gka_ref[...], HEAD_DIM)
        ka_o[...] = ka
        ka16_o[...] = ka.astype(BF16)

    def emit_va():
        va = proj(2)
        va_o[...] = va
        va16_o[...] = va.astype(BF16)

    def emit_sza():
        sza_o[...] = _silu(proj(3)).astype(BF16)

    def emit_qm():
        qm_o[...] = (_head_rms(proj(9), gqm_ref[...], HEAD_DIM) * scale).astype(BF16)

    def emit_szm():
        szm_o[...] = _silu(proj(10)).astype(BF16)

    def emit_gate(i, ref):
        ref[...] = _sigmoid(proj(i)).astype(BF16)

    pending = [emit_qa, emit_ka, emit_va, emit_sza, emit_qm, emit_szm,
               functools.partial(emit_gate, 11, ga_o), functools.partial(emit_gate, 12, gb_o),
               functools.partial(emit_gate, 13, gm_o)]

    def emit_next():
        if pending:
            pending.pop(0)()

    if fused:
        chunks = [slice(r, r + GLA_C) for r in range(0, x_ref.shape[0], GLA_C)]
        items = [(qb[rows].astype(F32), kb[rows].astype(F32), vb[rows].astype(F32), vb[rows], la[rows])
                 for rows in chunks]
        state = s_scr[...]
        outs = []
        for rows, item, (o, cum) in zip(chunks, items, _gla_intra(items, GLA_C, emit_next)):
            inter, state = _gla_state_step(item[0], item[1], item[3], cum, state)
            outs.append(_gla_finish(o + inter, ggla_ref, szb[rows]))
            emit_next()
        s_scr[...] = state
        outb_o[...] = jnp.concatenate(outs, axis=0)
    else:
        qb_o[...] = qb
        kb_o[...] = kb
        vb_o[...] = vb
        la_o[...] = la
        szb_o[...] = szb
    while pending:
        emit_next()

    if fused:
        @pl.when(pl.program_id(0) % per_stream == per_stream - 1)
        def _():
            sfin_o[0] = s_scr[...].reshape(H_B, DK_B, DV_B)


def _front(x2d, wts, tm, t, keep, s0=None):
    n = x2d.shape[0]
    tm = min(tm, n)
    row = lambda w: pl.BlockSpec((tm, w), lambda i: (i, 0))
    if keep == t:
        kv_rows, kv = n, row(W_A)
    else:
        assert t % tm == 0 and keep % tm == 0
        per_stream, kept = t // tm, keep // tm
        kv_rows = (n // t) * keep
        kv = pl.BlockSpec((tm, W_A), lambda i: (
            (i // per_stream) * kept + jnp.maximum(i % per_stream - (per_stream - kept), 0), 0))
    out = lambda rows_, w, d, spec: (jax.ShapeDtypeStruct((rows_, w), d), spec)
    head = [out(n, W_A, BF16, row(W_A)), out(kv_rows, W_A, F32, kv), out(kv_rows, W_A, F32, kv),
            out(n, W_A, BF16, row(W_A)), out(n, W_A, BF16, row(W_A)), out(n, W_A, BF16, row(W_A))]
    tail = [out(n, W_M, BF16, row(W_M)), out(n, W_M, BF16, row(W_M))] + [out(n, D_MODEL, BF16, row(D_MODEL))] * 3
    in_specs = [row(D_MODEL), _full((1, D_MODEL)), _full((D_IN, D_MODEL)),
                _full((GATE_RANK, W_BK)), _full((1, W_BK)), _full((1, W_A)), _full((1, W_A)), _full((1, W_M))]
    args = [x2d, wts["norm_in"], wts["w_in_t"], wts["w_gate2"], wts["b_gate"], wts["g_qa"], wts["g_ka"], wts["g_qm"]]
    if s0 is None:
        mid = [out(n, W_BK, BF16, row(W_BK)), out(n, W_BK, BF16, row(W_BK)), out(n, W_BV, BF16, row(W_BV)),
               out(n, W_BK, F32, row(W_BK)), out(n, W_BV, BF16, row(W_BV))]
        body, scratch = _front_kernel, []
    else:
        assert t % tm == 0 and tm % GLA_C == 0
        streams = n // t
        state = (streams, H_B, DK_B, DV_B)
        mid = [out(n, W_BV, BF16, row(W_BV)),
               (jax.ShapeDtypeStruct(state, F32),
                pl.BlockSpec((1,) + state[1:], lambda i: (i // (t // tm), 0, 0, 0)))]
        in_specs += [_full((1, W_BV)), _full(state)]
        args += [wts["g_gla"], s0]
        body, scratch = functools.partial(_front_kernel, per_stream=t // tm), [pltpu.VMEM((W_BK, DV_B), F32)]
    outs = head + mid + tail
    return pl.pallas_call(
        body,
        grid=(n // tm,),
        in_specs=in_specs,
        out_specs=[spec for _, spec in outs],
        out_shape=[shape for shape, _ in outs],
        scratch_shapes=scratch,
        compiler_params=_params("arbitrary"),
        name="front",
    )(*args)


def _back_kernel(x_ref, a_ref, b_ref, m_ref, ga_ref, gb_ref, gm_ref, wa_ref, wb_ref, wm_ref, wo_ref, y_ref):
    u = (ga_ref[...].astype(F32) * _dot(a_ref[...], wa_ref[...])
         + gb_ref[...].astype(F32) * _dot(b_ref[...], wb_ref[...])
         + gm_ref[...].astype(F32) * _dot(m_ref[...], wm_ref[...]))
    y_ref[...] = x_ref[...] + _dot(u.astype(BF16), wo_ref[...])


def _back(x2d, a, b, m, ga, gb, gm, wts, tm):
    n = x2d.shape[0]
    tm = min(tm, n)
    row = lambda w: pl.BlockSpec((tm, w), lambda i: (i, 0))
    return pl.pallas_call(
        _back_kernel,
        grid=(n // tm,),
        in_specs=[row(D_MODEL), row(W_A), row(W_BV), row(W_M), row(D_MODEL), row(D_MODEL), row(D_MODEL),
                  _full((W_A, D_MODEL)), _full((W_BV, D_MODEL)), _full((W_M, D_MODEL)),
                  _full((D_MODEL, D_MODEL))],
        out_specs=row(D_MODEL),
        out_shape=jax.ShapeDtypeStruct((n, D_MODEL), F32),
        compiler_params=_params("parallel"),
        name="back",
    )(x2d, a, b, m, ga, gb, gm, wts["w_up_a"], wts["w_up_b"], wts["w_up_m"], wts["w_out"])


def _bias_kernel(e_ref, tile_ref):
    e = jnp.broadcast_to(e_ref[0] * LOG2E, (CHUNK, 2 * BAND))
    tile_ref[0] = pltpu.roll(e, 0, 1, stride=1, stride_axis=0)[:, :BIAS_W]


def _bias_tile(rel_bias):
    j = jnp.arange(2 * BAND)
    dist = jnp.where(j < BIAS_W, BAND - j, REL_CLIP)
    e = rel_bias[:, jnp.clip(dist, -REL_CLIP, REL_CLIP) + REL_CLIP].reshape(H_A, 1, 2 * BAND)
    return pl.pallas_call(
        _bias_kernel,
        grid=(H_A,),
        in_specs=[pl.BlockSpec((1, 1, 2 * BAND), lambda h: (h, 0, 0))],
        out_specs=pl.BlockSpec((1, CHUNK, BIAS_W), lambda h: (h, 0, 0)),
        out_shape=jax.ShapeDtypeStruct((H_A, CHUNK, BIAS_W), F32),
        compiler_params=_params("parallel"),
        name="bias_tile",
    )(e)


def _softmax_pv_staged(scores, values):
    biased = [[s if b is None else s + b for s, b in s_list] for s_list in scores]
    tops = [functools.reduce(jnp.maximum, [jnp.max(s, axis=-1, keepdims=True) for s in s_list])
            for s_list in biased]
    probs = [[jnp.exp2(s - m) for s in s_list] for s_list, m in zip(biased, tops)]
    sums = [functools.reduce(jnp.add, [jnp.sum(x, axis=-1, keepdims=True) for x in p_list]) for p_list in probs]
    return [functools.reduce(jnp.add, [(v(x.astype(BF16)) if callable(v) else _dot(x.astype(BF16), v))
                                       for x, v in zip(p_list, v_list)]) / l
            for p_list, v_list, l in zip(probs, values, sums)]


def _head_pairs_attention(q_pairs, k_lists, v_lists, bias_fn):
    m_rows = q_pairs[0].shape[0]
    lane = lax.broadcasted_iota(jnp.int32, (m_rows, LANES), 1)
    first = lane < HEAD_DIM
    scores = []
    for p, qp in enumerate(q_pairs):
        zero = jnp.zeros_like(qp)
        q2 = jnp.concatenate([jnp.where(first, qp, zero), jnp.where(first, zero, qp)], axis=0)
        s_list = []
        for i, k in enumerate(k_lists[p]):
            b0, b1 = bias_fn(p, 0, i), bias_fn(p, 1, i)
            s_list.append((_dot_nt(q2, k), None if b0 is None else jnp.concatenate([b0, b1], axis=0)))
        scores.append(s_list)
    outs = _softmax_pv_staged(scores, v_lists)
    return [jnp.where(first, o[:m_rows], o[m_rows:]) for o in outs]


def _attn_back_kernel(q_ref, k0_ref, k1_ref, k2_ref, v0_ref, v1_ref, v2_ref, sz_ref, bias_ref,
                      qm_ref, mk_ref, mv_ref, szm_ref,
                      x_ref, b_ref, ga_ref, gb_ref, gm_ref, wa_ref, wb_ref, wm_ref, wo_ref, y_ref,
                      k_buf, v_buf, bias_buf, a_buf, m_buf):
    i = pl.program_id(1)
    for j, (k_ref, v_ref) in enumerate(((k0_ref, v0_ref), (k1_ref, v1_ref), (k2_ref, v2_ref))):
        k_buf[j * QBLK:(j + 1) * QBLK, :] = k_ref[0]
        v_buf[j * QBLK:(j + 1) * QBLK, :] = v_ref[0]

    @pl.when(i <= BAND // QBLK)
    def _():
        col = lax.broadcasted_iota(jnp.int32, (1, BIAS_W), 1)
        for c in range(QBLK // CHUNK):
            before_start = jnp.where(col + c * CHUNK < (BAND // QBLK - i) * QBLK, NEG, 0.0)
            for h in range(H_A):
                bias_buf[c, h] = bias_ref[h] + before_start

    lanes_m = [slice(p * LANES, (p + 1) * LANES) for p in range(H_M // 2)]
    mk = [[mk_ref[0, :, lanes].astype(BF16)] for lanes in lanes_m]
    mv = [[mv_ref[0, :, lanes].astype(BF16)] for lanes in lanes_m]
    blocks = [(slice(r, r + MEM_ROWS), p) for r in range(0, QBLK, MEM_ROWS) for p in range(len(lanes_m))]
    outs = _head_pairs_attention([qm_ref[0, rows, lanes_m[p]] for rows, p in blocks],
                                 [mk[p] for _, p in blocks], [mv[p] for _, p in blocks], lambda n, e, _: None)
    for (rows, p), o in zip(blocks, outs):
        m_buf[rows, lanes_m[p]] = (o * szm_ref[0, rows, lanes_m[p]].astype(F32)).astype(BF16)

    pairs = [slice(p * LANES, (p + 1) * LANES) for p in range(H_A // 2)]
    for c in range(QBLK // CHUNK):
        rows = slice(c * CHUNK, (c + 1) * CHUNK)
        win = slice(c * CHUNK, c * CHUNK + WINDOW)
        outs = _head_pairs_attention(
            [q_ref[0, rows, lanes] for lanes in pairs],
            [[k_buf[win, lanes]] for lanes in pairs], [[v_buf[win, lanes]] for lanes in pairs],
            lambda p, e, _: bias_buf[c, 2 * p + e, :, :WINDOW])
        for lanes, o in zip(pairs, outs):
            a_buf[rows, lanes] = (o * sz_ref[0, rows, lanes].astype(F32)).astype(BF16)

    u = (ga_ref[0].astype(F32) * _dot(a_buf[...], wa_ref[...])
         + gb_ref[0].astype(F32) * _dot(b_ref[0], wb_ref[...])
         + gm_ref[0].astype(F32) * _dot(m_buf[...], wm_ref[...]))
    y_ref[0] = x_ref[0] + _dot(u.astype(BF16), wo_ref[...])


def _attn_back(q, k, v, sz, bias, qm, mk, mv, szm, x, out_b, ga, gb, gm, wts):
    b, t, _ = q.shape
    nb = BAND // QBLK
    n = t // QBLK
    assert nb == 2 and t % QBLK == 0 and QBLK % MEM_ROWS == 0
    rows = lambda w, d=0: pl.BlockSpec((1, QBLK, w), lambda bi, i: (bi, jnp.maximum(i - d, 0), 0))
    kv_specs = [rows(W_A, nb - j) for j in range(nb + 1)]
    mem = pl.BlockSpec((1, N_MEM, W_M), lambda bi, i: (bi, 0, 0))
    return pl.pallas_call(
        _attn_back_kernel,
        grid=(b, n),
        in_specs=[rows(W_A)] + kv_specs + kv_specs + [rows(W_A), _full((H_A, CHUNK, BIAS_W)),
                  rows(W_M), mem, mem, rows(W_M),
                  rows(D_MODEL), rows(W_BV), rows(D_MODEL), rows(D_MODEL), rows(D_MODEL),
                  _full((W_A, D_MODEL)), _full((W_BV, D_MODEL)), _full((W_M, D_MODEL)),
                  _full((D_MODEL, D_MODEL))],
        out_specs=rows(D_MODEL),
        out_shape=jax.ShapeDtypeStruct((b, t, D_MODEL), F32),
        scratch_shapes=[pltpu.VMEM((KSPAN, W_A), BF16), pltpu.VMEM((KSPAN, W_A), BF16),
                        pltpu.VMEM((QBLK // CHUNK, H_A, CHUNK, BIAS_W), F32),
                        pltpu.VMEM((QBLK, W_A), BF16), pltpu.VMEM((QBLK, W_M), BF16)],
        compiler_params=_params("parallel", "arbitrary"),
        name="attn_back",
    )(q, k, k, k, v, v, v, sz, bias, qm, mk, mv, szm, x, out_b, ga, gb, gm,
      wts["w_up_a"], wts["w_up_b"], wts["w_up_m"], wts["w_out"])


STEP_STREAMS = 4


def _attn_step_kernel(q_ref, kn_ref, vn_ref, kp_ref, vp_ref, sz_ref, bias_ref, o_ref):
    s_len = q_ref.shape[1]
    for b in range(STEP_STREAMS):
        scores, values = [], []
        for h in range(H_A):
            lanes = slice(h * HEAD_DIM, (h + 1) * HEAD_DIM)
            q = q_ref[b, :, lanes]
            scores.append([(_dot(q, kp_ref[b, h].astype(BF16)), bias_ref[h, :s_len, :BAND]),
                           (_dot_nt(q, kn_ref[b, :, lanes]), bias_ref[h, :s_len, BAND:BAND + s_len])])
            values.append([functools.partial(_dot_nt, b=vp_ref[b, h].astype(BF16)), vn_ref[b, :, lanes]])
        o = jnp.concatenate(_softmax_pv_staged(scores, values), axis=-1)
        o_ref[b] = (o * sz_ref[b].astype(F32)).astype(BF16)


def _attn_step(q, k_new, v_new, k_past_t, v_past_t, sz, bias):
    b, s_len, _ = q.shape
    assert k_past_t.shape[1:] == (H_A, HEAD_DIM, BAND) and s_len <= CHUNK and b % STEP_STREAMS == 0
    new = pl.BlockSpec((STEP_STREAMS, s_len, W_A), lambda i: (i, 0, 0))
    past = pl.BlockSpec((STEP_STREAMS, H_A, HEAD_DIM, BAND), lambda i: (i, 0, 0, 0))
    return pl.pallas_call(
        _attn_step_kernel,
        grid=(b // STEP_STREAMS,),
        in_specs=[new, new, new, past, past, new, _full((H_A, CHUNK, BIAS_W))],
        out_specs=new,
        out_shape=jax.ShapeDtypeStruct((b, s_len, W_A), BF16),
        compiler_params=_params("parallel"),
        name="attn_step",
    )(q, k_new, v_new, k_past_t, v_past_t, sz, bias)


def _mem_kv_kernel(mem_ref, gmem_ref, w_ref, gkm_ref, mk_ref, mv_ref):
    h = _rms_rows(mem_ref[0], gmem_ref[...]).astype(BF16)
    kv = _dot(h, w_ref[...])
    mk_ref[0] = _head_rms(kv[:, :W_M], gkm_ref[...], HEAD_DIM)
    mv_ref[0] = kv[:, W_M:]


def _mem_kv(mem, wts):
    b = mem.shape[0]
    out = pl.BlockSpec((1, N_MEM, W_M), lambda i: (i, 0, 0))
    return pl.pallas_call(
        _mem_kv_kernel,
        grid=(b,),
        in_specs=[pl.BlockSpec((1, N_MEM, D_MODEL), lambda i: (i, 0, 0)), _full((1, D_MODEL)),
                  _full((D_MODEL, 2 * W_M)), _full((1, W_M))],
        out_specs=[out, out],
        out_shape=[jax.ShapeDtypeStruct((b, N_MEM, W_M), F32)] * 2,
        compiler_params=_params("parallel"),
        name="mem_kv",
    )(mem, wts["g_mem"], wts["w_mem_kv"], wts["g_km"])


MEM_ROWS = 128
MEM_STREAMS = 8


def _attn_mem_step_kernel(q_ref, mk_ref, mv_ref, sz_ref, o_ref):
    for b in range(q_ref.shape[0]):
        scores, values = [], []
        for h in range(H_M):
            lanes = slice(h * HEAD_DIM, (h + 1) * HEAD_DIM)
            scores.append([(_dot(q_ref[b, :, lanes], mk_ref[b, h].astype(BF16)), None)])
            values.append([functools.partial(_dot_nt, b=mv_ref[b, h].astype(BF16))])
        o = jnp.concatenate(_softmax_pv_staged(scores, values), axis=-1)
        o_ref[b] = (o * sz_ref[b].astype(F32)).astype(BF16)


def _attn_mem_step(q, mk_t, mv_t, sz):
    b, t, _ = q.shape
    nb = min(b, MEM_STREAMS)
    assert b % nb == 0 and mk_t.shape[1:] == (H_M, HEAD_DIM, N_MEM)
    rows = pl.BlockSpec((nb, t, W_M), lambda i: (i, 0, 0))
    mem = pl.BlockSpec((nb, H_M, HEAD_DIM, N_MEM), lambda i: (i, 0, 0, 0))
    return pl.pallas_call(
        _attn_mem_step_kernel,
        grid=(b // nb,),
        in_specs=[rows, mem, mem, rows],
        out_specs=rows,
        out_shape=jax.ShapeDtypeStruct((b, t, W_M), BF16),
        compiler_params=_params("parallel"),
        name="attn_mem_step",
    )(q, mk_t, mv_t, sz)


def _gla_intra(items, group, between=lambda: None):
    n = len(items)
    qs, ks, vs, v16s = ([it[j] for it in items] for j in range(4))
    c = qs[0].shape[0]
    las = [it[4] * LOG2E for it in items]
    r_cc = lax.broadcasted_iota(jnp.int32, (c, c), 0)
    c_cc = lax.broadcasted_iota(jnp.int32, (c, c), 1)
    causal = c_cc <= r_cc
    if group < c:
        causal = causal & (r_cc // group == c_cc // group)
    causal16 = causal.astype(BF16)
    cums = [_dot_exact_lhs(causal16, la) for la in las]
    cumxs = [cum - la for cum, la in zip(cums, las)]
    r_ck = lax.broadcasted_iota(jnp.int32, (c, W_BK), 0)

    head_of_k = lax.broadcasted_iota(jnp.int32, (W_BK, W_BV), 0) // DK_B
    head_of_v = lax.broadcasted_iota(jnp.int32, (W_BK, W_BV), 1) // DV_B
    spread = (head_of_k == head_of_v).astype(BF16)
    os = [_dot((q * k).astype(BF16), spread) * v for q, k, v in zip(qs, ks, vs)]

    def shifted(x, j):
        return pltpu.roll(x.reshape(c // SUBLANES, SUBLANES, W_BK), j % SUBLANES, 1).reshape(c, W_BK)

    def block_edges(cum, cumx, half):
        if half >= SUBLANES:
            nb = c // half
            first = jnp.broadcast_to(cumx.reshape(nb, half, W_BK)[:, 0:1, :], (nb, half, W_BK))
            last = jnp.broadcast_to(cum.reshape(nb, half, W_BK)[:, half - 1:half, :], (nb, half, W_BK))
            return first.reshape(c, W_BK), last.reshape(c, W_BK)
        r_in = r_ck % half
        first, last = cumx, cum
        for j in range(1, half):
            first = jnp.where(r_in == j, shifted(cumx, j), first)
            last = jnp.where(r_in == half - 1 - j, shifted(cum, -j), last)
        return first, last

    head_of_row = lax.broadcasted_iota(jnp.int32, (W_BK, c), 0) // DK_B
    head_rows = [jnp.where(head_of_row == h, 1.0, 0.0).astype(BF16) for h in range(H_B)]
    r_att = lax.broadcasted_iota(jnp.int32, (c, H_B * c), 0)
    s_att = lax.broadcasted_iota(jnp.int32, (c, H_B * c), 1) % c
    atts = [jnp.zeros((c, H_B * c), F32) for _ in range(n)]
    half = 1
    while half < group:
        edges = [block_edges(cum, cumx, half) for cum, cumx in zip(cums, cumxs)]
        odd = (r_ck // half) % 2 == 1
        qts = [jnp.where(odd, q * jnp.exp2(cum - first), 0.0).astype(BF16)
               for q, cum, (first, _) in zip(qs, cums, edges)]
        kts = [jnp.where(odd, 0.0, k * jnp.exp2(last - cum)).T.astype(BF16)
               for k, cum, (_, last) in zip(ks, cums, edges)]
        prods = [_dot(qt, jnp.concatenate([kt * m for m in head_rows], axis=1)) for qt, kt in zip(qts, kts)]
        if 2 * half < c:
            same = r_att // (2 * half) == s_att // (2 * half)
            prods = [jnp.where(same, a, 0.0) for a in prods]
        atts = [att + a for att, a in zip(atts, prods)]
        between()
        half *= 2
    if group > 1:
        att16s = [att.astype(BF16) for att in atts]
        os = [o + jnp.concatenate([_dot(att16[:, h * c:(h + 1) * c], v16[:, h * DV_B:(h + 1) * DV_B])
                                   for h in range(H_B)], axis=-1)
              for o, att16, v16 in zip(os, att16s, v16s)]
    return list(zip(os, cums))


def _gla_state_step(q, k, v16, cum, s_old):
    c = q.shape[0]
    q_in = (q * jnp.exp2(cum)).astype(BF16)
    k_out_t = (k * jnp.exp2(cum[c - 1:c, :] - cum)).T.astype(BF16)
    keep = jnp.exp2(cum.T[:, c - 1:c])
    kd = lambda h: slice(h * DK_B, (h + 1) * DK_B)
    vd = lambda h: slice(h * DV_B, (h + 1) * DV_B)
    o = jnp.concatenate([_dot(q_in[:, kd(h)], s_old[kd(h)].astype(BF16)) for h in range(H_B)], axis=-1)
    s_new = jnp.concatenate([s_old[kd(h)] * keep[kd(h)] + _dot(k_out_t[kd(h)], v16[:, vd(h)])
                             for h in range(H_B)], axis=0)
    return o, s_new


def _gla_finish(o, g_ref, sz):
    outs = [_rms_rows(o[:, h * DV_B:(h + 1) * DV_B], g_ref[:, h * DV_B:(h + 1) * DV_B]) for h in range(H_B)]
    return (jnp.concatenate(outs, axis=-1) * sz.astype(F32)).astype(BF16)


def _gla_step_kernel(q_ref, k_ref, v_ref, la_ref, sz_ref, g_ref, s0_ref, o_ref, s_ref, *, t):
    c = GLA_C
    n = c // t
    q, k, la = q_ref[...].astype(F32), k_ref[...].astype(F32), la_ref[...]
    v16 = v_ref[...]
    (o, cum), = _gla_intra([(q, k, v16.astype(F32), v16, la)], t)

    q_in = (q * jnp.exp2(cum)).astype(BF16)
    cum_t, k_t = cum.T, k.T
    stream_of_col = lax.broadcasted_iota(jnp.int32, (W_BK, c), 1) // t
    inter = []
    for s in range(n):
        rows = slice(s * t, (s + 1) * t)
        last = cum_t[:, (s + 1) * t - 1:(s + 1) * t]
        k_out_t = (k_t * jnp.exp2(jnp.where(stream_of_col == s, last - cum_t, NEG))).astype(BF16)
        keep = jnp.exp2(last)
        o_s = []
        for h in range(H_B):
            ks = slice(h * DK_B, (h + 1) * DK_B)
            vs = slice(h * DV_B, (h + 1) * DV_B)
            s_old = s0_ref[s, h]
            o_s.append(_dot(q_in[rows, ks], s_old.astype(BF16)))
            s_ref[s, h] = s_old * keep[ks] + _dot(k_out_t[ks], v16[:, vs])
        inter.append(jnp.concatenate(o_s, axis=-1))
    o_ref[...] = _gla_finish(o + jnp.concatenate(inter, axis=0), g_ref, sz_ref[...])


def _gla_step(q, k, v, la, sz, g, s0):
    b, t, _ = q.shape
    n = GLA_C // t
    assert GLA_C % t == 0 and t % (2 * GLA_BASE) == 0 and b % n == 0
    flat = lambda z: z.reshape(b * t, z.shape[-1])
    blk = lambda w: pl.BlockSpec((GLA_C, w), lambda i: (i, 0))
    state = pl.BlockSpec((n, H_B, DK_B, DV_B), lambda i: (i, 0, 0, 0))
    o, s_new = pl.pallas_call(
        functools.partial(_gla_step_kernel, t=t),
        grid=(b // n,),
        in_specs=[blk(W_BK), blk(W_BK), blk(W_BV), blk(W_BK), blk(W_BV),
                  pl.BlockSpec((1, W_BV), lambda i: (0, 0)), state],
        out_specs=[blk(W_BV), state],
        out_shape=[jax.ShapeDtypeStruct((b * t, W_BV), BF16),
                   jax.ShapeDtypeStruct((b, H_B, DK_B, DV_B), F32)],
        compiler_params=_params("parallel"),
        name="gla_step",
    )(flat(q), flat(k), flat(v), flat(la), flat(sz), g, s0)
    return o.reshape(b, t, W_BV), s_new


def _prep_weights(l, norm_in, w_in, g_qa, g_ka, w_gate2, b_gate, g_gla_out, g_mem, w_mem_kv, g_qm, g_km,
                  w_up_a, w_up_b, w_up_m, w_out):
    tile = lambda gain, n: jnp.tile(gain, n).reshape(1, -1)
    return {
        "norm_in": norm_in[l].reshape(1, D_MODEL),
        "w_in_t": w_in[l].T.astype(BF16),
        "w_gate2": w_gate2[l].astype(BF16),
        "b_gate": b_gate[l].reshape(1, W_BK),
        "g_qa": tile(g_qa[l], H_A), "g_ka": tile(g_ka[l], H_A), "g_qm": tile(g_qm[l], H_M),
        "g_km": tile(g_km[l], H_M), "g_gla": tile(g_gla_out[l], H_B),
        "g_mem": g_mem[l].reshape(1, D_MODEL),
        "w_mem_kv": w_mem_kv[l].astype(BF16),
        "w_up_a": w_up_a[l].astype(BF16), "w_up_b": w_up_b[l].astype(BF16),
        "w_up_m": w_up_m[l].astype(BF16), "w_out": w_out[l].astype(BF16),
    }


def _layer_long(x, wts, bias, mk, mv, s0, tm):
    b, t, _ = x.shape
    keep = min(BAND, t)
    r3 = lambda z: z.reshape(b, t, z.shape[-1])
    heads = lambda z: z.reshape(b, keep, H_A, HEAD_DIM)
    (qa, ka, va, ka16, va16, sza, out_b, s_new, qm, szm, ga, gb, gm) = _front(
        x.reshape(b * t, D_MODEL), wts, tm, t, keep, s0)
    y = _attn_back(r3(qa), r3(ka16), r3(va16), r3(sza), bias, r3(qm), mk, mv, r3(szm), x, r3(out_b),
                   r3(ga), r3(gb), r3(gm), wts)
    return y, heads(ka), heads(va), s_new


def _layer_short(x, wts, attend_a, attend_m, s0, tm):
    b, t, _ = x.shape
    keep = min(BAND, t)
    x2d = x.reshape(b * t, D_MODEL)
    r3 = lambda z: z.reshape(b, t, z.shape[-1])
    flat = lambda z: z.reshape(b * t, z.shape[-1])
    heads = lambda z: z.reshape(b, keep, H_A, HEAD_DIM)
    (qa, ka, va, ka16, va16, sza, qb, kb, vb, la, szb, qm, szm, ga, gb, gm) = _front(x2d, wts, tm, t, keep)
    out_b, s_new = _gla_step(r3(qb), r3(kb), r3(vb), r3(la), r3(szb), wts["g_gla"], s0)
    out_a = attend_a(r3(qa), r3(ka16), r3(va16), r3(sza))
    out_m = attend_m(r3(qm), r3(szm))
    y = _back(x2d, flat(out_a), flat(out_b), flat(out_m), ga, gb, gm, wts, tm)
    return y.reshape(b, t, D_MODEL), heads(ka), heads(va), s_new


def kernel(x_prompt, x_sample, mem_prompt, cache_a_k, cache_a_v, state_gla, cache_mem_k, cache_mem_v,
           norm_in, w_in, g_qa, g_ka, rel_bias, w_gate2, b_gate, g_gla_out, g_mem, w_mem_kv, g_qm, g_km,
           w_up_a, w_up_b, w_up_m, w_out):
    depth = w_in.shape[0]
    xp, xs = x_prompt, x_sample
    bp, tp, _ = xp.shape
    bs, ts, _ = xs.shape
    akp, avp, sgp, mkp, mvp, aks, avs, sgs = [], [], [], [], [], [], [], []
    for l in range(depth):
        wts = _prep_weights(l, norm_in, w_in, g_qa, g_ka, w_gate2, b_gate, g_gla_out, g_mem, w_mem_kv,
                            g_qm, g_km, w_up_a, w_up_b, w_up_m, w_out)
        bias = _bias_tile(rel_bias[l])
        mk, mv = _mem_kv(mem_prompt, wts)
        s0 = jnp.zeros((bp, H_B, DK_B, DV_B), state_gla.dtype)
        xp, ka, va, sp = _layer_long(xp, wts, bias, mk, mv, s0, 512)
        akp.append(ka)
        avp.append(va)
        sgp.append(sp)
        mkp.append(mk.reshape(bp, N_MEM, H_M, HEAD_DIM))
        mvp.append(mv.reshape(bp, N_MEM, H_M, HEAD_DIM))
        rows_last = lambda z: jnp.transpose(z, (0, 2, 3, 1))
        past_k, past_v = rows_last(cache_a_k[l]), rows_last(cache_a_v[l])
        mem_k, mem_v = rows_last(cache_mem_k[l]), rows_last(cache_mem_v[l])
        attend_s = lambda q, k, v, sz: _attn_step(q, k, v, past_k, past_v, sz, bias)
        attend_ms = lambda q, sz: _attn_mem_step(q, mem_k, mem_v, sz)
        xs, ka_s, va_s, ss = _layer_short(xs, wts, attend_s, attend_ms, state_gla[l], 512)
        aks.append(ka_s)
        avs.append(va_s)
        sgs.append(ss)
    return (xp, xs, jnp.stack(akp), jnp.stack(avp), jnp.stack(sgp), jnp.stack(mkp), jnp.stack(mvp),
            jnp.stack(aks), jnp.stack(avs), jnp.stack(sgs))
```

```python
import functools

import jax
import jax.numpy as jnp
from jax import lax
from jax.experimental import pallas as pl
from jax.experimental.pallas import tpu as pltpu

F32 = jnp.float32
BF16 = jnp.bfloat16

D_MODEL = 1024
CHUNK = 64
LEFT_CHUNKS = 8
HEAD_DIM = 64
H_A = 8
W_A = H_A * HEAD_DIM
REL_CLIP = 128
H_B = 4
DK_B = 64
DV_B = 128
W_BK = H_B * DK_B
W_BV = H_B * DV_B
GATE_RANK = 16
GATE_TAU = 16.0
N_MEM = 256
H_M = 4
W_M = H_M * HEAD_DIM
EPS = 1e-6
IN_SIZES = (W_A, W_A, W_A, W_A, W_BK, W_BK, W_BV, GATE_RANK, W_BV, W_M, W_M, D_MODEL, D_MODEL, D_MODEL)

LANES = 128
SUBLANES = 8
VMEM_LIMIT = 56 * 1024 * 1024
NEG = -1e30
LOG2E = 1.4426950408889634

BAND = LEFT_CHUNKS * CHUNK
QBLK = 256
KSPAN = BAND + QBLK
WINDOW = BAND + CHUNK
BIAS_W = BAND + 2 * CHUNK
GLA_C = 128
GLA_BASE = SUBLANES


def _dot(a, b):
    return jnp.dot(a, b, preferred_element_type=F32)


def _dot_nt(a, b):
    return lax.dot_general(a, b, (((1,), (1,)), ((), ())), preferred_element_type=F32)


def _split3(x):
    hi = x.astype(BF16)
    r = x - hi.astype(F32)
    mid = r.astype(BF16)
    lo = (r - mid.astype(F32)).astype(BF16)
    return hi, mid, lo


def _dot_exact_lhs(a01, x):
    hi, mid, lo = _split3(x)
    return _dot(a01, hi) + _dot(a01, mid) + _dot(a01, lo)


def _sigmoid(z):
    return 1.0 / (1.0 + jnp.exp(-z))


def _silu(z):
    return z * _sigmoid(z)


def _log_sigmoid(z):
    return jnp.minimum(z, 0.0) - jnp.log1p(jnp.exp(-jnp.abs(z)))


def _rms_rows(x, g):
    ms = jnp.mean(x * x, axis=-1, keepdims=True)
    return x * lax.rsqrt(ms + EPS) * g


def _head_mean_matrix(width, head):
    r = lax.broadcasted_iota(jnp.int32, (width, width), 0) // head
    c = lax.broadcasted_iota(jnp.int32, (width, width), 1) // head
    return jnp.where(r == c, 1.0 / head, 0.0).astype(BF16)


def _head_rms(x, g, head):
    avg = _head_mean_matrix(x.shape[-1], head)
    ms = _dot((x * x).astype(BF16), avg)
    return x * lax.rsqrt(ms + EPS) * g


def _full(shape):
    nd = len(shape)
    return pl.BlockSpec(shape, lambda *_: (0,) * nd, pipeline_mode=pl.Buffered(1))


def _params(*sem):
    return pltpu.CompilerParams(dimension_semantics=sem, vmem_limit_bytes=VMEM_LIMIT)


_IN_OFF = tuple(int(sum(IN_SIZES[:i])) for i in range(len(IN_SIZES) + 1))
D_IN = _IN_OFF[-1]


def _front_kernel(x_ref, nin_ref, w_ref, wg2_ref, bg_ref, gqa_ref, gka_ref, gqm_ref, *rest, per_stream=None):
    fused = per_stream is not None
    if fused:
        (ggla_ref, s0_ref, qa_o, kat_o, vat_o, ka16_o, va16_o, sza_o, outb_o, sfin_o, qm_o, szm_o,
         ga_o, gb_o, gm_o, s_scr, ka_o, va_o) = rest
    else:
        (qa_o, ka_o, va_o, ka16_o, va16_o, sza_o, qb_o, kb_o, vb_o, la_o, szb_o, qm_o, szm_o,
         ga_o, gb_o, gm_o) = rest
    if fused:
        @pl.when(pl.program_id(0) % per_stream == 0)
        def _():
            s_scr[...] = s0_ref[pl.program_id(0) // per_stream].reshape(W_BK, DV_B)

    h = _rms_rows(x_ref[...], nin_ref[...]).astype(BF16)

    def proj(i):
        return _dot_nt(h, w_ref[_IN_OFF[i]:_IN_OFF[i + 1], :])

    qb = (proj(4) * (DK_B ** -0.5)).astype(BF16)
    kb = proj(5).astype(BF16)
    vb = proj(6).astype(BF16)
    z = _dot(proj(7).astype(BF16), wg2_ref[...]) + bg_ref[...]
    la = _log_sigmoid(z) * (1.0 / GATE_TAU)
    szb = _silu(proj(8)).astype(BF16)

    scale = HEAD_DIM ** -0.5 * LOG2E

    def emit_qa():
        qa_o[...] = (_head_rms(proj(0), gqa_ref[...], HEAD_DIM) * scale).astype(BF16)

    def emit_ka():
        ka = _head_rms(proj(1), gka_ref[...], HEAD_DIM)
        ka_o[...] = ka
        ka16_o[...] = ka.astype(BF16)

    def emit_va():
        va = proj(2)
        va_o[...] = va
        va16_o[...] = va.astype(BF16)

    def emit_sza():
        sza_o[...] = _silu(proj(3)).astype(BF16)

    def emit_qm():
        qm_o[...] = (_head_rms(proj(9), gqm_ref[...], HEAD_DIM) * scale).astype(BF16)

    def emit_szm():
        szm_o[...] = _silu(proj(10)).astype(BF16)

    def emit_gate(i, ref):
        ref[...] = _sigmoid(proj(i)).astype(BF16)

    pending = [emit_qa, emit_ka, emit_va, emit_sza, emit_qm, emit_szm,
               functools.partial(emit_gate, 11, ga_o), functools.partial(emit_gate, 12, gb_o),
               functools.partial(emit_gate, 13, gm_o)]

    def emit_next():
        if pending:
            pending.pop(0)()

    if fused:
        chunks = [slice(r, r + GLA_C) for r in range(0, x_ref.shape[0], GLA_C)]
        items = [(qb[rows].astype(F32), kb[rows].astype(F32), vb[rows].astype(F32), vb[rows], la[rows])
                 for rows in chunks]
        state = s_scr[...]
        outs = []
        for rows, item, (o, cum) in zip(chunks, items, _gla_intra(items, GLA_C, emit_next)):
            inter, state = _gla_state_step(item[0], item[1], item[3], cum, state)
            outs.append(_gla_finish(o + inter, ggla_ref, szb[rows]))
            emit_next()
        s_scr[...] = state
        outb_o[...] = jnp.concatenate(outs, axis=0)
    else:
        qb_o[...] = qb
        kb_o[...] = kb
        vb_o[...] = vb
        la_o[...] = la
        szb_o[...] = szb
    while pending:
        emit_next()

    if fused:
        @pl.when(pl.program_id(0) % per_stream == per_stream - 1)
        def _():
            sfin_o[0] = s_scr[...].reshape(H_B, DK_B, DV_B)
            kat_o[0] = ka_o[...].T
            vat_o[0] = va_o[...].T


def _front(x2d, wts, tm, t, keep, s0=None):
    n = x2d.shape[0]
    tm = min(tm, n)
    row = lambda w: pl.BlockSpec((tm, w), lambda i: (i, 0))
    out = lambda rows_, w, d, spec: (jax.ShapeDtypeStruct((rows_, w), d), spec)
    tail = [out(n, W_M, BF16, row(W_M)), out(n, W_M, BF16, row(W_M))] + [out(n, D_MODEL, BF16, row(D_MODEL))] * 3
    in_specs = [row(D_MODEL), _full((1, D_MODEL)), _full((D_IN, D_MODEL)),
                _full((GATE_RANK, W_BK)), _full((1, W_BK)), _full((1, W_A)), _full((1, W_A)), _full((1, W_M))]
    args = [x2d, wts["norm_in"], wts["w_in_t"], wts["w_gate2"], wts["b_gate"], wts["g_qa"], wts["g_ka"], wts["g_qm"]]
    if s0 is None:
        assert keep == t
        kv = [out(n, W_A, F32, row(W_A))] * 2
        mid = [out(n, W_BK, BF16, row(W_BK)), out(n, W_BK, BF16, row(W_BK)), out(n, W_BV, BF16, row(W_BV)),
               out(n, W_BK, F32, row(W_BK)), out(n, W_BV, BF16, row(W_BV))]
        body, scratch = _front_kernel, []
    else:
        assert t % tm == 0 and tm % GLA_C == 0 and keep == tm
        streams, per_stream = n // t, t // tm
        state = (streams, H_B, DK_B, DV_B)
        kv = [(jax.ShapeDtypeStruct((streams, W_A, keep), F32),
               pl.BlockSpec((1, W_A, keep), lambda i: (i // per_stream, 0, 0)))] * 2
        mid = [out(n, W_BV, BF16, row(W_BV)),
               (jax.ShapeDtypeStruct(state, F32), pl.BlockSpec((1,) + state[1:], lambda i: (i // per_stream, 0, 0, 0)))]
        in_specs += [_full((1, W_BV)), _full(state)]
        args += [wts["g_gla"], s0]
        body = functools.partial(_front_kernel, per_stream=per_stream)
        scratch = [pltpu.VMEM((W_BK, DV_B), F32), pltpu.VMEM((tm, W_A), F32), pltpu.VMEM((tm, W_A), F32)]
    outs = [out(n, W_A, BF16, row(W_A))] + kv + [out(n, W_A, BF16, row(W_A))] * 3 + mid + tail
    return pl.pallas_call(
        body,
        grid=(n // tm,),
        in_specs=in_specs,
        out_specs=[spec for _, spec in outs],
        out_shape=[shape for shape, _ in outs],
        scratch_shapes=scratch,
        compiler_params=_params("arbitrary"),
        name="front",
    )(*args)


def _back_kernel(x_ref, a_ref, b_ref, m_ref, ga_ref, gb_ref, gm_ref, wa_ref, wb_ref, wm_ref, wo_ref, y_ref):
    u = (ga_ref[...].astype(F32) * _dot(a_ref[...], wa_ref[...])
         + gb_ref[...].astype(F32) * _dot(b_ref[...], wb_ref[...])
         + gm_ref[...].astype(F32) * _dot(m_ref[...], wm_ref[...]))
    y_ref[...] = x_ref[...] + _dot(u.astype(BF16), wo_ref[...])


def _back(x2d, a, b, m, ga, gb, gm, wts, tm):
    n = x2d.shape[0]
    tm = min(tm, n)
    row = lambda w: pl.BlockSpec((tm, w), lambda i: (i, 0))
    return pl.pallas_call(
        _back_kernel,
        grid=(n // tm,),
        in_specs=[row(D_MODEL), row(W_A), row(W_BV), row(W_M), row(D_MODEL), row(D_MODEL), row(D_MODEL),
                  _full((W_A, D_MODEL)), _full((W_BV, D_MODEL)), _full((W_M, D_MODEL)),
                  _full((D_MODEL, D_MODEL))],
        out_specs=row(D_MODEL),
        out_shape=jax.ShapeDtypeStruct((n, D_MODEL), F32),
        compiler_params=_params("parallel"),
        name="back",
    )(x2d, a, b, m, ga, gb, gm, wts["w_up_a"], wts["w_up_b"], wts["w_up_m"], wts["w_out"])


def _bias_kernel(e_ref, tile_ref):
    e = jnp.broadcast_to(e_ref[0] * LOG2E, (CHUNK, 2 * BAND))
    tile_ref[0] = pltpu.roll(e, 0, 1, stride=1, stride_axis=0)[:, :BIAS_W]


def _bias_tile(rel_bias):
    j = jnp.arange(2 * BAND)
    dist = jnp.where(j < BIAS_W, BAND - j, REL_CLIP)
    e = rel_bias[:, jnp.clip(dist, -REL_CLIP, REL_CLIP) + REL_CLIP].reshape(H_A, 1, 2 * BAND)
    return pl.pallas_call(
        _bias_kernel,
        grid=(H_A,),
        in_specs=[pl.BlockSpec((1, 1, 2 * BAND), lambda h: (h, 0, 0))],
        out_specs=pl.BlockSpec((1, CHUNK, BIAS_W), lambda h: (h, 0, 0)),
        out_shape=jax.ShapeDtypeStruct((H_A, CHUNK, BIAS_W), F32),
        compiler_params=_params("parallel"),
        name="bias_tile",
    )(e)


def _softmax_pv_staged(scores, values):
    biased = [[s if b is None else s + b for s, b in s_list] for s_list in scores]
    tops = [functools.reduce(jnp.maximum, [jnp.max(s, axis=-1, keepdims=True) for s in s_list])
            for s_list in biased]
    probs = [[jnp.exp2(s - m) for s in s_list] for s_list, m in zip(biased, tops)]
    sums = [functools.reduce(jnp.add, [jnp.sum(x, axis=-1, keepdims=True) for x in p_list]) for p_list in probs]
    return [functools.reduce(jnp.add, [(v(x.astype(BF16)) if callable(v) else _dot(x.astype(BF16), v))
                                       for x, v in zip(p_list, v_list)]) / l
            for p_list, v_list, l in zip(probs, values, sums)]


def _head_pairs_attention(q_pairs, k_lists, v_lists, bias_fn):
    m_rows = q_pairs[0].shape[0]
    lane = lax.broadcasted_iota(jnp.int32, (m_rows, LANES), 1)
    first = lane < HEAD_DIM
    scores = []
    for p, qp in enumerate(q_pairs):
        zero = jnp.zeros_like(qp)
        q2 = jnp.concatenate([jnp.where(first, qp, zero), jnp.where(first, zero, qp)], axis=0)
        s_list = []
        for i, k in enumerate(k_lists[p]):
            b0, b1 = bias_fn(p, 0, i), bias_fn(p, 1, i)
            s_list.append((_dot_nt(q2, k), None if b0 is None else jnp.concatenate([b0, b1], axis=0)))
        scores.append(s_list)
    outs = _softmax_pv_staged(scores, v_lists)
    return [jnp.where(first, o[:m_rows], o[m_rows:]) for o in outs]


def _attn_back_kernel(q_ref, k0_ref, k1_ref, k2_ref, v0_ref, v1_ref, v2_ref, sz_ref, bias_ref,
                      qm_ref, mk_ref, mv_ref, szm_ref,
                      x_ref, b_ref, ga_ref, gb_ref, gm_ref, wa_ref, wb_ref, wm_ref, wo_ref, y_ref,
                      k_buf, v_buf, bias_buf, a_buf, m_buf):
    i = pl.program_id(1)
    for j, (k_ref, v_ref) in enumerate(((k0_ref, v0_ref), (k1_ref, v1_ref), (k2_ref, v2_ref))):
        k_buf[j * QBLK:(j + 1) * QBLK, :] = k_ref[0]
        v_buf[j * QBLK:(j + 1) * QBLK, :] = v_ref[0]

    @pl.when(i <= BAND // QBLK)
    def _():
        col = lax.broadcasted_iota(jnp.int32, (1, BIAS_W), 1)
        for c in range(QBLK // CHUNK):
            before_start = jnp.where(col + c * CHUNK < (BAND // QBLK - i) * QBLK, NEG, 0.0)
            for h in range(H_A):
                bias_buf[c, h] = bias_ref[h] + before_start

    lanes_m = [slice(p * LANES, (p + 1) * LANES) for p in range(H_M // 2)]
    mk = [[mk_ref[0, :, lanes].astype(BF16)] for lanes in lanes_m]
    mv = [[mv_ref[0, :, lanes].astype(BF16)] for lanes in lanes_m]
    blocks = [(slice(r, r + MEM_ROWS), p) for r in range(0, QBLK, MEM_ROWS) for p in range(len(lanes_m))]
    outs = _head_pairs_attention([qm_ref[0, rows, lanes_m[p]] for rows, p in blocks],
                                 [mk[p] for _, p in blocks], [mv[p] for _, p in blocks], lambda n, e, _: None)
    for (rows, p), o in zip(blocks, outs):
        m_buf[rows, lanes_m[p]] = (o * szm_ref[0, rows, lanes_m[p]].astype(F32)).astype(BF16)

    pairs = [slice(p * LANES, (p + 1) * LANES) for p in range(H_A // 2)]
    for c in range(QBLK // CHUNK):
        rows = slice(c * CHUNK, (c + 1) * CHUNK)
        win = slice(c * CHUNK, c * CHUNK + WINDOW)
        outs = _head_pairs_attention(
            [q_ref[0, rows, lanes] for lanes in pairs],
            [[k_buf[win, lanes]] for lanes in pairs], [[v_buf[win, lanes]] for lanes in pairs],
            lambda p, e, _: bias_buf[c, 2 * p + e, :, :WINDOW])
        for lanes, o in zip(pairs, outs):
            a_buf[rows, lanes] = (o * sz_ref[0, rows, lanes].astype(F32)).astype(BF16)

    u = (ga_ref[0].astype(F32) * _dot(a_buf[...], wa_ref[...])
         + gb_ref[0].astype(F32) * _dot(b_ref[0], wb_ref[...])
         + gm_ref[0].astype(F32) * _dot(m_buf[...], wm_ref[...]))
    y_ref[0] = x_ref[0] + _dot(u.astype(BF16), wo_ref[...])


def _attn_back(q, k, v, sz, bias, qm, mk, mv, szm, x, out_b, ga, gb, gm, wts):
    b, t, _ = q.shape
    nb = BAND // QBLK
    n = t // QBLK
    assert nb == 2 and t % QBLK == 0 and QBLK % MEM_ROWS == 0
    rows = lambda w, d=0: pl.BlockSpec((1, QBLK, w), lambda bi, i: (bi, jnp.maximum(i - d, 0), 0))
    kv_specs = [rows(W_A, nb - j) for j in range(nb + 1)]
    mem = pl.BlockSpec((1, N_MEM, W_M), lambda bi, i: (bi, 0, 0))
    return pl.pallas_call(
        _attn_back_kernel,
        grid=(b, n),
        in_specs=[rows(W_A)] + kv_specs + kv_specs + [rows(W_A), _full((H_A, CHUNK, BIAS_W)),
                  rows(W_M), mem, mem, rows(W_M),
                  rows(D_MODEL), rows(W_BV), rows(D_MODEL), rows(D_MODEL), rows(D_MODEL),
                  _full((W_A, D_MODEL)), _full((W_BV, D_MODEL)), _full((W_M, D_MODEL)),
                  _full((D_MODEL, D_MODEL))],
        out_specs=rows(D_MODEL),
        out_shape=jax.ShapeDtypeStruct((b, t, D_MODEL), F32),
        scratch_shapes=[pltpu.VMEM((KSPAN, W_A), BF16), pltpu.VMEM((KSPAN, W_A), BF16),
                        pltpu.VMEM((QBLK // CHUNK, H_A, CHUNK, BIAS_W), F32),
                        pltpu.VMEM((QBLK, W_A), BF16), pltpu.VMEM((QBLK, W_M), BF16)],
        compiler_params=_params("parallel", "arbitrary"),
        name="attn_back",
    )(q, k, k, k, v, v, v, sz, bias, qm, mk, mv, szm, x, out_b, ga, gb, gm,
      wts["w_up_a"], wts["w_up_b"], wts["w_up_m"], wts["w_out"])


STEP_STREAMS = 4


def _attn_step_kernel(q_ref, kn_ref, vn_ref, kp_ref, vp_ref, sz_ref, bias_ref, o_ref):
    s_len = q_ref.shape[1]
    for b in range(STEP_STREAMS):
        scores, values = [], []
        for h in range(H_A):
            lanes = slice(h * HEAD_DIM, (h + 1) * HEAD_DIM)
            q = q_ref[b, :, lanes]
            scores.append([(_dot(q, kp_ref[b, h].astype(BF16)), bias_ref[h, :s_len, :BAND]),
                           (_dot_nt(q, kn_ref[b, :, lanes]), bias_ref[h, :s_len, BAND:BAND + s_len])])
            values.append([functools.partial(_dot_nt, b=vp_ref[b, h].astype(BF16)), vn_ref[b, :, lanes]])
        o = jnp.concatenate(_softmax_pv_staged(scores, values), axis=-1)
        o_ref[b] = (o * sz_ref[b].astype(F32)).astype(BF16)


def _attn_step(q, k_new, v_new, k_past_t, v_past_t, sz, bias):
    b, s_len, _ = q.shape
    assert k_past_t.shape[1:] == (H_A, HEAD_DIM, BAND) and s_len <= CHUNK and b % STEP_STREAMS == 0
    new = pl.BlockSpec((STEP_STREAMS, s_len, W_A), lambda i: (i, 0, 0))
    past = pl.BlockSpec((STEP_STREAMS, H_A, HEAD_DIM, BAND), lambda i: (i, 0, 0, 0))
    return pl.pallas_call(
        _attn_step_kernel,
        grid=(b // STEP_STREAMS,),
        in_specs=[new, new, new, past, past, new, _full((H_A, CHUNK, BIAS_W))],
        out_specs=new,
        out_shape=jax.ShapeDtypeStruct((b, s_len, W_A), BF16),
        compiler_params=_params("parallel"),
        name="attn_step",
    )(q, k_new, v_new, k_past_t, v_past_t, sz, bias)


def _mem_kv_kernel(mem_ref, gmem_ref, w_ref, gkm_ref, mk_ref, mv_ref):
    h = _rms_rows(mem_ref[0], gmem_ref[...]).astype(BF16)
    kv = _dot(h, w_ref[...])
    mk_ref[0] = _head_rms(kv[:, :W_M], gkm_ref[...], HEAD_DIM)
    mv_ref[0] = kv[:, W_M:]


def _mem_kv(mem, wts):
    b = mem.shape[0]
    out = pl.BlockSpec((1, N_MEM, W_M), lambda i: (i, 0, 0))
    return pl.pallas_call(
        _mem_kv_kernel,
        grid=(b,),
        in_specs=[pl.BlockSpec((1, N_MEM, D_MODEL), lambda i: (i, 0, 0)), _full((1, D_MODEL)),
                  _full((D_MODEL, 2 * W_M)), _full((1, W_M))],
        out_specs=[out, out],
        out_shape=[jax.ShapeDtypeStruct((b, N_MEM, W_M), F32)] * 2,
        compiler_params=_params("parallel"),
        name="mem_kv",
    )(mem, wts["g_mem"], wts["w_mem_kv"], wts["g_km"])


MEM_ROWS = 128
MEM_STREAMS = 8


def _attn_mem_step_kernel(q_ref, mk_ref, mv_ref, sz_ref, o_ref):
    for b in range(q_ref.shape[0]):
        scores, values = [], []
        for h in range(H_M):
            lanes = slice(h * HEAD_DIM, (h + 1) * HEAD_DIM)
            scores.append([(_dot(q_ref[b, :, lanes], mk_ref[b, h].astype(BF16)), None)])
            values.append([functools.partial(_dot_nt, b=mv_ref[b, h].astype(BF16))])
        o = jnp.concatenate(_softmax_pv_staged(scores, values), axis=-1)
        o_ref[b] = (o * sz_ref[b].astype(F32)).astype(BF16)


def _attn_mem_step(q, mk_t, mv_t, sz):
    b, t, _ = q.shape
    nb = min(b, MEM_STREAMS)
    assert b % nb == 0 and mk_t.shape[1:] == (H_M, HEAD_DIM, N_MEM)
    rows = pl.BlockSpec((nb, t, W_M), lambda i: (i, 0, 0))
    mem = pl.BlockSpec((nb, H_M, HEAD_DIM, N_MEM), lambda i: (i, 0, 0, 0))
    return pl.pallas_call(
        _attn_mem_step_kernel,
        grid=(b // nb,),
        in_specs=[rows, mem, mem, rows],
        out_specs=rows,
        out_shape=jax.ShapeDtypeStruct((b, t, W_M), BF16),
        compiler_params=_params("parallel"),
        name="attn_mem_step",
    )(q, mk_t, mv_t, sz)


def _gla_intra(items, group, between=lambda: None):
    n = len(items)
    qs, ks, vs, v16s = ([it[j] for it in items] for j in range(4))
    c = qs[0].shape[0]
    las = [it[4] * LOG2E for it in items]
    r_cc = lax.broadcasted_iota(jnp.int32, (c, c), 0)
    c_cc = lax.broadcasted_iota(jnp.int32, (c, c), 1)
    causal = c_cc <= r_cc
    if group < c:
        causal = causal & (r_cc // group == c_cc // group)
    causal16 = causal.astype(BF16)
    cums = [_dot_exact_lhs(causal16, la) for la in las]
    cumxs = [cum - la for cum, la in zip(cums, las)]
    r_ck = lax.broadcasted_iota(jnp.int32, (c, W_BK), 0)

    head_of_k = lax.broadcasted_iota(jnp.int32, (W_BK, W_BV), 0) // DK_B
    head_of_v = lax.broadcasted_iota(jnp.int32, (W_BK, W_BV), 1) // DV_B
    spread = (head_of_k == head_of_v).astype(BF16)
    os = [_dot((q * k).astype(BF16), spread) * v for q, k, v in zip(qs, ks, vs)]

    def shifted(x, j):
        return pltpu.roll(x.reshape(c // SUBLANES, SUBLANES, W_BK), j % SUBLANES, 1).reshape(c, W_BK)

    def block_edges(cum, cumx, half):
        if half >= SUBLANES:
            nb = c // half
            first = jnp.broadcast_to(cumx.reshape(nb, half, W_BK)[:, 0:1, :], (nb, half, W_BK))
            last = jnp.broadcast_to(cum.reshape(nb, half, W_BK)[:, half - 1:half, :], (nb, half, W_BK))
            return first.reshape(c, W_BK), last.reshape(c, W_BK)
        r_in = r_ck % half
        first, last = cumx, cum
        for j in range(1, half):
            first = jnp.where(r_in == j, shifted(cumx, j), first)
            last = jnp.where(r_in == half - 1 - j, shifted(cum, -j), last)
        return first, last

    head_of_row = lax.broadcasted_iota(jnp.int32, (W_BK, c), 0) // DK_B
    head_rows = [jnp.where(head_of_row == h, 1.0, 0.0).astype(BF16) for h in range(H_B)]
    r_att = lax.broadcasted_iota(jnp.int32, (c, H_B * c), 0)
    s_att = lax.broadcasted_iota(jnp.int32, (c, H_B * c), 1) % c
    atts = [jnp.zeros((c, H_B * c), F32) for _ in range(n)]
    half = 1
    while half < group:
        edges = [block_edges(cum, cumx, half) for cum, cumx in zip(cums, cumxs)]
        odd = (r_ck // half) % 2 == 1
        qts = [jnp.where(odd, q * jnp.exp2(cum - first), 0.0).astype(BF16)
               for q, cum, (first, _) in zip(qs, cums, edges)]
        kts = [jnp.where(odd, 0.0, k * jnp.exp2(last - cum)).T.astype(BF16)
               for k, cum, (_, last) in zip(ks, cums, edges)]
        prods = [_dot(qt, jnp.concatenate([kt * m for m in head_rows], axis=1)) for qt, kt in zip(qts, kts)]
        if 2 * half < c:
            same = r_att // (2 * half) == s_att // (2 * half)
            prods = [jnp.where(same, a, 0.0) for a in prods]
        atts = [att + a for att, a in zip(atts, prods)]
        between()
        half *= 2
    if group > 1:
        att16s = [att.astype(BF16) for att in atts]
        os = [o + jnp.concatenate([_dot(att16[:, h * c:(h + 1) * c], v16[:, h * DV_B:(h + 1) * DV_B])
                                   for h in range(H_B)], axis=-1)
              for o, att16, v16 in zip(os, att16s, v16s)]
    return list(zip(os, cums))


def _gla_state_step(q, k, v16, cum, s_old):
    c = q.shape[0]
    q_in = (q * jnp.exp2(cum)).astype(BF16)
    k_out_t = (k * jnp.exp2(cum[c - 1:c, :] - cum)).T.astype(BF16)
    keep = jnp.exp2(cum.T[:, c - 1:c])
    kd = lambda h: slice(h * DK_B, (h + 1) * DK_B)
    vd = lambda h: slice(h * DV_B, (h + 1) * DV_B)
    o = jnp.concatenate([_dot(q_in[:, kd(h)], s_old[kd(h)].astype(BF16)) for h in range(H_B)], axis=-1)
    s_new = jnp.concatenate([s_old[kd(h)] * keep[kd(h)] + _dot(k_out_t[kd(h)], v16[:, vd(h)])
                             for h in range(H_B)], axis=0)
    return o, s_new


def _gla_finish(o, g_ref, sz):
    outs = [_rms_rows(o[:, h * DV_B:(h + 1) * DV_B], g_ref[:, h * DV_B:(h + 1) * DV_B]) for h in range(H_B)]
    return (jnp.concatenate(outs, axis=-1) * sz.astype(F32)).astype(BF16)


def _gla_step_kernel(q_ref, k_ref, v_ref, la_ref, sz_ref, g_ref, s0_ref, o_ref, s_ref, *, t):
    c = GLA_C
    n = c // t
    q, k, la = q_ref[...].astype(F32), k_ref[...].astype(F32), la_ref[...]
    v16 = v_ref[...]
    (o, cum), = _gla_intra([(q, k, v16.astype(F32), v16, la)], t)

    q_in = (q * jnp.exp2(cum)).astype(BF16)
    cum_t, k_t = cum.T, k.T
    stream_of_col = lax.broadcasted_iota(jnp.int32, (W_BK, c), 1) // t
    inter = []
    for s in range(n):
        rows = slice(s * t, (s + 1) * t)
        last = cum_t[:, (s + 1) * t - 1:(s + 1) * t]
        k_out_t = (k_t * jnp.exp2(jnp.where(stream_of_col == s, last - cum_t, NEG))).astype(BF16)
        keep = jnp.exp2(last)
        o_s = []
        for h in range(H_B):
            ks = slice(h * DK_B, (h + 1) * DK_B)
            vs = slice(h * DV_B, (h + 1) * DV_B)
            s_old = s0_ref[s, h]
            o_s.append(_dot(q_in[rows, ks], s_old.astype(BF16)))
            s_ref[s, h] = s_old * keep[ks] + _dot(k_out_t[ks], v16[:, vs])
        inter.append(jnp.concatenate(o_s, axis=-1))
    o_ref[...] = _gla_finish(o + jnp.concatenate(inter, axis=0), g_ref, sz_ref[...])


def _gla_step(q, k, v, la, sz, g, s0):
    b, t, _ = q.shape
    n = GLA_C // t
    assert GLA_C % t == 0 and t % (2 * GLA_BASE) == 0 and b % n == 0
    flat = lambda z: z.reshape(b * t, z.shape[-1])
    blk = lambda w: pl.BlockSpec((GLA_C, w), lambda i: (i, 0))
    state = pl.BlockSpec((n, H_B, DK_B, DV_B), lambda i: (i, 0, 0, 0))
    o, s_new = pl.pallas_call(
        functools.partial(_gla_step_kernel, t=t),
        grid=(b // n,),
        in_specs=[blk(W_BK), blk(W_BK), blk(W_BV), blk(W_BK), blk(W_BV),
                  pl.BlockSpec((1, W_BV), lambda i: (0, 0)), state],
        out_specs=[blk(W_BV), state],
        out_shape=[jax.ShapeDtypeStruct((b * t, W_BV), BF16),
                   jax.ShapeDtypeStruct((b, H_B, DK_B, DV_B), F32)],
        compiler_params=_params("parallel"),
        name="gla_step",
    )(flat(q), flat(k), flat(v), flat(la), flat(sz), g, s0)
    return o.reshape(b, t, W_BV), s_new


def _prep_weights(l, norm_in, w_in, g_qa, g_ka, w_gate2, b_gate, g_gla_out, g_mem, w_mem_kv, g_qm, g_km,
                  w_up_a, w_up_b, w_up_m, w_out):
    tile = lambda gain, n: jnp.tile(gain, n).reshape(1, -1)
    return {
        "norm_in": norm_in[l].reshape(1, D_MODEL),
        "w_in_t": w_in[l].T.astype(BF16),
        "w_gate2": w_gate2[l].astype(BF16),
        "b_gate": b_gate[l].reshape(1, W_BK),
        "g_qa": tile(g_qa[l], H_A), "g_ka": tile(g_ka[l], H_A), "g_qm": tile(g_qm[l], H_M),
        "g_km": tile(g_km[l], H_M), "g_gla": tile(g_gla_out[l], H_B),
        "g_mem": g_mem[l].reshape(1, D_MODEL),
        "w_mem_kv": w_mem_kv[l].astype(BF16),
        "w_up_a": w_up_a[l].astype(BF16), "w_up_b": w_up_b[l].astype(BF16),
        "w_up_m": w_up_m[l].astype(BF16), "w_out": w_out[l].astype(BF16),
    }


def _layer_long(x, wts, bias, mk, mv, s0):
    b, t, _ = x.shape
    keep = min(BAND, t)
    r3 = lambda z: z.reshape(b, t, z.shape[-1])
    heads = lambda z: jnp.transpose(z.reshape(b, H_A, HEAD_DIM, keep), (0, 3, 1, 2))
    (qa, ka, va, ka16, va16, sza, out_b, s_new, qm, szm, ga, gb, gm) = _front(
        x.reshape(b * t, D_MODEL), wts, keep, t, keep, s0)
    y = _attn_back(r3(qa), r3(ka16), r3(va16), r3(sza), bias, r3(qm), mk, mv, r3(szm), x, r3(out_b),
                   r3(ga), r3(gb), r3(gm), wts)
    return y, heads(ka), heads(va), s_new


def _layer_short(x, wts, attend_a, attend_m, s0, tm):
    b, t, _ = x.shape
    keep = min(BAND, t)
    x2d = x.reshape(b * t, D_MODEL)
    r3 = lambda z: z.reshape(b, t, z.shape[-1])
    flat = lambda z: z.reshape(b * t, z.shape[-1])
    heads = lambda z: z.reshape(b, keep, H_A, HEAD_DIM)
    (qa, ka, va, ka16, va16, sza, qb, kb, vb, la, szb, qm, szm, ga, gb, gm) = _front(x2d, wts, tm, t, keep)
    out_b, s_new = _gla_step(r3(qb), r3(kb), r3(vb), r3(la), r3(szb), wts["g_gla"], s0)
    out_a = attend_a(r3(qa), r3(ka16), r3(va16), r3(sza))
    out_m = attend_m(r3(qm), r3(szm))
    y = _back(x2d, flat(out_a), flat(out_b), flat(out_m), ga, gb, gm, wts, tm)
    return y.reshape(b, t, D_MODEL), heads(ka), heads(va), s_new


def kernel(x_prompt, x_sample, mem_prompt, cache_a_k, cache_a_v, state_gla, cache_mem_k, cache_mem_v,
           norm_in, w_in, g_qa, g_ka, rel_bias, w_gate2, b_gate, g_gla_out, g_mem, w_mem_kv, g_qm, g_km,
           w_up_a, w_up_b, w_up_m, w_out):
    depth = w_in.shape[0]
    xp, xs = x_prompt, x_sample
    bp, tp, _ = xp.shape
    bs, ts, _ = xs.shape
    akp, avp, sgp, mkp, mvp, aks, avs, sgs = [], [], [], [], [], [], [], []
    for l in range(depth):
        wts = _prep_weights(l, norm_in, w_in, g_qa, g_ka, w_gate2, b_gate, g_gla_out, g_mem, w_mem_kv,
                            g_qm, g_km, w_up_a, w_up_b, w_up_m, w_out)
        bias = _bias_tile(rel_bias[l])
        mk, mv = _mem_kv(mem_prompt, wts)
        s0 = jnp.zeros((bp, H_B, DK_B, DV_B), state_gla.dtype)
        xp, ka, va, sp = _layer_long(xp, wts, bias, mk, mv, s0)
        akp.append(ka)
        avp.append(va)
        sgp.append(sp)
        mkp.append(mk.reshape(bp, N_MEM, H_M, HEAD_DIM))
        mvp.append(mv.reshape(bp, N_MEM, H_M, HEAD_DIM))
        rows_last = lambda z: jnp.transpose(z, (0, 2, 3, 1))
        past_k, past_v = rows_last(cache_a_k[l]), rows_last(cache_a_v[l])
        mem_k, mem_v = rows_last(cache_mem_k[l]), rows_last(cache_mem_v[l])
        attend_s = lambda q, k, v, sz: _attn_step(q, k, v, past_k, past_v, sz, bias)
        attend_ms = lambda q, sz: _attn_mem_step(q, mem_k, mem_v, sz)
        xs, ka_s, va_s, ss = _layer_short(xs, wts, attend_s, attend_ms, state_gla[l], 512)
        aks.append(ka_s)
        avs.append(va_s)
        sgs.append(ss)
    return (xp, xs, jnp.stack(akp), jnp.stack(avp), jnp.stack(sgp), jnp.stack(mkp), jnp.stack(mvp),
            jnp.stack(aks), jnp.stack(avs), jnp.stack(sgs))
```

```python
import functools

import jax
import jax.numpy as jnp
from jax import lax
from jax.experimental import pallas as pl
from jax.experimental.pallas import tpu as pltpu

F32 = jnp.float32
BF16 = jnp.bfloat16

D_MODEL = 1024
CHUNK = 64
LEFT_CHUNKS = 8
HEAD_DIM = 64
H_A = 8
W_A = H_A * HEAD_DIM
REL_CLIP = 128
H_B = 4
DK_B = 64
DV_B = 128
W_BK = H_B * DK_B
W_BV = H_B * DV_B
GATE_RANK = 16
GATE_TAU = 16.0
N_MEM = 256
H_M = 4
W_M = H_M * HEAD_DIM
EPS = 1e-6
IN_SIZES = (W_A, W_A, W_A, W_A, W_BK, W_BK, W_BV, GATE_RANK, W_BV, W_M, W_M, D_MODEL, D_MODEL, D_MODEL)

LANES = 128
SUBLANES = 8
VMEM_LIMIT = 56 * 1024 * 1024
NEG = -1e30
LOG2E = 1.4426950408889634

BAND = LEFT_CHUNKS * CHUNK
QBLK = 512
KSPAN = BAND + QBLK
WINDOW = BAND + CHUNK
BIAS_W = BAND + 2 * CHUNK
GLA_C = 128
GLA_BASE = SUBLANES


def _dot(a, b):
    return jnp.dot(a, b, preferred_element_type=F32)


def _dot_nt(a, b):
    return lax.dot_general(a, b, (((1,), (1,)), ((), ())), preferred_element_type=F32)


def _split3(x):
    hi = x.astype(BF16)
    r = x - hi.astype(F32)
    mid = r.astype(BF16)
    lo = (r - mid.astype(F32)).astype(BF16)
    return hi, mid, lo


def _dot_exact_lhs(a01, x):
    hi, mid, lo = _split3(x)
    return _dot(a01, hi) + _dot(a01, mid) + _dot(a01, lo)


def _sigmoid(z):
    return 1.0 / (1.0 + jnp.exp(-z))


def _silu(z):
    return z * _sigmoid(z)


def _log_sigmoid(z):
    return jnp.minimum(z, 0.0) - jnp.log1p(jnp.exp(-jnp.abs(z)))


def _rms_rows(x, g):
    ms = jnp.mean(x * x, axis=-1, keepdims=True)
    return x * lax.rsqrt(ms + EPS) * g


def _head_mean_matrix(width, head):
    r = lax.broadcasted_iota(jnp.int32, (width, width), 0) // head
    c = lax.broadcasted_iota(jnp.int32, (width, width), 1) // head
    return jnp.where(r == c, 1.0 / head, 0.0).astype(BF16)


def _head_rms(x, g, head):
    avg = _head_mean_matrix(x.shape[-1], head)
    ms = _dot((x * x).astype(BF16), avg)
    return x * lax.rsqrt(ms + EPS) * g


def _full(shape):
    nd = len(shape)
    return pl.BlockSpec(shape, lambda *_: (0,) * nd, pipeline_mode=pl.Buffered(1))


def _params(*sem):
    return pltpu.CompilerParams(dimension_semantics=sem, vmem_limit_bytes=VMEM_LIMIT)


_IN_OFF = tuple(int(sum(IN_SIZES[:i])) for i in range(len(IN_SIZES) + 1))
D_IN = _IN_OFF[-1]


def _front_kernel(x_ref, nin_ref, w_ref, wg2_ref, bg_ref, gqa_ref, gka_ref, gqm_ref, *rest, per_stream=None):
    fused = per_stream is not None
    if fused:
        (ggla_ref, s0_ref, qa_o, kat_o, vat_o, ka16_o, va16_o, sza_o, outb_o, sfin_o, qm_o, szm_o,
         ga_o, gb_o, gm_o, s_scr, ka_o, va_o) = rest
    else:
        (qa_o, ka_o, va_o, ka16_o, va16_o, sza_o, qb_o, kb_o, vb_o, la_o, szb_o, qm_o, szm_o,
         ga_o, gb_o, gm_o) = rest
    if fused:
        @pl.when(pl.program_id(0) % per_stream == 0)
        def _():
            s_scr[...] = s0_ref[pl.program_id(0) // per_stream].reshape(W_BK, DV_B)

    h = _rms_rows(x_ref[...], nin_ref[...]).astype(BF16)

    def proj(i):
        return _dot_nt(h, w_ref[_IN_OFF[i]:_IN_OFF[i + 1], :])

    qb = (proj(4) * (DK_B ** -0.5)).astype(BF16)
    kb = proj(5).astype(BF16)
    vb = proj(6).astype(BF16)
    z = _dot(proj(7).astype(BF16), wg2_ref[...]) + bg_ref[...]
    la = _log_sigmoid(z) * (1.0 / GATE_TAU)
    szb = _silu(proj(8)).astype(BF16)

    scale = HEAD_DIM ** -0.5 * LOG2E

    def emit_qa():
        qa_o[...] = (_head_rms(proj(0), gqa_ref[...], HEAD_DIM) * scale).astype(BF16)

    def emit_ka():
        ka = _head_rms(proj(1), gka_ref[...], HEAD_DIM)
        ka_o[...] = ka
        ka16_o[...] = ka.astype(BF16)

    def emit_va():
        va = proj(2)
        va_o[...] = va
        va16_o[...] = va.astype(BF16)

    def emit_sza():
        sza_o[...] = _silu(proj(3)).astype(BF16)

    def emit_qm():
        qm_o[...] = (_head_rms(proj(9), gqm_ref[...], HEAD_DIM) * scale).astype(BF16)

    def emit_szm():
        szm_o[...] = _silu(proj(10)).astype(BF16)

    def emit_gate(i, ref):
        ref[...] = _sigmoid(proj(i)).astype(BF16)

    pending = [emit_qa, emit_ka, emit_va, emit_sza, emit_qm, emit_szm,
               functools.partial(emit_gate, 11, ga_o), functools.partial(emit_gate, 12, gb_o),
               functools.partial(emit_gate, 13, gm_o)]

    def emit_next():
        if pending:
            pending.pop(0)()

    if fused:
        chunks = [slice(r, r + GLA_C) for r in range(0, x_ref.shape[0], GLA_C)]
        items = [(qb[rows].astype(F32), kb[rows].astype(F32), vb[rows].astype(F32), vb[rows], la[rows])
                 for rows in chunks]
        state = s_scr[...]
        outs = []
        for rows, item, (o, cum) in zip(chunks, items, _gla_intra(items, GLA_C, emit_next)):
            inter, state = _gla_state_step(item[0], item[1], item[3], cum, state)
            outs.append(_gla_finish(o + inter, ggla_ref, szb[rows]))
            emit_next()
        s_scr[...] = state
        outb_o[...] = jnp.concatenate(outs, axis=0)
    else:
        qb_o[...] = qb
        kb_o[...] = kb
        vb_o[...] = vb
        la_o[...] = la
        szb_o[...] = szb
    while pending:
        emit_next()

    if fused:
        @pl.when(pl.program_id(0) % per_stream == per_stream - 1)
        def _():
            sfin_o[0] = s_scr[...].reshape(H_B, DK_B, DV_B)
            kat_o[0] = ka_o[...].T
            vat_o[0] = va_o[...].T


def _front(x2d, wts, tm, t, keep, s0=None):
    n = x2d.shape[0]
    tm = min(tm, n)
    row = lambda w: pl.BlockSpec((tm, w), lambda i: (i, 0))
    out = lambda rows_, w, d, spec: (jax.ShapeDtypeStruct((rows_, w), d), spec)
    tail = [out(n, W_M, BF16, row(W_M)), out(n, W_M, BF16, row(W_M))] + [out(n, D_MODEL, BF16, row(D_MODEL))] * 3
    in_specs = [row(D_MODEL), _full((1, D_MODEL)), _full((D_IN, D_MODEL)),
                _full((GATE_RANK, W_BK)), _full((1, W_BK)), _full((1, W_A)), _full((1, W_A)), _full((1, W_M))]
    args = [x2d, wts["norm_in"], wts["w_in_t"], wts["w_gate2"], wts["b_gate"], wts["g_qa"], wts["g_ka"], wts["g_qm"]]
    if s0 is None:
        assert keep == t
        kv = [out(n, W_A, F32, row(W_A))] * 2
        mid = [out(n, W_BK, BF16, row(W_BK)), out(n, W_BK, BF16, row(W_BK)), out(n, W_BV, BF16, row(W_BV)),
               out(n, W_BK, F32, row(W_BK)), out(n, W_BV, BF16, row(W_BV))]
        body, scratch = _front_kernel, []
    else:
        assert t % tm == 0 and tm % GLA_C == 0 and keep == tm
        streams, per_stream = n // t, t // tm
        state = (streams, H_B, DK_B, DV_B)
        kv = [(jax.ShapeDtypeStruct((streams, W_A, keep), F32),
               pl.BlockSpec((1, W_A, keep), lambda i: (i // per_stream, 0, 0)))] * 2
        mid = [out(n, W_BV, BF16, row(W_BV)),
               (jax.ShapeDtypeStruct(state, F32), pl.BlockSpec((1,) + state[1:], lambda i: (i // per_stream, 0, 0, 0)))]
        in_specs += [_full((1, W_BV)), _full(state)]
        args += [wts["g_gla"], s0]
        body = functools.partial(_front_kernel, per_stream=per_stream)
        scratch = [pltpu.VMEM((W_BK, DV_B), F32), pltpu.VMEM((tm, W_A), F32), pltpu.VMEM((tm, W_A), F32)]
    outs = [out(n, W_A, BF16, row(W_A))] + kv + [out(n, W_A, BF16, row(W_A))] * 3 + mid + tail
    return pl.pallas_call(
        body,
        grid=(n // tm,),
        in_specs=in_specs,
        out_specs=[spec for _, spec in outs],
        out_shape=[shape for shape, _ in outs],
        scratch_shapes=scratch,
        compiler_params=_params("arbitrary"),
        name="front",
    )(*args)


def _back_kernel(x_ref, a_ref, b_ref, m_ref, ga_ref, gb_ref, gm_ref, wa_ref, wb_ref, wm_ref, wo_ref, y_ref):
    u = (ga_ref[...].astype(F32) * _dot(a_ref[...], wa_ref[...])
         + gb_ref[...].astype(F32) * _dot(b_ref[...], wb_ref[...])
         + gm_ref[...].astype(F32) * _dot(m_ref[...], wm_ref[...]))
    y_ref[...] = x_ref[...] + _dot(u.astype(BF16), wo_ref[...])


def _back(x2d, a, b, m, ga, gb, gm, wts, tm):
    n = x2d.shape[0]
    tm = min(tm, n)
    row = lambda w: pl.BlockSpec((tm, w), lambda i: (i, 0))
    return pl.pallas_call(
        _back_kernel,
        grid=(n // tm,),
        in_specs=[row(D_MODEL), row(W_A), row(W_BV), row(W_M), row(D_MODEL), row(D_MODEL), row(D_MODEL),
                  _full((W_A, D_MODEL)), _full((W_BV, D_MODEL)), _full((W_M, D_MODEL)),
                  _full((D_MODEL, D_MODEL))],
        out_specs=row(D_MODEL),
        out_shape=jax.ShapeDtypeStruct((n, D_MODEL), F32),
        compiler_params=_params("parallel"),
        name="back",
    )(x2d, a, b, m, ga, gb, gm, wts["w_up_a"], wts["w_up_b"], wts["w_up_m"], wts["w_out"])


def _bias_kernel(e_ref, tile_ref):
    e = jnp.broadcast_to(e_ref[0] * LOG2E, (CHUNK, 2 * BAND))
    tile_ref[0] = pltpu.roll(e, 0, 1, stride=1, stride_axis=0)[:, :BIAS_W]


def _bias_tile(rel_bias):
    j = jnp.arange(2 * BAND)
    dist = jnp.where(j < BIAS_W, BAND - j, REL_CLIP)
    e = rel_bias[:, jnp.clip(dist, -REL_CLIP, REL_CLIP) + REL_CLIP].reshape(H_A, 1, 2 * BAND)
    return pl.pallas_call(
        _bias_kernel,
        grid=(H_A,),
        in_specs=[pl.BlockSpec((1, 1, 2 * BAND), lambda h: (h, 0, 0))],
        out_specs=pl.BlockSpec((1, CHUNK, BIAS_W), lambda h: (h, 0, 0)),
        out_shape=jax.ShapeDtypeStruct((H_A, CHUNK, BIAS_W), F32),
        compiler_params=_params("parallel"),
        name="bias_tile",
    )(e)


def _softmax_pv_staged(scores, values):
    biased = [[s if b is None else s + b for s, b in s_list] for s_list in scores]
    tops = [functools.reduce(jnp.maximum, [jnp.max(s, axis=-1, keepdims=True) for s in s_list])
            for s_list in biased]
    probs = [[jnp.exp2(s - m) for s in s_list] for s_list, m in zip(biased, tops)]
    sums = [functools.reduce(jnp.add, [jnp.sum(x, axis=-1, keepdims=True) for x in p_list]) for p_list in probs]
    return [functools.reduce(jnp.add, [(v(x.astype(BF16)) if callable(v) else _dot(x.astype(BF16), v))
                                       for x, v in zip(p_list, v_list)]) / l
            for p_list, v_list, l in zip(probs, values, sums)]


def _head_pairs_attention(q_pairs, k_lists, v_lists, bias_fn):
    m_rows = q_pairs[0].shape[0]
    lane = lax.broadcasted_iota(jnp.int32, (m_rows, LANES), 1)
    first = lane < HEAD_DIM
    scores = []
    for p, qp in enumerate(q_pairs):
        zero = jnp.zeros_like(qp)
        q2 = jnp.concatenate([jnp.where(first, qp, zero), jnp.where(first, zero, qp)], axis=0)
        s_list = []
        for i, k in enumerate(k_lists[p]):
            b0, b1 = bias_fn(p, 0, i), bias_fn(p, 1, i)
            s_list.append((_dot_nt(q2, k), None if b0 is None else jnp.concatenate([b0, b1], axis=0)))
        scores.append(s_list)
    outs = _softmax_pv_staged(scores, v_lists)
    return [jnp.where(first, o[:m_rows], o[m_rows:]) for o in outs]


def _attn_back_kernel(q_ref, *refs):
    nk = BAND // QBLK + 1
    k_refs, v_refs = refs[:nk], refs[nk:2 * nk]
    (sz_ref, bias_ref, qm_ref, mk_ref, mv_ref, szm_ref, x_ref, b_ref, ga_ref, gb_ref, gm_ref,
     wa_ref, wb_ref, wm_ref, wo_ref, y_ref, k_buf, v_buf, bias_buf, a_buf, m_buf) = refs[2 * nk:]
    i = pl.program_id(1)
    for j, (k_ref, v_ref) in enumerate(zip(k_refs, v_refs)):
        k_buf[j * QBLK:(j + 1) * QBLK, :] = k_ref[0]
        v_buf[j * QBLK:(j + 1) * QBLK, :] = v_ref[0]

    @pl.when(i <= BAND // QBLK)
    def _():
        col = lax.broadcasted_iota(jnp.int32, (1, BIAS_W), 1)
        for c in range(QBLK // CHUNK):
            before_start = jnp.where(col + c * CHUNK < (BAND // QBLK - i) * QBLK, NEG, 0.0)
            for h in range(H_A):
                bias_buf[c, h] = bias_ref[h] + before_start

    lanes_m = [slice(p * LANES, (p + 1) * LANES) for p in range(H_M // 2)]
    mk = [[mk_ref[0, :, lanes].astype(BF16)] for lanes in lanes_m]
    mv = [[mv_ref[0, :, lanes].astype(BF16)] for lanes in lanes_m]
    blocks = [(slice(r, r + MEM_ROWS), p) for r in range(0, QBLK, MEM_ROWS) for p in range(len(lanes_m))]
    outs = _head_pairs_attention([qm_ref[0, rows, lanes_m[p]] for rows, p in blocks],
                                 [mk[p] for _, p in blocks], [mv[p] for _, p in blocks], lambda n, e, _: None)
    for (rows, p), o in zip(blocks, outs):
        m_buf[rows, lanes_m[p]] = (o * szm_ref[0, rows, lanes_m[p]].astype(F32)).astype(BF16)

    pairs = [slice(p * LANES, (p + 1) * LANES) for p in range(H_A // 2)]
    for c in range(QBLK // CHUNK):
        rows = slice(c * CHUNK, (c + 1) * CHUNK)
        win = slice(c * CHUNK, c * CHUNK + WINDOW)
        outs = _head_pairs_attention(
            [q_ref[0, rows, lanes] for lanes in pairs],
            [[k_buf[win, lanes]] for lanes in pairs], [[v_buf[win, lanes]] for lanes in pairs],
            lambda p, e, _: bias_buf[c, 2 * p + e, :, :WINDOW])
        for lanes, o in zip(pairs, outs):
            a_buf[rows, lanes] = (o * sz_ref[0, rows, lanes].astype(F32)).astype(BF16)

    u = (ga_ref[0].astype(F32) * _dot(a_buf[...], wa_ref[...])
         + gb_ref[0].astype(F32) * _dot(b_ref[0], wb_ref[...])
         + gm_ref[0].astype(F32) * _dot(m_buf[...], wm_ref[...]))
    y_ref[0] = x_ref[0] + _dot(u.astype(BF16), wo_ref[...])


def _attn_back(q, k, v, sz, bias, qm, mk, mv, szm, x, out_b, ga, gb, gm, wts):
    b, t, _ = q.shape
    nb = BAND // QBLK
    n = t // QBLK
    assert BAND % QBLK == 0 and t % QBLK == 0 and QBLK % MEM_ROWS == 0
    rows = lambda w, d=0: pl.BlockSpec((1, QBLK, w), lambda bi, i: (bi, jnp.maximum(i - d, 0), 0))
    kv_specs = [rows(W_A, nb - j) for j in range(nb + 1)]
    mem = pl.BlockSpec((1, N_MEM, W_M), lambda bi, i: (bi, 0, 0))
    return pl.pallas_call(
        _attn_back_kernel,
        grid=(b, n),
        in_specs=[rows(W_A)] + kv_specs + kv_specs + [rows(W_A), _full((H_A, CHUNK, BIAS_W)),
                  rows(W_M), mem, mem, rows(W_M),
                  rows(D_MODEL), rows(W_BV), rows(D_MODEL), rows(D_MODEL), rows(D_MODEL),
                  _full((W_A, D_MODEL)), _full((W_BV, D_MODEL)), _full((W_M, D_MODEL)),
                  _full((D_MODEL, D_MODEL))],
        out_specs=rows(D_MODEL),
        out_shape=jax.ShapeDtypeStruct((b, t, D_MODEL), F32),
        scratch_shapes=[pltpu.VMEM((KSPAN, W_A), BF16), pltpu.VMEM((KSPAN, W_A), BF16),
                        pltpu.VMEM((QBLK // CHUNK, H_A, CHUNK, BIAS_W), F32),
                        pltpu.VMEM((QBLK, W_A), BF16), pltpu.VMEM((QBLK, W_M), BF16)],
        compiler_params=_params("parallel", "arbitrary"),
        name="attn_back",
    )(q, *([k] * (nb + 1)), *([v] * (nb + 1)), sz, bias, qm, mk, mv, szm, x, out_b, ga, gb, gm,
      wts["w_up_a"], wts["w_up_b"], wts["w_up_m"], wts["w_out"])


STEP_STREAMS = 4


def _attn_step_kernel(q_ref, kn_ref, vn_ref, kp_ref, vp_ref, sz_ref, bias_ref, o_ref):
    s_len = q_ref.shape[1]
    for b in range(STEP_STREAMS):
        scores, values = [], []
        for h in range(H_A):
            lanes = slice(h * HEAD_DIM, (h + 1) * HEAD_DIM)
            q = q_ref[b, :, lanes]
            scores.append([(_dot(q, kp_ref[b, h].astype(BF16)), bias_ref[h, :s_len, :BAND]),
                           (_dot_nt(q, kn_ref[b, :, lanes]), bias_ref[h, :s_len, BAND:BAND + s_len])])
            values.append([functools.partial(_dot_nt, b=vp_ref[b, h].astype(BF16)), vn_ref[b, :, lanes]])
        o = jnp.concatenate(_softmax_pv_staged(scores, values), axis=-1)
        o_ref[b] = (o * sz_ref[b].astype(F32)).astype(BF16)


def _attn_step(q, k_new, v_new, k_past_t, v_past_t, sz, bias):
    b, s_len, _ = q.shape
    assert k_past_t.shape[1:] == (H_A, HEAD_DIM, BAND) and s_len <= CHUNK and b % STEP_STREAMS == 0
    new = pl.BlockSpec((STEP_STREAMS, s_len, W_A), lambda i: (i, 0, 0))
    past = pl.BlockSpec((STEP_STREAMS, H_A, HEAD_DIM, BAND), lambda i: (i, 0, 0, 0))
    return pl.pallas_call(
        _attn_step_kernel,
        grid=(b // STEP_STREAMS,),
        in_specs=[new, new, new, past, past, new, _full((H_A, CHUNK, BIAS_W))],
        out_specs=new,
        out_shape=jax.ShapeDtypeStruct((b, s_len, W_A), BF16),
        compiler_params=_params("parallel"),
        name="attn_step",
    )(q, k_new, v_new, k_past_t, v_past_t, sz, bias)


def _mem_kv_kernel(mem_ref, gmem_ref, w_ref, gkm_ref, mk_ref, mv_ref):
    h = _rms_rows(mem_ref[0], gmem_ref[...]).astype(BF16)
    kv = _dot(h, w_ref[...])
    mk_ref[0] = _head_rms(kv[:, :W_M], gkm_ref[...], HEAD_DIM)
    mv_ref[0] = kv[:, W_M:]


def _mem_kv(mem, wts):
    b = mem.shape[0]
    out = pl.BlockSpec((1, N_MEM, W_M), lambda i: (i, 0, 0))
    return pl.pallas_call(
        _mem_kv_kernel,
        grid=(b,),
        in_specs=[pl.BlockSpec((1, N_MEM, D_MODEL), lambda i: (i, 0, 0)), _full((1, D_MODEL)),
                  _full((D_MODEL, 2 * W_M)), _full((1, W_M))],
        out_specs=[out, out],
        out_shape=[jax.ShapeDtypeStruct((b, N_MEM, W_M), F32)] * 2,
        compiler_params=_params("parallel"),
        name="mem_kv",
    )(mem, wts["g_mem"], wts["w_mem_kv"], wts["g_km"])


MEM_ROWS = 128
MEM_STREAMS = 8


def _attn_mem_step_kernel(q_ref, mk_ref, mv_ref, sz_ref, o_ref):
    for b in range(q_ref.shape[0]):
        scores, values = [], []
        for h in range(H_M):
            lanes = slice(h * HEAD_DIM, (h + 1) * HEAD_DIM)
            scores.append([(_dot(q_ref[b, :, lanes], mk_ref[b, h].astype(BF16)), None)])
            values.append([functools.partial(_dot_nt, b=mv_ref[b, h].astype(BF16))])
        o = jnp.concatenate(_softmax_pv_staged(scores, values), axis=-1)
        o_ref[b] = (o * sz_ref[b].astype(F32)).astype(BF16)


def _attn_mem_step(q, mk_t, mv_t, sz):
    b, t, _ = q.shape
    nb = min(b, MEM_STREAMS)
    assert b % nb == 0 and mk_t.shape[1:] == (H_M, HEAD_DIM, N_MEM)
    rows = pl.BlockSpec((nb, t, W_M), lambda i: (i, 0, 0))
    mem = pl.BlockSpec((nb, H_M, HEAD_DIM, N_MEM), lambda i: (i, 0, 0, 0))
    return pl.pallas_call(
        _attn_mem_step_kernel,
        grid=(b // nb,),
        in_specs=[rows, mem, mem, rows],
        out_specs=rows,
        out_shape=jax.ShapeDtypeStruct((b, t, W_M), BF16),
        compiler_params=_params("parallel"),
        name="attn_mem_step",
    )(q, mk_t, mv_t, sz)


def _gla_intra(items, group, between=lambda: None):
    n = len(items)
    qs, ks, vs, v16s = ([it[j] for it in items] for j in range(4))
    c = qs[0].shape[0]
    las = [it[4] * LOG2E for it in items]
    r_cc = lax.broadcasted_iota(jnp.int32, (c, c), 0)
    c_cc = lax.broadcasted_iota(jnp.int32, (c, c), 1)
    causal = c_cc <= r_cc
    if group < c:
        causal = causal & (r_cc // group == c_cc // group)
    causal16 = causal.astype(BF16)
    cums = [_dot_exact_lhs(causal16, la) for la in las]
    cumxs = [cum - la for cum, la in zip(cums, las)]
    r_ck = lax.broadcasted_iota(jnp.int32, (c, W_BK), 0)

    head_of_k = lax.broadcasted_iota(jnp.int32, (W_BK, W_BV), 0) // DK_B
    head_of_v = lax.broadcasted_iota(jnp.int32, (W_BK, W_BV), 1) // DV_B
    spread = (head_of_k == head_of_v).astype(BF16)
    os = [_dot((q * k).astype(BF16), spread) * v for q, k, v in zip(qs, ks, vs)]

    def shifted(x, j):
        return pltpu.roll(x.reshape(c // SUBLANES, SUBLANES, W_BK), j % SUBLANES, 1).reshape(c, W_BK)

    def block_edges(cum, cumx, half):
        if half >= SUBLANES:
            nb = c // half
            first = jnp.broadcast_to(cumx.reshape(nb, half, W_BK)[:, 0:1, :], (nb, half, W_BK))
            last = jnp.broadcast_to(cum.reshape(nb, half, W_BK)[:, half - 1:half, :], (nb, half, W_BK))
            return first.reshape(c, W_BK), last.reshape(c, W_BK)
        r_in = r_ck % half
        first, last = cumx, cum
        for j in range(1, half):
            first = jnp.where(r_in == j, shifted(cumx, j), first)
            last = jnp.where(r_in == half - 1 - j, shifted(cum, -j), last)
        return first, last

    head_of_row = lax.broadcasted_iota(jnp.int32, (W_BK, c), 0) // DK_B
    head_rows = [jnp.where(head_of_row == h, 1.0, 0.0).astype(BF16) for h in range(H_B)]
    r_att = lax.broadcasted_iota(jnp.int32, (c, H_B * c), 0)
    s_att = lax.broadcasted_iota(jnp.int32, (c, H_B * c), 1) % c
    atts = [jnp.zeros((c, H_B * c), F32) for _ in range(n)]
    half = 1
    while half < group:
        edges = [block_edges(cum, cumx, half) for cum, cumx in zip(cums, cumxs)]
        odd = (r_ck // half) % 2 == 1
        qts = [jnp.where(odd, q * jnp.exp2(cum - first), 0.0).astype(BF16)
               for q, cum, (first, _) in zip(qs, cums, edges)]
        kts = [jnp.where(odd, 0.0, k * jnp.exp2(last - cum)).T.astype(BF16)
               for k, cum, (_, last) in zip(ks, cums, edges)]
        prods = [_dot(qt, jnp.concatenate([kt * m for m in head_rows], axis=1)) for qt, kt in zip(qts, kts)]
        if 2 * half < c:
            same = r_att // (2 * half) == s_att // (2 * half)
            prods = [jnp.where(same, a, 0.0) for a in prods]
        atts = [att + a for att, a in zip(atts, prods)]
        between()
        half *= 2
    if group > 1:
        att16s = [att.astype(BF16) for att in atts]
        os = [o + jnp.concatenate([_dot(att16[:, h * c:(h + 1) * c], v16[:, h * DV_B:(h + 1) * DV_B])
                                   for h in range(H_B)], axis=-1)
              for o, att16, v16 in zip(os, att16s, v16s)]
    return list(zip(os, cums))


def _gla_state_step(q, k, v16, cum, s_old):
    c = q.shape[0]
    q_in = (q * jnp.exp2(cum)).astype(BF16)
    k_out_t = (k * jnp.exp2(cum[c - 1:c, :] - cum)).T.astype(BF16)
    keep = jnp.exp2(cum.T[:, c - 1:c])
    kd = lambda h: slice(h * DK_B, (h + 1) * DK_B)
    vd = lambda h: slice(h * DV_B, (h + 1) * DV_B)
    o = jnp.concatenate([_dot(q_in[:, kd(h)], s_old[kd(h)].astype(BF16)) for h in range(H_B)], axis=-1)
    s_new = jnp.concatenate([s_old[kd(h)] * keep[kd(h)] + _dot(k_out_t[kd(h)], v16[:, vd(h)])
                             for h in range(H_B)], axis=0)
    return o, s_new


def _gla_finish(o, g_ref, sz):
    outs = [_rms_rows(o[:, h * DV_B:(h + 1) * DV_B], g_ref[:, h * DV_B:(h + 1) * DV_B]) for h in range(H_B)]
    return (jnp.concatenate(outs, axis=-1) * sz.astype(F32)).astype(BF16)


def _gla_step_kernel(q_ref, k_ref, v_ref, la_ref, sz_ref, g_ref, s0_ref, o_ref, s_ref, *, t):
    c = GLA_C
    n = c // t
    q, k, la = q_ref[...].astype(F32), k_ref[...].astype(F32), la_ref[...]
    v16 = v_ref[...]
    (o, cum), = _gla_intra([(q, k, v16.astype(F32), v16, la)], t)

    q_in = (q * jnp.exp2(cum)).astype(BF16)
    cum_t, k_t = cum.T, k.T
    stream_of_col = lax.broadcasted_iota(jnp.int32, (W_BK, c), 1) // t
    inter = []
    for s in range(n):
        rows = slice(s * t, (s + 1) * t)
        last = cum_t[:, (s + 1) * t - 1:(s + 1) * t]
        k_out_t = (k_t * jnp.exp2(jnp.where(stream_of_col == s, last - cum_t, NEG))).astype(BF16)
        keep = jnp.exp2(last)
        o_s = []
        for h in range(H_B):
            ks = slice(h * DK_B, (h + 1) * DK_B)
            vs = slice(h * DV_B, (h + 1) * DV_B)
            s_old = s0_ref[s, h]
            o_s.append(_dot(q_in[rows, ks], s_old.astype(BF16)))
            s_ref[s, h] = s_old * keep[ks] + _dot(k_out_t[ks], v16[:, vs])
        inter.append(jnp.concatenate(o_s, axis=-1))
    o_ref[...] = _gla_finish(o + jnp.concatenate(inter, axis=0), g_ref, sz_ref[...])


def _gla_step(q, k, v, la, sz, g, s0):
    b, t, _ = q.shape
    n = GLA_C // t
    assert GLA_C % t == 0 and t % (2 * GLA_BASE) == 0 and b % n == 0
    flat = lambda z: z.reshape(b * t, z.shape[-1])
    blk = lambda w: pl.BlockSpec((GLA_C, w), lambda i: (i, 0))
    state = pl.BlockSpec((n, H_B, DK_B, DV_B), lambda i: (i, 0, 0, 0))
    o, s_new = pl.pallas_call(
        functools.partial(_gla_step_kernel, t=t),
        grid=(b // n,),
        in_specs=[blk(W_BK), blk(W_BK), blk(W_BV), blk(W_BK), blk(W_BV),
                  pl.BlockSpec((1, W_BV), lambda i: (0, 0)), state],
        out_specs=[blk(W_BV), state],
        out_shape=[jax.ShapeDtypeStruct((b * t, W_BV), BF16),
                   jax.ShapeDtypeStruct((b, H_B, DK_B, DV_B), F32)],
        compiler_params=_params("parallel"),
        name="gla_step",
    )(flat(q), flat(k), flat(v), flat(la), flat(sz), g, s0)
    return o.reshape(b, t, W_BV), s_new


def _prep_weights(l, norm_in, w_in, g_qa, g_ka, w_gate2, b_gate, g_gla_out, g_mem, w_mem_kv, g_qm, g_km,
                  w_up_a, w_up_b, w_up_m, w_out):
    tile = lambda gain, n: jnp.tile(gain, n).reshape(1, -1)
    return {
        "norm_in": norm_in[l].reshape(1, D_MODEL),
        "w_in_t": w_in[l].T.astype(BF16),
        "w_gate2": w_gate2[l].astype(BF16),
        "b_gate": b_gate[l].reshape(1, W_BK),
        "g_qa": tile(g_qa[l], H_A), "g_ka": tile(g_ka[l], H_A), "g_qm": tile(g_qm[l], H_M),
        "g_km": tile(g_km[l], H_M), "g_gla": tile(g_gla_out[l], H_B),
        "g_mem": g_mem[l].reshape(1, D_MODEL),
        "w_mem_kv": w_mem_kv[l].astype(BF16),
        "w_up_a": w_up_a[l].astype(BF16), "w_up_b": w_up_b[l].astype(BF16),
        "w_up_m": w_up_m[l].astype(BF16), "w_out": w_out[l].astype(BF16),
    }


def _layer_long(x, wts, bias, mk, mv, s0):
    b, t, _ = x.shape
    keep = min(BAND, t)
    r3 = lambda z: z.reshape(b, t, z.shape[-1])
    heads = lambda z: jnp.transpose(z.reshape(b, H_A, HEAD_DIM, keep), (0, 3, 1, 2))
    (qa, ka, va, ka16, va16, sza, out_b, s_new, qm, szm, ga, gb, gm) = _front(
        x.reshape(b * t, D_MODEL), wts, keep, t, keep, s0)
    y = _attn_back(r3(qa), r3(ka16), r3(va16), r3(sza), bias, r3(qm), mk, mv, r3(szm), x, r3(out_b),
                   r3(ga), r3(gb), r3(gm), wts)
    return y, heads(ka), heads(va), s_new


def _layer_short(x, wts, attend_a, attend_m, s0, tm):
    b, t, _ = x.shape
    keep = min(BAND, t)
    x2d = x.reshape(b * t, D_MODEL)
    r3 = lambda z: z.reshape(b, t, z.shape[-1])
    flat = lambda z: z.reshape(b * t, z.shape[-1])
    heads = lambda z: z.reshape(b, keep, H_A, HEAD_DIM)
    (qa, ka, va, ka16, va16, sza, qb, kb, vb, la, szb, qm, szm, ga, gb, gm) = _front(x2d, wts, tm, t, keep)
    out_b, s_new = _gla_step(r3(qb), r3(kb), r3(vb), r3(la), r3(szb), wts["g_gla"], s0)
    out_a = attend_a(r3(qa), r3(ka16), r3(va16), r3(sza))
    out_m = attend_m(r3(qm), r3(szm))
    y = _back(x2d, flat(out_a), flat(out_b), flat(out_m), ga, gb, gm, wts, tm)
    return y.reshape(b, t, D_MODEL), heads(ka), heads(va), s_new


def kernel(x_prompt, x_sample, mem_prompt, cache_a_k, cache_a_v, state_gla, cache_mem_k, cache_mem_v,
           norm_in, w_in, g_qa, g_ka, rel_bias, w_gate2, b_gate, g_gla_out, g_mem, w_mem_kv, g_qm, g_km,
           w_up_a, w_up_b, w_up_m, w_out):
    depth = w_in.shape[0]
    xp, xs = x_prompt, x_sample
    bp, tp, _ = xp.shape
    bs, ts, _ = xs.shape
    akp, avp, sgp, mkp, mvp, aks, avs, sgs = [], [], [], [], [], [], [], []
    for l in range(depth):
        wts = _prep_weights(l, norm_in, w_in, g_qa, g_ka, w_gate2, b_gate, g_gla_out, g_mem, w_mem_kv,
                            g_qm, g_km, w_up_a, w_up_b, w_up_m, w_out)
        bias = _bias_tile(rel_bias[l])
        mk, mv = _mem_kv(mem_prompt, wts)
        s0 = jnp.zeros((bp, H_B, DK_B, DV_B), state_gla.dtype)
        xp, ka, va, sp = _layer_long(xp, wts, bias, mk, mv, s0)
        akp.append(ka)
        avp.append(va)
        sgp.append(sp)
        mkp.append(mk.reshape(bp, N_MEM, H_M, HEAD_DIM))
        mvp.append(mv.reshape(bp, N_MEM, H_M, HEAD_DIM))
        rows_last = lambda z: jnp.transpose(z, (0, 2, 3, 1))
        past_k, past_v = rows_last(cache_a_k[l]), rows_last(cache_a_v[l])
        mem_k, mem_v = rows_last(cache_mem_k[l]), rows_last(cache_mem_v[l])
        attend_s = lambda q, k, v, sz: _attn_step(q, k, v, past_k, past_v, sz, bias)
        attend_ms = lambda q, sz: _attn_mem_step(q, mem_k, mem_v, sz)
        xs, ka_s, va_s, ss = _layer_short(xs, wts, attend_s, attend_ms, state_gla[l], 512)
        aks.append(ka_s)
        avs.append(va_s)
        sgs.append(ss)
    return (xp, xs, jnp.stack(akp), jnp.stack(avp), jnp.stack(sgp), jnp.stack(mkp), jnp.stack(mvp),
            jnp.stack(aks), jnp.stack(avs), jnp.stack(sgs))
```

```python
import functools

import jax
import jax.numpy as jnp
from jax import lax
from jax.experimental import pallas as pl
from jax.experimental.pallas import tpu as pltpu

F32 = jnp.float32
BF16 = jnp.bfloat16

D_MODEL = 1024
CHUNK = 64
LEFT_CHUNKS = 8
HEAD_DIM = 64
H_A = 8
W_A = H_A * HEAD_DIM
REL_CLIP = 128
H_B = 4
DK_B = 64
DV_B = 128
W_BK = H_B * DK_B
W_BV = H_B * DV_B
GATE_RANK = 16
GATE_TAU = 16.0
N_MEM = 256
H_M = 4
W_M = H_M * HEAD_DIM
EPS = 1e-6
IN_SIZES = (W_A, W_A, W_A, W_A, W_BK, W_BK, W_BV, GATE_RANK, W_BV, W_M, W_M, D_MODEL, D_MODEL, D_MODEL)

LANES = 128
SUBLANES = 8
VMEM_LIMIT = 56 * 1024 * 1024
NEG = -1e30
LOG2E = 1.4426950408889634

BAND = LEFT_CHUNKS * CHUNK
QBLK = 512
KSPAN = BAND + QBLK
WINDOW = BAND + CHUNK
BIAS_W = BAND + 2 * CHUNK
GLA_C = 128
GLA_BASE = SUBLANES


def _dot(a, b):
    return jnp.dot(a, b, preferred_element_type=F32)


def _dot_nt(a, b):
    return lax.dot_general(a, b, (((1,), (1,)), ((), ())), preferred_element_type=F32)


def _split3(x):
    hi = x.astype(BF16)
    r = x - hi.astype(F32)
    mid = r.astype(BF16)
    lo = (r - mid.astype(F32)).astype(BF16)
    return hi, mid, lo


def _dot_exact_lhs(a01, x):
    hi, mid, lo = _split3(x)
    return _dot(a01, hi) + _dot(a01, mid) + _dot(a01, lo)


def _sigmoid(z):
    return 0.5 * jnp.tanh(0.5 * z) + 0.5


def _silu(z):
    return z * _sigmoid(z)


def _log_sigmoid(z):
    return jnp.minimum(z, 0.0) - jnp.log1p(jnp.exp(-jnp.abs(z)))


def _rms_rows(x, g):
    ms = jnp.mean(x * x, axis=-1, keepdims=True)
    return x * lax.rsqrt(ms + EPS) * g


def _head_mean_matrix(width, head):
    r = lax.broadcasted_iota(jnp.int32, (width, width), 0) // head
    c = lax.broadcasted_iota(jnp.int32, (width, width), 1) // head
    return jnp.where(r == c, 1.0 / head, 0.0).astype(BF16)


def _head_rms(x, g, head):
    avg = _head_mean_matrix(x.shape[-1], head)
    ms = _dot((x * x).astype(BF16), avg)
    return x * lax.rsqrt(ms + EPS) * g


def _full(shape):
    nd = len(shape)
    return pl.BlockSpec(shape, lambda *_: (0,) * nd, pipeline_mode=pl.Buffered(1))


def _params(*sem):
    return pltpu.CompilerParams(dimension_semantics=sem, vmem_limit_bytes=VMEM_LIMIT)


_IN_OFF = tuple(int(sum(IN_SIZES[:i])) for i in range(len(IN_SIZES) + 1))
D_IN = _IN_OFF[-1]


def _front_kernel(x_ref, nin_ref, w_ref, wg2_ref, bg_ref, gqa_ref, gka_ref, gqm_ref, *rest, per_stream=None):
    fused = per_stream is not None
    if fused:
        (ggla_ref, s0_ref, qa_o, kat_o, vat_o, ka16_o, va16_o, sza_o, outb_o, sfin_o, qm_o, szm_o,
         ga_o, gb_o, gm_o, s_scr, ka_o, va_o) = rest
    else:
        (qa_o, ka_o, va_o, ka16_o, va16_o, sza_o, qb_o, kb_o, vb_o, la_o, szb_o, qm_o, szm_o,
         ga_o, gb_o, gm_o) = rest
    if fused:
        @pl.when(pl.program_id(0) % per_stream == 0)
        def _():
            s_scr[...] = s0_ref[pl.program_id(0) // per_stream].reshape(W_BK, DV_B)

    h = _rms_rows(x_ref[...], nin_ref[...]).astype(BF16)

    def proj(i):
        return _dot_nt(h, w_ref[_IN_OFF[i]:_IN_OFF[i + 1], :])

    qb = (proj(4) * (DK_B ** -0.5)).astype(BF16)
    kb = proj(5).astype(BF16)
    vb = proj(6).astype(BF16)
    z = _dot(proj(7).astype(BF16), wg2_ref[...]) + bg_ref[...]
    la = _log_sigmoid(z) * (1.0 / GATE_TAU)
    szb = _silu(proj(8)).astype(BF16)

    scale = HEAD_DIM ** -0.5 * LOG2E

    def emit_qa():
        qa_o[...] = (_head_rms(proj(0), gqa_ref[...], HEAD_DIM) * scale).astype(BF16)

    def emit_ka():
        ka = _head_rms(proj(1), gka_ref[...], HEAD_DIM)
        ka_o[...] = ka
        ka16_o[...] = ka.astype(BF16)

    def emit_va():
        va = proj(2)
        va_o[...] = va
        va16_o[...] = va.astype(BF16)

    def emit_sza():
        sza_o[...] = _silu(proj(3)).astype(BF16)

    def emit_qm():
        qm_o[...] = (_head_rms(proj(9), gqm_ref[...], HEAD_DIM) * scale).astype(BF16)

    def emit_szm():
        szm_o[...] = _silu(proj(10)).astype(BF16)

    def emit_gate(i, ref):
        ref[...] = _sigmoid(proj(i)).astype(BF16)

    pending = [emit_qa, emit_ka, emit_va, emit_sza, emit_qm, emit_szm,
               functools.partial(emit_gate, 11, ga_o), functools.partial(emit_gate, 12, gb_o),
               functools.partial(emit_gate, 13, gm_o)]

    def emit_next():
        if pending:
            pending.pop(0)()

    if fused:
        chunks = [slice(r, r + GLA_C) for r in range(0, x_ref.shape[0], GLA_C)]
        items = [(qb[rows].astype(F32), kb[rows].astype(F32), vb[rows].astype(F32), vb[rows], la[rows])
                 for rows in chunks]
        state = s_scr[...]
        outs = []
        for rows, item, (o, cum) in zip(chunks, items, _gla_intra(items, GLA_C, emit_next)):
            inter, state = _gla_state_step(item[0], item[1], item[3], cum, state)
            outs.append(_gla_finish(o + inter, ggla_ref, szb[rows]))
            emit_next()
        s_scr[...] = state
        outb_o[...] = jnp.concatenate(outs, axis=0)
    else:
        qb_o[...] = qb
        kb_o[...] = kb
        vb_o[...] = vb
        la_o[...] = la
        szb_o[...] = szb
    while pending:
        emit_next()

    if fused:
        @pl.when(pl.program_id(0) % per_stream == per_stream - 1)
        def _():
            sfin_o[0] = s_scr[...].reshape(H_B, DK_B, DV_B)
            kat_o[0] = ka_o[...].T
            vat_o[0] = va_o[...].T


def _front(x2d, wts, tm, t, keep, s0=None):
    n = x2d.shape[0]
    tm = min(tm, n)
    row = lambda w: pl.BlockSpec((tm, w), lambda i: (i, 0))
    out = lambda rows_, w, d, spec: (jax.ShapeDtypeStruct((rows_, w), d), spec)
    tail = [out(n, W_M, BF16, row(W_M)), out(n, W_M, BF16, row(W_M))] + [out(n, D_MODEL, BF16, row(D_MODEL))] * 3
    in_specs = [row(D_MODEL), _full((1, D_MODEL)), _full((D_IN, D_MODEL)),
                _full((GATE_RANK, W_BK)), _full((1, W_BK)), _full((1, W_A)), _full((1, W_A)), _full((1, W_M))]
    args = [x2d, wts["norm_in"], wts["w_in_t"], wts["w_gate2"], wts["b_gate"], wts["g_qa"], wts["g_ka"], wts["g_qm"]]
    if s0 is None:
        assert keep == t
        kv = [out(n, W_A, F32, row(W_A))] * 2
        mid = [out(n, W_BK, BF16, row(W_BK)), out(n, W_BK, BF16, row(W_BK)), out(n, W_BV, BF16, row(W_BV)),
               out(n, W_BK, F32, row(W_BK)), out(n, W_BV, BF16, row(W_BV))]
        body, scratch = _front_kernel, []
    else:
        assert t % tm == 0 and tm % GLA_C == 0 and keep == tm
        streams, per_stream = n // t, t // tm
        state = (streams, H_B, DK_B, DV_B)
        kv = [(jax.ShapeDtypeStruct((streams, W_A, keep), F32),
               pl.BlockSpec((1, W_A, keep), lambda i: (i // per_stream, 0, 0)))] * 2
        mid = [out(n, W_BV, BF16, row(W_BV)),
               (jax.ShapeDtypeStruct(state, F32), pl.BlockSpec((1,) + state[1:], lambda i: (i // per_stream, 0, 0, 0)))]
        in_specs += [_full((1, W_BV)), _full(state)]
        args += [wts["g_gla"], s0]
        body = functools.partial(_front_kernel, per_stream=per_stream)
        scratch = [pltpu.VMEM((W_BK, DV_B), F32), pltpu.VMEM((tm, W_A), F32), pltpu.VMEM((tm, W_A), F32)]
    outs = [out(n, W_A, BF16, row(W_A))] + kv + [out(n, W_A, BF16, row(W_A))] * 3 + mid + tail
    return pl.pallas_call(
        body,
        grid=(n // tm,),
        in_specs=in_specs,
        out_specs=[spec for _, spec in outs],
        out_shape=[shape for shape, _ in outs],
        scratch_shapes=scratch,
        compiler_params=_params("arbitrary"),
        name="front",
    )(*args)


def _back_kernel(x_ref, a_ref, b_ref, m_ref, ga_ref, gb_ref, gm_ref, wa_ref, wb_ref, wm_ref, wo_ref, y_ref):
    u = (ga_ref[...].astype(F32) * _dot(a_ref[...], wa_ref[...])
         + gb_ref[...].astype(F32) * _dot(b_ref[...], wb_ref[...])
         + gm_ref[...].astype(F32) * _dot(m_ref[...], wm_ref[...]))
    y_ref[...] = x_ref[...] + _dot(u.astype(BF16), wo_ref[...])


def _back(x2d, a, b, m, ga, gb, gm, wts, tm):
    n = x2d.shape[0]
    tm = min(tm, n)
    row = lambda w: pl.BlockSpec((tm, w), lambda i: (i, 0))
    return pl.pallas_call(
        _back_kernel,
        grid=(n // tm,),
        in_specs=[row(D_MODEL), row(W_A), row(W_BV), row(W_M), row(D_MODEL), row(D_MODEL), row(D_MODEL),
                  _full((W_A, D_MODEL)), _full((W_BV, D_MODEL)), _full((W_M, D_MODEL)),
                  _full((D_MODEL, D_MODEL))],
        out_specs=row(D_MODEL),
        out_shape=jax.ShapeDtypeStruct((n, D_MODEL), F32),
        compiler_params=_params("parallel"),
        name="back",
    )(x2d, a, b, m, ga, gb, gm, wts["w_up_a"], wts["w_up_b"], wts["w_up_m"], wts["w_out"])


def _bias_kernel(e_ref, tile_ref):
    e = jnp.broadcast_to(e_ref[0] * LOG2E, (CHUNK, 2 * BAND))
    tile_ref[0] = pltpu.roll(e, 0, 1, stride=1, stride_axis=0)[:, :BIAS_W]


def _bias_tile(rel_bias):
    j = jnp.arange(2 * BAND)
    dist = jnp.where(j < BIAS_W, BAND - j, REL_CLIP)
    e = rel_bias[:, jnp.clip(dist, -REL_CLIP, REL_CLIP) + REL_CLIP].reshape(H_A, 1, 2 * BAND)
    return pl.pallas_call(
        _bias_kernel,
        grid=(H_A,),
        in_specs=[pl.BlockSpec((1, 1, 2 * BAND), lambda h: (h, 0, 0))],
        out_specs=pl.BlockSpec((1, CHUNK, BIAS_W), lambda h: (h, 0, 0)),
        out_shape=jax.ShapeDtypeStruct((H_A, CHUNK, BIAS_W), F32),
        compiler_params=_params("parallel"),
        name="bias_tile",
    )(e)


def _softmax_pv_staged(scores, values):
    biased = [[s if b is None else s + b for s, b in s_list] for s_list in scores]
    tops = [functools.reduce(jnp.maximum, [jnp.max(s, axis=-1, keepdims=True) for s in s_list])
            for s_list in biased]
    probs = [[jnp.exp2(s - m) for s in s_list] for s_list, m in zip(biased, tops)]
    sums = [functools.reduce(jnp.add, [jnp.sum(x, axis=-1, keepdims=True) for x in p_list]) for p_list in probs]
    return [functools.reduce(jnp.add, [(v(x.astype(BF16)) if callable(v) else _dot(x.astype(BF16), v))
                                       for x, v in zip(p_list, v_list)]) / l
            for p_list, v_list, l in zip(probs, values, sums)]


def _head_pairs_attention(q_pairs, k_lists, v_lists, bias_fn):
    m_rows = q_pairs[0].shape[0]
    lane = lax.broadcasted_iota(jnp.int32, (m_rows, LANES), 1)
    first = lane < HEAD_DIM
    scores = []
    for p, qp in enumerate(q_pairs):
        zero = jnp.zeros_like(qp)
        q2 = jnp.concatenate([jnp.where(first, qp, zero), jnp.where(first, zero, qp)], axis=0)
        s_list = []
        for i, k in enumerate(k_lists[p]):
            b0, b1 = bias_fn(p, 0, i), bias_fn(p, 1, i)
            s_list.append((_dot_nt(q2, k), None if b0 is None else jnp.concatenate([b0, b1], axis=0)))
        scores.append(s_list)
    outs = _softmax_pv_staged(scores, v_lists)
    return [jnp.where(first, o[:m_rows], o[m_rows:]) for o in outs]


def _attn_back_kernel(q_ref, *refs):
    nk = BAND // QBLK + 1
    k_refs, v_refs = refs[:nk], refs[nk:2 * nk]
    (sz_ref, bias_ref, qm_ref, mk_ref, mv_ref, szm_ref, x_ref, b_ref, ga_ref, gb_ref, gm_ref,
     wa_ref, wb_ref, wm_ref, wo_ref, y_ref, k_buf, v_buf, bias_buf, a_buf, m_buf) = refs[2 * nk:]
    i = pl.program_id(1)
    for j, (k_ref, v_ref) in enumerate(zip(k_refs, v_refs)):
        k_buf[j * QBLK:(j + 1) * QBLK, :] = k_ref[0]
        v_buf[j * QBLK:(j + 1) * QBLK, :] = v_ref[0]

    @pl.when(i <= BAND // QBLK)
    def _():
        col = lax.broadcasted_iota(jnp.int32, (1, BIAS_W), 1)
        for c in range(QBLK // CHUNK):
            before_start = jnp.where(col + c * CHUNK < (BAND // QBLK - i) * QBLK, NEG, 0.0)
            for h in range(H_A):
                bias_buf[c, h] = bias_ref[h] + before_start

    lanes_m = [slice(p * LANES, (p + 1) * LANES) for p in range(H_M // 2)]
    mk = [[mk_ref[0, :, lanes].astype(BF16)] for lanes in lanes_m]
    mv = [[mv_ref[0, :, lanes].astype(BF16)] for lanes in lanes_m]
    blocks = [(slice(r, r + MEM_ROWS), p) for r in range(0, QBLK, MEM_ROWS) for p in range(len(lanes_m))]
    outs = _head_pairs_attention([qm_ref[0, rows, lanes_m[p]] for rows, p in blocks],
                                 [mk[p] for _, p in blocks], [mv[p] for _, p in blocks], lambda n, e, _: None)
    for (rows, p), o in zip(blocks, outs):
        m_buf[rows, lanes_m[p]] = (o * szm_ref[0, rows, lanes_m[p]].astype(F32)).astype(BF16)

    pairs = [slice(p * LANES, (p + 1) * LANES) for p in range(H_A // 2)]
    for c in range(QBLK // CHUNK):
        rows = slice(c * CHUNK, (c + 1) * CHUNK)
        win = slice(c * CHUNK, c * CHUNK + WINDOW)
        outs = _head_pairs_attention(
            [q_ref[0, rows, lanes] for lanes in pairs],
            [[k_buf[win, lanes]] for lanes in pairs], [[v_buf[win, lanes]] for lanes in pairs],
            lambda p, e, _: bias_buf[c, 2 * p + e, :, :WINDOW])
        for lanes, o in zip(pairs, outs):
            a_buf[rows, lanes] = (o * sz_ref[0, rows, lanes].astype(F32)).astype(BF16)

    u = (ga_ref[0].astype(F32) * _dot(a_buf[...], wa_ref[...])
         + gb_ref[0].astype(F32) * _dot(b_ref[0], wb_ref[...])
         + gm_ref[0].astype(F32) * _dot(m_buf[...], wm_ref[...]))
    y_ref[0] = x_ref[0] + _dot(u.astype(BF16), wo_ref[...])


def _attn_back(q, k, v, sz, bias, qm, mk, mv, szm, x, out_b, ga, gb, gm, wts):
    b, t, _ = q.shape
    nb = BAND // QBLK
    n = t // QBLK
    assert BAND % QBLK == 0 and t % QBLK == 0 and QBLK % MEM_ROWS == 0
    rows = lambda w, d=0: pl.BlockSpec((1, QBLK, w), lambda bi, i: (bi, jnp.maximum(i - d, 0), 0))
    kv_specs = [rows(W_A, nb - j) for j in range(nb + 1)]
    mem = pl.BlockSpec((1, N_MEM, W_M), lambda bi, i: (bi, 0, 0))
    return pl.pallas_call(
        _attn_back_kernel,
        grid=(b, n),
        in_specs=[rows(W_A)] + kv_specs + kv_specs + [rows(W_A), _full((H_A, CHUNK, BIAS_W)),
                  rows(W_M), mem, mem, rows(W_M),
                  rows(D_MODEL), rows(W_BV), rows(D_MODEL), rows(D_MODEL), rows(D_MODEL),
                  _full((W_A, D_MODEL)), _full((W_BV, D_MODEL)), _full((W_M, D_MODEL)),
                  _full((D_MODEL, D_MODEL))],
        out_specs=rows(D_MODEL),
        out_shape=jax.ShapeDtypeStruct((b, t, D_MODEL), F32),
        scratch_shapes=[pltpu.VMEM((KSPAN, W_A), BF16), pltpu.VMEM((KSPAN, W_A), BF16),
                        pltpu.VMEM((QBLK // CHUNK, H_A, CHUNK, BIAS_W), F32),
                        pltpu.VMEM((QBLK, W_A), BF16), pltpu.VMEM((QBLK, W_M), BF16)],
        compiler_params=_params("parallel", "arbitrary"),
        name="attn_back",
    )(q, *([k] * (nb + 1)), *([v] * (nb + 1)), sz, bias, qm, mk, mv, szm, x, out_b, ga, gb, gm,
      wts["w_up_a"], wts["w_up_b"], wts["w_up_m"], wts["w_out"])


STEP_STREAMS = 4


def _attn_step_kernel(q_ref, kn_ref, vn_ref, kp_ref, vp_ref, sz_ref, bias_ref, o_ref):
    s_len = q_ref.shape[1]
    for b in range(STEP_STREAMS):
        scores, values = [], []
        for h in range(H_A):
            lanes = slice(h * HEAD_DIM, (h + 1) * HEAD_DIM)
            q = q_ref[b, :, lanes]
            scores.append([(_dot(q, kp_ref[b, h].astype(BF16)), bias_ref[h, :s_len, :BAND]),
                           (_dot_nt(q, kn_ref[b, :, lanes]), bias_ref[h, :s_len, BAND:BAND + s_len])])
            values.append([functools.partial(_dot_nt, b=vp_ref[b, h].astype(BF16)), vn_ref[b, :, lanes]])
        o = jnp.concatenate(_softmax_pv_staged(scores, values), axis=-1)
        o_ref[b] = (o * sz_ref[b].astype(F32)).astype(BF16)


def _attn_step(q, k_new, v_new, k_past_t, v_past_t, sz, bias):
    b, s_len, _ = q.shape
    assert k_past_t.shape[1:] == (H_A, HEAD_DIM, BAND) and s_len <= CHUNK and b % STEP_STREAMS == 0
    new = pl.BlockSpec((STEP_STREAMS, s_len, W_A), lambda i: (i, 0, 0))
    past = pl.BlockSpec((STEP_STREAMS, H_A, HEAD_DIM, BAND), lambda i: (i, 0, 0, 0))
    return pl.pallas_call(
        _attn_step_kernel,
        grid=(b // STEP_STREAMS,),
        in_specs=[new, new, new, past, past, new, _full((H_A, CHUNK, BIAS_W))],
        out_specs=new,
        out_shape=jax.ShapeDtypeStruct((b, s_len, W_A), BF16),
        compiler_params=_params("parallel"),
        name="attn_step",
    )(q, k_new, v_new, k_past_t, v_past_t, sz, bias)


def _mem_kv_kernel(mem_ref, gmem_ref, w_ref, gkm_ref, mk_ref, mv_ref):
    h = _rms_rows(mem_ref[0], gmem_ref[...]).astype(BF16)
    kv = _dot(h, w_ref[...])
    mk_ref[0] = _head_rms(kv[:, :W_M], gkm_ref[...], HEAD_DIM)
    mv_ref[0] = kv[:, W_M:]


def _mem_kv(mem, wts):
    b = mem.shape[0]
    out = pl.BlockSpec((1, N_MEM, W_M), lambda i: (i, 0, 0))
    return pl.pallas_call(
        _mem_kv_kernel,
        grid=(b,),
        in_specs=[pl.BlockSpec((1, N_MEM, D_MODEL), lambda i: (i, 0, 0)), _full((1, D_MODEL)),
                  _full((D_MODEL, 2 * W_M)), _full((1, W_M))],
        out_specs=[out, out],
        out_shape=[jax.ShapeDtypeStruct((b, N_MEM, W_M), F32)] * 2,
        compiler_params=_params("parallel"),
        name="mem_kv",
    )(mem, wts["g_mem"], wts["w_mem_kv"], wts["g_km"])


MEM_ROWS = 128
MEM_STREAMS = 8


def _attn_mem_step_kernel(q_ref, mk_ref, mv_ref, sz_ref, o_ref):
    for b in range(q_ref.shape[0]):
        scores, values = [], []
        for h in range(H_M):
            lanes = slice(h * HEAD_DIM, (h + 1) * HEAD_DIM)
            scores.append([(_dot(q_ref[b, :, lanes], mk_ref[b, h].astype(BF16)), None)])
            values.append([functools.partial(_dot_nt, b=mv_ref[b, h].astype(BF16))])
        o = jnp.concatenate(_softmax_pv_staged(scores, values), axis=-1)
        o_ref[b] = (o * sz_ref[b].astype(F32)).astype(BF16)


def _attn_mem_step(q, mk_t, mv_t, sz):
    b, t, _ = q.shape
    nb = min(b, MEM_STREAMS)
    assert b % nb == 0 and mk_t.shape[1:] == (H_M, HEAD_DIM, N_MEM)
    rows = pl.BlockSpec((nb, t, W_M), lambda i: (i, 0, 0))
    mem = pl.BlockSpec((nb, H_M, HEAD_DIM, N_MEM), lambda i: (i, 0, 0, 0))
    return pl.pallas_call(
        _attn_mem_step_kernel,
        grid=(b // nb,),
        in_specs=[rows, mem, mem, rows],
        out_specs=rows,
        out_shape=jax.ShapeDtypeStruct((b, t, W_M), BF16),
        compiler_params=_params("parallel"),
        name="attn_mem_step",
    )(q, mk_t, mv_t, sz)


def _gla_intra(items, group, between=lambda: None):
    n = len(items)
    qs, ks, vs, v16s = ([it[j] for it in items] for j in range(4))
    c = qs[0].shape[0]
    las = [it[4] * LOG2E for it in items]
    r_cc = lax.broadcasted_iota(jnp.int32, (c, c), 0)
    c_cc = lax.broadcasted_iota(jnp.int32, (c, c), 1)
    causal = c_cc <= r_cc
    if group < c:
        causal = causal & (r_cc // group == c_cc // group)
    causal16 = causal.astype(BF16)
    cums = [_dot_exact_lhs(causal16, la) for la in las]
    cumxs = [cum - la for cum, la in zip(cums, las)]
    r_ck = lax.broadcasted_iota(jnp.int32, (c, W_BK), 0)

    head_of_k = lax.broadcasted_iota(jnp.int32, (W_BK, W_BV), 0) // DK_B
    head_of_v = lax.broadcasted_iota(jnp.int32, (W_BK, W_BV), 1) // DV_B
    spread = (head_of_k == head_of_v).astype(BF16)
    os = [_dot((q * k).astype(BF16), spread) * v for q, k, v in zip(qs, ks, vs)]

    def shifted(x, j):
        return pltpu.roll(x.reshape(c // SUBLANES, SUBLANES, W_BK), j % SUBLANES, 1).reshape(c, W_BK)

    def block_edges(cum, cumx, half):
        if half >= SUBLANES:
            nb = c // half
            first = jnp.broadcast_to(cumx.reshape(nb, half, W_BK)[:, 0:1, :], (nb, half, W_BK))
            last = jnp.broadcast_to(cum.reshape(nb, half, W_BK)[:, half - 1:half, :], (nb, half, W_BK))
            return first.reshape(c, W_BK), last.reshape(c, W_BK)
        r_in = r_ck % half
        first, last = cumx, cum
        for j in range(1, half):
            first = jnp.where(r_in == j, shifted(cumx, j), first)
            last = jnp.where(r_in == half - 1 - j, shifted(cum, -j), last)
        return first, last

    head_of_row = lax.broadcasted_iota(jnp.int32, (W_BK, c), 0) // DK_B
    head_rows = [jnp.where(head_of_row == h, 1.0, 0.0).astype(BF16) for h in range(H_B)]
    r_att = lax.broadcasted_iota(jnp.int32, (c, H_B * c), 0)
    s_att = lax.broadcasted_iota(jnp.int32, (c, H_B * c), 1) % c
    atts = [jnp.zeros((c, H_B * c), F32) for _ in range(n)]
    half = 1
    while half < group:
        edges = [block_edges(cum, cumx, half) for cum, cumx in zip(cums, cumxs)]
        odd = (r_ck // half) % 2 == 1
        qts = [jnp.where(odd, q * jnp.exp2(cum - first), 0.0).astype(BF16)
               for q, cum, (first, _) in zip(qs, cums, edges)]
        kts = [jnp.where(odd, 0.0, k * jnp.exp2(last - cum)).T.astype(BF16)
               for k, cum, (_, last) in zip(ks, cums, edges)]
        prods = [_dot(qt, jnp.concatenate([kt * m for m in head_rows], axis=1)) for qt, kt in zip(qts, kts)]
        if 2 * half < c:
            same = r_att // (2 * half) == s_att // (2 * half)
            prods = [jnp.where(same, a, 0.0) for a in prods]
        atts = [att + a for att, a in zip(atts, prods)]
        between()
        half *= 2
    if group > 1:
        att16s = [att.astype(BF16) for att in atts]
        os = [o + jnp.concatenate([_dot(att16[:, h * c:(h + 1) * c], v16[:, h * DV_B:(h + 1) * DV_B])
                                   for h in range(H_B)], axis=-1)
              for o, att16, v16 in zip(os, att16s, v16s)]
    return list(zip(os, cums))


def _gla_state_step(q, k, v16, cum, s_old):
    c = q.shape[0]
    q_in = (q * jnp.exp2(cum)).astype(BF16)
    k_out_t = (k * jnp.exp2(cum[c - 1:c, :] - cum)).T.astype(BF16)
    keep = jnp.exp2(cum.T[:, c - 1:c])
    kd = lambda h: slice(h * DK_B, (h + 1) * DK_B)
    vd = lambda h: slice(h * DV_B, (h + 1) * DV_B)
    o = jnp.concatenate([_dot(q_in[:, kd(h)], s_old[kd(h)].astype(BF16)) for h in range(H_B)], axis=-1)
    s_new = jnp.concatenate([s_old[kd(h)] * keep[kd(h)] + _dot(k_out_t[kd(h)], v16[:, vd(h)])
                             for h in range(H_B)], axis=0)
    return o, s_new


def _gla_finish(o, g_ref, sz):
    outs = [_rms_rows(o[:, h * DV_B:(h + 1) * DV_B], g_ref[:, h * DV_B:(h + 1) * DV_B]) for h in range(H_B)]
    return (jnp.concatenate(outs, axis=-1) * sz.astype(F32)).astype(BF16)


def _gla_step_kernel(q_ref, k_ref, v_ref, la_ref, sz_ref, g_ref, s0_ref, o_ref, s_ref, *, t):
    c = GLA_C
    n = c // t
    q, k, la = q_ref[...].astype(F32), k_ref[...].astype(F32), la_ref[...]
    v16 = v_ref[...]
    (o, cum), = _gla_intra([(q, k, v16.astype(F32), v16, la)], t)

    q_in = (q * jnp.exp2(cum)).astype(BF16)
    cum_t, k_t = cum.T, k.T
    stream_of_col = lax.broadcasted_iota(jnp.int32, (W_BK, c), 1) // t
    inter = []
    for s in range(n):
        rows = slice(s * t, (s + 1) * t)
        last = cum_t[:, (s + 1) * t - 1:(s + 1) * t]
        k_out_t = (k_t * jnp.exp2(jnp.where(stream_of_col == s, last - cum_t, NEG))).astype(BF16)
        keep = jnp.exp2(last)
        o_s = []
        for h in range(H_B):
            ks = slice(h * DK_B, (h + 1) * DK_B)
            vs = slice(h * DV_B, (h + 1) * DV_B)
            s_old = s0_ref[s, h]
            o_s.append(_dot(q_in[rows, ks], s_old.astype(BF16)))
            s_ref[s, h] = s_old * keep[ks] + _dot(k_out_t[ks], v16[:, vs])
        inter.append(jnp.concatenate(o_s, axis=-1))
    o_ref[...] = _gla_finish(o + jnp.concatenate(inter, axis=0), g_ref, sz_ref[...])


def _gla_step(q, k, v, la, sz, g, s0):
    b, t, _ = q.shape
    n = GLA_C // t
    assert GLA_C % t == 0 and t % (2 * GLA_BASE) == 0 and b % n == 0
    flat = lambda z: z.reshape(b * t, z.shape[-1])
    blk = lambda w: pl.BlockSpec((GLA_C, w), lambda i: (i, 0))
    state = pl.BlockSpec((n, H_B, DK_B, DV_B), lambda i: (i, 0, 0, 0))
    o, s_new = pl.pallas_call(
        functools.partial(_gla_step_kernel, t=t),
        grid=(b // n,),
        in_specs=[blk(W_BK), blk(W_BK), blk(W_BV), blk(W_BK), blk(W_BV),
                  pl.BlockSpec((1, W_BV), lambda i: (0, 0)), state],
        out_specs=[blk(W_BV), state],
        out_shape=[jax.ShapeDtypeStruct((b * t, W_BV), BF16),
                   jax.ShapeDtypeStruct((b, H_B, DK_B, DV_B), F32)],
        compiler_params=_params("parallel"),
        name="gla_step",
    )(flat(q), flat(k), flat(v), flat(la), flat(sz), g, s0)
    return o.reshape(b, t, W_BV), s_new


def _prep_weights(l, norm_in, w_in, g_qa, g_ka, w_gate2, b_gate, g_gla_out, g_mem, w_mem_kv, g_qm, g_km,
                  w_up_a, w_up_b, w_up_m, w_out):
    tile = lambda gain, n: jnp.tile(gain, n).reshape(1, -1)
    return {
        "norm_in": norm_in[l].reshape(1, D_MODEL),
        "w_in_t": w_in[l].T.astype(BF16),
        "w_gate2": w_gate2[l].astype(BF16),
        "b_gate": b_gate[l].reshape(1, W_BK),
        "g_qa": tile(g_qa[l], H_A), "g_ka": tile(g_ka[l], H_A), "g_qm": tile(g_qm[l], H_M),
        "g_km": tile(g_km[l], H_M), "g_gla": tile(g_gla_out[l], H_B),
        "g_mem": g_mem[l].reshape(1, D_MODEL),
        "w_mem_kv": w_mem_kv[l].astype(BF16),
        "w_up_a": w_up_a[l].astype(BF16), "w_up_b": w_up_b[l].astype(BF16),
        "w_up_m": w_up_m[l].astype(BF16), "w_out": w_out[l].astype(BF16),
    }


def _layer_long(x, wts, bias, mk, mv, s0):
    b, t, _ = x.shape
    keep = min(BAND, t)
    r3 = lambda z: z.reshape(b, t, z.shape[-1])
    heads = lambda z: jnp.transpose(z.reshape(b, H_A, HEAD_DIM, keep), (0, 3, 1, 2))
    (qa, ka, va, ka16, va16, sza, out_b, s_new, qm, szm, ga, gb, gm) = _front(
        x.reshape(b * t, D_MODEL), wts, keep, t, keep, s0)
    y = _attn_back(r3(qa), r3(ka16), r3(va16), r3(sza), bias, r3(qm), mk, mv, r3(szm), x, r3(out_b),
                   r3(ga), r3(gb), r3(gm), wts)
    return y, heads(ka), heads(va), s_new


def _layer_short(x, wts, attend_a, attend_m, s0, tm):
    b, t, _ = x.shape
    keep = min(BAND, t)
    x2d = x.reshape(b * t, D_MODEL)
    r3 = lambda z: z.reshape(b, t, z.shape[-1])
    flat = lambda z: z.reshape(b * t, z.shape[-1])
    heads = lambda z: z.reshape(b, keep, H_A, HEAD_DIM)
    (qa, ka, va, ka16, va16, sza, qb, kb, vb, la, szb, qm, szm, ga, gb, gm) = _front(x2d, wts, tm, t, keep)
    out_b, s_new = _gla_step(r3(qb), r3(kb), r3(vb), r3(la), r3(szb), wts["g_gla"], s0)
    out_a = attend_a(r3(qa), r3(ka16), r3(va16), r3(sza))
    out_m = attend_m(r3(qm), r3(szm))
    y = _back(x2d, flat(out_a), flat(out_b), flat(out_m), ga, gb, gm, wts, tm)
    return y.reshape(b, t, D_MODEL), heads(ka), heads(va), s_new


def kernel(x_prompt, x_sample, mem_prompt, cache_a_k, cache_a_v, state_gla, cache_mem_k, cache_mem_v,
           norm_in, w_in, g_qa, g_ka, rel_bias, w_gate2, b_gate, g_gla_out, g_mem, w_mem_kv, g_qm, g_km,
           w_up_a, w_up_b, w_up_m, w_out):
    depth = w_in.shape[0]
    xp, xs = x_prompt, x_sample
    bp, tp, _ = xp.shape
    bs, ts, _ = xs.shape
    akp, avp, sgp, mkp, mvp, aks, avs, sgs = [], [], [], [], [], [], [], []
    for l in range(depth):
        wts = _prep_weights(l, norm_in, w_in, g_qa, g_ka, w_gate2, b_gate, g_gla_out, g_mem, w_mem_kv,
                            g_qm, g_km, w_up_a, w_up_b, w_up_m, w_out)
        bias = _bias_tile(rel_bias[l])
        mk, mv = _mem_kv(mem_prompt, wts)
        s0 = jnp.zeros((bp, H_B, DK_B, DV_B), state_gla.dtype)
        xp, ka, va, sp = _layer_long(xp, wts, bias, mk, mv, s0)
        akp.append(ka)
        avp.append(va)
        sgp.append(sp)
        mkp.append(mk.reshape(bp, N_MEM, H_M, HEAD_DIM))
        mvp.append(mv.reshape(bp, N_MEM, H_M, HEAD_DIM))
        rows_last = lambda z: jnp.transpose(z, (0, 2, 3, 1))
        past_k, past_v = rows_last(cache_a_k[l]), rows_last(cache_a_v[l])
        mem_k, mem_v = rows_last(cache_mem_k[l]), rows_last(cache_mem_v[l])
        attend_s = lambda q, k, v, sz: _attn_step(q, k, v, past_k, past_v, sz, bias)
        attend_ms = lambda q, sz: _attn_mem_step(q, mem_k, mem_v, sz)
        xs, ka_s, va_s, ss = _layer_short(xs, wts, attend_s, attend_ms, state_gla[l], 512)
        aks.append(ka_s)
        avs.append(va_s)
        sgs.append(ss)
    return (xp, xs, jnp.stack(akp), jnp.stack(avp), jnp.stack(sgp), jnp.stack(mkp), jnp.stack(mvp),
            jnp.stack(aks), jnp.stack(avs), jnp.stack(sgs))
```

```python
import functools

import jax
import jax.numpy as jnp
from jax import lax
from jax.experimental import pallas as pl
from jax.experimental.pallas import tpu as pltpu

F32 = jnp.float32
BF16 = jnp.bfloat16

D_MODEL = 1024
CHUNK = 64
LEFT_CHUNKS = 8
HEAD_DIM = 64
H_A = 8
W_A = H_A * HEAD_DIM
REL_CLIP = 128
H_B = 4
DK_B = 64
DV_B = 128
W_BK = H_B * DK_B
W_BV = H_B * DV_B
GATE_RANK = 16
GATE_TAU = 16.0
N_MEM = 256
H_M = 4
W_M = H_M * HEAD_DIM
EPS = 1e-6
IN_SIZES = (W_A, W_A, W_A, W_A, W_BK, W_BK, W_BV, GATE_RANK, W_BV, W_M, W_M, D_MODEL, D_MODEL, D_MODEL)

LANES = 128
SUBLANES = 8
VMEM_LIMIT = 56 * 1024 * 1024
NEG = -1e30
LOG2E = 1.4426950408889634

BAND = LEFT_CHUNKS * CHUNK
QBLK = 512
KSPAN = BAND + QBLK
WINDOW = BAND + CHUNK
BIAS_W = BAND + 2 * CHUNK
GLA_C = 128
GLA_BASE = SUBLANES


def _dot(a, b):
    return jnp.dot(a, b, preferred_element_type=F32)


def _dot_nt(a, b):
    return lax.dot_general(a, b, (((1,), (1,)), ((), ())), preferred_element_type=F32)


def _split3(x):
    hi = x.astype(BF16)
    r = x - hi.astype(F32)
    mid = r.astype(BF16)
    lo = (r - mid.astype(F32)).astype(BF16)
    return hi, mid, lo


def _dot_exact_lhs(a01, x):
    hi, mid, lo = _split3(x)
    return _dot(a01, hi) + _dot(a01, mid) + _dot(a01, lo)


def _sigmoid_of_twice(p):
    return 0.5 * jnp.tanh(p) + 0.5


def _silu_of_twice(p):
    return p * jnp.tanh(p) + p


def _log_sigmoid(z):
    return jnp.minimum(z, 0.0) - jnp.log1p(jnp.exp(-jnp.abs(z)))


def _rms_rows(x, g):
    ms = jnp.mean(x * x, axis=-1, keepdims=True)
    return x * lax.rsqrt(ms + EPS) * g


def _head_mean_matrix(width, head):
    r = lax.broadcasted_iota(jnp.int32, (width, width), 0) // head
    c = lax.broadcasted_iota(jnp.int32, (width, width), 1) // head
    return jnp.where(r == c, 1.0 / head, 0.0).astype(BF16)


def _head_rms(x, g, head):
    avg = _head_mean_matrix(x.shape[-1], head)
    ms = _dot((x * x).astype(BF16), avg)
    return x * lax.rsqrt(ms + EPS) * g


def _full(shape):
    nd = len(shape)
    return pl.BlockSpec(shape, lambda *_: (0,) * nd, pipeline_mode=pl.Buffered(1))


def _params(*sem):
    return pltpu.CompilerParams(dimension_semantics=sem, vmem_limit_bytes=VMEM_LIMIT)


_IN_OFF = tuple(int(sum(IN_SIZES[:i])) for i in range(len(IN_SIZES) + 1))
_HALVED_GROUPS = (3, 8, 10, 11, 12, 13)
D_IN = _IN_OFF[-1]


def _front_kernel(x_ref, nin_ref, w_ref, wg2_ref, bg_ref, gqa_ref, gka_ref, gqm_ref, *rest, per_stream=None):
    fused = per_stream is not None
    if fused:
        (ggla_ref, s0_ref, qa_o, kat_o, vat_o, ka16_o, va16_o, sza_o, outb_o, sfin_o, qm_o, szm_o,
         ga_o, gb_o, gm_o, s_scr, ka_o, va_o) = rest
    else:
        (qa_o, ka_o, va_o, ka16_o, va16_o, sza_o, qb_o, kb_o, vb_o, la_o, szb_o, qm_o, szm_o,
         ga_o, gb_o, gm_o) = rest
    if fused:
        @pl.when(pl.program_id(0) % per_stream == 0)
        def _():
            s_scr[...] = s0_ref[pl.program_id(0) // per_stream].reshape(W_BK, DV_B)

    h = _rms_rows(x_ref[...], nin_ref[...]).astype(BF16)

    def proj(i):
        return _dot_nt(h, w_ref[_IN_OFF[i]:_IN_OFF[i + 1], :])

    qb = (proj(4) * (DK_B ** -0.5)).astype(BF16)
    kb = proj(5).astype(BF16)
    vb = proj(6).astype(BF16)
    z = _dot(proj(7).astype(BF16), wg2_ref[...]) + bg_ref[...]
    la = _log_sigmoid(z) * (1.0 / GATE_TAU)
    szb = _silu_of_twice(proj(8)).astype(BF16)

    scale = HEAD_DIM ** -0.5 * LOG2E

    def emit_qa():
        qa_o[...] = (_head_rms(proj(0), gqa_ref[...], HEAD_DIM) * scale).astype(BF16)

    def emit_ka():
        ka = _head_rms(proj(1), gka_ref[...], HEAD_DIM)
        ka_o[...] = ka
        ka16_o[...] = ka.astype(BF16)

    def emit_va():
        va = proj(2)
        va_o[...] = va
        va16_o[...] = va.astype(BF16)

    def emit_sza():
        sza_o[...] = _silu_of_twice(proj(3)).astype(BF16)

    def emit_qm():
        qm_o[...] = (_head_rms(proj(9), gqm_ref[...], HEAD_DIM) * scale).astype(BF16)

    def emit_szm():
        szm_o[...] = _silu_of_twice(proj(10)).astype(BF16)

    def emit_gate(i, ref):
        ref[...] = _sigmoid_of_twice(proj(i)).astype(BF16)

    pending = [emit_qa, emit_ka, emit_va, emit_sza, emit_qm, emit_szm,
               functools.partial(emit_gate, 11, ga_o), functools.partial(emit_gate, 12, gb_o),
               functools.partial(emit_gate, 13, gm_o)]

    def emit_next():
        if pending:
            pending.pop(0)()

    if fused:
        chunks = [slice(r, r + GLA_C) for r in range(0, x_ref.shape[0], GLA_C)]
        items = [(qb[rows].astype(F32), kb[rows].astype(F32), vb[rows].astype(F32), vb[rows], la[rows])
                 for rows in chunks]
        state = s_scr[...]
        outs = []
        for rows, item, (o, cum) in zip(chunks, items, _gla_intra(items, GLA_C, emit_next)):
            inter, state = _gla_state_step(item[0], item[1], item[3], cum, state)
            outs.append(_gla_finish(o + inter, ggla_ref, szb[rows]))
            emit_next()
        s_scr[...] = state
        outb_o[...] = jnp.concatenate(outs, axis=0)
    else:
        qb_o[...] = qb
        kb_o[...] = kb
        vb_o[...] = vb
        la_o[...] = la
        szb_o[...] = szb
    while pending:
        emit_next()

    if fused:
        @pl.when(pl.program_id(0) % per_stream == per_stream - 1)
        def _():
            sfin_o[0] = s_scr[...].reshape(H_B, DK_B, DV_B)
            kat_o[0] = ka_o[...].T
            vat_o[0] = va_o[...].T


def _front(x2d, wts, tm, t, keep, s0=None):
    n = x2d.shape[0]
    tm = min(tm, n)
    row = lambda w: pl.BlockSpec((tm, w), lambda i: (i, 0))
    out = lambda rows_, w, d, spec: (jax.ShapeDtypeStruct((rows_, w), d), spec)
    tail = [out(n, W_M, BF16, row(W_M)), out(n, W_M, BF16, row(W_M))] + [out(n, D_MODEL, BF16, row(D_MODEL))] * 3
    in_specs = [row(D_MODEL), _full((1, D_MODEL)), _full((D_IN, D_MODEL)),
                _full((GATE_RANK, W_BK)), _full((1, W_BK)), _full((1, W_A)), _full((1, W_A)), _full((1, W_M))]
    args = [x2d, wts["norm_in"], wts["w_in_t"], wts["w_gate2"], wts["b_gate"], wts["g_qa"], wts["g_ka"], wts["g_qm"]]
    if s0 is None:
        assert keep == t
        kv = [out(n, W_A, F32, row(W_A))] * 2
        mid = [out(n, W_BK, BF16, row(W_BK)), out(n, W_BK, BF16, row(W_BK)), out(n, W_BV, BF16, row(W_BV)),
               out(n, W_BK, F32, row(W_BK)), out(n, W_BV, BF16, row(W_BV))]
        body, scratch = _front_kernel, []
    else:
        assert t % tm == 0 and tm % GLA_C == 0 and keep == tm
        streams, per_stream = n // t, t // tm
        state = (streams, H_B, DK_B, DV_B)
        kv = [(jax.ShapeDtypeStruct((streams, W_A, keep), F32),
               pl.BlockSpec((1, W_A, keep), lambda i: (i // per_stream, 0, 0)))] * 2
        mid = [out(n, W_BV, BF16, row(W_BV)),
               (jax.ShapeDtypeStruct(state, F32), pl.BlockSpec((1,) + state[1:], lambda i: (i // per_stream, 0, 0, 0)))]
        in_specs += [_full((1, W_BV)), _full(state)]
        args += [wts["g_gla"], s0]
        body = functools.partial(_front_kernel, per_stream=per_stream)
        scratch = [pltpu.VMEM((W_BK, DV_B), F32), pltpu.VMEM((tm, W_A), F32), pltpu.VMEM((tm, W_A), F32)]
    outs = [out(n, W_A, BF16, row(W_A))] + kv + [out(n, W_A, BF16, row(W_A))] * 3 + mid + tail
    return pl.pallas_call(
        body,
        grid=(n // tm,),
        in_specs=in_specs,
        out_specs=[spec for _, spec in outs],
        out_shape=[shape for shape, _ in outs],
        scratch_shapes=scratch,
        compiler_params=_params("arbitrary"),
        name="front",
    )(*args)


def _back_kernel(x_ref, a_ref, b_ref, m_ref, ga_ref, gb_ref, gm_ref, wa_ref, wb_ref, wm_ref, wo_ref, y_ref):
    u = (ga_ref[...].astype(F32) * _dot(a_ref[...], wa_ref[...])
         + gb_ref[...].astype(F32) * _dot(b_ref[...], wb_ref[...])
         + gm_ref[...].astype(F32) * _dot(m_ref[...], wm_ref[...]))
    y_ref[...] = x_ref[...] + _dot(u.astype(BF16), wo_ref[...])


def _back(x2d, a, b, m, ga, gb, gm, wts, tm):
    n = x2d.shape[0]
    tm = min(tm, n)
    row = lambda w: pl.BlockSpec((tm, w), lambda i: (i, 0))
    return pl.pallas_call(
        _back_kernel,
        grid=(n // tm,),
        in_specs=[row(D_MODEL), row(W_A), row(W_BV), row(W_M), row(D_MODEL), row(D_MODEL), row(D_MODEL),
                  _full((W_A, D_MODEL)), _full((W_BV, D_MODEL)), _full((W_M, D_MODEL)),
                  _full((D_MODEL, D_MODEL))],
        out_specs=row(D_MODEL),
        out_shape=jax.ShapeDtypeStruct((n, D_MODEL), F32),
        compiler_params=_params("parallel"),
        name="back",
    )(x2d, a, b, m, ga, gb, gm, wts["w_up_a"], wts["w_up_b"], wts["w_up_m"], wts["w_out"])


def _bias_kernel(e_ref, tile_ref):
    e = jnp.broadcast_to(e_ref[0] * LOG2E, (CHUNK, 2 * BAND))
    tile_ref[0] = pltpu.roll(e, 0, 1, stride=1, stride_axis=0)[:, :BIAS_W]


def _bias_tile(rel_bias):
    j = jnp.arange(2 * BAND)
    dist = jnp.where(j < BIAS_W, BAND - j, REL_CLIP)
    e = rel_bias[:, jnp.clip(dist, -REL_CLIP, REL_CLIP) + REL_CLIP].reshape(H_A, 1, 2 * BAND)
    return pl.pallas_call(
        _bias_kernel,
        grid=(H_A,),
        in_specs=[pl.BlockSpec((1, 1, 2 * BAND), lambda h: (h, 0, 0))],
        out_specs=pl.BlockSpec((1, CHUNK, BIAS_W), lambda h: (h, 0, 0)),
        out_shape=jax.ShapeDtypeStruct((H_A, CHUNK, BIAS_W), F32),
        compiler_params=_params("parallel"),
        name="bias_tile",
    )(e)


def _softmax_pv_staged(scores, values):
    biased = [[s if b is None else s + b for s, b in s_list] for s_list in scores]
    tops = [functools.reduce(jnp.maximum, [jnp.max(s, axis=-1, keepdims=True) for s in s_list])
            for s_list in biased]
    probs = [[jnp.exp2(s - m) for s in s_list] for s_list, m in zip(biased, tops)]
    sums = [functools.reduce(jnp.add, [jnp.sum(x, axis=-1, keepdims=True) for x in p_list]) for p_list in probs]
    return [functools.reduce(jnp.add, [(v(x.astype(BF16)) if callable(v) else _dot(x.astype(BF16), v))
                                       for x, v in zip(p_list, v_list)]) / l
            for p_list, v_list, l in zip(probs, values, sums)]


def _head_pairs_attention(q_pairs, k_lists, v_lists, bias_fn):
    m_rows = q_pairs[0].shape[0]
    lane = lax.broadcasted_iota(jnp.int32, (m_rows, LANES), 1)
    first = lane < HEAD_DIM
    scores = []
    for p, qp in enumerate(q_pairs):
        zero = jnp.zeros_like(qp)
        q2 = jnp.concatenate([jnp.where(first, qp, zero), jnp.where(first, zero, qp)], axis=0)
        s_list = []
        for i, k in enumerate(k_lists[p]):
            b0, b1 = bias_fn(p, 0, i), bias_fn(p, 1, i)
            s_list.append((_dot_nt(q2, k), None if b0 is None else jnp.concatenate([b0, b1], axis=0)))
        scores.append(s_list)
    outs = _softmax_pv_staged(scores, v_lists)
    return [jnp.where(first, o[:m_rows], o[m_rows:]) for o in outs]


def _attn_back_kernel(q_ref, *refs):
    nk = BAND // QBLK + 1
    k_refs, v_refs = refs[:nk], refs[nk:2 * nk]
    (sz_ref, bias_ref, qm_ref, mk_ref, mv_ref, szm_ref, x_ref, b_ref, ga_ref, gb_ref, gm_ref,
     wa_ref, wb_ref, wm_ref, wo_ref, y_ref, k_buf, v_buf, bias_buf, a_buf, m_buf) = refs[2 * nk:]
    i = pl.program_id(1)
    for j, (k_ref, v_ref) in enumerate(zip(k_refs, v_refs)):
        k_buf[j * QBLK:(j + 1) * QBLK, :] = k_ref[0]
        v_buf[j * QBLK:(j + 1) * QBLK, :] = v_ref[0]

    @pl.when(i <= BAND // QBLK)
    def _():
        col = lax.broadcasted_iota(jnp.int32, (1, BIAS_W), 1)
        for c in range(QBLK // CHUNK):
            before_start = jnp.where(col + c * CHUNK < (BAND // QBLK - i) * QBLK, NEG, 0.0)
            for h in range(H_A):
                bias_buf[c, h] = bias_ref[h] + before_start

    lanes_m = [slice(p * LANES, (p + 1) * LANES) for p in range(H_M // 2)]
    mk = [[mk_ref[0, :, lanes].astype(BF16)] for lanes in lanes_m]
    mv = [[mv_ref[0, :, lanes].astype(BF16)] for lanes in lanes_m]
    blocks = [(slice(r, r + MEM_ROWS), p) for r in range(0, QBLK, MEM_ROWS) for p in range(len(lanes_m))]
    outs = _head_pairs_attention([qm_ref[0, rows, lanes_m[p]] for rows, p in blocks],
                                 [mk[p] for _, p in blocks], [mv[p] for _, p in blocks], lambda n, e, _: None)
    for (rows, p), o in zip(blocks, outs):
        m_buf[rows, lanes_m[p]] = (o * szm_ref[0, rows, lanes_m[p]].astype(F32)).astype(BF16)

    pairs = [slice(p * LANES, (p + 1) * LANES) for p in range(H_A // 2)]
    for c in range(QBLK // CHUNK):
        rows = slice(c * CHUNK, (c + 1) * CHUNK)
        win = slice(c * CHUNK, c * CHUNK + WINDOW)
        outs = _head_pairs_attention(
            [q_ref[0, rows, lanes] for lanes in pairs],
            [[k_buf[win, lanes]] for lanes in pairs], [[v_buf[win, lanes]] for lanes in pairs],
            lambda p, e, _: bias_buf[c, 2 * p + e, :, :WINDOW])
        for lanes, o in zip(pairs, outs):
            a_buf[rows, lanes] = (o * sz_ref[0, rows, lanes].astype(F32)).astype(BF16)

    u = (ga_ref[0].astype(F32) * _dot(a_buf[...], wa_ref[...])
         + gb_ref[0].astype(F32) * _dot(b_ref[0], wb_ref[...])
         + gm_ref[0].astype(F32) * _dot(m_buf[...], wm_ref[...]))
    y_ref[0] = x_ref[0] + _dot(u.astype(BF16), wo_ref[...])


def _attn_back(q, k, v, sz, bias, qm, mk, mv, szm, x, out_b, ga, gb, gm, wts):
    b, t, _ = q.shape
    nb = BAND // QBLK
    n = t // QBLK
    assert BAND % QBLK == 0 and t % QBLK == 0 and QBLK % MEM_ROWS == 0
    rows = lambda w, d=0: pl.BlockSpec((1, QBLK, w), lambda bi, i: (bi, jnp.maximum(i - d, 0), 0))
    kv_specs = [rows(W_A, nb - j) for j in range(nb + 1)]
    mem = pl.BlockSpec((1, N_MEM, W_M), lambda bi, i: (bi, 0, 0))
    return pl.pallas_call(
        _attn_back_kernel,
        grid=(b, n),
        in_specs=[rows(W_A)] + kv_specs + kv_specs + [rows(W_A), _full((H_A, CHUNK, BIAS_W)),
                  rows(W_M), mem, mem, rows(W_M),
                  rows(D_MODEL), rows(W_BV), rows(D_MODEL), rows(D_MODEL), rows(D_MODEL),
                  _full((W_A, D_MODEL)), _full((W_BV, D_MODEL)), _full((W_M, D_MODEL)),
                  _full((D_MODEL, D_MODEL))],
        out_specs=rows(D_MODEL),
        out_shape=jax.ShapeDtypeStruct((b, t, D_MODEL), F32),
        scratch_shapes=[pltpu.VMEM((KSPAN, W_A), BF16), pltpu.VMEM((KSPAN, W_A), BF16),
                        pltpu.VMEM((QBLK // CHUNK, H_A, CHUNK, BIAS_W), F32),
                        pltpu.VMEM((QBLK, W_A), BF16), pltpu.VMEM((QBLK, W_M), BF16)],
        compiler_params=_params("parallel", "arbitrary"),
        name="attn_back",
    )(q, *([k] * (nb + 1)), *([v] * (nb + 1)), sz, bias, qm, mk, mv, szm, x, out_b, ga, gb, gm,
      wts["w_up_a"], wts["w_up_b"], wts["w_up_m"], wts["w_out"])


STEP_STREAMS = 4


def _attn_step_kernel(q_ref, kn_ref, vn_ref, kp_ref, vp_ref, sz_ref, bias_ref, o_ref):
    s_len = q_ref.shape[1]
    for b in range(STEP_STREAMS):
        scores, values = [], []
        for h in range(H_A):
            lanes = slice(h * HEAD_DIM, (h + 1) * HEAD_DIM)
            q = q_ref[b, :, lanes]
            scores.append([(_dot(q, kp_ref[b, h].astype(BF16)), bias_ref[h, :s_len, :BAND]),
                           (_dot_nt(q, kn_ref[b, :, lanes]), bias_ref[h, :s_len, BAND:BAND + s_len])])
            values.append([functools.partial(_dot_nt, b=vp_ref[b, h].astype(BF16)), vn_ref[b, :, lanes]])
        o = jnp.concatenate(_softmax_pv_staged(scores, values), axis=-1)
        o_ref[b] = (o * sz_ref[b].astype(F32)).astype(BF16)


def _attn_step(q, k_new, v_new, k_past_t, v_past_t, sz, bias):
    b, s_len, _ = q.shape
    assert k_past_t.shape[1:] == (H_A, HEAD_DIM, BAND) and s_len <= CHUNK and b % STEP_STREAMS == 0
    new = pl.BlockSpec((STEP_STREAMS, s_len, W_A), lambda i: (i, 0, 0))
    past = pl.BlockSpec((STEP_STREAMS, H_A, HEAD_DIM, BAND), lambda i: (i, 0, 0, 0))
    return pl.pallas_call(
        _attn_step_kernel,
        grid=(b // STEP_STREAMS,),
        in_specs=[new, new, new, past, past, new, _full((H_A, CHUNK, BIAS_W))],
        out_specs=new,
        out_shape=jax.ShapeDtypeStruct((b, s_len, W_A), BF16),
        compiler_params=_params("parallel"),
        name="attn_step",
    )(q, k_new, v_new, k_past_t, v_past_t, sz, bias)


def _mem_kv_kernel(mem_ref, gmem_ref, w_ref, gkm_ref, mk_ref, mv_ref):
    h = _rms_rows(mem_ref[0], gmem_ref[...]).astype(BF16)
    kv = _dot(h, w_ref[...])
    mk_ref[0] = _head_rms(kv[:, :W_M], gkm_ref[...], HEAD_DIM)
    mv_ref[0] = kv[:, W_M:]


def _mem_kv(mem, wts):
    b = mem.shape[0]
    out = pl.BlockSpec((1, N_MEM, W_M), lambda i: (i, 0, 0))
    return pl.pallas_call(
        _mem_kv_kernel,
        grid=(b,),
        in_specs=[pl.BlockSpec((1, N_MEM, D_MODEL), lambda i: (i, 0, 0)), _full((1, D_MODEL)),
                  _full((D_MODEL, 2 * W_M)), _full((1, W_M))],
        out_specs=[out, out],
        out_shape=[jax.ShapeDtypeStruct((b, N_MEM, W_M), F32)] * 2,
        compiler_params=_params("parallel"),
        name="mem_kv",
    )(mem, wts["g_mem"], wts["w_mem_kv"], wts["g_km"])


MEM_ROWS = 128
MEM_STREAMS = 8


def _attn_mem_step_kernel(q_ref, mk_ref, mv_ref, sz_ref, o_ref):
    for b in range(q_ref.shape[0]):
        scores, values = [], []
        for h in range(H_M):
            lanes = slice(h * HEAD_DIM, (h + 1) * HEAD_DIM)
            scores.append([(_dot(q_ref[b, :, lanes], mk_ref[b, h].astype(BF16)), None)])
            values.append([functools.partial(_dot_nt, b=mv_ref[b, h].astype(BF16))])
        o = jnp.concatenate(_softmax_pv_staged(scores, values), axis=-1)
        o_ref[b] = (o * sz_ref[b].astype(F32)).astype(BF16)


def _attn_mem_step(q, mk_t, mv_t, sz):
    b, t, _ = q.shape
    nb = min(b, MEM_STREAMS)
    assert b % nb == 0 and mk_t.shape[1:] == (H_M, HEAD_DIM, N_MEM)
    rows = pl.BlockSpec((nb, t, W_M), lambda i: (i, 0, 0))
    mem = pl.BlockSpec((nb, H_M, HEAD_DIM, N_MEM), lambda i: (i, 0, 0, 0))
    return pl.pallas_call(
        _attn_mem_step_kernel,
        grid=(b // nb,),
        in_specs=[rows, mem, mem, rows],
        out_specs=rows,
        out_shape=jax.ShapeDtypeStruct((b, t, W_M), BF16),
        compiler_params=_params("parallel"),
        name="attn_mem_step",
    )(q, mk_t, mv_t, sz)


def _gla_intra(items, group, between=lambda: None):
    n = len(items)
    qs, ks, vs, v16s = ([it[j] for it in items] for j in range(4))
    c = qs[0].shape[0]
    las = [it[4] * LOG2E for it in items]
    r_cc = lax.broadcasted_iota(jnp.int32, (c, c), 0)
    c_cc = lax.broadcasted_iota(jnp.int32, (c, c), 1)
    causal = c_cc <= r_cc
    if group < c:
        causal = causal & (r_cc // group == c_cc // group)
    causal16 = causal.astype(BF16)
    cums = [_dot_exact_lhs(causal16, la) for la in las]
    cumxs = [cum - la for cum, la in zip(cums, las)]
    r_ck = lax.broadcasted_iota(jnp.int32, (c, W_BK), 0)

    head_of_k = lax.broadcasted_iota(jnp.int32, (W_BK, W_BV), 0) // DK_B
    head_of_v = lax.broadcasted_iota(jnp.int32, (W_BK, W_BV), 1) // DV_B
    spread = (head_of_k == head_of_v).astype(BF16)
    os = [_dot((q * k).astype(BF16), spread) * v for q, k, v in zip(qs, ks, vs)]

    def shifted(x, j):
        return pltpu.roll(x.reshape(c // SUBLANES, SUBLANES, W_BK), j % SUBLANES, 1).reshape(c, W_BK)

    def block_edges(cum, cumx, half):
        if half >= SUBLANES:
            nb = c // half
            first = jnp.broadcast_to(cumx.reshape(nb, half, W_BK)[:, 0:1, :], (nb, half, W_BK))
            last = jnp.broadcast_to(cum.reshape(nb, half, W_BK)[:, half - 1:half, :], (nb, half, W_BK))
            return first.reshape(c, W_BK), last.reshape(c, W_BK)
        r_in = r_ck % half
        first, last = cumx, cum
        for j in range(1, half):
            first = jnp.where(r_in == j, shifted(cumx, j), first)
            last = jnp.where(r_in == half - 1 - j, shifted(cum, -j), last)
        return first, last

    head_of_row = lax.broadcasted_iota(jnp.int32, (W_BK, c), 0) // DK_B
    head_rows = [jnp.where(head_of_row == h, 1.0, 0.0).astype(BF16) for h in range(H_B)]
    r_att = lax.broadcasted_iota(jnp.int32, (c, H_B * c), 0)
    s_att = lax.broadcasted_iota(jnp.int32, (c, H_B * c), 1) % c
    atts = [jnp.zeros((c, H_B * c), F32) for _ in range(n)]
    half = 1
    while half < group:
        edges = [block_edges(cum, cumx, half) for cum, cumx in zip(cums, cumxs)]
        odd = (r_ck // half) % 2 == 1
        qts = [jnp.where(odd, q * jnp.exp2(cum - first), 0.0).astype(BF16)
               for q, cum, (first, _) in zip(qs, cums, edges)]
        kts = [jnp.where(odd, 0.0, k * jnp.exp2(last - cum)).T.astype(BF16)
               for k, cum, (_, last) in zip(ks, cums, edges)]
        prods = [_dot(qt, jnp.concatenate([kt * m for m in head_rows], axis=1)) for qt, kt in zip(qts, kts)]
        if 2 * half < c:
            same = r_att // (2 * half) == s_att // (2 * half)
            prods = [jnp.where(same, a, 0.0) for a in prods]
        atts = [att + a for att, a in zip(atts, prods)]
        between()
        half *= 2
    if group > 1:
        att16s = [att.astype(BF16) for att in atts]
        os = [o + jnp.concatenate([_dot(att16[:, h * c:(h + 1) * c], v16[:, h * DV_B:(h + 1) * DV_B])
                                   for h in range(H_B)], axis=-1)
              for o, att16, v16 in zip(os, att16s, v16s)]
    return list(zip(os, cums))


def _gla_state_step(q, k, v16, cum, s_old):
    c = q.shape[0]
    q_in = (q * jnp.exp2(cum)).astype(BF16)
    k_out_t = (k * jnp.exp2(cum[c - 1:c, :] - cum)).T.astype(BF16)
    keep = jnp.exp2(cum.T[:, c - 1:c])
    kd = lambda h: slice(h * DK_B, (h + 1) * DK_B)
    vd = lambda h: slice(h * DV_B, (h + 1) * DV_B)
    o = jnp.concatenate([_dot(q_in[:, kd(h)], s_old[kd(h)].astype(BF16)) for h in range(H_B)], axis=-1)
    s_new = jnp.concatenate([s_old[kd(h)] * keep[kd(h)] + _dot(k_out_t[kd(h)], v16[:, vd(h)])
                             for h in range(H_B)], axis=0)
    return o, s_new


def _gla_finish(o, g_ref, sz):
    outs = [_rms_rows(o[:, h * DV_B:(h + 1) * DV_B], g_ref[:, h * DV_B:(h + 1) * DV_B]) for h in range(H_B)]
    return (jnp.concatenate(outs, axis=-1) * sz.astype(F32)).astype(BF16)


def _gla_step_kernel(q_ref, k_ref, v_ref, la_ref, sz_ref, g_ref, s0_ref, o_ref, s_ref, *, t):
    c = GLA_C
    n = c // t
    q, k, la = q_ref[...].astype(F32), k_ref[...].astype(F32), la_ref[...]
    v16 = v_ref[...]
    (o, cum), = _gla_intra([(q, k, v16.astype(F32), v16, la)], t)

    q_in = (q * jnp.exp2(cum)).astype(BF16)
    cum_t, k_t = cum.T, k.T
    stream_of_col = lax.broadcasted_iota(jnp.int32, (W_BK, c), 1) // t
    inter = []
    for s in range(n):
        rows = slice(s * t, (s + 1) * t)
        last = cum_t[:, (s + 1) * t - 1:(s + 1) * t]
        k_out_t = (k_t * jnp.exp2(jnp.where(stream_of_col == s, last - cum_t, NEG))).astype(BF16)
        keep = jnp.exp2(last)
        o_s = []
        for h in range(H_B):
            ks = slice(h * DK_B, (h + 1) * DK_B)
            vs = slice(h * DV_B, (h + 1) * DV_B)
            s_old = s0_ref[s, h]
            o_s.append(_dot(q_in[rows, ks], s_old.astype(BF16)))
            s_ref[s, h] = s_old * keep[ks] + _dot(k_out_t[ks], v16[:, vs])
        inter.append(jnp.concatenate(o_s, axis=-1))
    o_ref[...] = _gla_finish(o + jnp.concatenate(inter, axis=0), g_ref, sz_ref[...])


def _gla_step(q, k, v, la, sz, g, s0):
    b, t, _ = q.shape
    n = GLA_C // t
    assert GLA_C % t == 0 and t % (2 * GLA_BASE) == 0 and b % n == 0
    flat = lambda z: z.reshape(b * t, z.shape[-1])
    blk = lambda w: pl.BlockSpec((GLA_C, w), lambda i: (i, 0))
    state = pl.BlockSpec((n, H_B, DK_B, DV_B), lambda i: (i, 0, 0, 0))
    o, s_new = pl.pallas_call(
        functools.partial(_gla_step_kernel, t=t),
        grid=(b // n,),
        in_specs=[blk(W_BK), blk(W_BK), blk(W_BV), blk(W_BK), blk(W_BV),
                  pl.BlockSpec((1, W_BV), lambda i: (0, 0)), state],
        out_specs=[blk(W_BV), state],
        out_shape=[jax.ShapeDtypeStruct((b * t, W_BV), BF16),
                   jax.ShapeDtypeStruct((b, H_B, DK_B, DV_B), F32)],
        compiler_params=_params("parallel"),
        name="gla_step",
    )(flat(q), flat(k), flat(v), flat(la), flat(sz), g, s0)
    return o.reshape(b, t, W_BV), s_new


def _prep_weights(l, norm_in, w_in, g_qa, g_ka, w_gate2, b_gate, g_gla_out, g_mem, w_mem_kv, g_qm, g_km,
                  w_up_a, w_up_b, w_up_m, w_out):
    tile = lambda gain, n: jnp.tile(gain, n).reshape(1, -1)
    halved = jnp.concatenate([jnp.full((size,), 0.5 if i in _HALVED_GROUPS else 1.0, F32)
                              for i, size in enumerate(IN_SIZES)])
    return {
        "norm_in": norm_in[l].reshape(1, D_MODEL),
        "w_in_t": (w_in[l].T * halved[:, None]).astype(BF16),
        "w_gate2": w_gate2[l].astype(BF16),
        "b_gate": b_gate[l].reshape(1, W_BK),
        "g_qa": tile(g_qa[l], H_A), "g_ka": tile(g_ka[l], H_A), "g_qm": tile(g_qm[l], H_M),
        "g_km": tile(g_km[l], H_M), "g_gla": tile(g_gla_out[l], H_B),
        "g_mem": g_mem[l].reshape(1, D_MODEL),
        "w_mem_kv": w_mem_kv[l].astype(BF16),
        "w_up_a": w_up_a[l].astype(BF16), "w_up_b": w_up_b[l].astype(BF16),
        "w_up_m": w_up_m[l].astype(BF16), "w_out": w_out[l].astype(BF16),
    }


def _layer_long(x, wts, bias, mk, mv, s0):
    b, t, _ = x.shape
    keep = min(BAND, t)
    r3 = lambda z: z.reshape(b, t, z.shape[-1])
    heads = lambda z: jnp.transpose(z.reshape(b, H_A, HEAD_DIM, keep), (0, 3, 1, 2))
    (qa, ka, va, ka16, va16, sza, out_b, s_new, qm, szm, ga, gb, gm) = _front(
        x.reshape(b * t, D_MODEL), wts, keep, t, keep, s0)
    y = _attn_back(r3(qa), r3(ka16), r3(va16), r3(sza), bias, r3(qm), mk, mv, r3(szm), x, r3(out_b),
                   r3(ga), r3(gb), r3(gm), wts)
    return y, heads(ka), heads(va), s_new


def _layer_short(x, wts, attend_a, attend_m, s0, tm):
    b, t, _ = x.shape
    keep = min(BAND, t)
    x2d = x.reshape(b * t, D_MODEL)
    r3 = lambda z: z.reshape(b, t, z.shape[-1])
    flat = lambda z: z.reshape(b * t, z.shape[-1])
    heads = lambda z: z.reshape(b, keep, H_A, HEAD_DIM)
    (qa, ka, va, ka16, va16, sza, qb, kb, vb, la, szb, qm, szm, ga, gb, gm) = _front(x2d, wts, tm, t, keep)
    out_b, s_new = _gla_step(r3(qb), r3(kb), r3(vb), r3(la), r3(szb), wts["g_gla"], s0)
    out_a = attend_a(r3(qa), r3(ka16), r3(va16), r3(sza))
    out_m = attend_m(r3(qm), r3(szm))
    y = _back(x2d, flat(out_a), flat(out_b), flat(out_m), ga, gb, gm, wts, tm)
    return y.reshape(b, t, D_MODEL), heads(ka), heads(va), s_new


def kernel(x_prompt, x_sample, mem_prompt, cache_a_k, cache_a_v, state_gla, cache_mem_k, cache_mem_v,
           norm_in, w_in, g_qa, g_ka, rel_bias, w_gate2, b_gate, g_gla_out, g_mem, w_mem_kv, g_qm, g_km,
           w_up_a, w_up_b, w_up_m, w_out):
    depth = w_in.shape[0]
    xp, xs = x_prompt, x_sample
    bp, tp, _ = xp.shape
    bs, ts, _ = xs.shape
    akp, avp, sgp, mkp, mvp, aks, avs, sgs = [], [], [], [], [], [], [], []
    for l in range(depth):
        wts = _prep_weights(l, norm_in, w_in, g_qa, g_ka, w_gate2, b_gate, g_gla_out, g_mem, w_mem_kv,
                            g_qm, g_km, w_up_a, w_up_b, w_up_m, w_out)
        bias = _bias_tile(rel_bias[l])
        mk, mv = _mem_kv(mem_prompt, wts)
        s0 = jnp.zeros((bp, H_B, DK_B, DV_B), state_gla.dtype)
        xp, ka, va, sp = _layer_long(xp, wts, bias, mk, mv, s0)
        akp.append(ka)
        avp.append(va)
        sgp.append(sp)
        mkp.append(mk.reshape(bp, N_MEM, H_M, HEAD_DIM))
        mvp.append(mv.reshape(bp, N_MEM, H_M, HEAD_DIM))
        rows_last = lambda z: jnp.transpose(z, (0, 2, 3, 1))
        past_k, past_v = rows_last(cache_a_k[l]), rows_last(cache_a_v[l])
        mem_k, mem_v = rows_last(cache_mem_k[l]), rows_last(cache_mem_v[l])
        attend_s = lambda q, k, v, sz: _attn_step(q, k, v, past_k, past_v, sz, bias)
        attend_ms = lambda q, sz: _attn_mem_step(q, mem_k, mem_v, sz)
        xs, ka_s, va_s, ss = _layer_short(xs, wts, attend_s, attend_ms, state_gla[l], 512)
        aks.append(ka_s)
        avs.append(va_s)
        sgs.append(ss)
    return (xp, xs, jnp.stack(akp), jnp.stack(avp), jnp.stack(sgp), jnp.stack(mkp), jnp.stack(mvp),
            jnp.stack(aks), jnp.stack(avs), jnp.stack(sgs))
```

```python
import functools

import jax
import jax.numpy as jnp
from jax import lax
from jax.experimental import pallas as pl
from jax.experimental.pallas import tpu as pltpu

F32 = jnp.float32
BF16 = jnp.bfloat16

D_MODEL = 1024
CHUNK = 64
LEFT_CHUNKS = 8
HEAD_DIM = 64
H_A = 8
W_A = H_A * HEAD_DIM
REL_CLIP = 128
H_B = 4
DK_B = 64
DV_B = 128
W_BK = H_B * DK_B
W_BV = H_B * DV_B
GATE_RANK = 16
GATE_TAU = 16.0
N_MEM = 256
H_M = 4
W_M = H_M * HEAD_DIM
EPS = 1e-6
IN_SIZES = (W_A, W_A, W_A, W_A, W_BK, W_BK, W_BV, GATE_RANK, W_BV, W_M, W_M, D_MODEL, D_MODEL, D_MODEL)

LANES = 128
SUBLANES = 8
VMEM_LIMIT = 56 * 1024 * 1024
NEG = -1e30
LOG2E = 1.4426950408889634

BAND = LEFT_CHUNKS * CHUNK
QBLK = 512
KSPAN = BAND + QBLK
WINDOW = BAND + CHUNK
BIAS_W = BAND + 2 * CHUNK
GLA_C = 128


def _dot(a, b):
    return jnp.dot(a, b, preferred_element_type=F32)


def _dot_nt(a, b):
    return lax.dot_general(a, b, (((1,), (1,)), ((), ())), preferred_element_type=F32)


def _split3(x):
    hi = x.astype(BF16)
    r = x - hi.astype(F32)
    mid = r.astype(BF16)
    lo = (r - mid.astype(F32)).astype(BF16)
    return hi, mid, lo


def _dot_exact_lhs(a01, x):
    hi, mid, lo = _split3(x)
    return _dot(a01, hi) + _dot(a01, mid) + _dot(a01, lo)


def _sigmoid_of_twice(p):
    return 0.5 * jnp.tanh(p) + 0.5


def _silu_of_twice(p):
    return p * jnp.tanh(p) + p


def _log_sigmoid(z):
    return jnp.minimum(z, 0.0) - jnp.log1p(jnp.exp(-jnp.abs(z)))


def _rms_rows(x, g):
    ms = jnp.mean(x * x, axis=-1, keepdims=True)
    return x * lax.rsqrt(ms + EPS) * g


def _head_mean_matrix(width, head):
    r = lax.broadcasted_iota(jnp.int32, (width, width), 0) // head
    c = lax.broadcasted_iota(jnp.int32, (width, width), 1) // head
    return jnp.where(r == c, 1.0 / head, 0.0).astype(BF16)


def _head_rms(x, g, avg):
    ms = _dot((x * x).astype(BF16), avg)
    return x * lax.rsqrt(ms + EPS) * g


def _full(shape):
    nd = len(shape)
    return pl.BlockSpec(shape, lambda *_: (0,) * nd, pipeline_mode=pl.Buffered(1))


def _params(*sem):
    return pltpu.CompilerParams(dimension_semantics=sem, vmem_limit_bytes=VMEM_LIMIT)


_IN_OFF = tuple(int(sum(IN_SIZES[:i])) for i in range(len(IN_SIZES) + 1))
_HALVED_GROUPS = (3, 8, 10, 11, 12, 13)
D_IN = _IN_OFF[-1]


def _front_kernel(x_ref, nin_ref, w_ref, wg2_ref, bg_ref, gqa_ref, gka_ref, gqm_ref, *rest, per_stream=None):
    fused = per_stream is not None
    if fused:
        (ggla_ref, s0_ref, qa_o, kat_o, vat_o, ka16_o, va16_o, sza_o, outb_o, sfin_o, qm_o, szm_o,
         ga_o, gb_o, gm_o, s_scr, ka_o, va_o) = rest
    else:
        (qa_o, ka_o, va_o, ka16_o, va16_o, sza_o, qb_o, kb_o, vb_o, la_o, szb_o, qm_o, szm_o,
         ga_o, gb_o, gm_o) = rest
    if fused:
        @pl.when(pl.program_id(0) % per_stream == 0)
        def _():
            s_scr[...] = s0_ref[pl.program_id(0) // per_stream].reshape(W_BK, DV_B)

    h = _rms_rows(x_ref[...], nin_ref[...]).astype(BF16)

    def proj(i):
        return _dot_nt(h, w_ref[_IN_OFF[i]:_IN_OFF[i + 1], :])

    qb = (proj(4) * (DK_B ** -0.5)).astype(BF16)
    kb = proj(5).astype(BF16)
    vb = proj(6).astype(BF16)
    z = _dot(proj(7).astype(BF16), wg2_ref[...]) + bg_ref[...]
    la = _log_sigmoid(z) * (1.0 / GATE_TAU)
    szb = _silu_of_twice(proj(8)).astype(BF16)

    scale = HEAD_DIM ** -0.5 * LOG2E
    avg_a = _head_mean_matrix(W_A, HEAD_DIM)

    def emit_qa():
        qa_o[...] = (_head_rms(proj(0), gqa_ref[...], avg_a) * scale).astype(BF16)

    def emit_ka():
        ka = _head_rms(proj(1), gka_ref[...], avg_a)
        ka_o[...] = ka
        ka16_o[...] = ka.astype(BF16)

    def emit_va():
        va = proj(2)
        va_o[...] = va
        va16_o[...] = va.astype(BF16)

    def emit_sza():
        sza_o[...] = _silu_of_twice(proj(3)).astype(BF16)

    def emit_qm():
        qm_o[...] = (_head_rms(proj(9), gqm_ref[...], _head_mean_matrix(W_M, HEAD_DIM)) * scale).astype(BF16)

    def emit_szm():
        szm_o[...] = _silu_of_twice(proj(10)).astype(BF16)

    def emit_gate(i, ref):
        ref[...] = _sigmoid_of_twice(proj(i)).astype(BF16)

    pending = [emit_qa, emit_ka, emit_va, emit_sza, emit_qm, emit_szm,
               functools.partial(emit_gate, 11, ga_o), functools.partial(emit_gate, 12, gb_o),
               functools.partial(emit_gate, 13, gm_o)]

    def emit_next():
        if pending:
            pending.pop(0)()

    if fused:
        chunks = [slice(r, r + GLA_C) for r in range(0, x_ref.shape[0], GLA_C)]
        items = [(qb[rows].astype(F32), kb[rows].astype(F32), vb[rows].astype(F32), vb[rows], la[rows])
                 for rows in chunks]
        state = s_scr[...]
        outs = []
        for rows, item, (o, cum) in zip(chunks, items, _gla_intra(items, GLA_C, emit_next)):
            inter, state = _gla_state_step(item[0], item[1], item[3], cum, state)
            outs.append(_gla_finish(o + inter, ggla_ref, szb[rows]))
            emit_next()
        s_scr[...] = state
        outb_o[...] = jnp.concatenate(outs, axis=0)
    else:
        qb_o[...] = qb
        kb_o[...] = kb
        vb_o[...] = vb
        la_o[...] = la
        szb_o[...] = szb
    while pending:
        emit_next()

    if fused:
        @pl.when(pl.program_id(0) % per_stream == per_stream - 1)
        def _():
            sfin_o[0] = s_scr[...].reshape(H_B, DK_B, DV_B)
            kat_o[0] = ka_o[...].T
            vat_o[0] = va_o[...].T


def _front(x2d, wts, tm, t, keep, s0=None):
    n = x2d.shape[0]
    tm = min(tm, n)
    row = lambda w: pl.BlockSpec((tm, w), lambda i: (i, 0))
    out = lambda rows_, w, d, spec: (jax.ShapeDtypeStruct((rows_, w), d), spec)
    tail = [out(n, W_M, BF16, row(W_M)), out(n, W_M, BF16, row(W_M))] + [out(n, D_MODEL, BF16, row(D_MODEL))] * 3
    in_specs = [row(D_MODEL), _full((1, D_MODEL)), _full((D_IN, D_MODEL)),
                _full((GATE_RANK, W_BK)), _full((1, W_BK)), _full((1, W_A)), _full((1, W_A)), _full((1, W_M))]
    args = [x2d, wts["norm_in"], wts["w_in_t"], wts["w_gate2"], wts["b_gate"], wts["g_qa"], wts["g_ka"], wts["g_qm"]]
    if s0 is None:
        assert keep == t
        kv = [out(n, W_A, F32, row(W_A))] * 2
        mid = [out(n, W_BK, BF16, row(W_BK)), out(n, W_BK, BF16, row(W_BK)), out(n, W_BV, BF16, row(W_BV)),
               out(n, W_BK, F32, row(W_BK)), out(n, W_BV, BF16, row(W_BV))]
        body, scratch = _front_kernel, []
    else:
        assert t % tm == 0 and tm % GLA_C == 0 and keep == tm
        streams, per_stream = n // t, t // tm
        state = (streams, H_B, DK_B, DV_B)
        kv = [(jax.ShapeDtypeStruct((streams, W_A, keep), F32),
               pl.BlockSpec((1, W_A, keep), lambda i: (i // per_stream, 0, 0)))] * 2
        mid = [out(n, W_BV, BF16, row(W_BV)),
               (jax.ShapeDtypeStruct(state, F32), pl.BlockSpec((1,) + state[1:], lambda i: (i // per_stream, 0, 0, 0)))]
        in_specs += [_full((1, W_BV)), _full(state)]
        args += [wts["g_gla"], s0]
        body = functools.partial(_front_kernel, per_stream=per_stream)
        scratch = [pltpu.VMEM((W_BK, DV_B), F32), pltpu.VMEM((tm, W_A), F32), pltpu.VMEM((tm, W_A), F32)]
    outs = [out(n, W_A, BF16, row(W_A))] + kv + [out(n, W_A, BF16, row(W_A))] * 3 + mid + tail
    return pl.pallas_call(
        body,
        grid=(n // tm,),
        in_specs=in_specs,
        out_specs=[spec for _, spec in outs],
        out_shape=[shape for shape, _ in outs],
        scratch_shapes=scratch,
        compiler_params=_params("arbitrary"),
        name="front",
    )(*args)


def _back_kernel(x_ref, a_ref, b_ref, m_ref, ga_ref, gb_ref, gm_ref, wa_ref, wb_ref, wm_ref, wo_ref, y_ref):
    u = (ga_ref[...].astype(F32) * _dot(a_ref[...], wa_ref[...])
         + gb_ref[...].astype(F32) * _dot(b_ref[...], wb_ref[...])
         + gm_ref[...].astype(F32) * _dot(m_ref[...], wm_ref[...]))
    y_ref[...] = x_ref[...] + _dot(u.astype(BF16), wo_ref[...])


def _back(x2d, a, b, m, ga, gb, gm, wts, tm):
    n = x2d.shape[0]
    tm = min(tm, n)
    row = lambda w: pl.BlockSpec((tm, w), lambda i: (i, 0))
    return pl.pallas_call(
        _back_kernel,
        grid=(n // tm,),
        in_specs=[row(D_MODEL), row(W_A), row(W_BV), row(W_M), row(D_MODEL), row(D_MODEL), row(D_MODEL),
                  _full((W_A, D_MODEL)), _full((W_BV, D_MODEL)), _full((W_M, D_MODEL)),
                  _full((D_MODEL, D_MODEL))],
        out_specs=row(D_MODEL),
        out_shape=jax.ShapeDtypeStruct((n, D_MODEL), F32),
        compiler_params=_params("parallel"),
        name="back",
    )(x2d, a, b, m, ga, gb, gm, wts["w_up_a"], wts["w_up_b"], wts["w_up_m"], wts["w_out"])


def _bias_kernel(e_ref, tile_ref):
    e = jnp.broadcast_to(e_ref[0] * LOG2E, (CHUNK, 2 * BAND))
    tile_ref[0] = pltpu.roll(e, 0, 1, stride=1, stride_axis=0)[:, :BIAS_W]


def _bias_tile(rel_bias):
    j = jnp.arange(2 * BAND)
    dist = jnp.where(j < BIAS_W, BAND - j, REL_CLIP)
    e = rel_bias[:, jnp.clip(dist, -REL_CLIP, REL_CLIP) + REL_CLIP].reshape(H_A, 1, 2 * BAND)
    return pl.pallas_call(
        _bias_kernel,
        grid=(H_A,),
        in_specs=[pl.BlockSpec((1, 1, 2 * BAND), lambda h: (h, 0, 0))],
        out_specs=pl.BlockSpec((1, CHUNK, BIAS_W), lambda h: (h, 0, 0)),
        out_shape=jax.ShapeDtypeStruct((H_A, CHUNK, BIAS_W), F32),
        compiler_params=_params("parallel"),
        name="bias_tile",
    )(e)


def _softmax_pv_staged(scores, values):
    biased = [[s if b is None else s + b for s, b in s_list] for s_list in scores]
    tops = [functools.reduce(jnp.maximum, [jnp.max(s, axis=-1, keepdims=True) for s in s_list])
            for s_list in biased]
    probs = [[jnp.exp2(s - m) for s in s_list] for s_list, m in zip(biased, tops)]
    sums = [functools.reduce(jnp.add, [jnp.sum(x, axis=-1, keepdims=True) for x in p_list]) for p_list in probs]
    return [functools.reduce(jnp.add, [(v(x.astype(BF16)) if callable(v) else _dot(x.astype(BF16), v))
                                       for x, v in zip(p_list, v_list)]) / l
            for p_list, v_list, l in zip(probs, values, sums)]


def _head_pairs_attention(q_pairs, k_lists, v_lists, bias_fn):
    m_rows = q_pairs[0].shape[0]
    lane = lax.broadcasted_iota(jnp.int32, (m_rows, LANES), 1)
    first = lane < HEAD_DIM
    scores = []
    for p, qp in enumerate(q_pairs):
        zero = jnp.zeros_like(qp)
        q2 = jnp.concatenate([jnp.where(first, qp, zero), jnp.where(first, zero, qp)], axis=0)
        s_list = []
        for i, k in enumerate(k_lists[p]):
            b0, b1 = bias_fn(p, 0, i), bias_fn(p, 1, i)
            s_list.append((_dot_nt(q2, k), None if b0 is None else jnp.concatenate([b0, b1], axis=0)))
        scores.append(s_list)
    outs = _softmax_pv_staged(scores, v_lists)
    return [jnp.where(first, o[:m_rows], o[m_rows:]) for o in outs]


def _attn_back_kernel(q_ref, *refs):
    nk = BAND // QBLK + 1
    k_refs, v_refs = refs[:nk], refs[nk:2 * nk]
    (sz_ref, bias_ref, qm_ref, mk_ref, mv_ref, szm_ref, x_ref, b_ref, ga_ref, gb_ref, gm_ref,
     wa_ref, wb_ref, wm_ref, wo_ref, y_ref, k_buf, v_buf, bias_buf, a_buf, m_buf) = refs[2 * nk:]
    i = pl.program_id(1)
    for j, (k_ref, v_ref) in enumerate(zip(k_refs, v_refs)):
        k_buf[j * QBLK:(j + 1) * QBLK, :] = k_ref[0]
        v_buf[j * QBLK:(j + 1) * QBLK, :] = v_ref[0]

    @pl.when(i <= BAND // QBLK)
    def _():
        col = lax.broadcasted_iota(jnp.int32, (1, BIAS_W), 1)
        for c in range(QBLK // CHUNK):
            before_start = jnp.where(col + c * CHUNK < (BAND // QBLK - i) * QBLK, NEG, 0.0)
            for h in range(H_A):
                bias_buf[c, h] = bias_ref[h] + before_start

    lanes_m = [slice(p * LANES, (p + 1) * LANES) for p in range(H_M // 2)]
    mk = [[mk_ref[0, :, lanes].astype(BF16)] for lanes in lanes_m]
    mv = [[mv_ref[0, :, lanes].astype(BF16)] for lanes in lanes_m]
    blocks = [(slice(r, r + MEM_ROWS), p) for r in range(0, QBLK, MEM_ROWS) for p in range(len(lanes_m))]
    outs = _head_pairs_attention([qm_ref[0, rows, lanes_m[p]] for rows, p in blocks],
                                 [mk[p] for _, p in blocks], [mv[p] for _, p in blocks], lambda n, e, _: None)
    for (rows, p), o in zip(blocks, outs):
        m_buf[rows, lanes_m[p]] = (o * szm_ref[0, rows, lanes_m[p]].astype(F32)).astype(BF16)

    pairs = [slice(p * LANES, (p + 1) * LANES) for p in range(H_A // 2)]
    for c in range(QBLK // CHUNK):
        rows = slice(c * CHUNK, (c + 1) * CHUNK)
        win = slice(c * CHUNK, c * CHUNK + WINDOW)
        outs = _head_pairs_attention(
            [q_ref[0, rows, lanes] for lanes in pairs],
            [[k_buf[win, lanes]] for lanes in pairs], [[v_buf[win, lanes]] for lanes in pairs],
            lambda p, e, _: bias_buf[c, 2 * p + e, :, :WINDOW])
        for lanes, o in zip(pairs, outs):
            a_buf[rows, lanes] = (o * sz_ref[0, rows, lanes].astype(F32)).astype(BF16)

    u = (ga_ref[0].astype(F32) * _dot(a_buf[...], wa_ref[...])
         + gb_ref[0].astype(F32) * _dot(b_ref[0], wb_ref[...])
         + gm_ref[0].astype(F32) * _dot(m_buf[...], wm_ref[...]))
    y_ref[0] = x_ref[0] + _dot(u.astype(BF16), wo_ref[...])


def _attn_back(q, k, v, sz, bias, qm, mk, mv, szm, x, out_b, ga, gb, gm, wts):
    b, t, _ = q.shape
    nb = BAND // QBLK
    n = t // QBLK
    assert BAND % QBLK == 0 and t % QBLK == 0 and QBLK % MEM_ROWS == 0
    rows = lambda w, d=0: pl.BlockSpec((1, QBLK, w), lambda bi, i: (bi, jnp.maximum(i - d, 0), 0))
    kv_specs = [rows(W_A, nb - j) for j in range(nb + 1)]
    mem = pl.BlockSpec((1, N_MEM, W_M), lambda bi, i: (bi, 0, 0))
    return pl.pallas_call(
        _attn_back_kernel,
        grid=(b, n),
        in_specs=[rows(W_A)] + kv_specs + kv_specs + [rows(W_A), _full((H_A, CHUNK, BIAS_W)),
                  rows(W_M), mem, mem, rows(W_M),
                  rows(D_MODEL), rows(W_BV), rows(D_MODEL), rows(D_MODEL), rows(D_MODEL),
                  _full((W_A, D_MODEL)), _full((W_BV, D_MODEL)), _full((W_M, D_MODEL)),
                  _full((D_MODEL, D_MODEL))],
        out_specs=rows(D_MODEL),
        out_shape=jax.ShapeDtypeStruct((b, t, D_MODEL), F32),
        scratch_shapes=[pltpu.VMEM((KSPAN, W_A), BF16), pltpu.VMEM((KSPAN, W_A), BF16),
                        pltpu.VMEM((QBLK // CHUNK, H_A, CHUNK, BIAS_W), F32),
                        pltpu.VMEM((QBLK, W_A), BF16), pltpu.VMEM((QBLK, W_M), BF16)],
        compiler_params=_params("parallel", "arbitrary"),
        name="attn_back",
    )(q, *([k] * (nb + 1)), *([v] * (nb + 1)), sz, bias, qm, mk, mv, szm, x, out_b, ga, gb, gm,
      wts["w_up_a"], wts["w_up_b"], wts["w_up_m"], wts["w_out"])


STEP_STREAMS = 8


def _attn_step_kernel(q_ref, kn_ref, vn_ref, kp_ref, vp_ref, sz_ref, bias_ref, o_ref):
    s_len = q_ref.shape[1]
    for b in range(STEP_STREAMS):
        scores, values = [], []
        for h in range(H_A):
            lanes = slice(h * HEAD_DIM, (h + 1) * HEAD_DIM)
            q = q_ref[b, :, lanes]
            scores.append([(_dot(q, kp_ref[b, h].astype(BF16)), bias_ref[h, :s_len, :BAND]),
                           (_dot_nt(q, kn_ref[b, :, lanes]), bias_ref[h, :s_len, BAND:BAND + s_len])])
            values.append([functools.partial(_dot_nt, b=vp_ref[b, h].astype(BF16)), vn_ref[b, :, lanes]])
        o = jnp.concatenate(_softmax_pv_staged(scores, values), axis=-1)
        o_ref[b] = (o * sz_ref[b].astype(F32)).astype(BF16)


def _attn_step(q, k_new, v_new, k_past_t, v_past_t, sz, bias):
    b, s_len, _ = q.shape
    assert k_past_t.shape[1:] == (H_A, HEAD_DIM, BAND) and s_len <= CHUNK and b % STEP_STREAMS == 0
    new = pl.BlockSpec((STEP_STREAMS, s_len, W_A), lambda i: (i, 0, 0))
    past = pl.BlockSpec((STEP_STREAMS, H_A, HEAD_DIM, BAND), lambda i: (i, 0, 0, 0))
    return pl.pallas_call(
        _attn_step_kernel,
        grid=(b // STEP_STREAMS,),
        in_specs=[new, new, new, past, past, new, _full((H_A, CHUNK, BIAS_W))],
        out_specs=new,
        out_shape=jax.ShapeDtypeStruct((b, s_len, W_A), BF16),
        compiler_params=_params("parallel"),
        name="attn_step",
    )(q, k_new, v_new, k_past_t, v_past_t, sz, bias)


def _mem_kv_kernel(mem_ref, gmem_ref, w_ref, gkm_ref, mk_ref, mv_ref):
    h = _rms_rows(mem_ref[0], gmem_ref[...]).astype(BF16)
    kv = _dot(h, w_ref[...])
    mk_ref[0] = _head_rms(kv[:, :W_M], gkm_ref[...], _head_mean_matrix(W_M, HEAD_DIM))
    mv_ref[0] = kv[:, W_M:]


def _mem_kv(mem, wts):
    b = mem.shape[0]
    out = pl.BlockSpec((1, N_MEM, W_M), lambda i: (i, 0, 0))
    return pl.pallas_call(
        _mem_kv_kernel,
        grid=(b,),
        in_specs=[pl.BlockSpec((1, N_MEM, D_MODEL), lambda i: (i, 0, 0)), _full((1, D_MODEL)),
                  _full((D_MODEL, 2 * W_M)), _full((1, W_M))],
        out_specs=[out, out],
        out_shape=[jax.ShapeDtypeStruct((b, N_MEM, W_M), F32)] * 2,
        compiler_params=_params("parallel"),
        name="mem_kv",
    )(mem, wts["g_mem"], wts["w_mem_kv"], wts["g_km"])


MEM_ROWS = 128
MEM_STREAMS = 16


def _attn_mem_step_kernel(q_ref, mk_ref, mv_ref, sz_ref, o_ref):
    for b in range(q_ref.shape[0]):
        scores, values = [], []
        for h in range(H_M):
            lanes = slice(h * HEAD_DIM, (h + 1) * HEAD_DIM)
            scores.append([(_dot(q_ref[b, :, lanes], mk_ref[b, h].astype(BF16)), None)])
            values.append([functools.partial(_dot_nt, b=mv_ref[b, h].astype(BF16))])
        o = jnp.concatenate(_softmax_pv_staged(scores, values), axis=-1)
        o_ref[b] = (o * sz_ref[b].astype(F32)).astype(BF16)


def _attn_mem_step(q, mk_t, mv_t, sz):
    b, t, _ = q.shape
    nb = min(b, MEM_STREAMS)
    assert b % nb == 0 and mk_t.shape[1:] == (H_M, HEAD_DIM, N_MEM)
    rows = pl.BlockSpec((nb, t, W_M), lambda i: (i, 0, 0))
    mem = pl.BlockSpec((nb, H_M, HEAD_DIM, N_MEM), lambda i: (i, 0, 0, 0))
    return pl.pallas_call(
        _attn_mem_step_kernel,
        grid=(b // nb,),
        in_specs=[rows, mem, mem, rows],
        out_specs=rows,
        out_shape=jax.ShapeDtypeStruct((b, t, W_M), BF16),
        compiler_params=_params("parallel"),
        name="attn_mem_step",
    )(q, mk_t, mv_t, sz)


def _gla_constants(c, group):
    r = lax.broadcasted_iota(jnp.int32, (c, c), 0)
    s = lax.broadcasted_iota(jnp.int32, (c, c), 1)
    causal = (s <= r) & (r // group == s // group)
    head_of_k = lax.broadcasted_iota(jnp.int32, (W_BK, W_BV), 0) // DK_B
    head_of_v = lax.broadcasted_iota(jnp.int32, (W_BK, W_BV), 1) // DV_B
    head_of_row = lax.broadcasted_iota(jnp.int32, (H_B, W_BK, c), 1) // DK_B
    head = lax.broadcasted_iota(jnp.int32, (H_B, W_BK, c), 0)
    return causal.astype(BF16), (head_of_k == head_of_v).astype(BF16), (head_of_row == head).astype(BF16)


def _gla_intra(items, group, between=lambda: None):
    n = len(items)
    qs, ks, vs, v16s = ([it[j] for it in items] for j in range(4))
    c = qs[0].shape[0]
    las = [it[4] * LOG2E for it in items]
    causal16, spread, head_rows = _gla_constants(c, group)
    head_rows = [head_rows[h] for h in range(H_B)]
    cums =[_dot_exact_lhs(causal16, la) for la in las]
    cumxs = [cum - la for cum, la in zip(cums, las)]
    r_ck = lax.broadcasted_iota(jnp.int32, (c, W_BK), 0)

    os = [_dot((q * k).astype(BF16), spread) * v for q, k, v in zip(qs, ks, vs)]

    def shifted(x, j):
        return pltpu.roll(x.reshape(c // SUBLANES, SUBLANES, W_BK), j % SUBLANES, 1).reshape(c, W_BK)

    def block_edges(cum, cumx, half):
        if half >= SUBLANES:
            nb = c // half
            first = jnp.broadcast_to(cumx.reshape(nb, half, W_BK)[:, 0:1, :], (nb, half, W_BK))
            last = jnp.broadcast_to(cum.reshape(nb, half, W_BK)[:, half - 1:half, :], (nb, half, W_BK))
            return first.reshape(c, W_BK), last.reshape(c, W_BK)
        r_in = r_ck % half
        first, last = cumx, cum
        for j in range(1, half):
            first = jnp.where(r_in == j, shifted(cumx, j), first)
            last = jnp.where(r_in == half - 1 - j, shifted(cum, -j), last)
        return first, last

    t_xor_s = lax.broadcasted_iota(jnp.int32, (c, H_B * c), 0) ^ (lax.broadcasted_iota(jnp.int32, (c, H_B * c), 1) % c)
    atts = [jnp.zeros((c, H_B * c), F32) for _ in range(n)]
    half = 1
    while half < group:
        edges = [block_edges(cum, cumx, half) for cum, cumx in zip(cums, cumxs)]
        odd = (r_ck & half) != 0
        qts = [jnp.where(odd, q * jnp.exp2(cum - first), 0.0).astype(BF16)
               for q, cum, (first, _) in zip(qs, cums, edges)]
        kts = [jnp.where(odd, 0.0, k * jnp.exp2(last - cum)).T.astype(BF16)
               for k, cum, (_, last) in zip(ks, cums, edges)]
        prods = [_dot(qt, jnp.concatenate([kt * m for m in head_rows], axis=1)) for qt, kt in zip(qts, kts)]
        if 2 * half < c:
            prods = [jnp.where(t_xor_s < 2 * half, a, 0.0) for a in prods]
        atts = [att + a for att, a in zip(atts, prods)]
        between()
        half *= 2
    if group > 1:
        att16s = [att.astype(BF16) for att in atts]
        os = [o + jnp.concatenate([_dot(att16[:, h * c:(h + 1) * c], v16[:, h * DV_B:(h + 1) * DV_B])
                                   for h in range(H_B)], axis=-1)
              for o, att16, v16 in zip(os, att16s, v16s)]
    return list(zip(os, cums))


def _gla_state_step(q, k, v16, cum, s_old):
    c = q.shape[0]
    q_in = (q * jnp.exp2(cum)).astype(BF16)
    k_out_t = (k * jnp.exp2(cum[c - 1:c, :] - cum)).T.astype(BF16)
    keep = jnp.exp2(cum.T[:, c - 1:c])
    kd = lambda h: slice(h * DK_B, (h + 1) * DK_B)
    vd = lambda h: slice(h * DV_B, (h + 1) * DV_B)
    o = jnp.concatenate([_dot(q_in[:, kd(h)], s_old[kd(h)].astype(BF16)) for h in range(H_B)], axis=-1)
    s_new = jnp.concatenate([s_old[kd(h)] * keep[kd(h)] + _dot(k_out_t[kd(h)], v16[:, vd(h)])
                             for h in range(H_B)], axis=0)
    return o, s_new


def _gla_finish(o, g_ref, sz):
    outs = [_rms_rows(o[:, h * DV_B:(h + 1) * DV_B], g_ref[:, h * DV_B:(h + 1) * DV_B]) for h in range(H_B)]
    return (jnp.concatenate(outs, axis=-1) * sz.astype(F32)).astype(BF16)


def _gla_step_kernel(q_ref, k_ref, v_ref, la_ref, sz_ref, g_ref, s0_ref, o_ref, s_ref, *, t):
    c = GLA_C
    n = c // t
    q, k, la = q_ref[...].astype(F32), k_ref[...].astype(F32), la_ref[...]
    v16 = v_ref[...]
    (o, cum), = _gla_intra([(q, k, v16.astype(F32), v16, la)], t)

    q_in = (q * jnp.exp2(cum)).astype(BF16)
    cum_t, k_t = cum.T, k.T
    stream_of_col = lax.broadcasted_iota(jnp.int32, (W_BK, c), 1) // t
    inter = []
    for s in range(n):
        rows = slice(s * t, (s + 1) * t)
        last = cum_t[:, (s + 1) * t - 1:(s + 1) * t]
        k_out_t = (k_t * jnp.exp2(jnp.where(stream_of_col == s, last - cum_t, NEG))).astype(BF16)
        keep = jnp.exp2(last)
        o_s = []
        for h in range(H_B):
            ks = slice(h * DK_B, (h + 1) * DK_B)
            vs = slice(h * DV_B, (h + 1) * DV_B)
            s_old = s0_ref[s, h]
            o_s.append(_dot(q_in[rows, ks], s_old.astype(BF16)))
            s_ref[s, h] = s_old * keep[ks] + _dot(k_out_t[ks], v16[:, vs])
        inter.append(jnp.concatenate(o_s, axis=-1))
    o_ref[...] = _gla_finish(o + jnp.concatenate(inter, axis=0), g_ref, sz_ref[...])


def _gla_step(q, k, v, la, sz, g, s0):
    b, t, _ = q.shape
    n = GLA_C // t
    assert GLA_C % t == 0 and t % (2 * SUBLANES) == 0 and b % n == 0
    flat = lambda z: z.reshape(b * t, z.shape[-1])
    blk = lambda w: pl.BlockSpec((GLA_C, w), lambda i: (i, 0))
    state = pl.BlockSpec((n, H_B, DK_B, DV_B), lambda i: (i, 0, 0, 0))
    o, s_new = pl.pallas_call(
        functools.partial(_gla_step_kernel, t=t),
        grid=(b // n,),
        in_specs=[blk(W_BK), blk(W_BK), blk(W_BV), blk(W_BK), blk(W_BV),
                  pl.BlockSpec((1, W_BV), lambda i: (0, 0)), state],
        out_specs=[blk(W_BV), state],
        out_shape=[jax.ShapeDtypeStruct((b * t, W_BV), BF16),
                   jax.ShapeDtypeStruct((b, H_B, DK_B, DV_B), F32)],
        compiler_params=_params("parallel"),
        name="gla_step",
    )(flat(q), flat(k), flat(v), flat(la), flat(sz), g, s0)
    return o.reshape(b, t, W_BV), s_new


def _prep_weights(l, norm_in, w_in, g_qa, g_ka, w_gate2, b_gate, g_gla_out, g_mem, w_mem_kv, g_qm, g_km,
                  w_up_a, w_up_b, w_up_m, w_out):
    tile = lambda gain, n: jnp.tile(gain, n).reshape(1, -1)
    halved = jnp.concatenate([jnp.full((size,), 0.5 if i in _HALVED_GROUPS else 1.0, F32)
                              for i, size in enumerate(IN_SIZES)])
    return {
        "norm_in": norm_in[l].reshape(1, D_MODEL),
        "w_in_t": (w_in[l].T * halved[:, None]).astype(BF16),
        "w_gate2": w_gate2[l].astype(BF16),
        "b_gate": b_gate[l].reshape(1, W_BK),
        "g_qa": tile(g_qa[l], H_A), "g_ka": tile(g_ka[l], H_A), "g_qm": tile(g_qm[l], H_M),
        "g_km": tile(g_km[l], H_M), "g_gla": tile(g_gla_out[l], H_B),
        "g_mem": g_mem[l].reshape(1, D_MODEL),
        "w_mem_kv": w_mem_kv[l].astype(BF16),
        "w_up_a": w_up_a[l].astype(BF16), "w_up_b": w_up_b[l].astype(BF16),
        "w_up_m": w_up_m[l].astype(BF16), "w_out": w_out[l].astype(BF16),
    }


def _layer_long(x, wts, bias, mk, mv, s0):
    b, t, _ = x.shape
    keep = min(BAND, t)
    r3 = lambda z: z.reshape(b, t, z.shape[-1])
    heads = lambda z: jnp.transpose(z.reshape(b, H_A, HEAD_DIM, keep), (0, 3, 1, 2))
    (qa, ka, va, ka16, va16, sza, out_b, s_new, qm, szm, ga, gb, gm) = _front(
        x.reshape(b * t, D_MODEL), wts, keep, t, keep, s0)
    y = _attn_back(r3(qa), r3(ka16), r3(va16), r3(sza), bias, r3(qm), mk, mv, r3(szm), x, r3(out_b),
                   r3(ga), r3(gb), r3(gm), wts)
    return y, heads(ka), heads(va), s_new


def _layer_short(x, wts, attend_a, attend_m, s0, tm):
    b, t, _ = x.shape
    keep = min(BAND, t)
    x2d = x.reshape(b * t, D_MODEL)
    r3 = lambda z: z.reshape(b, t, z.shape[-1])
    flat = lambda z: z.reshape(b * t, z.shape[-1])
    heads = lambda z: z.reshape(b, keep, H_A, HEAD_DIM)
    (qa, ka, va, ka16, va16, sza, qb, kb, vb, la, szb, qm, szm, ga, gb, gm) = _front(x2d, wts, tm, t, keep)
    out_b, s_new = _gla_step(r3(qb), r3(kb), r3(vb), r3(la), r3(szb), wts["g_gla"], s0)
    out_a = attend_a(r3(qa), r3(ka16), r3(va16), r3(sza))
    out_m = attend_m(r3(qm), r3(szm))
    y = _back(x2d, flat(out_a), flat(out_b), flat(out_m), ga, gb, gm, wts, tm)
    return y.reshape(b, t, D_MODEL), heads(ka), heads(va), s_new


def kernel(x_prompt, x_sample, mem_prompt, cache_a_k, cache_a_v, state_gla, cache_mem_k, cache_mem_v,
           norm_in, w_in, g_qa, g_ka, rel_bias, w_gate2, b_gate, g_gla_out, g_mem, w_mem_kv, g_qm, g_km,
           w_up_a, w_up_b, w_up_m, w_out):
    depth = w_in.shape[0]
    xp, xs = x_prompt, x_sample
    bp, tp, _ = xp.shape
    bs, ts, _ = xs.shape
    akp, avp, sgp, mkp, mvp, aks, avs, sgs = [], [], [], [], [], [], [], []
    for l in range(depth):
        wts = _prep_weights(l, norm_in, w_in, g_qa, g_ka, w_gate2, b_gate, g_gla_out, g_mem, w_mem_kv,
                            g_qm, g_km, w_up_a, w_up_b, w_up_m, w_out)
        bias = _bias_tile(rel_bias[l])
        mk, mv = _mem_kv(mem_prompt, wts)
        s0 = jnp.zeros((bp, H_B, DK_B, DV_B), state_gla.dtype)
        xp, ka, va, sp = _layer_long(xp, wts, bias, mk, mv, s0)
        akp.append(ka)
        avp.append(va)
        sgp.append(sp)
        mkp.append(mk.reshape(bp, N_MEM, H_M, HEAD_DIM))
        mvp.append(mv.reshape(bp, N_MEM, H_M, HEAD_DIM))
        rows_last = lambda z: jnp.transpose(z, (0, 2, 3, 1))
        past_k, past_v = rows_last(cache_a_k[l]), rows_last(cache_a_v[l])
        mem_k, mem_v = rows_last(cache_mem_k[l]), rows_last(cache_mem_v[l])
        attend_s = lambda q, k, v, sz: _attn_step(q, k, v, past_k, past_v, sz, bias)
        attend_ms = lambda q, sz: _attn_mem_step(q, mem_k, mem_v, sz)
        xs, ka_s, va_s, ss = _layer_short(xs, wts, attend_s, attend_ms, state_gla[l], 512)
        aks.append(ka_s)
        avs.append(va_s)
        sgs.append(ss)
    return (xp, xs, jnp.stack(akp), jnp.stack(avp), jnp.stack(sgp), jnp.stack(mkp), jnp.stack(mvp),
            jnp.stack(aks), jnp.stack(avs), jnp.stack(sgs))
```

```python
import functools

import jax
import jax.numpy as jnp
from jax import lax
from jax.experimental import pallas as pl
from jax.experimental.pallas import tpu as pltpu

F32 = jnp.float32
BF16 = jnp.bfloat16

D_MODEL = 1024
CHUNK = 64
LEFT_CHUNKS = 8
HEAD_DIM = 64
H_A = 8
W_A = H_A * HEAD_DIM
REL_CLIP = 128
H_B = 4
DK_B = 64
DV_B = 128
W_BK = H_B * DK_B
W_BV = H_B * DV_B
GATE_RANK = 16
GATE_TAU = 16.0
N_MEM = 256
H_M = 4
W_M = H_M * HEAD_DIM
EPS = 1e-6
IN_SIZES = (W_A, W_A, W_A, W_A, W_BK, W_BK, W_BV, GATE_RANK, W_BV, W_M, W_M, D_MODEL, D_MODEL, D_MODEL)

LANES = 128
SUBLANES = 8
VMEM_LIMIT = 56 * 1024 * 1024
NEG = -1e30
LOG2E = 1.4426950408889634

BAND = LEFT_CHUNKS * CHUNK
QBLK = 512
KSPAN = BAND + QBLK
WINDOW = BAND + CHUNK
BIAS_W = BAND + 2 * CHUNK
GLA_C = 128


def _dot(a, b):
    return jnp.dot(a, b, preferred_element_type=F32)


def _dot_nt(a, b):
    return lax.dot_general(a, b, (((1,), (1,)), ((), ())), preferred_element_type=F32)


def _split3(x):
    hi = x.astype(BF16)
    r = x - hi.astype(F32)
    mid = r.astype(BF16)
    lo = (r - mid.astype(F32)).astype(BF16)
    return hi, mid, lo


def _dot_exact_lhs(a01, x):
    hi, mid, lo = _split3(x)
    return _dot(a01, hi) + _dot(a01, mid) + _dot(a01, lo)


def _sigmoid_of_twice(p):
    return 0.5 * jnp.tanh(p) + 0.5


def _silu_of_twice(p):
    return p * jnp.tanh(p) + p


def _log_sigmoid(z):
    return jnp.minimum(z, 0.0) - jnp.log1p(jnp.exp(-jnp.abs(z)))


def _rms_rows(x, g):
    ms = jnp.mean(x * x, axis=-1, keepdims=True)
    return x * lax.rsqrt(ms + EPS) * g


def _head_mean_matrix(width, head):
    r = lax.broadcasted_iota(jnp.int32, (width, width), 0) // head
    c = lax.broadcasted_iota(jnp.int32, (width, width), 1) // head
    return jnp.where(r == c, 1.0 / head, 0.0).astype(BF16)


def _head_rms(x, g, avg):
    ms = _dot((x * x).astype(BF16), avg)
    return x * lax.rsqrt(ms + EPS) * g


def _full(shape):
    nd = len(shape)
    return pl.BlockSpec(shape, lambda *_: (0,) * nd, pipeline_mode=pl.Buffered(1))


def _params(*sem):
    return pltpu.CompilerParams(dimension_semantics=sem, vmem_limit_bytes=VMEM_LIMIT)


_IN_OFF = tuple(int(sum(IN_SIZES[:i])) for i in range(len(IN_SIZES) + 1))
_HALVED_GROUPS = (3, 8, 10, 11, 12, 13)
D_IN = _IN_OFF[-1]


def _col_table(groups):
    table, off = {}, 0
    for name, width in groups:
        table[name] = (off, width)
        off += width
    return table, off


_ACT_COLS, ACT_W = _col_table((("qa", W_A), ("ka", W_A), ("va", W_A), ("sza", W_A), ("out_b", W_BV),
                               ("qm", W_M), ("szm", W_M)))
_GATE_COLS, GATE_W = _col_table((("a", D_MODEL), ("b", D_MODEL), ("m", D_MODEL)))


def _front_kernel(x_ref, nin_ref, w_ref, wg2_ref, bg_ref, gqa_ref, gka_ref, gqm_ref, *rest, per_stream=None):
    fused = per_stream is not None
    if fused:
        (ggla_ref, s0_ref, kat_o, vat_o, sfin_o, acts_o, gates_o, s_scr, ka_o, va_o) = rest
        cols = lambda ref, name, table: ref.at[:, table[name][0]:table[name][0] + table[name][1]]
        qa_o, ka16_o, va16_o, sza_o, outb_o, qm_o, szm_o = (cols(acts_o, k, _ACT_COLS) for k in _ACT_COLS)
        ga_o, gb_o, gm_o = (cols(gates_o, k, _GATE_COLS) for k in _GATE_COLS)
    else:
        (qa_o, ka_o, va_o, ka16_o, va16_o, sza_o, qb_o, kb_o, vb_o, la_o, szb_o, qm_o, szm_o,
         ga_o, gb_o, gm_o) = rest
    if fused:
        @pl.when(pl.program_id(0) % per_stream == 0)
        def _():
            s_scr[...] = s0_ref[pl.program_id(0) // per_stream].reshape(W_BK, DV_B)

    h = _rms_rows(x_ref[...], nin_ref[...]).astype(BF16)

    def proj(i):
        return _dot_nt(h, w_ref[_IN_OFF[i]:_IN_OFF[i + 1], :])

    qb = (proj(4) * (DK_B ** -0.5)).astype(BF16)
    kb = proj(5).astype(BF16)
    vb = proj(6).astype(BF16)
    z = _dot(proj(7).astype(BF16), wg2_ref[...]) + bg_ref[...]
    la = _log_sigmoid(z) * (1.0 / GATE_TAU)
    szb = _silu_of_twice(proj(8)).astype(BF16)

    scale = HEAD_DIM ** -0.5 * LOG2E
    avg_a = _head_mean_matrix(W_A, HEAD_DIM)

    def emit_qa():
        qa_o[...] = (_head_rms(proj(0), gqa_ref[...], avg_a) * scale).astype(BF16)

    def emit_ka():
        ka = _head_rms(proj(1), gka_ref[...], avg_a)
        ka_o[...] = ka
        ka16_o[...] = ka.astype(BF16)

    def emit_va():
        va = proj(2)
        va_o[...] = va
        va16_o[...] = va.astype(BF16)

    def emit_sza():
        sza_o[...] = _silu_of_twice(proj(3)).astype(BF16)

    def emit_qm():
        qm_o[...] = (_head_rms(proj(9), gqm_ref[...], _head_mean_matrix(W_M, HEAD_DIM)) * scale).astype(BF16)

    def emit_szm():
        szm_o[...] = _silu_of_twice(proj(10)).astype(BF16)

    def emit_gate(i, ref):
        ref[...] = _sigmoid_of_twice(proj(i)).astype(BF16)

    pending = [emit_qa, emit_ka, emit_va, emit_sza, emit_qm, emit_szm,
               functools.partial(emit_gate, 11, ga_o), functools.partial(emit_gate, 12, gb_o),
               functools.partial(emit_gate, 13, gm_o)]

    def emit_next():
        if pending:
            pending.pop(0)()

    if fused:
        chunks = [slice(r, r + GLA_C) for r in range(0, x_ref.shape[0], GLA_C)]
        items = [(qb[rows].astype(F32), kb[rows].astype(F32), vb[rows].astype(F32), vb[rows], la[rows])
                 for rows in chunks]
        state = s_scr[...]
        outs = []
        for rows, item, (o, cum) in zip(chunks, items, _gla_intra(items, GLA_C, emit_next)):
            inter, state = _gla_state_step(item[0], item[1], item[3], cum, state)
            outs.append(_gla_finish(o + inter, ggla_ref, szb[rows]))
            emit_next()
        s_scr[...] = state
        outb_o[...] = jnp.concatenate(outs, axis=0)
    else:
        qb_o[...] = qb
        kb_o[...] = kb
        vb_o[...] = vb
        la_o[...] = la
        szb_o[...] = szb
    while pending:
        emit_next()

    if fused:
        @pl.when(pl.program_id(0) % per_stream == per_stream - 1)
        def _():
            sfin_o[0] = s_scr[...].reshape(H_B, DK_B, DV_B)
            kat_o[0] = ka_o[...].T
            vat_o[0] = va_o[...].T


def _front(x2d, wts, tm, t, keep, s0=None):
    n = x2d.shape[0]
    tm = min(tm, n)
    row = lambda w: pl.BlockSpec((tm, w), lambda i: (i, 0))
    out = lambda rows_, w, d, spec: (jax.ShapeDtypeStruct((rows_, w), d), spec)
    tail = [out(n, W_M, BF16, row(W_M)), out(n, W_M, BF16, row(W_M))] + [out(n, D_MODEL, BF16, row(D_MODEL))] * 3
    in_specs = [row(D_MODEL), _full((1, D_MODEL)), _full((D_IN, D_MODEL)),
                _full((GATE_RANK, W_BK)), _full((1, W_BK)), _full((1, W_A)), _full((1, W_A)), _full((1, W_M))]
    args = [x2d, wts["norm_in"], wts["w_in_t"], wts["w_gate2"], wts["b_gate"], wts["g_qa"], wts["g_ka"], wts["g_qm"]]
    if s0 is None:
        assert keep == t
        kv = [out(n, W_A, F32, row(W_A))] * 2
        mid = [out(n, W_BK, BF16, row(W_BK)), out(n, W_BK, BF16, row(W_BK)), out(n, W_BV, BF16, row(W_BV)),
               out(n, W_BK, F32, row(W_BK)), out(n, W_BV, BF16, row(W_BV))]
        body, scratch = _front_kernel, []
    else:
        assert t % tm == 0 and tm % GLA_C == 0 and keep == tm
        streams, per_stream = n // t, t // tm
        state = (streams, H_B, DK_B, DV_B)
        kv = [(jax.ShapeDtypeStruct((streams, W_A, keep), F32),
               pl.BlockSpec((1, W_A, keep), lambda i: (i // per_stream, 0, 0)))] * 2
        outs = kv + [(jax.ShapeDtypeStruct(state, F32),
                      pl.BlockSpec((1,) + state[1:], lambda i: (i // per_stream, 0, 0, 0))),
                     out(n, ACT_W, BF16, row(ACT_W)), out(n, GATE_W, BF16, row(GATE_W))]
        in_specs += [_full((1, W_BV)), _full(state)]
        args += [wts["g_gla"], s0]
        body = functools.partial(_front_kernel, per_stream=per_stream)
        scratch = [pltpu.VMEM((W_BK, DV_B), F32), pltpu.VMEM((tm, W_A), F32), pltpu.VMEM((tm, W_A), F32)]
    if s0 is None:
        outs = [out(n, W_A, BF16, row(W_A))] + kv + [out(n, W_A, BF16, row(W_A))] * 3 + mid + tail
    return pl.pallas_call(
        body,
        grid=(n // tm,),
        in_specs=in_specs,
        out_specs=[spec for _, spec in outs],
        out_shape=[shape for shape, _ in outs],
        scratch_shapes=scratch,
        compiler_params=_params("arbitrary"),
        name="front",
    )(*args)


def _back_kernel(x_ref, a_ref, b_ref, m_ref, ga_ref, gb_ref, gm_ref, wa_ref, wb_ref, wm_ref, wo_ref, y_ref):
    u = (ga_ref[...].astype(F32) * _dot(a_ref[...], wa_ref[...])
         + gb_ref[...].astype(F32) * _dot(b_ref[...], wb_ref[...])
         + gm_ref[...].astype(F32) * _dot(m_ref[...], wm_ref[...]))
    y_ref[...] = x_ref[...] + _dot(u.astype(BF16), wo_ref[...])


def _back(x2d, a, b, m, ga, gb, gm, wts, tm):
    n = x2d.shape[0]
    tm = min(tm, n)
    row = lambda w: pl.BlockSpec((tm, w), lambda i: (i, 0))
    return pl.pallas_call(
        _back_kernel,
        grid=(n // tm,),
        in_specs=[row(D_MODEL), row(W_A), row(W_BV), row(W_M), row(D_MODEL), row(D_MODEL), row(D_MODEL),
                  _full((W_A, D_MODEL)), _full((W_BV, D_MODEL)), _full((W_M, D_MODEL)),
                  _full((D_MODEL, D_MODEL))],
        out_specs=row(D_MODEL),
        out_shape=jax.ShapeDtypeStruct((n, D_MODEL), F32),
        compiler_params=_params("parallel"),
        name="back",
    )(x2d, a, b, m, ga, gb, gm, wts["w_up_a"], wts["w_up_b"], wts["w_up_m"], wts["w_out"])


def _bias_kernel(e_ref, tile_ref):
    e = jnp.broadcast_to(e_ref[0] * LOG2E, (CHUNK, 2 * BAND))
    tile_ref[0] = pltpu.roll(e, 0, 1, stride=1, stride_axis=0)[:, :BIAS_W]


def _bias_tile(rel_bias):
    j = jnp.arange(2 * BAND)
    dist = jnp.where(j < BIAS_W, BAND - j, REL_CLIP)
    e = rel_bias[:, jnp.clip(dist, -REL_CLIP, REL_CLIP) + REL_CLIP].reshape(H_A, 1, 2 * BAND)
    return pl.pallas_call(
        _bias_kernel,
        grid=(H_A,),
        in_specs=[pl.BlockSpec((1, 1, 2 * BAND), lambda h: (h, 0, 0))],
        out_specs=pl.BlockSpec((1, CHUNK, BIAS_W), lambda h: (h, 0, 0)),
        out_shape=jax.ShapeDtypeStruct((H_A, CHUNK, BIAS_W), F32),
        compiler_params=_params("parallel"),
        name="bias_tile",
    )(e)


def _softmax_pv_staged(scores, values):
    biased = [[s if b is None else s + b for s, b in s_list] for s_list in scores]
    tops = [functools.reduce(jnp.maximum, [jnp.max(s, axis=-1, keepdims=True) for s in s_list])
            for s_list in biased]
    probs = [[jnp.exp2(s - m) for s in s_list] for s_list, m in zip(biased, tops)]
    sums = [functools.reduce(jnp.add, [jnp.sum(x, axis=-1, keepdims=True) for x in p_list]) for p_list in probs]
    return [functools.reduce(jnp.add, [(v(x.astype(BF16)) if callable(v) else _dot(x.astype(BF16), v))
                                       for x, v in zip(p_list, v_list)]) / l
            for p_list, v_list, l in zip(probs, values, sums)]


def _head_pairs_attention(q_pairs, k_lists, v_lists, bias_fn):
    m_rows = q_pairs[0].shape[0]
    lane = lax.broadcasted_iota(jnp.int32, (m_rows, LANES), 1)
    first = lane < HEAD_DIM
    scores = []
    for p, qp in enumerate(q_pairs):
        zero = jnp.zeros_like(qp)
        q2 = jnp.concatenate([jnp.where(first, qp, zero), jnp.where(first, zero, qp)], axis=0)
        s_list = []
        for i, k in enumerate(k_lists[p]):
            b0, b1 = bias_fn(p, 0, i), bias_fn(p, 1, i)
            s_list.append((_dot_nt(q2, k), None if b0 is None else jnp.concatenate([b0, b1], axis=0)))
        scores.append(s_list)
    outs = _softmax_pv_staged(scores, v_lists)
    return [jnp.where(first, o[:m_rows], o[m_rows:]) for o in outs]


def _attn_back_kernel(q_ref, *refs):
    nk = BAND // QBLK + 1
    k_refs, v_refs = refs[:nk], refs[nk:2 * nk]
    (sz_ref, bias_ref, qm_ref, mk_ref, mv_ref, szm_ref, x_ref, b_ref, ga_ref, gb_ref, gm_ref,
     wa_ref, wb_ref, wm_ref, wo_ref, y_ref, k_buf, v_buf, bias_buf, a_buf, m_buf) = refs[2 * nk:]
    i = pl.program_id(1)
    for j, (k_ref, v_ref) in enumerate(zip(k_refs, v_refs)):
        k_buf[j * QBLK:(j + 1) * QBLK, :] = k_ref[0]
        v_buf[j * QBLK:(j + 1) * QBLK, :] = v_ref[0]

    @pl.when(i <= BAND // QBLK)
    def _():
        col = lax.broadcasted_iota(jnp.int32, (1, BIAS_W), 1)
        for c in range(QBLK // CHUNK):
            before_start = jnp.where(col + c * CHUNK < (BAND // QBLK - i) * QBLK, NEG, 0.0)
            for h in range(H_A):
                bias_buf[c, h] = bias_ref[h] + before_start

    lanes_m = [slice(p * LANES, (p + 1) * LANES) for p in range(H_M // 2)]
    mk = [[mk_ref[0, :, lanes].astype(BF16)] for lanes in lanes_m]
    mv = [[mv_ref[0, :, lanes].astype(BF16)] for lanes in lanes_m]
    blocks = [(slice(r, r + MEM_ROWS), p) for r in range(0, QBLK, MEM_ROWS) for p in range(len(lanes_m))]
    outs = _head_pairs_attention([qm_ref[0, rows, lanes_m[p]] for rows, p in blocks],
                                 [mk[p] for _, p in blocks], [mv[p] for _, p in blocks], lambda n, e, _: None)
    for (rows, p), o in zip(blocks, outs):
        m_buf[rows, lanes_m[p]] = (o * szm_ref[0, rows, lanes_m[p]].astype(F32)).astype(BF16)

    pairs = [slice(p * LANES, (p + 1) * LANES) for p in range(H_A // 2)]
    for c in range(QBLK // CHUNK):
        rows = slice(c * CHUNK, (c + 1) * CHUNK)
        win = slice(c * CHUNK, c * CHUNK + WINDOW)
        outs = _head_pairs_attention(
            [q_ref[0, rows, lanes] for lanes in pairs],
            [[k_buf[win, lanes]] for lanes in pairs], [[v_buf[win, lanes]] for lanes in pairs],
            lambda p, e, _: bias_buf[c, 2 * p + e, :, :WINDOW])
        for lanes, o in zip(pairs, outs):
            a_buf[rows, lanes] = (o * sz_ref[0, rows, lanes].astype(F32)).astype(BF16)

    u = (ga_ref[0].astype(F32) * _dot(a_buf[...], wa_ref[...])
         + gb_ref[0].astype(F32) * _dot(b_ref[0], wb_ref[...])
         + gm_ref[0].astype(F32) * _dot(m_buf[...], wm_ref[...]))
    y_ref[0] = x_ref[0] + _dot(u.astype(BF16), wo_ref[...])


def _attn_back(acts, gates, bias, mk, mv, x, wts):
    b, t, _ = x.shape
    nb = BAND // QBLK
    n = t // QBLK
    assert BAND % QBLK == 0 and t % QBLK == 0 and QBLK % MEM_ROWS == 0

    def group(table, name, d=0):
        off, w = table[name]
        assert off % w == 0
        return pl.BlockSpec((1, QBLK, w), lambda bi, i: (bi, jnp.maximum(i - d, 0), off // w))

    act = functools.partial(group, _ACT_COLS)
    gate = functools.partial(group, _GATE_COLS)
    mem = pl.BlockSpec((1, N_MEM, W_M), lambda bi, i: (bi, 0, 0))
    return pl.pallas_call(
        _attn_back_kernel,
        grid=(b, n),
        in_specs=[act("qa")] + [act("ka", nb - j) for j in range(nb + 1)] + [act("va", nb - j) for j in range(nb + 1)]
                 + [act("sza"), _full((H_A, CHUNK, BIAS_W)), act("qm"), mem, mem, act("szm"),
                    pl.BlockSpec((1, QBLK, D_MODEL), lambda bi, i: (bi, i, 0)), act("out_b"),
                    gate("a"), gate("b"), gate("m"),
                    _full((W_A, D_MODEL)), _full((W_BV, D_MODEL)), _full((W_M, D_MODEL)),
                    _full((D_MODEL, D_MODEL))],
        out_specs=pl.BlockSpec((1, QBLK, D_MODEL), lambda bi, i: (bi, i, 0)),
        out_shape=jax.ShapeDtypeStruct((b, t, D_MODEL), F32),
        scratch_shapes=[pltpu.VMEM((KSPAN, W_A), BF16), pltpu.VMEM((KSPAN, W_A), BF16),
                        pltpu.VMEM((QBLK // CHUNK, H_A, CHUNK, BIAS_W), F32),
                        pltpu.VMEM((QBLK, W_A), BF16), pltpu.VMEM((QBLK, W_M), BF16)],
        compiler_params=_params("parallel", "arbitrary"),
        name="attn_back",
    )(*([acts] * (2 * nb + 4)), bias, acts, mk, mv, acts, x, acts, gates, gates, gates,
      wts["w_up_a"], wts["w_up_b"], wts["w_up_m"], wts["w_out"])


STEP_STREAMS = 8


def _attn_step_kernel(q_ref, kn_ref, vn_ref, kp_ref, vp_ref, sz_ref, bias_ref, o_ref):
    s_len = q_ref.shape[1]
    for b in range(STEP_STREAMS):
        scores, values = [], []
        for h in range(H_A):
            lanes = slice(h * HEAD_DIM, (h + 1) * HEAD_DIM)
            q = q_ref[b, :, lanes]
            scores.append([(_dot(q, kp_ref[b, h].astype(BF16)), bias_ref[h, :s_len, :BAND]),
                           (_dot_nt(q, kn_ref[b, :, lanes]), bias_ref[h, :s_len, BAND:BAND + s_len])])
            values.append([functools.partial(_dot_nt, b=vp_ref[b, h].astype(BF16)), vn_ref[b, :, lanes]])
        o = jnp.concatenate(_softmax_pv_staged(scores, values), axis=-1)
        o_ref[b] = (o * sz_ref[b].astype(F32)).astype(BF16)


def _attn_step(q, k_new, v_new, k_past_t, v_past_t, sz, bias):
    b, s_len, _ = q.shape
    assert k_past_t.shape[1:] == (H_A, HEAD_DIM, BAND) and s_len <= CHUNK and b % STEP_STREAMS == 0
    new = pl.BlockSpec((STEP_STREAMS, s_len, W_A), lambda i: (i, 0, 0))
    past = pl.BlockSpec((STEP_STREAMS, H_A, HEAD_DIM, BAND), lambda i: (i, 0, 0, 0))
    return pl.pallas_call(
        _attn_step_kernel,
        grid=(b // STEP_STREAMS,),
        in_specs=[new, new, new, past, past, new, _full((H_A, CHUNK, BIAS_W))],
        out_specs=new,
        out_shape=jax.ShapeDtypeStruct((b, s_len, W_A), BF16),
        compiler_params=_params("parallel"),
        name="attn_step",
    )(q, k_new, v_new, k_past_t, v_past_t, sz, bias)


def _mem_kv_kernel(mem_ref, gmem_ref, w_ref, gkm_ref, mk_ref, mv_ref):
    h = _rms_rows(mem_ref[0], gmem_ref[...]).astype(BF16)
    kv = _dot(h, w_ref[...])
    mk_ref[0] = _head_rms(kv[:, :W_M], gkm_ref[...], _head_mean_matrix(W_M, HEAD_DIM))
    mv_ref[0] = kv[:, W_M:]


def _mem_kv(mem, wts):
    b = mem.shape[0]
    out = pl.BlockSpec((1, N_MEM, W_M), lambda i: (i, 0, 0))
    return pl.pallas_call(
        _mem_kv_kernel,
        grid=(b,),
        in_specs=[pl.BlockSpec((1, N_MEM, D_MODEL), lambda i: (i, 0, 0)), _full((1, D_MODEL)),
                  _full((D_MODEL, 2 * W_M)), _full((1, W_M))],
        out_specs=[out, out],
        out_shape=[jax.ShapeDtypeStruct((b, N_MEM, W_M), F32)] * 2,
        compiler_params=_params("parallel"),
        name="mem_kv",
    )(mem, wts["g_mem"], wts["w_mem_kv"], wts["g_km"])


MEM_ROWS = 128
MEM_STREAMS = 16


def _attn_mem_step_kernel(q_ref, mk_ref, mv_ref, sz_ref, o_ref):
    for b in range(q_ref.shape[0]):
        scores, values = [], []
        for h in range(H_M):
            lanes = slice(h * HEAD_DIM, (h + 1) * HEAD_DIM)
            scores.append([(_dot(q_ref[b, :, lanes], mk_ref[b, h].astype(BF16)), None)])
            values.append([functools.partial(_dot_nt, b=mv_ref[b, h].astype(BF16))])
        o = jnp.concatenate(_softmax_pv_staged(scores, values), axis=-1)
        o_ref[b] = (o * sz_ref[b].astype(F32)).astype(BF16)


def _attn_mem_step(q, mk_t, mv_t, sz):
    b, t, _ = q.shape
    nb = min(b, MEM_STREAMS)
    assert b % nb == 0 and mk_t.shape[1:] == (H_M, HEAD_DIM, N_MEM)
    rows = pl.BlockSpec((nb, t, W_M), lambda i: (i, 0, 0))
    mem = pl.BlockSpec((nb, H_M, HEAD_DIM, N_MEM), lambda i: (i, 0, 0, 0))
    return pl.pallas_call(
        _attn_mem_step_kernel,
        grid=(b // nb,),
        in_specs=[rows, mem, mem, rows],
        out_specs=rows,
        out_shape=jax.ShapeDtypeStruct((b, t, W_M), BF16),
        compiler_params=_params("parallel"),
        name="attn_mem_step",
    )(q, mk_t, mv_t, sz)


def _gla_constants(c, group):
    r = lax.broadcasted_iota(jnp.int32, (c, c), 0)
    s = lax.broadcasted_iota(jnp.int32, (c, c), 1)
    causal = (s <= r) & (r // group == s // group)
    head_of_k = lax.broadcasted_iota(jnp.int32, (W_BK, W_BV), 0) // DK_B
    head_of_v = lax.broadcasted_iota(jnp.int32, (W_BK, W_BV), 1) // DV_B
    head_of_row = lax.broadcasted_iota(jnp.int32, (H_B, W_BK, c), 1) // DK_B
    head = lax.broadcasted_iota(jnp.int32, (H_B, W_BK, c), 0)
    return causal.astype(BF16), (head_of_k == head_of_v).astype(BF16), (head_of_row == head).astype(BF16)


def _gla_intra(items, group, between=lambda: None):
    n = len(items)
    qs, ks, vs, v16s = ([it[j] for it in items] for j in range(4))
    c = qs[0].shape[0]
    las = [it[4] * LOG2E for it in items]
    causal16, spread, head_rows = _gla_constants(c, group)
    head_rows = [head_rows[h] for h in range(H_B)]
    cums =[_dot_exact_lhs(causal16, la) for la in las]
    cumxs = [cum - la for cum, la in zip(cums, las)]
    r_ck = lax.broadcasted_iota(jnp.int32, (c, W_BK), 0)

    os = [_dot((q * k).astype(BF16), spread) * v for q, k, v in zip(qs, ks, vs)]

    def shifted(x, j):
        return pltpu.roll(x.reshape(c // SUBLANES, SUBLANES, W_BK), j % SUBLANES, 1).reshape(c, W_BK)

    def block_edges(cum, cumx, half):
        if half >= SUBLANES:
            nb = c // half
            first = jnp.broadcast_to(cumx.reshape(nb, half, W_BK)[:, 0:1, :], (nb, half, W_BK))
            last = jnp.broadcast_to(cum.reshape(nb, half, W_BK)[:, half - 1:half, :], (nb, half, W_BK))
            return first.reshape(c, W_BK), last.reshape(c, W_BK)
        r_in = r_ck % half
        first, last = cumx, cum
        for j in range(1, half):
            first = jnp.where(r_in == j, shifted(cumx, j), first)
            last = jnp.where(r_in == half - 1 - j, shifted(cum, -j), last)
        return first, last

    t_xor_s = lax.broadcasted_iota(jnp.int32, (c, H_B * c), 0) ^ (lax.broadcasted_iota(jnp.int32, (c, H_B * c), 1) % c)
    atts = [jnp.zeros((c, H_B * c), F32) for _ in range(n)]
    half = 1
    while half < group:
        edges = [block_edges(cum, cumx, half) for cum, cumx in zip(cums, cumxs)]
        odd = (r_ck & half) != 0
        qts = [jnp.where(odd, q * jnp.exp2(cum - first), 0.0).astype(BF16)
               for q, cum, (first, _) in zip(qs, cums, edges)]
        kts = [jnp.where(odd, 0.0, k * jnp.exp2(last - cum)).T.astype(BF16)
               for k, cum, (_, last) in zip(ks, cums, edges)]
        prods = [_dot(qt, jnp.concatenate([kt * m for m in head_rows], axis=1)) for qt, kt in zip(qts, kts)]
        if 2 * half < c:
            prods = [jnp.where(t_xor_s < 2 * half, a, 0.0) for a in prods]
        atts = [att + a for att, a in zip(atts, prods)]
        between()
        half *= 2
    if group > 1:
        att16s = [att.astype(BF16) for att in atts]
        os = [o + jnp.concatenate([_dot(att16[:, h * c:(h + 1) * c], v16[:, h * DV_B:(h + 1) * DV_B])
                                   for h in range(H_B)], axis=-1)
              for o, att16, v16 in zip(os, att16s, v16s)]
    return list(zip(os, cums))


def _gla_state_step(q, k, v16, cum, s_old):
    c = q.shape[0]
    q_in = (q * jnp.exp2(cum)).astype(BF16)
    k_out_t = (k * jnp.exp2(cum[c - 1:c, :] - cum)).T.astype(BF16)
    keep = jnp.exp2(cum.T[:, c - 1:c])
    kd = lambda h: slice(h * DK_B, (h + 1) * DK_B)
    vd = lambda h: slice(h * DV_B, (h + 1) * DV_B)
    o = jnp.concatenate([_dot(q_in[:, kd(h)], s_old[kd(h)].astype(BF16)) for h in range(H_B)], axis=-1)
    s_new = jnp.concatenate([s_old[kd(h)] * keep[kd(h)] + _dot(k_out_t[kd(h)], v16[:, vd(h)])
                             for h in range(H_B)], axis=0)
    return o, s_new


def _gla_finish(o, g_ref, sz):
    outs = [_rms_rows(o[:, h * DV_B:(h + 1) * DV_B], g_ref[:, h * DV_B:(h + 1) * DV_B]) for h in range(H_B)]
    return (jnp.concatenate(outs, axis=-1) * sz.astype(F32)).astype(BF16)


def _gla_step_kernel(q_ref, k_ref, v_ref, la_ref, sz_ref, g_ref, s0_ref, o_ref, s_ref, *, t):
    c = GLA_C
    n = c // t
    q, k, la = q_ref[...].astype(F32), k_ref[...].astype(F32), la_ref[...]
    v16 = v_ref[...]
    (o, cum), = _gla_intra([(q, k, v16.astype(F32), v16, la)], t)

    q_in = (q * jnp.exp2(cum)).astype(BF16)
    cum_t, k_t = cum.T, k.T
    stream_of_col = lax.broadcasted_iota(jnp.int32, (W_BK, c), 1) // t
    inter = []
    for s in range(n):
        rows = slice(s * t, (s + 1) * t)
        last = cum_t[:, (s + 1) * t - 1:(s + 1) * t]
        k_out_t = (k_t * jnp.exp2(jnp.where(stream_of_col == s, last - cum_t, NEG))).astype(BF16)
        keep = jnp.exp2(last)
        o_s = []
        for h in range(H_B):
            ks = slice(h * DK_B, (h + 1) * DK_B)
            vs = slice(h * DV_B, (h + 1) * DV_B)
            s_old = s0_ref[s, h]
            o_s.append(_dot(q_in[rows, ks], s_old.astype(BF16)))
            s_ref[s, h] = s_old * keep[ks] + _dot(k_out_t[ks], v16[:, vs])
        inter.append(jnp.concatenate(o_s, axis=-1))
    o_ref[...] = _gla_finish(o + jnp.concatenate(inter, axis=0), g_ref, sz_ref[...])


def _gla_step(q, k, v, la, sz, g, s0):
    b, t, _ = q.shape
    n = GLA_C // t
    assert GLA_C % t == 0 and t % (2 * SUBLANES) == 0 and b % n == 0
    flat = lambda z: z.reshape(b * t, z.shape[-1])
    blk = lambda w: pl.BlockSpec((GLA_C, w), lambda i: (i, 0))
    state = pl.BlockSpec((n, H_B, DK_B, DV_B), lambda i: (i, 0, 0, 0))
    o, s_new = pl.pallas_call(
        functools.partial(_gla_step_kernel, t=t),
        grid=(b // n,),
        in_specs=[blk(W_BK), blk(W_BK), blk(W_BV), blk(W_BK), blk(W_BV),
                  pl.BlockSpec((1, W_BV), lambda i: (0, 0)), state],
        out_specs=[blk(W_BV), state],
        out_shape=[jax.ShapeDtypeStruct((b * t, W_BV), BF16),
                   jax.ShapeDtypeStruct((b, H_B, DK_B, DV_B), F32)],
        compiler_params=_params("parallel"),
        name="gla_step",
    )(flat(q), flat(k), flat(v), flat(la), flat(sz), g, s0)
    return o.reshape(b, t, W_BV), s_new


def _prep_weights(l, norm_in, w_in, g_qa, g_ka, w_gate2, b_gate, g_gla_out, g_mem, w_mem_kv, g_qm, g_km,
                  w_up_a, w_up_b, w_up_m, w_out):
    tile = lambda gain, n: jnp.tile(gain, n).reshape(1, -1)
    halved = jnp.concatenate([jnp.full((size,), 0.5 if i in _HALVED_GROUPS else 1.0, F32)
                              for i, size in enumerate(IN_SIZES)])
    return {
        "norm_in": norm_in[l].reshape(1, D_MODEL),
        "w_in_t": (w_in[l].T * halved[:, None]).astype(BF16),
        "w_gate2": w_gate2[l].astype(BF16),
        "b_gate": b_gate[l].reshape(1, W_BK),
        "g_qa": tile(g_qa[l], H_A), "g_ka": tile(g_ka[l], H_A), "g_qm": tile(g_qm[l], H_M),
        "g_km": tile(g_km[l], H_M), "g_gla": tile(g_gla_out[l], H_B),
        "g_mem": g_mem[l].reshape(1, D_MODEL),
        "w_mem_kv": w_mem_kv[l].astype(BF16),
        "w_up_a": w_up_a[l].astype(BF16), "w_up_b": w_up_b[l].astype(BF16),
        "w_up_m": w_up_m[l].astype(BF16), "w_out": w_out[l].astype(BF16),
    }


def _layer_long(x, wts, bias, mk, mv, s0):
    b, t, _ = x.shape
    keep = min(BAND, t)
    r3 = lambda z: z.reshape(b, t, z.shape[-1])
    heads = lambda z: jnp.transpose(z.reshape(b, H_A, HEAD_DIM, keep), (0, 3, 1, 2))
    ka, va, s_new, acts, gates = _front(x.reshape(b * t, D_MODEL), wts, keep, t, keep, s0)
    y = _attn_back(r3(acts), r3(gates), bias, mk, mv, x, wts)
    return y, heads(ka), heads(va), s_new


def _layer_short(x, wts, attend_a, attend_m, s0, tm):
    b, t, _ = x.shape
    keep = min(BAND, t)
    x2d = x.reshape(b * t, D_MODEL)
    r3 = lambda z: z.reshape(b, t, z.shape[-1])
    flat = lambda z: z.reshape(b * t, z.shape[-1])
    heads = lambda z: z.reshape(b, keep, H_A, HEAD_DIM)
    (qa, ka, va, ka16, va16, sza, qb, kb, vb, la, szb, qm, szm, ga, gb, gm) = _front(x2d, wts, tm, t, keep)
    out_b, s_new = _gla_step(r3(qb), r3(kb), r3(vb), r3(la), r3(szb), wts["g_gla"], s0)
    out_a = attend_a(r3(qa), r3(ka16), r3(va16), r3(sza))
    out_m = attend_m(r3(qm), r3(szm))
    y = _back(x2d, flat(out_a), flat(out_b), flat(out_m), ga, gb, gm, wts, tm)
    return y.reshape(b, t, D_MODEL), heads(ka), heads(va), s_new


def kernel(x_prompt, x_sample, mem_prompt, cache_a_k, cache_a_v, state_gla, cache_mem_k, cache_mem_v,
           norm_in, w_in, g_qa, g_ka, rel_bias, w_gate2, b_gate, g_gla_out, g_mem, w_mem_kv, g_qm, g_km,
           w_up_a, w_up_b, w_up_m, w_out):
    depth = w_in.shape[0]
    xp, xs = x_prompt, x_sample
    bp, tp, _ = xp.shape
    bs, ts, _ = xs.shape
    akp, avp, sgp, mkp, mvp, aks, avs, sgs = [], [], [], [], [], [], [], []
    for l in range(depth):
        wts = _prep_weights(l, norm_in, w_in, g_qa, g_ka, w_gate2, b_gate, g_gla_out, g_mem, w_mem_kv,
                            g_qm, g_km, w_up_a, w_up_b, w_up_m, w_out)
        bias = _bias_tile(rel_bias[l])
        mk, mv = _mem_kv(mem_prompt, wts)
        s0 = jnp.zeros((bp, H_B, DK_B, DV_B), state_gla.dtype)
        xp, ka, va, sp = _layer_long(xp, wts, bias, mk, mv, s0)
        akp.append(ka)
        avp.append(va)
        sgp.append(sp)
        mkp.append(mk.reshape(bp, N_MEM, H_M, HEAD_DIM))
        mvp.append(mv.reshape(bp, N_MEM, H_M, HEAD_DIM))
        rows_last = lambda z: jnp.transpose(z, (0, 2, 3, 1))
        past_k, past_v = rows_last(cache_a_k[l]), rows_last(cache_a_v[l])
        mem_k, mem_v = rows_last(cache_mem_k[l]), rows_last(cache_mem_v[l])
        attend_s = lambda q, k, v, sz: _attn_step(q, k, v, past_k, past_v, sz, bias)
        attend_ms = lambda q, sz: _attn_mem_step(q, mem_k, mem_v, sz)
        xs, ka_s, va_s, ss = _layer_short(xs, wts, attend_s, attend_ms, state_gla[l], 512)
        aks.append(ka_s)
        avs.append(va_s)
        sgs.append(ss)
    return (xp, xs, jnp.stack(akp), jnp.stack(avp), jnp.stack(sgp), jnp.stack(mkp), jnp.stack(mvp),
            jnp.stack(aks), jnp.stack(avs), jnp.stack(sgs))
```

```python
import functools

import jax
import jax.numpy as jnp
from jax import lax
from jax.experimental import pallas as pl
from jax.experimental.pallas import tpu as pltpu

F32 = jnp.float32
BF16 = jnp.bfloat16

D_MODEL = 1024
CHUNK = 64
LEFT_CHUNKS = 8
HEAD_DIM = 64
H_A = 8
W_A = H_A * HEAD_DIM
REL_CLIP = 128
H_B = 4
DK_B = 64
DV_B = 128
W_BK = H_B * DK_B
W_BV = H_B * DV_B
GATE_RANK = 16
GATE_TAU = 16.0
N_MEM = 256
H_M = 4
W_M = H_M * HEAD_DIM
EPS = 1e-6
IN_SIZES = (W_A, W_A, W_A, W_A, W_BK, W_BK, W_BV, GATE_RANK, W_BV, W_M, W_M, D_MODEL, D_MODEL, D_MODEL)

LANES = 128
SUBLANES = 8
VMEM_LIMIT = 56 * 1024 * 1024
NEG = -1e30
LOG2E = 1.4426950408889634

BAND = LEFT_CHUNKS * CHUNK
QBLK = 512
KSPAN = BAND + QBLK
WINDOW = BAND + CHUNK
BIAS_W = BAND + 2 * CHUNK
GLA_C = 128
PROJ_PER_LEVEL = 5


def _dot(a, b):
    return jnp.dot(a, b, preferred_element_type=F32)


def _dot_nt(a, b):
    return lax.dot_general(a, b, (((1,), (1,)), ((), ())), preferred_element_type=F32)


def _split3(x):
    hi = x.astype(BF16)
    r = x - hi.astype(F32)
    mid = r.astype(BF16)
    lo = (r - mid.astype(F32)).astype(BF16)
    return hi, mid, lo


def _dot_exact_lhs(a01, x):
    hi, mid, lo = _split3(x)
    return _dot(a01, hi) + _dot(a01, mid) + _dot(a01, lo)


def _sigmoid_of_twice(p):
    return 0.5 * jnp.tanh(p) + 0.5


def _silu_of_twice(p):
    return p * jnp.tanh(p) + p


def _log_sigmoid(z):
    return jnp.minimum(z, 0.0) - jnp.log1p(jnp.exp(-jnp.abs(z)))


def _rms_rows(x, g):
    ms = jnp.mean(x * x, axis=-1, keepdims=True)
    return x * lax.rsqrt(ms + EPS) * g


def _head_mean_matrix(width, head):
    r = lax.broadcasted_iota(jnp.int32, (width, width), 0) // head
    c = lax.broadcasted_iota(jnp.int32, (width, width), 1) // head
    return jnp.where(r == c, 1.0 / head, 0.0).astype(BF16)


def _head_rms(x, g, avg):
    ms = _dot((x * x).astype(BF16), avg)
    return x * lax.rsqrt(ms + EPS) * g


def _full(shape):
    nd = len(shape)
    return pl.BlockSpec(shape, lambda *_: (0,) * nd, pipeline_mode=pl.Buffered(1))


def _params(*sem):
    return pltpu.CompilerParams(dimension_semantics=sem, vmem_limit_bytes=VMEM_LIMIT)


_IN_OFF = tuple(int(sum(IN_SIZES[:i])) for i in range(len(IN_SIZES) + 1))
_HALVED_GROUPS = (3, 8, 10, 11, 12, 13)
D_IN = _IN_OFF[-1]


def _col_table(groups):
    table, off = {}, 0
    for name, width in groups:
        table[name] = (off, width)
        off += width
    return table, off


_ACT_COLS, ACT_W = _col_table((("qa", W_A), ("ka", W_A), ("va", W_A), ("sza", W_A), ("out_b", W_BV),
                               ("qm", W_M), ("szm", W_M)))
_GATE_COLS, GATE_W = _col_table((("a", D_MODEL), ("b", D_MODEL), ("m", D_MODEL)))


def _front_kernel(x_ref, nin_ref, w_ref, wg2_ref, bg_ref, gqa_ref, gka_ref, gqm_ref, *rest, per_stream=None):
    fused = per_stream is not None
    if fused:
        (ggla_ref, s0_ref, kat_o, vat_o, sfin_o, acts_o, gates_o, s_scr, ka_o, va_o) = rest
        cols = lambda ref, name, table: ref.at[:, table[name][0]:table[name][0] + table[name][1]]
        qa_o, ka16_o, va16_o, sza_o, outb_o, qm_o, szm_o = (cols(acts_o, k, _ACT_COLS) for k in _ACT_COLS)
        ga_o, gb_o, gm_o = (cols(gates_o, k, _GATE_COLS) for k in _GATE_COLS)
    else:
        (qa_o, ka_o, va_o, ka16_o, va16_o, sza_o, qb_o, kb_o, vb_o, la_o, szb_o, qm_o, szm_o,
         ga_o, gb_o, gm_o) = rest
    if fused:
        @pl.when(pl.program_id(0) % per_stream == 0)
        def _():
            s_scr[...] = s0_ref[pl.program_id(0) // per_stream].reshape(W_BK, DV_B)

    h = _rms_rows(x_ref[...], nin_ref[...]).astype(BF16)

    def proj(i):
        return _dot_nt(h, w_ref[_IN_OFF[i]:_IN_OFF[i + 1], :])

    qb = (proj(4) * (DK_B ** -0.5)).astype(BF16)
    kb = proj(5).astype(BF16)
    vb = proj(6).astype(BF16)
    z = _dot(proj(7).astype(BF16), wg2_ref[...]) + bg_ref[...]
    la = _log_sigmoid(z) * (1.0 / GATE_TAU)
    szb = _silu_of_twice(proj(8)).astype(BF16)

    scale = HEAD_DIM ** -0.5 * LOG2E
    avg_a = _head_mean_matrix(W_A, HEAD_DIM)

    def emit_qa():
        qa_o[...] = (_head_rms(proj(0), gqa_ref[...], avg_a) * scale).astype(BF16)

    def emit_ka():
        ka = _head_rms(proj(1), gka_ref[...], avg_a)
        ka_o[...] = ka
        ka16_o[...] = ka.astype(BF16)

    def emit_va():
        va = proj(2)
        va_o[...] = va
        va16_o[...] = va.astype(BF16)

    def emit_sza():
        sza_o[...] = _silu_of_twice(proj(3)).astype(BF16)

    def emit_qm():
        qm_o[...] = (_head_rms(proj(9), gqm_ref[...], _head_mean_matrix(W_M, HEAD_DIM)) * scale).astype(BF16)

    def emit_szm():
        szm_o[...] = _silu_of_twice(proj(10)).astype(BF16)

    def emit_gate(i, ref):
        ref[...] = _sigmoid_of_twice(proj(i)).astype(BF16)

    pending = [emit_qa, emit_ka, emit_va, emit_sza, emit_qm, emit_szm,
               functools.partial(emit_gate, 11, ga_o), functools.partial(emit_gate, 12, gb_o),
               functools.partial(emit_gate, 13, gm_o)]

    def emit_next():
        if pending:
            pending.pop(0)()

    if fused:
        chunks = [slice(r, r + GLA_C) for r in range(0, x_ref.shape[0], GLA_C)]
        items = [(qb[rows].astype(F32), kb[rows].astype(F32), vb[rows].astype(F32), vb[rows], la[rows])
                 for rows in chunks]
        state = s_scr[...]
        outs = []
        def emit_some():
            for _ in range(PROJ_PER_LEVEL):
                emit_next()

        for rows, item, (o, cum) in zip(chunks, items, _gla_intra(items, GLA_C, emit_some)):
            inter, state = _gla_state_step(item[0], item[1], item[3], cum, state)
            outs.append(_gla_finish(o + inter, ggla_ref, szb[rows]))
            emit_next()
        s_scr[...] = state
        outb_o[...] = jnp.concatenate(outs, axis=0)
    else:
        qb_o[...] = qb
        kb_o[...] = kb
        vb_o[...] = vb
        la_o[...] = la
        szb_o[...] = szb
    while pending:
        emit_next()

    if fused:
        @pl.when(pl.program_id(0) % per_stream == per_stream - 1)
        def _():
            sfin_o[0] = s_scr[...].reshape(H_B, DK_B, DV_B)
            kat_o[0] = ka_o[...].T
            vat_o[0] = va_o[...].T


def _front(x2d, wts, tm, t, keep, s0=None):
    n = x2d.shape[0]
    tm = min(tm, n)
    row = lambda w: pl.BlockSpec((tm, w), lambda i: (i, 0))
    out = lambda rows_, w, d, spec: (jax.ShapeDtypeStruct((rows_, w), d), spec)
    tail = [out(n, W_M, BF16, row(W_M)), out(n, W_M, BF16, row(W_M))] + [out(n, D_MODEL, BF16, row(D_MODEL))] * 3
    in_specs = [row(D_MODEL), _full((1, D_MODEL)), _full((D_IN, D_MODEL)),
                _full((GATE_RANK, W_BK)), _full((1, W_BK)), _full((1, W_A)), _full((1, W_A)), _full((1, W_M))]
    args = [x2d, wts["norm_in"], wts["w_in_t"], wts["w_gate2"], wts["b_gate"], wts["g_qa"], wts["g_ka"], wts["g_qm"]]
    if s0 is None:
        assert keep == t
        kv = [out(n, W_A, F32, row(W_A))] * 2
        mid = [out(n, W_BK, BF16, row(W_BK)), out(n, W_BK, BF16, row(W_BK)), out(n, W_BV, BF16, row(W_BV)),
               out(n, W_BK, F32, row(W_BK)), out(n, W_BV, BF16, row(W_BV))]
        body, scratch = _front_kernel, []
    else:
        assert t % tm == 0 and tm % GLA_C == 0 and keep == tm
        streams, per_stream = n // t, t // tm
        state = (streams, H_B, DK_B, DV_B)
        kv = [(jax.ShapeDtypeStruct((streams, W_A, keep), F32),
               pl.BlockSpec((1, W_A, keep), lambda i: (i // per_stream, 0, 0)))] * 2
        outs = kv + [(jax.ShapeDtypeStruct(state, F32),
                      pl.BlockSpec((1,) + state[1:], lambda i: (i // per_stream, 0, 0, 0))),
                     out(n, ACT_W, BF16, row(ACT_W)), out(n, GATE_W, BF16, row(GATE_W))]
        in_specs += [_full((1, W_BV)), _full(state)]
        args += [wts["g_gla"], s0]
        body = functools.partial(_front_kernel, per_stream=per_stream)
        scratch = [pltpu.VMEM((W_BK, DV_B), F32), pltpu.VMEM((tm, W_A), F32), pltpu.VMEM((tm, W_A), F32)]
    if s0 is None:
        outs = [out(n, W_A, BF16, row(W_A))] + kv + [out(n, W_A, BF16, row(W_A))] * 3 + mid + tail
    return pl.pallas_call(
        body,
        grid=(n // tm,),
        in_specs=in_specs,
        out_specs=[spec for _, spec in outs],
        out_shape=[shape for shape, _ in outs],
        scratch_shapes=scratch,
        compiler_params=_params("arbitrary"),
        name="front",
    )(*args)


def _back_kernel(x_ref, a_ref, b_ref, m_ref, ga_ref, gb_ref, gm_ref, wa_ref, wb_ref, wm_ref, wo_ref, y_ref):
    u = (ga_ref[...].astype(F32) * _dot(a_ref[...], wa_ref[...])
         + gb_ref[...].astype(F32) * _dot(b_ref[...], wb_ref[...])
         + gm_ref[...].astype(F32) * _dot(m_ref[...], wm_ref[...]))
    y_ref[...] = x_ref[...] + _dot(u.astype(BF16), wo_ref[...])


def _back(x2d, a, b, m, ga, gb, gm, wts, tm):
    n = x2d.shape[0]
    tm = min(tm, n)
    row = lambda w: pl.BlockSpec((tm, w), lambda i: (i, 0))
    return pl.pallas_call(
        _back_kernel,
        grid=(n // tm,),
        in_specs=[row(D_MODEL), row(W_A), row(W_BV), row(W_M), row(D_MODEL), row(D_MODEL), row(D_MODEL),
                  _full((W_A, D_MODEL)), _full((W_BV, D_MODEL)), _full((W_M, D_MODEL)),
                  _full((D_MODEL, D_MODEL))],
        out_specs=row(D_MODEL),
        out_shape=jax.ShapeDtypeStruct((n, D_MODEL), F32),
        compiler_params=_params("parallel"),
        name="back",
    )(x2d, a, b, m, ga, gb, gm, wts["w_up_a"], wts["w_up_b"], wts["w_up_m"], wts["w_out"])


def _bias_kernel(e_ref, tile_ref):
    e = jnp.broadcast_to(e_ref[0] * LOG2E, (CHUNK, 2 * BAND))
    tile_ref[0] = pltpu.roll(e, 0, 1, stride=1, stride_axis=0)[:, :BIAS_W]


def _bias_tile(rel_bias):
    j = jnp.arange(2 * BAND)
    dist = jnp.where(j < BIAS_W, BAND - j, REL_CLIP)
    e = rel_bias[:, jnp.clip(dist, -REL_CLIP, REL_CLIP) + REL_CLIP].reshape(H_A, 1, 2 * BAND)
    return pl.pallas_call(
        _bias_kernel,
        grid=(H_A,),
        in_specs=[pl.BlockSpec((1, 1, 2 * BAND), lambda h: (h, 0, 0))],
        out_specs=pl.BlockSpec((1, CHUNK, BIAS_W), lambda h: (h, 0, 0)),
        out_shape=jax.ShapeDtypeStruct((H_A, CHUNK, BIAS_W), F32),
        compiler_params=_params("parallel"),
        name="bias_tile",
    )(e)


def _softmax_pv_staged(scores, values):
    biased = [[s if b is None else s + b for s, b in s_list] for s_list in scores]
    tops = [functools.reduce(jnp.maximum, [jnp.max(s, axis=-1, keepdims=True) for s in s_list])
            for s_list in biased]
    probs = [[jnp.exp2(s - m) for s in s_list] for s_list, m in zip(biased, tops)]
    sums = [functools.reduce(jnp.add, [jnp.sum(x, axis=-1, keepdims=True) for x in p_list]) for p_list in probs]
    return [functools.reduce(jnp.add, [(v(x.astype(BF16)) if callable(v) else _dot(x.astype(BF16), v))
                                       for x, v in zip(p_list, v_list)]) / l
            for p_list, v_list, l in zip(probs, values, sums)]


def _head_pairs_attention(q_pairs, k_lists, v_lists, bias_fn):
    m_rows = q_pairs[0].shape[0]
    lane = lax.broadcasted_iota(jnp.int32, (m_rows, LANES), 1)
    first = lane < HEAD_DIM
    scores = []
    for p, qp in enumerate(q_pairs):
        zero = jnp.zeros_like(qp)
        q2 = jnp.concatenate([jnp.where(first, qp, zero), jnp.where(first, zero, qp)], axis=0)
        s_list = []
        for i, k in enumerate(k_lists[p]):
            b0, b1 = bias_fn(p, 0, i), bias_fn(p, 1, i)
            s_list.append((_dot_nt(q2, k), None if b0 is None else jnp.concatenate([b0, b1], axis=0)))
        scores.append(s_list)
    outs = _softmax_pv_staged(scores, v_lists)
    return [jnp.where(first, o[:m_rows], o[m_rows:]) for o in outs]


def _attn_back_kernel(q_ref, *refs):
    nk = BAND // QBLK + 1
    k_refs, v_refs = refs[:nk], refs[nk:2 * nk]
    (sz_ref, bias_ref, qm_ref, mk_ref, mv_ref, szm_ref, x_ref, b_ref, ga_ref, gb_ref, gm_ref,
     wa_ref, wb_ref, wm_ref, wo_ref, y_ref, k_buf, v_buf, bias_buf, a_buf, m_buf) = refs[2 * nk:]
    i = pl.program_id(1)
    for j, (k_ref, v_ref) in enumerate(zip(k_refs, v_refs)):
        k_buf[j * QBLK:(j + 1) * QBLK, :] = k_ref[0]
        v_buf[j * QBLK:(j + 1) * QBLK, :] = v_ref[0]

    @pl.when(i <= BAND // QBLK)
    def _():
        col = lax.broadcasted_iota(jnp.int32, (1, BIAS_W), 1)
        for c in range(QBLK // CHUNK):
            before_start = jnp.where(col + c * CHUNK < (BAND // QBLK - i) * QBLK, NEG, 0.0)
            for h in range(H_A):
                bias_buf[c, h] = bias_ref[h] + before_start

    lanes_m = [slice(p * LANES, (p + 1) * LANES) for p in range(H_M // 2)]
    mk = [[mk_ref[0, :, lanes].astype(BF16)] for lanes in lanes_m]
    mv = [[mv_ref[0, :, lanes].astype(BF16)] for lanes in lanes_m]
    blocks = [(slice(r, r + MEM_ROWS), p) for r in range(0, QBLK, MEM_ROWS) for p in range(len(lanes_m))]
    outs = _head_pairs_attention([qm_ref[0, rows, lanes_m[p]] for rows, p in blocks],
                                 [mk[p] for _, p in blocks], [mv[p] for _, p in blocks], lambda n, e, _: None)
    for (rows, p), o in zip(blocks, outs):
        m_buf[rows, lanes_m[p]] = (o * szm_ref[0, rows, lanes_m[p]].astype(F32)).astype(BF16)

    pairs = [slice(p * LANES, (p + 1) * LANES) for p in range(H_A // 2)]
    for c in range(QBLK // CHUNK):
        rows = slice(c * CHUNK, (c + 1) * CHUNK)
        win = slice(c * CHUNK, c * CHUNK + WINDOW)
        outs = _head_pairs_attention(
            [q_ref[0, rows, lanes] for lanes in pairs],
            [[k_buf[win, lanes]] for lanes in pairs], [[v_buf[win, lanes]] for lanes in pairs],
            lambda p, e, _: bias_buf[c, 2 * p + e, :, :WINDOW])
        for lanes, o in zip(pairs, outs):
            a_buf[rows, lanes] = (o * sz_ref[0, rows, lanes].astype(F32)).astype(BF16)

    u = (ga_ref[0].astype(F32) * _dot(a_buf[...], wa_ref[...])
         + gb_ref[0].astype(F32) * _dot(b_ref[0], wb_ref[...])
         + gm_ref[0].astype(F32) * _dot(m_buf[...], wm_ref[...]))
    y_ref[0] = x_ref[0] + _dot(u.astype(BF16), wo_ref[...])


def _attn_back(acts, gates, bias, mk, mv, x, wts):
    b, t, _ = x.shape
    nb = BAND // QBLK
    n = t // QBLK
    assert BAND % QBLK == 0 and t % QBLK == 0 and QBLK % MEM_ROWS == 0

    def group(table, name, d=0):
        off, w = table[name]
        assert off % w == 0
        return pl.BlockSpec((1, QBLK, w), lambda bi, i: (bi, jnp.maximum(i - d, 0), off // w))

    act = functools.partial(group, _ACT_COLS)
    gate = functools.partial(group, _GATE_COLS)
    mem = pl.BlockSpec((1, N_MEM, W_M), lambda bi, i: (bi, 0, 0))
    return pl.pallas_call(
        _attn_back_kernel,
        grid=(b, n),
        in_specs=[act("qa")] + [act("ka", nb - j) for j in range(nb + 1)] + [act("va", nb - j) for j in range(nb + 1)]
                 + [act("sza"), _full((H_A, CHUNK, BIAS_W)), act("qm"), mem, mem, act("szm"),
                    pl.BlockSpec((1, QBLK, D_MODEL), lambda bi, i: (bi, i, 0)), act("out_b"),
                    gate("a"), gate("b"), gate("m"),
                    _full((W_A, D_MODEL)), _full((W_BV, D_MODEL)), _full((W_M, D_MODEL)),
                    _full((D_MODEL, D_MODEL))],
        out_specs=pl.BlockSpec((1, QBLK, D_MODEL), lambda bi, i: (bi, i, 0)),
        out_shape=jax.ShapeDtypeStruct((b, t, D_MODEL), F32),
        scratch_shapes=[pltpu.VMEM((KSPAN, W_A), BF16), pltpu.VMEM((KSPAN, W_A), BF16),
                        pltpu.VMEM((QBLK // CHUNK, H_A, CHUNK, BIAS_W), F32),
                        pltpu.VMEM((QBLK, W_A), BF16), pltpu.VMEM((QBLK, W_M), BF16)],
        compiler_params=_params("parallel", "arbitrary"),
        name="attn_back",
    )(*([acts] * (2 * nb + 4)), bias, acts, mk, mv, acts, x, acts, gates, gates, gates,
      wts["w_up_a"], wts["w_up_b"], wts["w_up_m"], wts["w_out"])


STEP_STREAMS = 8


def _attn_step_kernel(q_ref, kn_ref, vn_ref, kp_ref, vp_ref, sz_ref, bias_ref, o_ref):
    s_len = q_ref.shape[1]
    for b in range(STEP_STREAMS):
        scores, values = [], []
        for h in range(H_A):
            lanes = slice(h * HEAD_DIM, (h + 1) * HEAD_DIM)
            q = q_ref[b, :, lanes]
            scores.append([(_dot(q, kp_ref[b, h].astype(BF16)), bias_ref[h, :s_len, :BAND]),
                           (_dot_nt(q, kn_ref[b, :, lanes]), bias_ref[h, :s_len, BAND:BAND + s_len])])
            values.append([functools.partial(_dot_nt, b=vp_ref[b, h].astype(BF16)), vn_ref[b, :, lanes]])
        o = jnp.concatenate(_softmax_pv_staged(scores, values), axis=-1)
        o_ref[b] = (o * sz_ref[b].astype(F32)).astype(BF16)


def _attn_step(q, k_new, v_new, k_past_t, v_past_t, sz, bias):
    b, s_len, _ = q.shape
    assert k_past_t.shape[1:] == (H_A, HEAD_DIM, BAND) and s_len <= CHUNK and b % STEP_STREAMS == 0
    new = pl.BlockSpec((STEP_STREAMS, s_len, W_A), lambda i: (i, 0, 0))
    past = pl.BlockSpec((STEP_STREAMS, H_A, HEAD_DIM, BAND), lambda i: (i, 0, 0, 0))
    return pl.pallas_call(
        _attn_step_kernel,
        grid=(b // STEP_STREAMS,),
        in_specs=[new, new, new, past, past, new, _full((H_A, CHUNK, BIAS_W))],
        out_specs=new,
        out_shape=jax.ShapeDtypeStruct((b, s_len, W_A), BF16),
        compiler_params=_params("parallel"),
        name="attn_step",
    )(q, k_new, v_new, k_past_t, v_past_t, sz, bias)


def _mem_kv_kernel(mem_ref, gmem_ref, w_ref, gkm_ref, mk_ref, mv_ref):
    h = _rms_rows(mem_ref[0], gmem_ref[...]).astype(BF16)
    kv = _dot(h, w_ref[...])
    mk_ref[0] = _head_rms(kv[:, :W_M], gkm_ref[...], _head_mean_matrix(W_M, HEAD_DIM))
    mv_ref[0] = kv[:, W_M:]


def _mem_kv(mem, wts):
    b = mem.shape[0]
    out = pl.BlockSpec((1, N_MEM, W_M), lambda i: (i, 0, 0))
    return pl.pallas_call(
        _mem_kv_kernel,
        grid=(b,),
        in_specs=[pl.BlockSpec((1, N_MEM, D_MODEL), lambda i: (i, 0, 0)), _full((1, D_MODEL)),
                  _full((D_MODEL, 2 * W_M)), _full((1, W_M))],
        out_specs=[out, out],
        out_shape=[jax.ShapeDtypeStruct((b, N_MEM, W_M), F32)] * 2,
        compiler_params=_params("parallel"),
        name="mem_kv",
    )(mem, wts["g_mem"], wts["w_mem_kv"], wts["g_km"])


MEM_ROWS = 128
MEM_STREAMS = 16


def _attn_mem_step_kernel(q_ref, mk_ref, mv_ref, sz_ref, o_ref):
    for b in range(q_ref.shape[0]):
        scores, values = [], []
        for h in range(H_M):
            lanes = slice(h * HEAD_DIM, (h + 1) * HEAD_DIM)
            scores.append([(_dot(q_ref[b, :, lanes], mk_ref[b, h].astype(BF16)), None)])
            values.append([functools.partial(_dot_nt, b=mv_ref[b, h].astype(BF16))])
        o = jnp.concatenate(_softmax_pv_staged(scores, values), axis=-1)
        o_ref[b] = (o * sz_ref[b].astype(F32)).astype(BF16)


def _attn_mem_step(q, mk_t, mv_t, sz):
    b, t, _ = q.shape
    nb = min(b, MEM_STREAMS)
    assert b % nb == 0 and mk_t.shape[1:] == (H_M, HEAD_DIM, N_MEM)
    rows = pl.BlockSpec((nb, t, W_M), lambda i: (i, 0, 0))
    mem = pl.BlockSpec((nb, H_M, HEAD_DIM, N_MEM), lambda i: (i, 0, 0, 0))
    return pl.pallas_call(
        _attn_mem_step_kernel,
        grid=(b // nb,),
        in_specs=[rows, mem, mem, rows],
        out_specs=rows,
        out_shape=jax.ShapeDtypeStruct((b, t, W_M), BF16),
        compiler_params=_params("parallel"),
        name="attn_mem_step",
    )(q, mk_t, mv_t, sz)


def _gla_constants(c, group):
    r = lax.broadcasted_iota(jnp.int32, (c, c), 0)
    s = lax.broadcasted_iota(jnp.int32, (c, c), 1)
    causal = (s <= r) & (r // group == s // group)
    head_of_k = lax.broadcasted_iota(jnp.int32, (W_BK, W_BV), 0) // DK_B
    head_of_v = lax.broadcasted_iota(jnp.int32, (W_BK, W_BV), 1) // DV_B
    head_of_row = lax.broadcasted_iota(jnp.int32, (H_B, W_BK, c), 1) // DK_B
    head = lax.broadcasted_iota(jnp.int32, (H_B, W_BK, c), 0)
    return causal.astype(BF16), (head_of_k == head_of_v).astype(BF16), (head_of_row == head).astype(BF16)


def _gla_intra(items, group, between=lambda: None):
    n = len(items)
    qs, ks, vs, v16s = ([it[j] for it in items] for j in range(4))
    c = qs[0].shape[0]
    las = [it[4] * LOG2E for it in items]
    causal16, spread, head_rows = _gla_constants(c, group)
    head_rows = [head_rows[h] for h in range(H_B)]
    cums = [_dot_exact_lhs(causal16, la) for la in las]
    cumxs = [cum - la for cum, la in zip(cums, las)]
    r_ck = lax.broadcasted_iota(jnp.int32, (c, W_BK), 0)

    os = [_dot((q * k).astype(BF16), spread) * v for q, k, v in zip(qs, ks, vs)]

    def shifted(x, j):
        return pltpu.roll(x.reshape(c // SUBLANES, SUBLANES, W_BK), j % SUBLANES, 1).reshape(c, W_BK)

    def block_edges(cum, cumx, half):
        if half >= SUBLANES:
            nb = c // half
            first = jnp.broadcast_to(cumx.reshape(nb, half, W_BK)[:, 0:1, :], (nb, half, W_BK))
            last = jnp.broadcast_to(cum.reshape(nb, half, W_BK)[:, half - 1:half, :], (nb, half, W_BK))
            return first.reshape(c, W_BK), last.reshape(c, W_BK)
        r_in = r_ck % half
        first, last = cumx, cum
        for j in range(1, half):
            first = jnp.where(r_in == j, shifted(cumx, j), first)
            last = jnp.where(r_in == half - 1 - j, shifted(cum, -j), last)
        return first, last

    t_xor_s = lax.broadcasted_iota(jnp.int32, (c, H_B * c), 0) ^ (lax.broadcasted_iota(jnp.int32, (c, H_B * c), 1) % c)
    atts = [jnp.zeros((c, H_B * c), F32) for _ in range(n)]
    half = 1
    while half < group:
        edges = [block_edges(cum, cumx, half) for cum, cumx in zip(cums, cumxs)]
        odd = (r_ck & half) != 0
        qts = [jnp.where(odd, q * jnp.exp2(cum - first), 0.0).astype(BF16)
               for q, cum, (first, _) in zip(qs, cums, edges)]
        kts = [jnp.where(odd, 0.0, k * jnp.exp2(last - cum)).T.astype(BF16)
               for k, cum, (_, last) in zip(ks, cums, edges)]
        prods = [_dot(qt, jnp.concatenate([kt * m for m in head_rows], axis=1)) for qt, kt in zip(qts, kts)]
        if 2 * half < c:
            prods = [jnp.where(t_xor_s < 2 * half, a, 0.0) for a in prods]
        atts = [att + a for att, a in zip(atts, prods)]
        between()
        half *= 2
    if group > 1:
        att16s = [att.astype(BF16) for att in atts]
        os = [o + jnp.concatenate([_dot(att16[:, h * c:(h + 1) * c], v16[:, h * DV_B:(h + 1) * DV_B])
                                   for h in range(H_B)], axis=-1)
              for o, att16, v16 in zip(os, att16s, v16s)]
    return list(zip(os, cums))


def _gla_state_step(q, k, v16, cum, s_old):
    c = q.shape[0]
    q_in = (q * jnp.exp2(cum)).astype(BF16)
    k_out_t = (k * jnp.exp2(cum[c - 1:c, :] - cum)).T.astype(BF16)
    keep = jnp.exp2(cum.T[:, c - 1:c])
    kd = lambda h: slice(h * DK_B, (h + 1) * DK_B)
    vd = lambda h: slice(h * DV_B, (h + 1) * DV_B)
    o = jnp.concatenate([_dot(q_in[:, kd(h)], s_old[kd(h)].astype(BF16)) for h in range(H_B)], axis=-1)
    s_new = jnp.concatenate([s_old[kd(h)] * keep[kd(h)] + _dot(k_out_t[kd(h)], v16[:, vd(h)])
                             for h in range(H_B)], axis=0)
    return o, s_new


def _gla_finish(o, g_ref, sz):
    outs = [_rms_rows(o[:, h * DV_B:(h + 1) * DV_B], g_ref[:, h * DV_B:(h + 1) * DV_B]) for h in range(H_B)]
    return (jnp.concatenate(outs, axis=-1) * sz.astype(F32)).astype(BF16)


def _gla_step_kernel(q_ref, k_ref, v_ref, la_ref, sz_ref, g_ref, s0_ref, o_ref, s_ref, *, t):
    c = GLA_C
    n = c // t
    q, k, la = q_ref[...].astype(F32), k_ref[...].astype(F32), la_ref[...]
    v16 = v_ref[...]
    (o, cum), = _gla_intra([(q, k, v16.astype(F32), v16, la)], t)

    q_in = (q * jnp.exp2(cum)).astype(BF16)
    cum_t, k_t = cum.T, k.T
    stream_of_col = lax.broadcasted_iota(jnp.int32, (W_BK, c), 1) // t
    inter = []
    for s in range(n):
        rows = slice(s * t, (s + 1) * t)
        last = cum_t[:, (s + 1) * t - 1:(s + 1) * t]
        k_out_t = (k_t * jnp.exp2(jnp.where(stream_of_col == s, last - cum_t, NEG))).astype(BF16)
        keep = jnp.exp2(last)
        o_s = []
        for h in range(H_B):
            ks = slice(h * DK_B, (h + 1) * DK_B)
            vs = slice(h * DV_B, (h + 1) * DV_B)
            s_old = s0_ref[s, h]
            o_s.append(_dot(q_in[rows, ks], s_old.astype(BF16)))
            s_ref[s, h] = s_old * keep[ks] + _dot(k_out_t[ks], v16[:, vs])
        inter.append(jnp.concatenate(o_s, axis=-1))
    o_ref[...] = _gla_finish(o + jnp.concatenate(inter, axis=0), g_ref, sz_ref[...])


def _gla_step(q, k, v, la, sz, g, s0):
    b, t, _ = q.shape
    n = GLA_C // t
    assert GLA_C % t == 0 and t % (2 * SUBLANES) == 0 and b % n == 0
    flat = lambda z: z.reshape(b * t, z.shape[-1])
    blk = lambda w: pl.BlockSpec((GLA_C, w), lambda i: (i, 0))
    state = pl.BlockSpec((n, H_B, DK_B, DV_B), lambda i: (i, 0, 0, 0))
    o, s_new = pl.pallas_call(
        functools.partial(_gla_step_kernel, t=t),
        grid=(b // n,),
        in_specs=[blk(W_BK), blk(W_BK), blk(W_BV), blk(W_BK), blk(W_BV),
                  pl.BlockSpec((1, W_BV), lambda i: (0, 0)), state],
        out_specs=[blk(W_BV), state],
        out_shape=[jax.ShapeDtypeStruct((b * t, W_BV), BF16),
                   jax.ShapeDtypeStruct((b, H_B, DK_B, DV_B), F32)],
        compiler_params=_params("parallel"),
        name="gla_step",
    )(flat(q), flat(k), flat(v), flat(la), flat(sz), g, s0)
    return o.reshape(b, t, W_BV), s_new


def _prep_weights(l, norm_in, w_in, g_qa, g_ka, w_gate2, b_gate, g_gla_out, g_mem, w_mem_kv, g_qm, g_km,
                  w_up_a, w_up_b, w_up_m, w_out):
    tile = lambda gain, n: jnp.tile(gain, n).reshape(1, -1)
    halved = jnp.concatenate([jnp.full((size,), 0.5 if i in _HALVED_GROUPS else 1.0, F32)
                              for i, size in enumerate(IN_SIZES)])
    return {
        "norm_in": norm_in[l].reshape(1, D_MODEL),
        "w_in_t": (w_in[l].T * halved[:, None]).astype(BF16),
        "w_gate2": w_gate2[l].astype(BF16),
        "b_gate": b_gate[l].reshape(1, W_BK),
        "g_qa": tile(g_qa[l], H_A), "g_ka": tile(g_ka[l], H_A), "g_qm": tile(g_qm[l], H_M),
        "g_km": tile(g_km[l], H_M), "g_gla": tile(g_gla_out[l], H_B),
        "g_mem": g_mem[l].reshape(1, D_MODEL),
        "w_mem_kv": w_mem_kv[l].astype(BF16),
        "w_up_a": w_up_a[l].astype(BF16), "w_up_b": w_up_b[l].astype(BF16),
        "w_up_m": w_up_m[l].astype(BF16), "w_out": w_out[l].astype(BF16),
    }


def _layer_long(x, wts, bias, mk, mv, s0):
    b, t, _ = x.shape
    keep = min(BAND, t)
    r3 = lambda z: z.reshape(b, t, z.shape[-1])
    heads = lambda z: jnp.transpose(z.reshape(b, H_A, HEAD_DIM, keep), (0, 3, 1, 2))
    ka, va, s_new, acts, gates = _front(x.reshape(b * t, D_MODEL), wts, keep, t, keep, s0)
    y = _attn_back(r3(acts), r3(gates), bias, mk, mv, x, wts)
    return y, heads(ka), heads(va), s_new


def _layer_short(x, wts, attend_a, attend_m, s0, tm):
    b, t, _ = x.shape
    keep = min(BAND, t)
    x2d = x.reshape(b * t, D_MODEL)
    r3 = lambda z: z.reshape(b, t, z.shape[-1])
    flat = lambda z: z.reshape(b * t, z.shape[-1])
    heads = lambda z: z.reshape(b, keep, H_A, HEAD_DIM)
    (qa, ka, va, ka16, va16, sza, qb, kb, vb, la, szb, qm, szm, ga, gb, gm) = _front(x2d, wts, tm, t, keep)
    out_b, s_new = _gla_step(r3(qb), r3(kb), r3(vb), r3(la), r3(szb), wts["g_gla"], s0)
    out_a = attend_a(r3(qa), r3(ka16), r3(va16), r3(sza))
    out_m = attend_m(r3(qm), r3(szm))
    y = _back(x2d, flat(out_a), flat(out_b), flat(out_m), ga, gb, gm, wts, tm)
    return y.reshape(b, t, D_MODEL), heads(ka), heads(va), s_new


def kernel(x_prompt, x_sample, mem_prompt, cache_a_k, cache_a_v, state_gla, cache_mem_k, cache_mem_v,
           norm_in, w_in, g_qa, g_ka, rel_bias, w_gate2, b_gate, g_gla_out, g_mem, w_mem_kv, g_qm, g_km,
           w_up_a, w_up_b, w_up_m, w_out):
    depth = w_in.shape[0]
    xp, xs = x_prompt, x_sample
    bp, tp, _ = xp.shape
    bs, ts, _ = xs.shape
    akp, avp, sgp, mkp, mvp, aks, avs, sgs = [], [], [], [], [], [], [], []
    for l in range(depth):
        wts = _prep_weights(l, norm_in, w_in, g_qa, g_ka, w_gate2, b_gate, g_gla_out, g_mem, w_mem_kv,
                            g_qm, g_km, w_up_a, w_up_b, w_up_m, w_out)
        bias = _bias_tile(rel_bias[l])
        mk, mv = _mem_kv(mem_prompt, wts)
        s0 = jnp.zeros((bp, H_B, DK_B, DV_B), state_gla.dtype)
        xp, ka, va, sp = _layer_long(xp, wts, bias, mk, mv, s0)
        akp.append(ka)
        avp.append(va)
        sgp.append(sp)
        mkp.append(mk.reshape(bp, N_MEM, H_M, HEAD_DIM))
        mvp.append(mv.reshape(bp, N_MEM, H_M, HEAD_DIM))
        rows_last = lambda z: jnp.transpose(z, (0, 2, 3, 1))
        past_k, past_v = rows_last(cache_a_k[l]), rows_last(cache_a_v[l])
        mem_k, mem_v = rows_last(cache_mem_k[l]), rows_last(cache_mem_v[l])
        attend_s = lambda q, k, v, sz: _attn_step(q, k, v, past_k, past_v, sz, bias)
        attend_ms = lambda q, sz: _attn_mem_step(q, mem_k, mem_v, sz)
        xs, ka_s, va_s, ss = _layer_short(xs, wts, attend_s, attend_ms, state_gla[l], 512)
        aks.append(ka_s)
        avs.append(va_s)
        sgs.append(ss)
    return (xp, xs, jnp.stack(akp), jnp.stack(avp), jnp.stack(sgp), jnp.stack(mkp), jnp.stack(mvp),
            jnp.stack(aks), jnp.stack(avs), jnp.stack(sgs))
```

```python
import functools

import jax
import jax.numpy as jnp
from jax import lax
from jax.experimental import pallas as pl
from jax.experimental.pallas import tpu as pltpu

F32 = jnp.float32
BF16 = jnp.bfloat16

D_MODEL = 1024
CHUNK = 64
LEFT_CHUNKS = 8
HEAD_DIM = 64
H_A = 8
W_A = H_A * HEAD_DIM
REL_CLIP = 128
H_B = 4
DK_B = 64
DV_B = 128
W_BK = H_B * DK_B
W_BV = H_B * DV_B
GATE_RANK = 16
GATE_TAU = 16.0
N_MEM = 256
H_M = 4
W_M = H_M * HEAD_DIM
EPS = 1e-6
IN_SIZES = (W_A, W_A, W_A, W_A, W_BK, W_BK, W_BV, GATE_RANK, W_BV, W_M, W_M, D_MODEL, D_MODEL, D_MODEL)

LANES = 128
SUBLANES = 8
VMEM_LIMIT = 56 * 1024 * 1024
NEG = -1e30
LOG2E = 1.4426950408889634

BAND = LEFT_CHUNKS * CHUNK
QBLK = 512
KSPAN = BAND + QBLK
WINDOW = BAND + CHUNK
BIAS_W = BAND + 2 * CHUNK
GLA_C = 128
PROJ_PER_LEVEL = 5


def _dot(a, b):
    return jnp.dot(a, b, preferred_element_type=F32)


def _dot_nt(a, b):
    return lax.dot_general(a, b, (((1,), (1,)), ((), ())), preferred_element_type=F32)


def _split3(x):
    hi = x.astype(BF16)
    r = x - hi.astype(F32)
    mid = r.astype(BF16)
    lo = (r - mid.astype(F32)).astype(BF16)
    return hi, mid, lo


def _dot_exact_lhs(a01, x):
    hi, mid, lo = _split3(x)
    return _dot(a01, hi) + _dot(a01, mid) + _dot(a01, lo)


def _sigmoid_of_twice(p):
    return 0.5 * jnp.tanh(p) + 0.5


def _silu_of_twice(p):
    return p * jnp.tanh(p) + p


def _log_sigmoid(z):
    return jnp.minimum(z, 0.0) - jnp.log1p(jnp.exp(-jnp.abs(z)))


def _rms_rows(x, g):
    ms = jnp.mean(x * x, axis=-1, keepdims=True)
    return x * lax.rsqrt(ms + EPS) * g


def _head_mean_matrix(width, head):
    r = lax.broadcasted_iota(jnp.int32, (width, width), 0) // head
    c = lax.broadcasted_iota(jnp.int32, (width, width), 1) // head
    return jnp.where(r == c, 1.0 / head, 0.0).astype(BF16)


def _head_rms(x, g, avg):
    ms = _dot((x * x).astype(BF16), avg)
    return x * lax.rsqrt(ms + EPS) * g


def _full(shape):
    nd = len(shape)
    return pl.BlockSpec(shape, lambda *_: (0,) * nd, pipeline_mode=pl.Buffered(1))


def _params(*sem):
    return pltpu.CompilerParams(dimension_semantics=sem, vmem_limit_bytes=VMEM_LIMIT)


_IN_OFF = tuple(int(sum(IN_SIZES[:i])) for i in range(len(IN_SIZES) + 1))
_HALVED_GROUPS = (3, 8, 10, 11, 12, 13)
D_IN = _IN_OFF[-1]


def _col_table(groups):
    table, off = {}, 0
    for name, width in groups:
        table[name] = (off, width)
        off += width
    return table, off


_ACT_COLS, ACT_W = _col_table((("qa", W_A), ("ka", W_A), ("va", W_A), ("sza", W_A), ("out_b", W_BV),
                               ("qm", W_M), ("szm", W_M)))
_GATE_COLS, GATE_W = _col_table((("a", D_MODEL), ("b", D_MODEL), ("m", D_MODEL)))


def _front_kernel(x_ref, nin_ref, w_ref, wg2_ref, bg_ref, gqa_ref, gka_ref, gqm_ref, *rest, per_stream=None):
    fused = per_stream is not None
    if fused:
        (ggla_ref, s0_ref, kat_o, vat_o, sfin_o, acts_o, gates_o, s_scr, ka_o, va_o) = rest
        cols = lambda ref, name, table: ref.at[:, table[name][0]:table[name][0] + table[name][1]]
        qa_o, ka16_o, va16_o, sza_o, outb_o, qm_o, szm_o = (cols(acts_o, k, _ACT_COLS) for k in _ACT_COLS)
        ga_o, gb_o, gm_o = (cols(gates_o, k, _GATE_COLS) for k in _GATE_COLS)
    else:
        (qa_o, ka_o, va_o, ka16_o, va16_o, sza_o, qb_o, kb_o, vb_o, la_o, szb_o, qm_o, szm_o,
         ga_o, gb_o, gm_o) = rest
    if fused:
        @pl.when(pl.program_id(0) % per_stream == 0)
        def _():
            s_scr[...] = s0_ref[pl.program_id(0) // per_stream].reshape(W_BK, DV_B)

    h = _rms_rows(x_ref[...], nin_ref[...]).astype(BF16)

    def proj(i):
        return _dot_nt(h, w_ref[_IN_OFF[i]:_IN_OFF[i + 1], :])

    qb = (proj(4) * (DK_B ** -0.5)).astype(BF16)
    kb = proj(5).astype(BF16)
    vb = proj(6).astype(BF16)
    z = _dot(proj(7).astype(BF16), wg2_ref[...]) + bg_ref[...]
    la = _log_sigmoid(z) * (1.0 / GATE_TAU)
    szb = _silu_of_twice(proj(8)).astype(BF16)

    scale = HEAD_DIM ** -0.5 * LOG2E
    avg_a = _head_mean_matrix(W_A, HEAD_DIM)

    def emit_qa():
        qa_o[...] = (_head_rms(proj(0), gqa_ref[...], avg_a) * scale).astype(BF16)

    def emit_ka():
        ka = _head_rms(proj(1), gka_ref[...], avg_a)
        ka_o[...] = ka
        ka16_o[...] = ka.astype(BF16)

    def emit_va():
        va = proj(2)
        va_o[...] = va
        va16_o[...] = va.astype(BF16)

    def emit_sza():
        sza_o[...] = _silu_of_twice(proj(3)).astype(BF16)

    def emit_qm():
        qm_o[...] = (_head_rms(proj(9), gqm_ref[...], _head_mean_matrix(W_M, HEAD_DIM)) * scale).astype(BF16)

    def emit_szm():
        szm_o[...] = _silu_of_twice(proj(10)).astype(BF16)

    def emit_gate(i, ref):
        ref[...] = _sigmoid_of_twice(proj(i)).astype(BF16)

    pending = [emit_qa, emit_ka, emit_va, emit_sza, emit_qm, emit_szm,
               functools.partial(emit_gate, 11, ga_o), functools.partial(emit_gate, 12, gb_o),
               functools.partial(emit_gate, 13, gm_o)]

    def emit_next():
        if pending:
            pending.pop(0)()

    if fused:
        chunks = [slice(r, r + GLA_C) for r in range(0, x_ref.shape[0], GLA_C)]
        items = [(qb[rows].astype(F32), kb[rows].astype(F32), vb[rows].astype(F32), vb[rows], la[rows])
                 for rows in chunks]
        state = s_scr[...]
        outs = []
        def emit_some():
            for _ in range(PROJ_PER_LEVEL):
                emit_next()

        for rows, item, (o, cum) in zip(chunks, items, _gla_intra(items, GLA_C, emit_some)):
            inter, state = _gla_state_step(item[0], item[1], item[3], cum, state)
            outs.append(_gla_finish(o + inter, ggla_ref, szb[rows]))
            emit_next()
        s_scr[...] = state
        outb_o[...] = jnp.concatenate(outs, axis=0)
    else:
        qb_o[...] = qb
        kb_o[...] = kb
        vb_o[...] = vb
        la_o[...] = la
        szb_o[...] = szb
    while pending:
        emit_next()

    if fused:
        @pl.when(pl.program_id(0) % per_stream == per_stream - 1)
        def _():
            sfin_o[0] = s_scr[...].reshape(H_B, DK_B, DV_B)
            kat_o[0] = ka_o[...].T
            vat_o[0] = va_o[...].T


def _front(x2d, wts, tm, t, keep, s0=None):
    n = x2d.shape[0]
    tm = min(tm, n)
    row = lambda w: pl.BlockSpec((tm, w), lambda i: (i, 0))
    out = lambda rows_, w, d, spec: (jax.ShapeDtypeStruct((rows_, w), d), spec)
    tail = [out(n, W_M, BF16, row(W_M)), out(n, W_M, BF16, row(W_M))] + [out(n, D_MODEL, BF16, row(D_MODEL))] * 3
    in_specs = [row(D_MODEL), _full((1, D_MODEL)), _full((D_IN, D_MODEL)),
                _full((GATE_RANK, W_BK)), _full((1, W_BK)), _full((1, W_A)), _full((1, W_A)), _full((1, W_M))]
    args = [x2d, wts["norm_in"], wts["w_in_t"], wts["w_gate2"], wts["b_gate"], wts["g_qa"], wts["g_ka"], wts["g_qm"]]
    if s0 is None:
        assert keep == t
        kv = [out(n, W_A, F32, row(W_A))] * 2
        mid = [out(n, W_BK, BF16, row(W_BK)), out(n, W_BK, BF16, row(W_BK)), out(n, W_BV, BF16, row(W_BV)),
               out(n, W_BK, F32, row(W_BK)), out(n, W_BV, BF16, row(W_BV))]
        body, scratch = _front_kernel, []
    else:
        assert t % tm == 0 and tm % GLA_C == 0 and keep == tm
        streams, per_stream = n // t, t // tm
        state = (streams, H_B, DK_B, DV_B)
        kv = [(jax.ShapeDtypeStruct((streams, W_A, keep), F32),
               pl.BlockSpec((1, W_A, keep), lambda i: (i // per_stream, 0, 0)))] * 2
        outs = kv + [(jax.ShapeDtypeStruct(state, F32),
                      pl.BlockSpec((1,) + state[1:], lambda i: (i // per_stream, 0, 0, 0))),
                     out(n, ACT_W, BF16, row(ACT_W)), out(n, GATE_W, BF16, row(GATE_W))]
        in_specs += [_full((1, W_BV)), _full(state)]
        args += [wts["g_gla"], s0]
        body = functools.partial(_front_kernel, per_stream=per_stream)
        scratch = [pltpu.VMEM((W_BK, DV_B), F32), pltpu.VMEM((tm, W_A), F32), pltpu.VMEM((tm, W_A), F32)]
    if s0 is None:
        outs = [out(n, W_A, BF16, row(W_A))] + kv + [out(n, W_A, BF16, row(W_A))] * 3 + mid + tail
    return pl.pallas_call(
        body,
        grid=(n // tm,),
        in_specs=in_specs,
        out_specs=[spec for _, spec in outs],
        out_shape=[shape for shape, _ in outs],
        scratch_shapes=scratch,
        compiler_params=_params("arbitrary"),
        name="front",
    )(*args)


def _back_kernel(x_ref, a_ref, b_ref, m_ref, ga_ref, gb_ref, gm_ref, wa_ref, wb_ref, wm_ref, wo_ref, y_ref):
    u = (ga_ref[...].astype(F32) * _dot(a_ref[...], wa_ref[...])
         + gb_ref[...].astype(F32) * _dot(b_ref[...], wb_ref[...])
         + gm_ref[...].astype(F32) * _dot(m_ref[...], wm_ref[...]))
    y_ref[...] = x_ref[...] + _dot(u.astype(BF16), wo_ref[...])


def _back(x2d, a, b, m, ga, gb, gm, wts, tm):
    n = x2d.shape[0]
    tm = min(tm, n)
    row = lambda w: pl.BlockSpec((tm, w), lambda i: (i, 0))
    return pl.pallas_call(
        _back_kernel,
        grid=(n // tm,),
        in_specs=[row(D_MODEL), row(W_A), row(W_BV), row(W_M), row(D_MODEL), row(D_MODEL), row(D_MODEL),
                  _full((W_A, D_MODEL)), _full((W_BV, D_MODEL)), _full((W_M, D_MODEL)),
                  _full((D_MODEL, D_MODEL))],
        out_specs=row(D_MODEL),
        out_shape=jax.ShapeDtypeStruct((n, D_MODEL), F32),
        compiler_params=_params("parallel"),
        name="back",
    )(x2d, a, b, m, ga, gb, gm, wts["w_up_a"], wts["w_up_b"], wts["w_up_m"], wts["w_out"])


def _bias_kernel(e_ref, tile_ref):
    for h in range(H_A):
        e = jnp.broadcast_to(e_ref[h] * LOG2E, (CHUNK, 2 * BAND))
        tile_ref[h] = pltpu.roll(e, 0, 1, stride=1, stride_axis=0)[:, :BIAS_W]


def _bias_tile(rel_bias):
    j = jnp.arange(2 * BAND)
    dist = jnp.where(j < BIAS_W, BAND - j, REL_CLIP)
    e = rel_bias[:, jnp.clip(dist, -REL_CLIP, REL_CLIP) + REL_CLIP].reshape(H_A, 1, 2 * BAND)
    return pl.pallas_call(
        _bias_kernel,
        grid=(1,),
        in_specs=[pl.BlockSpec((H_A, 1, 2 * BAND), lambda i: (0, 0, 0))],
        out_specs=pl.BlockSpec((H_A, CHUNK, BIAS_W), lambda i: (0, 0, 0)),
        out_shape=jax.ShapeDtypeStruct((H_A, CHUNK, BIAS_W), F32),
        compiler_params=_params("arbitrary"),
        name="bias_tile",
    )(e)


def _softmax_pv_staged(scores, values):
    biased = [[s if b is None else s + b for s, b in s_list] for s_list in scores]
    tops = [functools.reduce(jnp.maximum, [jnp.max(s, axis=-1, keepdims=True) for s in s_list])
            for s_list in biased]
    probs = [[jnp.exp2(s - m) for s in s_list] for s_list, m in zip(biased, tops)]
    sums = [functools.reduce(jnp.add, [jnp.sum(x, axis=-1, keepdims=True) for x in p_list]) for p_list in probs]
    return [functools.reduce(jnp.add, [(v(x.astype(BF16)) if callable(v) else _dot(x.astype(BF16), v))
                                       for x, v in zip(p_list, v_list)]) / l
            for p_list, v_list, l in zip(probs, values, sums)]


def _head_pairs_attention(q_pairs, k_lists, v_lists, bias_fn):
    m_rows = q_pairs[0].shape[0]
    lane = lax.broadcasted_iota(jnp.int32, (m_rows, LANES), 1)
    first = lane < HEAD_DIM
    scores = []
    for p, qp in enumerate(q_pairs):
        zero = jnp.zeros_like(qp)
        q2 = jnp.concatenate([jnp.where(first, qp, zero), jnp.where(first, zero, qp)], axis=0)
        s_list = []
        for i, k in enumerate(k_lists[p]):
            b0, b1 = bias_fn(p, 0, i), bias_fn(p, 1, i)
            s_list.append((_dot_nt(q2, k), None if b0 is None else jnp.concatenate([b0, b1], axis=0)))
        scores.append(s_list)
    outs = _softmax_pv_staged(scores, v_lists)
    return [jnp.where(first, o[:m_rows], o[m_rows:]) for o in outs]


def _attn_back_kernel(q_ref, *refs):
    nk = BAND // QBLK + 1
    k_refs, v_refs = refs[:nk], refs[nk:2 * nk]
    (sz_ref, bias_ref, qm_ref, mk_ref, mv_ref, szm_ref, x_ref, b_ref, ga_ref, gb_ref, gm_ref,
     wa_ref, wb_ref, wm_ref, wo_ref, y_ref, k_buf, v_buf, bias_buf, a_buf, m_buf) = refs[2 * nk:]
    i = pl.program_id(1)
    for j, (k_ref, v_ref) in enumerate(zip(k_refs, v_refs)):
        k_buf[j * QBLK:(j + 1) * QBLK, :] = k_ref[0]
        v_buf[j * QBLK:(j + 1) * QBLK, :] = v_ref[0]

    @pl.when(i <= BAND // QBLK)
    def _():
        col = lax.broadcasted_iota(jnp.int32, (1, BIAS_W), 1)
        for c in range(QBLK // CHUNK):
            before_start = jnp.where(col + c * CHUNK < (BAND // QBLK - i) * QBLK, NEG, 0.0)
            for h in range(H_A):
                bias_buf[c, h] = bias_ref[h] + before_start

    lanes_m = [slice(p * LANES, (p + 1) * LANES) for p in range(H_M // 2)]
    mk = [[mk_ref[0, :, lanes].astype(BF16)] for lanes in lanes_m]
    mv = [[mv_ref[0, :, lanes].astype(BF16)] for lanes in lanes_m]
    blocks = [(slice(r, r + MEM_ROWS), p) for r in range(0, QBLK, MEM_ROWS) for p in range(len(lanes_m))]
    outs = _head_pairs_attention([qm_ref[0, rows, lanes_m[p]] for rows, p in blocks],
                                 [mk[p] for _, p in blocks], [mv[p] for _, p in blocks], lambda n, e, _: None)
    for (rows, p), o in zip(blocks, outs):
        m_buf[rows, lanes_m[p]] = (o * szm_ref[0, rows, lanes_m[p]].astype(F32)).astype(BF16)

    pairs = [slice(p * LANES, (p + 1) * LANES) for p in range(H_A // 2)]
    for c in range(QBLK // CHUNK):
        rows = slice(c * CHUNK, (c + 1) * CHUNK)
        win = slice(c * CHUNK, c * CHUNK + WINDOW)
        outs = _head_pairs_attention(
            [q_ref[0, rows, lanes] for lanes in pairs],
            [[k_buf[win, lanes]] for lanes in pairs], [[v_buf[win, lanes]] for lanes in pairs],
            lambda p, e, _: bias_buf[c, 2 * p + e, :, :WINDOW])
        for lanes, o in zip(pairs, outs):
            a_buf[rows, lanes] = (o * sz_ref[0, rows, lanes].astype(F32)).astype(BF16)

    u = (ga_ref[0].astype(F32) * _dot(a_buf[...], wa_ref[...])
         + gb_ref[0].astype(F32) * _dot(b_ref[0], wb_ref[...])
         + gm_ref[0].astype(F32) * _dot(m_buf[...], wm_ref[...]))
    y_ref[0] = x_ref[0] + _dot(u.astype(BF16), wo_ref[...])


def _attn_back(acts, gates, bias, mk, mv, x, wts):
    b, t, _ = x.shape
    nb = BAND // QBLK
    n = t // QBLK
    assert BAND % QBLK == 0 and t % QBLK == 0 and QBLK % MEM_ROWS == 0

    def group(table, name, d=0):
        off, w = table[name]
        assert off % w == 0
        return pl.BlockSpec((1, QBLK, w), lambda bi, i: (bi, jnp.maximum(i - d, 0), off // w))

    act = functools.partial(group, _ACT_COLS)
    gate = functools.partial(group, _GATE_COLS)
    mem = pl.BlockSpec((1, N_MEM, W_M), lambda bi, i: (bi, 0, 0))
    return pl.pallas_call(
        _attn_back_kernel,
        grid=(b, n),
        in_specs=[act("qa")] + [act("ka", nb - j) for j in range(nb + 1)] + [act("va", nb - j) for j in range(nb + 1)]
                 + [act("sza"), _full((H_A, CHUNK, BIAS_W)), act("qm"), mem, mem, act("szm"),
                    pl.BlockSpec((1, QBLK, D_MODEL), lambda bi, i: (bi, i, 0)), act("out_b"),
                    gate("a"), gate("b"), gate("m"),
                    _full((W_A, D_MODEL)), _full((W_BV, D_MODEL)), _full((W_M, D_MODEL)),
                    _full((D_MODEL, D_MODEL))],
        out_specs=pl.BlockSpec((1, QBLK, D_MODEL), lambda bi, i: (bi, i, 0)),
        out_shape=jax.ShapeDtypeStruct((b, t, D_MODEL), F32),
        scratch_shapes=[pltpu.VMEM((KSPAN, W_A), BF16), pltpu.VMEM((KSPAN, W_A), BF16),
                        pltpu.VMEM((QBLK // CHUNK, H_A, CHUNK, BIAS_W), F32),
                        pltpu.VMEM((QBLK, W_A), BF16), pltpu.VMEM((QBLK, W_M), BF16)],
        compiler_params=_params("parallel", "arbitrary"),
        name="attn_back",
    )(*([acts] * (2 * nb + 4)), bias, acts, mk, mv, acts, x, acts, gates, gates, gates,
      wts["w_up_a"], wts["w_up_b"], wts["w_up_m"], wts["w_out"])


STEP_STREAMS = 8


def _attn_step_kernel(q_ref, kn_ref, vn_ref, kp_ref, vp_ref, sz_ref, bias_ref, o_ref):
    s_len = q_ref.shape[1]
    for b in range(STEP_STREAMS):
        scores, values = [], []
        for h in range(H_A):
            lanes = slice(h * HEAD_DIM, (h + 1) * HEAD_DIM)
            q = q_ref[b, :, lanes]
            scores.append([(_dot(q, kp_ref[b, h].astype(BF16)), bias_ref[h, :s_len, :BAND]),
                           (_dot_nt(q, kn_ref[b, :, lanes]), bias_ref[h, :s_len, BAND:BAND + s_len])])
            values.append([functools.partial(_dot_nt, b=vp_ref[b, h].astype(BF16)), vn_ref[b, :, lanes]])
        o = jnp.concatenate(_softmax_pv_staged(scores, values), axis=-1)
        o_ref[b] = (o * sz_ref[b].astype(F32)).astype(BF16)


def _attn_step(q, k_new, v_new, k_past_t, v_past_t, sz, bias):
    b, s_len, _ = q.shape
    assert k_past_t.shape[1:] == (H_A, HEAD_DIM, BAND) and s_len <= CHUNK and b % STEP_STREAMS == 0
    new = pl.BlockSpec((STEP_STREAMS, s_len, W_A), lambda i: (i, 0, 0))
    past = pl.BlockSpec((STEP_STREAMS, H_A, HEAD_DIM, BAND), lambda i: (i, 0, 0, 0))
    return pl.pallas_call(
        _attn_step_kernel,
        grid=(b // STEP_STREAMS,),
        in_specs=[new, new, new, past, past, new, _full((H_A, CHUNK, BIAS_W))],
        out_specs=new,
        out_shape=jax.ShapeDtypeStruct((b, s_len, W_A), BF16),
        compiler_params=_params("parallel"),
        name="attn_step",
    )(q, k_new, v_new, k_past_t, v_past_t, sz, bias)


def _mem_kv_kernel(mem_ref, gmem_ref, w_ref, gkm_ref, mk_ref, mv_ref):
    h = _rms_rows(mem_ref[...], gmem_ref[...]).astype(BF16)
    kv = _dot(h, w_ref[...])
    mk_ref[...] = _head_rms(kv[:, :W_M], gkm_ref[...], _head_mean_matrix(W_M, HEAD_DIM))
    mv_ref[...] = kv[:, W_M:]


def _mem_kv(mem, wts):
    b = mem.shape[0]
    rows = b * N_MEM
    out = pl.BlockSpec((rows, W_M), lambda i: (0, 0))
    mk, mv = pl.pallas_call(
        _mem_kv_kernel,
        grid=(1,),
        in_specs=[pl.BlockSpec((rows, D_MODEL), lambda i: (0, 0)), _full((1, D_MODEL)),
                  _full((D_MODEL, 2 * W_M)), _full((1, W_M))],
        out_specs=[out, out],
        out_shape=[jax.ShapeDtypeStruct((rows, W_M), F32)] * 2,
        compiler_params=_params("arbitrary"),
        name="mem_kv",
    )(mem.reshape(rows, D_MODEL), wts["g_mem"], wts["w_mem_kv"], wts["g_km"])
    return mk.reshape(b, N_MEM, W_M), mv.reshape(b, N_MEM, W_M)


MEM_ROWS = 128
MEM_STREAMS = 16


def _attn_mem_step_kernel(q_ref, mk_ref, mv_ref, sz_ref, o_ref):
    for b in range(q_ref.shape[0]):
        scores, values = [], []
        for h in range(H_M):
            lanes = slice(h * HEAD_DIM, (h + 1) * HEAD_DIM)
            scores.append([(_dot(q_ref[b, :, lanes], mk_ref[b, h].astype(BF16)), None)])
            values.append([functools.partial(_dot_nt, b=mv_ref[b, h].astype(BF16))])
        o = jnp.concatenate(_softmax_pv_staged(scores, values), axis=-1)
        o_ref[b] = (o * sz_ref[b].astype(F32)).astype(BF16)


def _attn_mem_step(q, mk_t, mv_t, sz):
    b, t, _ = q.shape
    nb = min(b, MEM_STREAMS)
    assert b % nb == 0 and mk_t.shape[1:] == (H_M, HEAD_DIM, N_MEM)
    rows = pl.BlockSpec((nb, t, W_M), lambda i: (i, 0, 0))
    mem = pl.BlockSpec((nb, H_M, HEAD_DIM, N_MEM), lambda i: (i, 0, 0, 0))
    return pl.pallas_call(
        _attn_mem_step_kernel,
        grid=(b // nb,),
        in_specs=[rows, mem, mem, rows],
        out_specs=rows,
        out_shape=jax.ShapeDtypeStruct((b, t, W_M), BF16),
        compiler_params=_params("parallel"),
        name="attn_mem_step",
    )(q, mk_t, mv_t, sz)


def _gla_constants(c, group):
    r = lax.broadcasted_iota(jnp.int32, (c, c), 0)
    s = lax.broadcasted_iota(jnp.int32, (c, c), 1)
    causal = (s <= r) & (r // group == s // group)
    head_of_k = lax.broadcasted_iota(jnp.int32, (W_BK, W_BV), 0) // DK_B
    head_of_v = lax.broadcasted_iota(jnp.int32, (W_BK, W_BV), 1) // DV_B
    head_of_row = lax.broadcasted_iota(jnp.int32, (H_B, W_BK, c), 1) // DK_B
    head = lax.broadcasted_iota(jnp.int32, (H_B, W_BK, c), 0)
    return causal.astype(BF16), (head_of_k == head_of_v).astype(BF16), (head_of_row == head).astype(BF16)


def _gla_intra(items, group, between=lambda: None):
    n = len(items)
    qs, ks, vs, v16s = ([it[j] for it in items] for j in range(4))
    c = qs[0].shape[0]
    las = [it[4] * LOG2E for it in items]
    causal16, spread, head_rows = _gla_constants(c, group)
    head_rows = [head_rows[h] for h in range(H_B)]
    cums = [_dot_exact_lhs(causal16, la) for la in las]
    cumxs = [cum - la for cum, la in zip(cums, las)]
    r_ck = lax.broadcasted_iota(jnp.int32, (c, W_BK), 0)

    os = [_dot((q * k).astype(BF16), spread) * v for q, k, v in zip(qs, ks, vs)]

    def shifted(x, j):
        return pltpu.roll(x.reshape(c // SUBLANES, SUBLANES, W_BK), j % SUBLANES, 1).reshape(c, W_BK)

    def block_edges(cum, cumx, half):
        if half >= SUBLANES:
            nb = c // half
            first = jnp.broadcast_to(cumx.reshape(nb, half, W_BK)[:, 0:1, :], (nb, half, W_BK))
            last = jnp.broadcast_to(cum.reshape(nb, half, W_BK)[:, half - 1:half, :], (nb, half, W_BK))
            return first.reshape(c, W_BK), last.reshape(c, W_BK)
        r_in = r_ck % half
        first, last = cumx, cum
        for j in range(1, half):
            first = jnp.where(r_in == j, shifted(cumx, j), first)
            last = jnp.where(r_in == half - 1 - j, shifted(cum, -j), last)
        return first, last

    t_xor_s = lax.broadcasted_iota(jnp.int32, (c, H_B * c), 0) ^ (lax.broadcasted_iota(jnp.int32, (c, H_B * c), 1) % c)
    atts = [jnp.zeros((c, H_B * c), F32) for _ in range(n)]
    half = 1
    while half < group:
        edges = [block_edges(cum, cumx, half) for cum, cumx in zip(cums, cumxs)]
        odd = (r_ck & half) != 0
        qts = [jnp.where(odd, q * jnp.exp2(cum - first), 0.0).astype(BF16)
               for q, cum, (first, _) in zip(qs, cums, edges)]
        kts = [jnp.where(odd, 0.0, k * jnp.exp2(last - cum)).T.astype(BF16)
               for k, cum, (_, last) in zip(ks, cums, edges)]
        prods = [_dot(qt, jnp.concatenate([kt * m for m in head_rows], axis=1)) for qt, kt in zip(qts, kts)]
        if 2 * half < c:
            prods = [jnp.where(t_xor_s < 2 * half, a, 0.0) for a in prods]
        atts = [att + a for att, a in zip(atts, prods)]
        between()
        half *= 2
    if group > 1:
        att16s = [att.astype(BF16) for att in atts]
        os = [o + jnp.concatenate([_dot(att16[:, h * c:(h + 1) * c], v16[:, h * DV_B:(h + 1) * DV_B])
                                   for h in range(H_B)], axis=-1)
              for o, att16, v16 in zip(os, att16s, v16s)]
    return list(zip(os, cums))


def _gla_state_step(q, k, v16, cum, s_old):
    c = q.shape[0]
    q_in = (q * jnp.exp2(cum)).astype(BF16)
    k_out_t = (k * jnp.exp2(cum[c - 1:c, :] - cum)).T.astype(BF16)
    keep = jnp.exp2(cum.T[:, c - 1:c])
    kd = lambda h: slice(h * DK_B, (h + 1) * DK_B)
    vd = lambda h: slice(h * DV_B, (h + 1) * DV_B)
    o = jnp.concatenate([_dot(q_in[:, kd(h)], s_old[kd(h)].astype(BF16)) for h in range(H_B)], axis=-1)
    s_new = jnp.concatenate([s_old[kd(h)] * keep[kd(h)] + _dot(k_out_t[kd(h)], v16[:, vd(h)])
                             for h in range(H_B)], axis=0)
    return o, s_new


def _gla_finish(o, g_ref, sz):
    outs = [_rms_rows(o[:, h * DV_B:(h + 1) * DV_B], g_ref[:, h * DV_B:(h + 1) * DV_B]) for h in range(H_B)]
    return (jnp.concatenate(outs, axis=-1) * sz.astype(F32)).astype(BF16)


def _gla_step_kernel(q_ref, k_ref, v_ref, la_ref, sz_ref, g_ref, s0_ref, o_ref, s_ref, *, t):
    c = GLA_C
    n = c // t
    q, k, la = q_ref[...].astype(F32), k_ref[...].astype(F32), la_ref[...]
    v16 = v_ref[...]
    (o, cum), = _gla_intra([(q, k, v16.astype(F32), v16, la)], t)

    q_in = (q * jnp.exp2(cum)).astype(BF16)
    cum_t, k_t = cum.T, k.T
    stream_of_col = lax.broadcasted_iota(jnp.int32, (W_BK, c), 1) // t
    inter = []
    for s in range(n):
        rows = slice(s * t, (s + 1) * t)
        last = cum_t[:, (s + 1) * t - 1:(s + 1) * t]
        k_out_t = (k_t * jnp.exp2(jnp.where(stream_of_col == s, last - cum_t, NEG))).astype(BF16)
        keep = jnp.exp2(last)
        o_s = []
        for h in range(H_B):
            ks = slice(h * DK_B, (h + 1) * DK_B)
            vs = slice(h * DV_B, (h + 1) * DV_B)
            s_old = s0_ref[s, h]
            o_s.append(_dot(q_in[rows, ks], s_old.astype(BF16)))
            s_ref[s, h] = s_old * keep[ks] + _dot(k_out_t[ks], v16[:, vs])
        inter.append(jnp.concatenate(o_s, axis=-1))
    o_ref[...] = _gla_finish(o + jnp.concatenate(inter, axis=0), g_ref, sz_ref[...])


def _gla_step(q, k, v, la, sz, g, s0):
    b, t, _ = q.shape
    n = GLA_C // t
    assert GLA_C % t == 0 and t % (2 * SUBLANES) == 0 and b % n == 0
    flat = lambda z: z.reshape(b * t, z.shape[-1])
    blk = lambda w: pl.BlockSpec((GLA_C, w), lambda i: (i, 0))
    state = pl.BlockSpec((n, H_B, DK_B, DV_B), lambda i: (i, 0, 0, 0))
    o, s_new = pl.pallas_call(
        functools.partial(_gla_step_kernel, t=t),
        grid=(b // n,),
        in_specs=[blk(W_BK), blk(W_BK), blk(W_BV), blk(W_BK), blk(W_BV),
                  pl.BlockSpec((1, W_BV), lambda i: (0, 0)), state],
        out_specs=[blk(W_BV), state],
        out_shape=[jax.ShapeDtypeStruct((b * t, W_BV), BF16),
                   jax.ShapeDtypeStruct((b, H_B, DK_B, DV_B), F32)],
        compiler_params=_params("parallel"),
        name="gla_step",
    )(flat(q), flat(k), flat(v), flat(la), flat(sz), g, s0)
    return o.reshape(b, t, W_BV), s_new


def _prep_weights(l, norm_in, w_in, g_qa, g_ka, w_gate2, b_gate, g_gla_out, g_mem, w_mem_kv, g_qm, g_km,
                  w_up_a, w_up_b, w_up_m, w_out):
    tile = lambda gain, n: jnp.tile(gain, n).reshape(1, -1)
    halved = jnp.concatenate([jnp.full((size,), 0.5 if i in _HALVED_GROUPS else 1.0, F32)
                              for i, size in enumerate(IN_SIZES)])
    return {
        "norm_in": norm_in[l].reshape(1, D_MODEL),
        "w_in_t": (w_in[l].T * halved[:, None]).astype(BF16),
        "w_gate2": w_gate2[l].astype(BF16),
        "b_gate": b_gate[l].reshape(1, W_BK),
        "g_qa": tile(g_qa[l], H_A), "g_ka": tile(g_ka[l], H_A), "g_qm": tile(g_qm[l], H_M),
        "g_km": tile(g_km[l], H_M), "g_gla": tile(g_gla_out[l], H_B),
        "g_mem": g_mem[l].reshape(1, D_MODEL),
        "w_mem_kv": w_mem_kv[l].astype(BF16),
        "w_up_a": w_up_a[l].astype(BF16), "w_up_b": w_up_b[l].astype(BF16),
        "w_up_m": w_up_m[l].astype(BF16), "w_out": w_out[l].astype(BF16),
    }


def _layer_long(x, wts, bias, mk, mv, s0):
    b, t, _ = x.shape
    keep = min(BAND, t)
    r3 = lambda z: z.reshape(b, t, z.shape[-1])
    heads = lambda z: jnp.transpose(z.reshape(b, H_A, HEAD_DIM, keep), (0, 3, 1, 2))
    ka, va, s_new, acts, gates = _front(x.reshape(b * t, D_MODEL), wts, keep, t, keep, s0)
    y = _attn_back(r3(acts), r3(gates), bias, mk, mv, x, wts)
    return y, heads(ka), heads(va), s_new


def _layer_short(x, wts, attend_a, attend_m, s0, tm):
    b, t, _ = x.shape
    keep = min(BAND, t)
    x2d = x.reshape(b * t, D_MODEL)
    r3 = lambda z: z.reshape(b, t, z.shape[-1])
    flat = lambda z: z.reshape(b * t, z.shape[-1])
    heads = lambda z: z.reshape(b, keep, H_A, HEAD_DIM)
    (qa, ka, va, ka16, va16, sza, qb, kb, vb, la, szb, qm, szm, ga, gb, gm) = _front(x2d, wts, tm, t, keep)
    out_b, s_new = _gla_step(r3(qb), r3(kb), r3(vb), r3(la), r3(szb), wts["g_gla"], s0)
    out_a = attend_a(r3(qa), r3(ka16), r3(va16), r3(sza))
    out_m = attend_m(r3(qm), r3(szm))
    y = _back(x2d, flat(out_a), flat(out_b), flat(out_m), ga, gb, gm, wts, tm)
    return y.reshape(b, t, D_MODEL), heads(ka), heads(va), s_new


def kernel(x_prompt, x_sample, mem_prompt, cache_a_k, cache_a_v, state_gla, cache_mem_k, cache_mem_v,
           norm_in, w_in, g_qa, g_ka, rel_bias, w_gate2, b_gate, g_gla_out, g_mem, w_mem_kv, g_qm, g_km,
           w_up_a, w_up_b, w_up_m, w_out):
    depth = w_in.shape[0]
    xp, xs = x_prompt, x_sample
    bp, tp, _ = xp.shape
    bs, ts, _ = xs.shape
    akp, avp, sgp, mkp, mvp, aks, avs, sgs = [], [], [], [], [], [], [], []
    for l in range(depth):
        wts = _prep_weights(l, norm_in, w_in, g_qa, g_ka, w_gate2, b_gate, g_gla_out, g_mem, w_mem_kv,
                            g_qm, g_km, w_up_a, w_up_b, w_up_m, w_out)
        bias = _bias_tile(rel_bias[l])
        mk, mv = _mem_kv(mem_prompt, wts)
        s0 = jnp.zeros((bp, H_B, DK_B, DV_B), state_gla.dtype)
        xp, ka, va, sp = _layer_long(xp, wts, bias, mk, mv, s0)
        akp.append(ka)
        avp.append(va)
        sgp.append(sp)
        mkp.append(mk.reshape(bp, N_MEM, H_M, HEAD_DIM))
        mvp.append(mv.reshape(bp, N_MEM, H_M, HEAD_DIM))
        rows_last = lambda z: jnp.transpose(z, (0, 2, 3, 1))
        past_k, past_v = rows_last(cache_a_k[l]), rows_last(cache_a_v[l])
        mem_k, mem_v = rows_last(cache_mem_k[l]), rows_last(cache_mem_v[l])
        attend_s = lambda q, k, v, sz: _attn_step(q, k, v, past_k, past_v, sz, bias)
        attend_ms = lambda q, sz: _attn_mem_step(q, mem_k, mem_v, sz)
        xs, ka_s, va_s, ss = _layer_short(xs, wts, attend_s, attend_ms, state_gla[l], 512)
        aks.append(ka_s)
        avs.append(va_s)
        sgs.append(ss)
    return (xp, xs, jnp.stack(akp), jnp.stack(avp), jnp.stack(sgp), jnp.stack(mkp), jnp.stack(mvp),
            jnp.stack(aks), jnp.stack(avs), jnp.stack(sgs))
```

```python
import functools

import jax
import jax.numpy as jnp
from jax import lax
from jax.experimental import pallas as pl
from jax.experimental.pallas import tpu as pltpu

F32 = jnp.float32
BF16 = jnp.bfloat16

D_MODEL = 1024
CHUNK = 64
LEFT_CHUNKS = 8
HEAD_DIM = 64
H_A = 8
W_A = H_A * HEAD_DIM
REL_CLIP = 128
H_B = 4
DK_B = 64
DV_B = 128
W_BK = H_B * DK_B
W_BV = H_B * DV_B
GATE_RANK = 16
GATE_TAU = 16.0
N_MEM = 256
H_M = 4
W_M = H_M * HEAD_DIM
EPS = 1e-6
IN_SIZES = (W_A, W_A, W_A, W_A, W_BK, W_BK, W_BV, GATE_RANK, W_BV, W_M, W_M, D_MODEL, D_MODEL, D_MODEL)

LANES = 128
SUBLANES = 8
VMEM_LIMIT = 56 * 1024 * 1024
NEG = -1e30
LOG2E = 1.4426950408889634

BAND = LEFT_CHUNKS * CHUNK
QBLK = 512
KSPAN = BAND + QBLK
WINDOW = BAND + CHUNK
BIAS_W = BAND + 2 * CHUNK
GLA_C = 128
PROJ_PER_LEVEL = 5


def _dot(a, b):
    return jnp.dot(a, b, preferred_element_type=F32)


def _dot_nt(a, b):
    return lax.dot_general(a, b, (((1,), (1,)), ((), ())), preferred_element_type=F32)


def _split3(x):
    hi = x.astype(BF16)
    r = x - hi.astype(F32)
    mid = r.astype(BF16)
    lo = (r - mid.astype(F32)).astype(BF16)
    return hi, mid, lo


def _dot_exact_lhs(a01, x):
    hi, mid, lo = _split3(x)
    return _dot(a01, hi) + _dot(a01, mid) + _dot(a01, lo)


def _sigmoid_of_twice(p):
    return 0.5 * jnp.tanh(p) + 0.5


def _silu_of_twice(p):
    return p * jnp.tanh(p) + p


def _log_sigmoid(z):
    return jnp.minimum(z, 0.0) - jnp.log1p(jnp.exp(-jnp.abs(z)))


def _rms_rows(x, g):
    ms = jnp.mean(x * x, axis=-1, keepdims=True)
    return x * lax.rsqrt(ms + EPS) * g


def _head_mean_matrix(width, head):
    r = lax.broadcasted_iota(jnp.int32, (width, width), 0) // head
    c = lax.broadcasted_iota(jnp.int32, (width, width), 1) // head
    return jnp.where(r == c, 1.0 / head, 0.0).astype(BF16)


def _head_rms(x, g, avg):
    ms = _dot((x * x).astype(BF16), avg)
    return x * lax.rsqrt(ms + EPS) * g


def _full(shape):
    nd = len(shape)
    return pl.BlockSpec(shape, lambda *_: (0,) * nd, pipeline_mode=pl.Buffered(1))


def _params(*sem):
    return pltpu.CompilerParams(dimension_semantics=sem, vmem_limit_bytes=VMEM_LIMIT)


_IN_OFF = tuple(int(sum(IN_SIZES[:i])) for i in range(len(IN_SIZES) + 1))
_HALVED_GROUPS = (3, 8, 10, 11, 12, 13)
D_IN = _IN_OFF[-1]


def _col_table(groups):
    table, off = {}, 0
    for name, width in groups:
        table[name] = (off, width)
        off += width
    return table, off


_ACT_COLS, ACT_W = _col_table((("qa", W_A), ("ka", W_A), ("va", W_A), ("sza", W_A), ("out_b", W_BV),
                               ("qm", W_M), ("szm", W_M)))
_GATE_COLS, GATE_W = _col_table((("a", D_MODEL), ("b", D_MODEL), ("m", D_MODEL)))


def _front_kernel(x_ref, nin_ref, w_ref, wg2_ref, bg_ref, gqa_ref, gka_ref, gqm_ref, *rest, per_stream=None):
    fused = per_stream is not None
    if fused:
        (ggla_ref, s0_ref, kat_o, vat_o, sfin_o, acts_o, gates_o, s_scr, ka_o, va_o) = rest
        cols = lambda ref, name, table: ref.at[:, table[name][0]:table[name][0] + table[name][1]]
        qa_o, ka16_o, va16_o, sza_o, outb_o, qm_o, szm_o = (cols(acts_o, k, _ACT_COLS) for k in _ACT_COLS)
        ga_o, gb_o, gm_o = (cols(gates_o, k, _GATE_COLS) for k in _GATE_COLS)
    else:
        (qa_o, ka_o, va_o, ka16_o, va16_o, sza_o, qb_o, kb_o, vb_o, la_o, szb_o, qm_o, szm_o,
         ga_o, gb_o, gm_o) = rest
    if fused:
        @pl.when(pl.program_id(0) % per_stream == 0)
        def _():
            s_scr[...] = s0_ref[pl.program_id(0) // per_stream].reshape(W_BK, DV_B)

    h = _rms_rows(x_ref[...], nin_ref[...]).astype(BF16)

    def proj(i):
        return _dot_nt(h, w_ref[_IN_OFF[i]:_IN_OFF[i + 1], :])

    z = _dot(proj(7).astype(BF16), wg2_ref[...]) + bg_ref[...]
    la = _log_sigmoid(z) * (1.0 / GATE_TAU)
    qb = (proj(4) * (DK_B ** -0.5)).astype(BF16)
    kb = proj(5).astype(BF16)
    vb = proj(6).astype(BF16)
    szb = _silu_of_twice(proj(8)).astype(BF16)

    scale = HEAD_DIM ** -0.5 * LOG2E
    avg_a = _head_mean_matrix(W_A, HEAD_DIM)

    def emit_qa():
        qa_o[...] = (_head_rms(proj(0), gqa_ref[...], avg_a) * scale).astype(BF16)

    def emit_ka():
        ka = _head_rms(proj(1), gka_ref[...], avg_a)
        ka_o[...] = ka
        ka16_o[...] = ka.astype(BF16)

    def emit_va():
        va = proj(2)
        va_o[...] = va
        va16_o[...] = va.astype(BF16)

    def emit_sza():
        sza_o[...] = _silu_of_twice(proj(3)).astype(BF16)

    def emit_qm():
        qm_o[...] = (_head_rms(proj(9), gqm_ref[...], _head_mean_matrix(W_M, HEAD_DIM)) * scale).astype(BF16)

    def emit_szm():
        szm_o[...] = _silu_of_twice(proj(10)).astype(BF16)

    def emit_gate(i, ref):
        ref[...] = _sigmoid_of_twice(proj(i)).astype(BF16)

    pending = [emit_qa, emit_ka, emit_va, emit_sza, emit_qm, emit_szm,
               functools.partial(emit_gate, 11, ga_o), functools.partial(emit_gate, 12, gb_o),
               functools.partial(emit_gate, 13, gm_o)]

    def emit_next():
        if pending:
            pending.pop(0)()

    if fused:
        chunks = [slice(r, r + GLA_C) for r in range(0, x_ref.shape[0], GLA_C)]
        items = [(qb[rows].astype(F32), kb[rows].astype(F32), vb[rows].astype(F32), vb[rows], la[rows])
                 for rows in chunks]
        state = s_scr[...]
        outs = []
        def emit_some():
            for _ in range(PROJ_PER_LEVEL):
                emit_next()

        for rows, item, (o, cum) in zip(chunks, items, _gla_intra(items, GLA_C, emit_some)):
            inter, state = _gla_state_step(item[0], item[1], item[3], cum, state)
            outs.append(_gla_finish(o + inter, ggla_ref, szb[rows]))
            emit_next()
        s_scr[...] = state
        outb_o[...] = jnp.concatenate(outs, axis=0)
    else:
        qb_o[...] = qb
        kb_o[...] = kb
        vb_o[...] = vb
        la_o[...] = la
        szb_o[...] = szb
    while pending:
        emit_next()

    if fused:
        @pl.when(pl.program_id(0) % per_stream == per_stream - 1)
        def _():
            sfin_o[0] = s_scr[...].reshape(H_B, DK_B, DV_B)
            kat_o[0] = ka_o[...].T
            vat_o[0] = va_o[...].T


def _front(x2d, wts, tm, t, keep, s0=None):
    n = x2d.shape[0]
    tm = min(tm, n)
    row = lambda w: pl.BlockSpec((tm, w), lambda i: (i, 0))
    out = lambda rows_, w, d, spec: (jax.ShapeDtypeStruct((rows_, w), d), spec)
    tail = [out(n, W_M, BF16, row(W_M)), out(n, W_M, BF16, row(W_M))] + [out(n, D_MODEL, BF16, row(D_MODEL))] * 3
    in_specs = [row(D_MODEL), _full((1, D_MODEL)), _full((D_IN, D_MODEL)),
                _full((GATE_RANK, W_BK)), _full((1, W_BK)), _full((1, W_A)), _full((1, W_A)), _full((1, W_M))]
    args = [x2d, wts["norm_in"], wts["w_in_t"], wts["w_gate2"], wts["b_gate"], wts["g_qa"], wts["g_ka"], wts["g_qm"]]
    if s0 is None:
        assert keep == t
        kv = [out(n, W_A, F32, row(W_A))] * 2
        mid = [out(n, W_BK, BF16, row(W_BK)), out(n, W_BK, BF16, row(W_BK)), out(n, W_BV, BF16, row(W_BV)),
               out(n, W_BK, F32, row(W_BK)), out(n, W_BV, BF16, row(W_BV))]
        body, scratch = _front_kernel, []
    else:
        assert t % tm == 0 and tm % GLA_C == 0 and keep == tm
        streams, per_stream = n // t, t // tm
        state = (streams, H_B, DK_B, DV_B)
        kv = [(jax.ShapeDtypeStruct((streams, W_A, keep), F32),
               pl.BlockSpec((1, W_A, keep), lambda i: (i // per_stream, 0, 0)))] * 2
        outs = kv + [(jax.ShapeDtypeStruct(state, F32),
                      pl.BlockSpec((1,) + state[1:], lambda i: (i // per_stream, 0, 0, 0))),
                     out(n, ACT_W, BF16, row(ACT_W)), out(n, GATE_W, BF16, row(GATE_W))]
        in_specs += [_full((1, W_BV)), _full(state)]
        args += [wts["g_gla"], s0]
        body = functools.partial(_front_kernel, per_stream=per_stream)
        scratch = [pltpu.VMEM((W_BK, DV_B), F32), pltpu.VMEM((tm, W_A), F32), pltpu.VMEM((tm, W_A), F32)]
    if s0 is None:
        outs = [out(n, W_A, BF16, row(W_A))] + kv + [out(n, W_A, BF16, row(W_A))] * 3 + mid + tail
    return pl.pallas_call(
        body,
        grid=(n // tm,),
        in_specs=in_specs,
        out_specs=[spec for _, spec in outs],
        out_shape=[shape for shape, _ in outs],
        scratch_shapes=scratch,
        compiler_params=_params("arbitrary"),
        name="front",
    )(*args)


def _back_kernel(x_ref, a_ref, b_ref, m_ref, ga_ref, gb_ref, gm_ref, wa_ref, wb_ref, wm_ref, wo_ref, y_ref):
    u = (ga_ref[...].astype(F32) * _dot(a_ref[...], wa_ref[...])
         + gb_ref[...].astype(F32) * _dot(b_ref[...], wb_ref[...])
         + gm_ref[...].astype(F32) * _dot(m_ref[...], wm_ref[...]))
    y_ref[...] = x_ref[...] + _dot(u.astype(BF16), wo_ref[...])


def _back(x2d, a, b, m, ga, gb, gm, wts, tm):
    n = x2d.shape[0]
    tm = min(tm, n)
    row = lambda w: pl.BlockSpec((tm, w), lambda i: (i, 0))
    return pl.pallas_call(
        _back_kernel,
        grid=(n // tm,),
        in_specs=[row(D_MODEL), row(W_A), row(W_BV), row(W_M), row(D_MODEL), row(D_MODEL), row(D_MODEL),
                  _full((W_A, D_MODEL)), _full((W_BV, D_MODEL)), _full((W_M, D_MODEL)),
                  _full((D_MODEL, D_MODEL))],
        out_specs=row(D_MODEL),
        out_shape=jax.ShapeDtypeStruct((n, D_MODEL), F32),
        compiler_params=_params("parallel"),
        name="back",
    )(x2d, a, b, m, ga, gb, gm, wts["w_up_a"], wts["w_up_b"], wts["w_up_m"], wts["w_out"])


def _bias_kernel(e_ref, tile_ref):
    for h in range(H_A):
        e = jnp.broadcast_to(e_ref[h] * LOG2E, (CHUNK, 2 * BAND))
        tile_ref[h] = pltpu.roll(e, 0, 1, stride=1, stride_axis=0)[:, :BIAS_W]


def _bias_tile(rel_bias):
    j = jnp.arange(2 * BAND)
    dist = jnp.where(j < BIAS_W, BAND - j, REL_CLIP)
    e = rel_bias[:, jnp.clip(dist, -REL_CLIP, REL_CLIP) + REL_CLIP].reshape(H_A, 1, 2 * BAND)
    return pl.pallas_call(
        _bias_kernel,
        grid=(1,),
        in_specs=[pl.BlockSpec((H_A, 1, 2 * BAND), lambda i: (0, 0, 0))],
        out_specs=pl.BlockSpec((H_A, CHUNK, BIAS_W), lambda i: (0, 0, 0)),
        out_shape=jax.ShapeDtypeStruct((H_A, CHUNK, BIAS_W), F32),
        compiler_params=_params("arbitrary"),
        name="bias_tile",
    )(e)


def _softmax_pv_staged(scores, values):
    biased = [[s if b is None else s + b for s, b in s_list] for s_list in scores]
    tops = [functools.reduce(jnp.maximum, [jnp.max(s, axis=-1, keepdims=True) for s in s_list])
            for s_list in biased]
    probs = [[jnp.exp2(s - m) for s in s_list] for s_list, m in zip(biased, tops)]
    sums = [functools.reduce(jnp.add, [jnp.sum(x, axis=-1, keepdims=True) for x in p_list]) for p_list in probs]
    return [functools.reduce(jnp.add, [(v(x.astype(BF16)) if callable(v) else _dot(x.astype(BF16), v))
                                       for x, v in zip(p_list, v_list)]) / l
            for p_list, v_list, l in zip(probs, values, sums)]


def _head_pairs_attention(q_pairs, k_lists, v_lists, bias_fn):
    m_rows = q_pairs[0].shape[0]
    lane = lax.broadcasted_iota(jnp.int32, (m_rows, LANES), 1)
    first = lane < HEAD_DIM
    scores = []
    for p, qp in enumerate(q_pairs):
        zero = jnp.zeros_like(qp)
        q2 = jnp.concatenate([jnp.where(first, qp, zero), jnp.where(first, zero, qp)], axis=0)
        s_list = []
        for i, k in enumerate(k_lists[p]):
            b0, b1 = bias_fn(p, 0, i), bias_fn(p, 1, i)
            s_list.append((_dot_nt(q2, k), None if b0 is None else jnp.concatenate([b0, b1], axis=0)))
        scores.append(s_list)
    outs = _softmax_pv_staged(scores, v_lists)
    return [jnp.where(first, o[:m_rows], o[m_rows:]) for o in outs]


def _attn_back_kernel(q_ref, *refs):
    nk = BAND // QBLK + 1
    k_refs, v_refs = refs[:nk], refs[nk:2 * nk]
    (sz_ref, bias_ref, qm_ref, mk_ref, mv_ref, szm_ref, x_ref, b_ref, ga_ref, gb_ref, gm_ref,
     wa_ref, wb_ref, wm_ref, wo_ref, y_ref, k_buf, v_buf, bias_buf, a_buf, m_buf) = refs[2 * nk:]
    i = pl.program_id(1)
    for j, (k_ref, v_ref) in enumerate(zip(k_refs, v_refs)):
        k_buf[j * QBLK:(j + 1) * QBLK, :] = k_ref[0]
        v_buf[j * QBLK:(j + 1) * QBLK, :] = v_ref[0]

    @pl.when(i <= BAND // QBLK)
    def _():
        col = lax.broadcasted_iota(jnp.int32, (1, BIAS_W), 1)
        for c in range(QBLK // CHUNK):
            before_start = jnp.where(col + c * CHUNK < (BAND // QBLK - i) * QBLK, NEG, 0.0)
            for h in range(H_A):
                bias_buf[c, h] = bias_ref[h] + before_start

    lanes_m = [slice(p * LANES, (p + 1) * LANES) for p in range(H_M // 2)]
    mk = [[mk_ref[0, :, lanes].astype(BF16)] for lanes in lanes_m]
    mv = [[mv_ref[0, :, lanes].astype(BF16)] for lanes in lanes_m]
    blocks = [(slice(r, r + MEM_ROWS), p) for r in range(0, QBLK, MEM_ROWS) for p in range(len(lanes_m))]
    outs = _head_pairs_attention([qm_ref[0, rows, lanes_m[p]] for rows, p in blocks],
                                 [mk[p] for _, p in blocks], [mv[p] for _, p in blocks], lambda n, e, _: None)
    for (rows, p), o in zip(blocks, outs):
        m_buf[rows, lanes_m[p]] = (o * szm_ref[0, rows, lanes_m[p]].astype(F32)).astype(BF16)

    pairs = [slice(p * LANES, (p + 1) * LANES) for p in range(H_A // 2)]
    for c in range(QBLK // CHUNK):
        rows = slice(c * CHUNK, (c + 1) * CHUNK)
        win = slice(c * CHUNK, c * CHUNK + WINDOW)
        outs = _head_pairs_attention(
            [q_ref[0, rows, lanes] for lanes in pairs],
            [[k_buf[win, lanes]] for lanes in pairs], [[v_buf[win, lanes]] for lanes in pairs],
            lambda p, e, _: bias_buf[c, 2 * p + e, :, :WINDOW])
        for lanes, o in zip(pairs, outs):
            a_buf[rows, lanes] = (o * sz_ref[0, rows, lanes].astype(F32)).astype(BF16)

    u = (ga_ref[0].astype(F32) * _dot(a_buf[...], wa_ref[...])
         + gb_ref[0].astype(F32) * _dot(b_ref[0], wb_ref[...])
         + gm_ref[0].astype(F32) * _dot(m_buf[...], wm_ref[...]))
    y_ref[0] = x_ref[0] + _dot(u.astype(BF16), wo_ref[...])


def _attn_back(acts, gates, bias, mk, mv, x, wts):
    b, t, _ = x.shape
    nb = BAND // QBLK
    n = t // QBLK
    assert BAND % QBLK == 0 and t % QBLK == 0 and QBLK % MEM_ROWS == 0

    def group(table, name, d=0):
        off, w = table[name]
        assert off % w == 0
        return pl.BlockSpec((1, QBLK, w), lambda bi, i: (bi, jnp.maximum(i - d, 0), off // w))

    act = functools.partial(group, _ACT_COLS)
    gate = functools.partial(group, _GATE_COLS)
    mem = pl.BlockSpec((1, N_MEM, W_M), lambda bi, i: (bi, 0, 0))
    return pl.pallas_call(
        _attn_back_kernel,
        grid=(b, n),
        in_specs=[act("qa")] + [act("ka", nb - j) for j in range(nb + 1)] + [act("va", nb - j) for j in range(nb + 1)]
                 + [act("sza"), _full((H_A, CHUNK, BIAS_W)), act("qm"), mem, mem, act("szm"),
                    pl.BlockSpec((1, QBLK, D_MODEL), lambda bi, i: (bi, i, 0)), act("out_b"),
                    gate("a"), gate("b"), gate("m"),
                    _full((W_A, D_MODEL)), _full((W_BV, D_MODEL)), _full((W_M, D_MODEL)),
                    _full((D_MODEL, D_MODEL))],
        out_specs=pl.BlockSpec((1, QBLK, D_MODEL), lambda bi, i: (bi, i, 0)),
        out_shape=jax.ShapeDtypeStruct((b, t, D_MODEL), F32),
        scratch_shapes=[pltpu.VMEM((KSPAN, W_A), BF16), pltpu.VMEM((KSPAN, W_A), BF16),
                        pltpu.VMEM((QBLK // CHUNK, H_A, CHUNK, BIAS_W), F32),
                        pltpu.VMEM((QBLK, W_A), BF16), pltpu.VMEM((QBLK, W_M), BF16)],
        compiler_params=_params("parallel", "arbitrary"),
        name="attn_back",
    )(*([acts] * (2 * nb + 4)), bias, acts, mk, mv, acts, x, acts, gates, gates, gates,
      wts["w_up_a"], wts["w_up_b"], wts["w_up_m"], wts["w_out"])


STEP_STREAMS = 8


def _attn_step_kernel(q_ref, kn_ref, vn_ref, kp_ref, vp_ref, sz_ref, bias_ref, o_ref):
    s_len = q_ref.shape[1]
    for b in range(STEP_STREAMS):
        scores, values = [], []
        for h in range(H_A):
            lanes = slice(h * HEAD_DIM, (h + 1) * HEAD_DIM)
            q = q_ref[b, :, lanes]
            scores.append([(_dot(q, kp_ref[b, h].astype(BF16)), bias_ref[h, :s_len, :BAND]),
                           (_dot_nt(q, kn_ref[b, :, lanes]), bias_ref[h, :s_len, BAND:BAND + s_len])])
            values.append([functools.partial(_dot_nt, b=vp_ref[b, h].astype(BF16)), vn_ref[b, :, lanes]])
        o = jnp.concatenate(_softmax_pv_staged(scores, values), axis=-1)
        o_ref[b] = (o * sz_ref[b].astype(F32)).astype(BF16)


def _attn_step(q, k_new, v_new, k_past_t, v_past_t, sz, bias):
    b, s_len, _ = q.shape
    assert k_past_t.shape[1:] == (H_A, HEAD_DIM, BAND) and s_len <= CHUNK and b % STEP_STREAMS == 0
    new = pl.BlockSpec((STEP_STREAMS, s_len, W_A), lambda i: (i, 0, 0))
    past = pl.BlockSpec((STEP_STREAMS, H_A, HEAD_DIM, BAND), lambda i: (i, 0, 0, 0))
    return pl.pallas_call(
        _attn_step_kernel,
        grid=(b // STEP_STREAMS,),
        in_specs=[new, new, new, past, past, new, _full((H_A, CHUNK, BIAS_W))],
        out_specs=new,
        out_shape=jax.ShapeDtypeStruct((b, s_len, W_A), BF16),
        compiler_params=_params("parallel"),
        name="attn_step",
    )(q, k_new, v_new, k_past_t, v_past_t, sz, bias)


def _mem_kv_kernel(mem_ref, gmem_ref, w_ref, gkm_ref, mk_ref, mv_ref):
    h = _rms_rows(mem_ref[...], gmem_ref[...]).astype(BF16)
    kv = _dot(h, w_ref[...])
    mk_ref[...] = _head_rms(kv[:, :W_M], gkm_ref[...], _head_mean_matrix(W_M, HEAD_DIM))
    mv_ref[...] = kv[:, W_M:]


def _mem_kv(mem, wts):
    b = mem.shape[0]
    rows = b * N_MEM
    out = pl.BlockSpec((rows, W_M), lambda i: (0, 0))
    mk, mv = pl.pallas_call(
        _mem_kv_kernel,
        grid=(1,),
        in_specs=[pl.BlockSpec((rows, D_MODEL), lambda i: (0, 0)), _full((1, D_MODEL)),
                  _full((D_MODEL, 2 * W_M)), _full((1, W_M))],
        out_specs=[out, out],
        out_shape=[jax.ShapeDtypeStruct((rows, W_M), F32)] * 2,
        compiler_params=_params("arbitrary"),
        name="mem_kv",
    )(mem.reshape(rows, D_MODEL), wts["g_mem"], wts["w_mem_kv"], wts["g_km"])
    return mk.reshape(b, N_MEM, W_M), mv.reshape(b, N_MEM, W_M)


MEM_ROWS = 128
MEM_STREAMS = 16


def _attn_mem_step_kernel(q_ref, mk_ref, mv_ref, sz_ref, o_ref):
    for b in range(q_ref.shape[0]):
        scores, values = [], []
        for h in range(H_M):
            lanes = slice(h * HEAD_DIM, (h + 1) * HEAD_DIM)
            scores.append([(_dot(q_ref[b, :, lanes], mk_ref[b, h].astype(BF16)), None)])
            values.append([functools.partial(_dot_nt, b=mv_ref[b, h].astype(BF16))])
        o = jnp.concatenate(_softmax_pv_staged(scores, values), axis=-1)
        o_ref[b] = (o * sz_ref[b].astype(F32)).astype(BF16)


def _attn_mem_step(q, mk_t, mv_t, sz):
    b, t, _ = q.shape
    nb = min(b, MEM_STREAMS)
    assert b % nb == 0 and mk_t.shape[1:] == (H_M, HEAD_DIM, N_MEM)
    rows = pl.BlockSpec((nb, t, W_M), lambda i: (i, 0, 0))
    mem = pl.BlockSpec((nb, H_M, HEAD_DIM, N_MEM), lambda i: (i, 0, 0, 0))
    return pl.pallas_call(
        _attn_mem_step_kernel,
        grid=(b // nb,),
        in_specs=[rows, mem, mem, rows],
        out_specs=rows,
        out_shape=jax.ShapeDtypeStruct((b, t, W_M), BF16),
        compiler_params=_params("parallel"),
        name="attn_mem_step",
    )(q, mk_t, mv_t, sz)


def _gla_constants(c, group):
    r = lax.broadcasted_iota(jnp.int32, (c, c), 0)
    s = lax.broadcasted_iota(jnp.int32, (c, c), 1)
    causal = (s <= r) & (r // group == s // group)
    head_of_k = lax.broadcasted_iota(jnp.int32, (W_BK, W_BV), 0) // DK_B
    head_of_v = lax.broadcasted_iota(jnp.int32, (W_BK, W_BV), 1) // DV_B
    head_of_row = lax.broadcasted_iota(jnp.int32, (H_B, W_BK, c), 1) // DK_B
    head = lax.broadcasted_iota(jnp.int32, (H_B, W_BK, c), 0)
    return causal.astype(BF16), (head_of_k == head_of_v).astype(BF16), (head_of_row == head).astype(BF16)


def _gla_intra(items, group, between=lambda: None):
    n = len(items)
    qs, ks, vs, v16s = ([it[j] for it in items] for j in range(4))
    c = qs[0].shape[0]
    las = [it[4] * LOG2E for it in items]
    causal16, spread, head_rows = _gla_constants(c, group)
    head_rows = [head_rows[h] for h in range(H_B)]
    cums = [_dot_exact_lhs(causal16, la) for la in las]
    cumxs = [cum - la for cum, la in zip(cums, las)]
    r_ck = lax.broadcasted_iota(jnp.int32, (c, W_BK), 0)

    os = [_dot((q * k).astype(BF16), spread) * v for q, k, v in zip(qs, ks, vs)]

    def shifted(x, j):
        return pltpu.roll(x.reshape(c // SUBLANES, SUBLANES, W_BK), j % SUBLANES, 1).reshape(c, W_BK)

    def block_edges(cum, cumx, half):
        if half >= SUBLANES:
            nb = c // half
            first = jnp.broadcast_to(cumx.reshape(nb, half, W_BK)[:, 0:1, :], (nb, half, W_BK))
            last = jnp.broadcast_to(cum.reshape(nb, half, W_BK)[:, half - 1:half, :], (nb, half, W_BK))
            return first.reshape(c, W_BK), last.reshape(c, W_BK)
        r_in = r_ck % half
        first, last = cumx, cum
        for j in range(1, half):
            first = jnp.where(r_in == j, shifted(cumx, j), first)
            last = jnp.where(r_in == half - 1 - j, shifted(cum, -j), last)
        return first, last

    t_xor_s = lax.broadcasted_iota(jnp.int32, (c, H_B * c), 0) ^ (lax.broadcasted_iota(jnp.int32, (c, H_B * c), 1) % c)
    atts = [jnp.zeros((c, H_B * c), F32) for _ in range(n)]
    half = 1
    while half < group:
        edges = [block_edges(cum, cumx, half) for cum, cumx in zip(cums, cumxs)]
        odd = (r_ck & half) != 0
        qts = [jnp.where(odd, q * jnp.exp2(cum - first), 0.0).astype(BF16)
               for q, cum, (first, _) in zip(qs, cums, edges)]
        kts = [jnp.where(odd, 0.0, k * jnp.exp2(last - cum)).T.astype(BF16)
               for k, cum, (_, last) in zip(ks, cums, edges)]
        prods = [_dot(qt, jnp.concatenate([kt * m for m in head_rows], axis=1)) for qt, kt in zip(qts, kts)]
        if 2 * half < c:
            prods = [jnp.where(t_xor_s < 2 * half, a, 0.0) for a in prods]
        atts = [att + a for att, a in zip(atts, prods)]
        between()
        half *= 2
    if group > 1:
        att16s = [att.astype(BF16) for att in atts]
        os = [o + jnp.concatenate([_dot(att16[:, h * c:(h + 1) * c], v16[:, h * DV_B:(h + 1) * DV_B])
                                   for h in range(H_B)], axis=-1)
              for o, att16, v16 in zip(os, att16s, v16s)]
    return list(zip(os, cums))


def _gla_state_step(q, k, v16, cum, s_old):
    c = q.shape[0]
    q_in = (q * jnp.exp2(cum)).astype(BF16)
    k_out_t = (k * jnp.exp2(cum[c - 1:c, :] - cum)).T.astype(BF16)
    keep = jnp.exp2(cum.T[:, c - 1:c])
    kd = lambda h: slice(h * DK_B, (h + 1) * DK_B)
    vd = lambda h: slice(h * DV_B, (h + 1) * DV_B)
    o = jnp.concatenate([_dot(q_in[:, kd(h)], s_old[kd(h)].astype(BF16)) for h in range(H_B)], axis=-1)
    s_new = jnp.concatenate([s_old[kd(h)] * keep[kd(h)] + _dot(k_out_t[kd(h)], v16[:, vd(h)])
                             for h in range(H_B)], axis=0)
    return o, s_new


def _gla_finish(o, g_ref, sz):
    outs = [_rms_rows(o[:, h * DV_B:(h + 1) * DV_B], g_ref[:, h * DV_B:(h + 1) * DV_B]) for h in range(H_B)]
    return (jnp.concatenate(outs, axis=-1) * sz.astype(F32)).astype(BF16)


def _gla_step_kernel(q_ref, k_ref, v_ref, la_ref, sz_ref, g_ref, s0_ref, o_ref, s_ref, *, t):
    c = GLA_C
    n = c // t
    q, k, la = q_ref[...].astype(F32), k_ref[...].astype(F32), la_ref[...]
    v16 = v_ref[...]
    (o, cum), = _gla_intra([(q, k, v16.astype(F32), v16, la)], t)

    q_in = (q * jnp.exp2(cum)).astype(BF16)
    cum_t, k_t = cum.T, k.T
    stream_of_col = lax.broadcasted_iota(jnp.int32, (W_BK, c), 1) // t
    inter = []
    for s in range(n):
        rows = slice(s * t, (s + 1) * t)
        last = cum_t[:, (s + 1) * t - 1:(s + 1) * t]
        k_out_t = (k_t * jnp.exp2(jnp.where(stream_of_col == s, last - cum_t, NEG))).astype(BF16)
        keep = jnp.exp2(last)
        o_s = []
        for h in range(H_B):
            ks = slice(h * DK_B, (h + 1) * DK_B)
            vs = slice(h * DV_B, (h + 1) * DV_B)
            s_old = s0_ref[s, h]
            o_s.append(_dot(q_in[rows, ks], s_old.astype(BF16)))
            s_ref[s, h] = s_old * keep[ks] + _dot(k_out_t[ks], v16[:, vs])
        inter.append(jnp.concatenate(o_s, axis=-1))
    o_ref[...] = _gla_finish(o + jnp.concatenate(inter, axis=0), g_ref, sz_ref[...])


def _gla_step(q, k, v, la, sz, g, s0):
    b, t, _ = q.shape
    n = GLA_C // t
    assert GLA_C % t == 0 and t % (2 * SUBLANES) == 0 and b % n == 0
    flat = lambda z: z.reshape(b * t, z.shape[-1])
    blk = lambda w: pl.BlockSpec((GLA_C, w), lambda i: (i, 0))
    state = pl.BlockSpec((n, H_B, DK_B, DV_B), lambda i: (i, 0, 0, 0))
    o, s_new = pl.pallas_call(
        functools.partial(_gla_step_kernel, t=t),
        grid=(b // n,),
        in_specs=[blk(W_BK), blk(W_BK), blk(W_BV), blk(W_BK), blk(W_BV),
                  pl.BlockSpec((1, W_BV), lambda i: (0, 0)), state],
        out_specs=[blk(W_BV), state],
        out_shape=[jax.ShapeDtypeStruct((b * t, W_BV), BF16),
                   jax.ShapeDtypeStruct((b, H_B, DK_B, DV_B), F32)],
        compiler_params=_params("parallel"),
        name="gla_step",
    )(flat(q), flat(k), flat(v), flat(la), flat(sz), g, s0)
    return o.reshape(b, t, W_BV), s_new


def _prep_weights(l, norm_in, w_in, g_qa, g_ka, w_gate2, b_gate, g_gla_out, g_mem, w_mem_kv, g_qm, g_km,
                  w_up_a, w_up_b, w_up_m, w_out):
    tile = lambda gain, n: jnp.tile(gain, n).reshape(1, -1)
    halved = jnp.concatenate([jnp.full((size,), 0.5 if i in _HALVED_GROUPS else 1.0, F32)
                              for i, size in enumerate(IN_SIZES)])
    return {
        "norm_in": norm_in[l].reshape(1, D_MODEL),
        "w_in_t": (w_in[l].T * halved[:, None]).astype(BF16),
        "w_gate2": w_gate2[l].astype(BF16),
        "b_gate": b_gate[l].reshape(1, W_BK),
        "g_qa": tile(g_qa[l], H_A), "g_ka": tile(g_ka[l], H_A), "g_qm": tile(g_qm[l], H_M),
        "g_km": tile(g_km[l], H_M), "g_gla": tile(g_gla_out[l], H_B),
        "g_mem": g_mem[l].reshape(1, D_MODEL),
        "w_mem_kv": w_mem_kv[l].astype(BF16),
        "w_up_a": w_up_a[l].astype(BF16), "w_up_b": w_up_b[l].astype(BF16),
        "w_up_m": w_up_m[l].astype(BF16), "w_out": w_out[l].astype(BF16),
    }


def _layer_long(x, wts, bias, mk, mv, s0):
    b, t, _ = x.shape
    keep = min(BAND, t)
    r3 = lambda z: z.reshape(b, t, z.shape[-1])
    heads = lambda z: jnp.transpose(z.reshape(b, H_A, HEAD_DIM, keep), (0, 3, 1, 2))
    ka, va, s_new, acts, gates = _front(x.reshape(b * t, D_MODEL), wts, keep, t, keep, s0)
    y = _attn_back(r3(acts), r3(gates), bias, mk, mv, x, wts)
    return y, heads(ka), heads(va), s_new


def _layer_short(x, wts, attend_a, attend_m, s0, tm):
    b, t, _ = x.shape
    keep = min(BAND, t)
    x2d = x.reshape(b * t, D_MODEL)
    r3 = lambda z: z.reshape(b, t, z.shape[-1])
    flat = lambda z: z.reshape(b * t, z.shape[-1])
    heads = lambda z: z.reshape(b, keep, H_A, HEAD_DIM)
    (qa, ka, va, ka16, va16, sza, qb, kb, vb, la, szb, qm, szm, ga, gb, gm) = _front(x2d, wts, tm, t, keep)
    out_b, s_new = _gla_step(r3(qb), r3(kb), r3(vb), r3(la), r3(szb), wts["g_gla"], s0)
    out_a = attend_a(r3(qa), r3(ka16), r3(va16), r3(sza))
    out_m = attend_m(r3(qm), r3(szm))
    y = _back(x2d, flat(out_a), flat(out_b), flat(out_m), ga, gb, gm, wts, tm)
    return y.reshape(b, t, D_MODEL), heads(ka), heads(va), s_new


def kernel(x_prompt, x_sample, mem_prompt, cache_a_k, cache_a_v, state_gla, cache_mem_k, cache_mem_v,
           norm_in, w_in, g_qa, g_ka, rel_bias, w_gate2, b_gate, g_gla_out, g_mem, w_mem_kv, g_qm, g_km,
           w_up_a, w_up_b, w_up_m, w_out):
    depth = w_in.shape[0]
    xp, xs = x_prompt, x_sample
    bp, tp, _ = xp.shape
    bs, ts, _ = xs.shape
    akp, avp, sgp, mkp, mvp, aks, avs, sgs = [], [], [], [], [], [], [], []
    for l in range(depth):
        wts = _prep_weights(l, norm_in, w_in, g_qa, g_ka, w_gate2, b_gate, g_gla_out, g_mem, w_mem_kv,
                            g_qm, g_km, w_up_a, w_up_b, w_up_m, w_out)
        bias = _bias_tile(rel_bias[l])
        mk, mv = _mem_kv(mem_prompt, wts)
        s0 = jnp.zeros((bp, H_B, DK_B, DV_B), state_gla.dtype)
        xp, ka, va, sp = _layer_long(xp, wts, bias, mk, mv, s0)
        akp.append(ka)
        avp.append(va)
        sgp.append(sp)
        mkp.append(mk.reshape(bp, N_MEM, H_M, HEAD_DIM))
        mvp.append(mv.reshape(bp, N_MEM, H_M, HEAD_DIM))
        rows_last = lambda z: jnp.transpose(z, (0, 2, 3, 1))
        past_k, past_v = rows_last(cache_a_k[l]), rows_last(cache_a_v[l])
        mem_k, mem_v = rows_last(cache_mem_k[l]), rows_last(cache_mem_v[l])
        attend_s = lambda q, k, v, sz: _attn_step(q, k, v, past_k, past_v, sz, bias)
        attend_ms = lambda q, sz: _attn_mem_step(q, mem_k, mem_v, sz)
        xs, ka_s, va_s, ss = _layer_short(xs, wts, attend_s, attend_ms, state_gla[l], 512)
        aks.append(ka_s)
        avs.append(va_s)
        sgs.append(ss)
    return (xp, xs, jnp.stack(akp), jnp.stack(avp), jnp.stack(sgp), jnp.stack(mkp), jnp.stack(mvp),
            jnp.stack(aks), jnp.stack(avs), jnp.stack(sgs))
```

```python
import functools

import jax
import jax.numpy as jnp
from jax import lax
from jax.experimental import pallas as pl
from jax.experimental.pallas import tpu as pltpu

F32 = jnp.float32
BF16 = jnp.bfloat16

D_MODEL = 1024
CHUNK = 64
LEFT_CHUNKS = 8
HEAD_DIM = 64
H_A = 8
W_A = H_A * HEAD_DIM
REL_CLIP = 128
H_B = 4
DK_B = 64
DV_B = 128
W_BK = H_B * DK_B
W_BV = H_B * DV_B
GATE_RANK = 16
GATE_TAU = 16.0
N_MEM = 256
H_M = 4
W_M = H_M * HEAD_DIM
EPS = 1e-6
IN_SIZES = (W_A, W_A, W_A, W_A, W_BK, W_BK, W_BV, GATE_RANK, W_BV, W_M, W_M, D_MODEL, D_MODEL, D_MODEL)

LANES = 128
SUBLANES = 8
VMEM_LIMIT = 56 * 1024 * 1024
NEG = -1e30
LOG2E = 1.4426950408889634

BAND = LEFT_CHUNKS * CHUNK
QBLK = 512
KSPAN = BAND + QBLK
WINDOW = BAND + CHUNK
BIAS_W = BAND + 2 * CHUNK
GLA_C = 128
PROJ_PER_LEVEL = 5


def _dot(a, b):
    return jnp.dot(a, b, preferred_element_type=F32)


def _dot_nt(a, b):
    return lax.dot_general(a, b, (((1,), (1,)), ((), ())), preferred_element_type=F32)


def _split3(x):
    hi = x.astype(BF16)
    r = x - hi.astype(F32)
    mid = r.astype(BF16)
    lo = (r - mid.astype(F32)).astype(BF16)
    return hi, mid, lo


def _dot_exact_lhs(a01, x):
    hi, mid, lo = _split3(x)
    return _dot(a01, hi) + _dot(a01, mid) + _dot(a01, lo)


def _sigmoid_of_twice(p):
    return 0.5 * jnp.tanh(p) + 0.5


def _silu_of_twice(p):
    return p * jnp.tanh(p) + p


def _log_sigmoid(z):
    return jnp.minimum(z, 0.0) - jnp.log1p(jnp.exp(-jnp.abs(z)))


def _rms_rows(x, g):
    ms = jnp.mean(x * x, axis=-1, keepdims=True)
    return x * lax.rsqrt(ms + EPS) * g


def _head_mean_matrix(width, head):
    r = lax.broadcasted_iota(jnp.int32, (width, width), 0) // head
    c = lax.broadcasted_iota(jnp.int32, (width, width), 1) // head
    return jnp.where(r == c, 1.0 / head, 0.0).astype(BF16)


def _head_rms(x, g, avg):
    ms = _dot((x * x).astype(BF16), avg)
    return x * lax.rsqrt(ms + EPS) * g


def _full(shape):
    nd = len(shape)
    return pl.BlockSpec(shape, lambda *_: (0,) * nd, pipeline_mode=pl.Buffered(1))


def _params(*sem):
    return pltpu.CompilerParams(dimension_semantics=sem, vmem_limit_bytes=VMEM_LIMIT)


_IN_OFF = tuple(int(sum(IN_SIZES[:i])) for i in range(len(IN_SIZES) + 1))
_HALVED_GROUPS = (3, 8, 10, 11, 12, 13)
D_IN = _IN_OFF[-1]


def _col_table(groups):
    table, off = {}, 0
    for name, width in groups:
        table[name] = (off, width)
        off += width
    return table, off


_ACT_COLS, ACT_W = _col_table((("qa", W_A), ("ka", W_A), ("va", W_A), ("sza", W_A), ("out_b", W_BV),
                               ("qm", W_M), ("szm", W_M)))
_GATE_COLS, GATE_W = _col_table((("a", D_MODEL), ("b", D_MODEL), ("m", D_MODEL)))


def _front_kernel(x_ref, nin_ref, w_ref, wg2_ref, bg_ref, gqa_ref, gka_ref, gqm_ref, *rest, per_stream=None):
    fused = per_stream is not None
    if fused:
        (ggla_ref, s0_ref, kat_o, vat_o, sfin_o, acts_o, gates_o, s_scr, ka_o, va_o) = rest
        cols = lambda ref, name, table: ref.at[:, table[name][0]:table[name][0] + table[name][1]]
        qa_o, ka16_o, va16_o, sza_o, outb_o, qm_o, szm_o = (cols(acts_o, k, _ACT_COLS) for k in _ACT_COLS)
        ga_o, gb_o, gm_o = (cols(gates_o, k, _GATE_COLS) for k in _GATE_COLS)
    else:
        (qa_o, ka_o, va_o, ka16_o, va16_o, sza_o, qb_o, kb_o, vb_o, la_o, szb_o, qm_o, szm_o,
         ga_o, gb_o, gm_o) = rest
    if fused:
        @pl.when(pl.program_id(0) % per_stream == 0)
        def _():
            s_scr[...] = s0_ref[pl.program_id(0) // per_stream].reshape(W_BK, DV_B)

    h = _rms_rows(x_ref[...], nin_ref[...]).astype(BF16)

    def proj(i):
        return _dot_nt(h, w_ref[_IN_OFF[i]:_IN_OFF[i + 1], :])

    z = _dot(proj(7).astype(BF16), wg2_ref[...]) + bg_ref[...]
    la = _log_sigmoid(z) * (1.0 / GATE_TAU)
    qb = (proj(4) * (DK_B ** -0.5)).astype(BF16)
    kb = proj(5).astype(BF16)
    vb = proj(6).astype(BF16)
    szb = _silu_of_twice(proj(8)).astype(BF16)

    scale = HEAD_DIM ** -0.5 * LOG2E
    avg_a = _head_mean_matrix(W_A, HEAD_DIM)

    def emit_qa():
        qa_o[...] = (_head_rms(proj(0), gqa_ref[...], avg_a) * scale).astype(BF16)

    def emit_ka():
        ka = _head_rms(proj(1), gka_ref[...], avg_a)
        ka_o[...] = ka
        ka16_o[...] = ka.astype(BF16)

    def emit_va():
        va = proj(2)
        va_o[...] = va
        va16_o[...] = va.astype(BF16)

    def emit_sza():
        sza_o[...] = _silu_of_twice(proj(3)).astype(BF16)

    def emit_qm():
        qm_o[...] = (_head_rms(proj(9), gqm_ref[...], _head_mean_matrix(W_M, HEAD_DIM)) * scale).astype(BF16)

    def emit_szm():
        szm_o[...] = _silu_of_twice(proj(10)).astype(BF16)

    def emit_gate(i, ref):
        ref[...] = _sigmoid_of_twice(proj(i)).astype(BF16)

    pending = [emit_qa, emit_ka, emit_va, emit_sza, emit_qm, emit_szm,
               functools.partial(emit_gate, 11, ga_o), functools.partial(emit_gate, 12, gb_o),
               functools.partial(emit_gate, 13, gm_o)]

    def emit_next():
        if pending:
            pending.pop(0)()

    if fused:
        chunks = [slice(r, r + GLA_C) for r in range(0, x_ref.shape[0], GLA_C)]
        items = [(qb[rows].astype(F32), kb[rows].astype(F32), vb[rows].astype(F32), vb[rows], la[rows])
                 for rows in chunks]
        state = s_scr[...]
        outs = []
        def emit_some():
            for _ in range(PROJ_PER_LEVEL):
                emit_next()

        for rows, item, (o, cum) in zip(chunks, items, _gla_intra(items, GLA_C, emit_some)):
            inter, state = _gla_state_step(item[0], item[1], item[3], cum, state)
            outs.append(_gla_finish(o + inter, ggla_ref, szb[rows]))
            emit_next()
        s_scr[...] = state
        outb_o[...] = jnp.concatenate(outs, axis=0)
    else:
        qb_o[...] = qb
        kb_o[...] = kb
        vb_o[...] = vb
        la_o[...] = la
        szb_o[...] = szb
    while pending:
        emit_next()

    if fused:
        @pl.when(pl.program_id(0) % per_stream == per_stream - 1)
        def _():
            sfin_o[0] = s_scr[...].reshape(H_B, DK_B, DV_B)
            kat_o[0] = ka_o[...].T
            vat_o[0] = va_o[...].T


def _front(x2d, wts, tm, t, keep, s0=None):
    n = x2d.shape[0]
    tm = min(tm, n)
    row = lambda w: pl.BlockSpec((tm, w), lambda i: (i, 0))
    out = lambda rows_, w, d, spec: (jax.ShapeDtypeStruct((rows_, w), d), spec)
    tail = [out(n, W_M, BF16, row(W_M)), out(n, W_M, BF16, row(W_M))] + [out(n, D_MODEL, BF16, row(D_MODEL))] * 3
    in_specs = [row(D_MODEL), _full((1, D_MODEL)), _full((D_IN, D_MODEL)),
                _full((GATE_RANK, W_BK)), _full((1, W_BK)), _full((1, W_A)), _full((1, W_A)), _full((1, W_M))]
    args = [x2d, wts["norm_in"], wts["w_in_t"], wts["w_gate2"], wts["b_gate"], wts["g_qa"], wts["g_ka"], wts["g_qm"]]
    if s0 is None:
        assert keep == t
        kv = [out(n, W_A, F32, row(W_A))] * 2
        mid = [out(n, W_BK, BF16, row(W_BK)), out(n, W_BK, BF16, row(W_BK)), out(n, W_BV, BF16, row(W_BV)),
               out(n, W_BK, F32, row(W_BK)), out(n, W_BV, BF16, row(W_BV))]
        body, scratch = _front_kernel, []
    else:
        assert t % tm == 0 and tm % GLA_C == 0 and keep == tm
        streams, per_stream = n // t, t // tm
        state = (streams, H_B, DK_B, DV_B)
        kv = [(jax.ShapeDtypeStruct((streams, W_A, keep), F32),
               pl.BlockSpec((1, W_A, keep), lambda i: (i // per_stream, 0, 0)))] * 2
        outs = kv + [(jax.ShapeDtypeStruct(state, F32),
                      pl.BlockSpec((1,) + state[1:], lambda i: (i // per_stream, 0, 0, 0))),
                     out(n, ACT_W, BF16, row(ACT_W)), out(n, GATE_W, BF16, row(GATE_W))]
        in_specs += [_full((1, W_BV)), _full(state)]
        args += [wts["g_gla"], s0]
        body = functools.partial(_front_kernel, per_stream=per_stream)
        scratch = [pltpu.VMEM((W_BK, DV_B), F32), pltpu.VMEM((tm, W_A), F32), pltpu.VMEM((tm, W_A), F32)]
    if s0 is None:
        outs = [out(n, W_A, BF16, row(W_A))] + kv + [out(n, W_A, BF16, row(W_A))] * 3 + mid + tail
    return pl.pallas_call(
        body,
        grid=(n // tm,),
        in_specs=in_specs,
        out_specs=[spec for _, spec in outs],
        out_shape=[shape for shape, _ in outs],
        scratch_shapes=scratch,
        compiler_params=_params("arbitrary"),
        name="front",
    )(*args)


def _back_kernel(x_ref, a_ref, b_ref, m_ref, ga_ref, gb_ref, gm_ref, wa_ref, wb_ref, wm_ref, wo_ref, y_ref):
    u = (ga_ref[...].astype(F32) * _dot(a_ref[...], wa_ref[...])
         + gb_ref[...].astype(F32) * _dot(b_ref[...], wb_ref[...])
         + gm_ref[...].astype(F32) * _dot(m_ref[...], wm_ref[...]))
    y_ref[...] = x_ref[...] + _dot(u.astype(BF16), wo_ref[...])


def _back(x2d, a, b, m, ga, gb, gm, wts, tm):
    n = x2d.shape[0]
    tm = min(tm, n)
    row = lambda w: pl.BlockSpec((tm, w), lambda i: (i, 0))
    return pl.pallas_call(
        _back_kernel,
        grid=(n // tm,),
        in_specs=[row(D_MODEL), row(W_A), row(W_BV), row(W_M), row(D_MODEL), row(D_MODEL), row(D_MODEL),
                  _full((W_A, D_MODEL)), _full((W_BV, D_MODEL)), _full((W_M, D_MODEL)),
                  _full((D_MODEL, D_MODEL))],
        out_specs=row(D_MODEL),
        out_shape=jax.ShapeDtypeStruct((n, D_MODEL), F32),
        compiler_params=_params("parallel"),
        name="back",
    )(x2d, a, b, m, ga, gb, gm, wts["w_up_a"], wts["w_up_b"], wts["w_up_m"], wts["w_out"])


def _bias_kernel(e_ref, tile_ref):
    for h in range(H_A):
        e = jnp.broadcast_to(e_ref[h] * LOG2E, (CHUNK, 2 * BAND))
        tile_ref[h] = pltpu.roll(e, 0, 1, stride=1, stride_axis=0)[:, :BIAS_W]


def _bias_tile(rel_bias):
    j = jnp.arange(2 * BAND)
    dist = jnp.where(j < BIAS_W, BAND - j, REL_CLIP)
    e = rel_bias[:, jnp.clip(dist, -REL_CLIP, REL_CLIP) + REL_CLIP].reshape(H_A, 1, 2 * BAND)
    return pl.pallas_call(
        _bias_kernel,
        grid=(1,),
        in_specs=[pl.BlockSpec((H_A, 1, 2 * BAND), lambda i: (0, 0, 0))],
        out_specs=pl.BlockSpec((H_A, CHUNK, BIAS_W), lambda i: (0, 0, 0)),
        out_shape=jax.ShapeDtypeStruct((H_A, CHUNK, BIAS_W), F32),
        compiler_params=_params("arbitrary"),
        name="bias_tile",
    )(e)


def _softmax_pv_staged(scores, values):
    biased = [[s if b is None else s + b for s, b in s_list] for s_list in scores]
    tops = [functools.reduce(jnp.maximum, [jnp.max(s, axis=-1, keepdims=True) for s in s_list])
            for s_list in biased]
    probs = [[jnp.exp2(s - m) for s in s_list] for s_list, m in zip(biased, tops)]
    sums = [functools.reduce(jnp.add, [jnp.sum(x, axis=-1, keepdims=True) for x in p_list]) for p_list in probs]
    return [functools.reduce(jnp.add, [(v(x.astype(BF16)) if callable(v) else _dot(x.astype(BF16), v))
                                       for x, v in zip(p_list, v_list)]) / l
            for p_list, v_list, l in zip(probs, values, sums)]


def _head_pairs_attention(q_pairs, k_lists, v_lists, bias_fn):
    m_rows = q_pairs[0].shape[0]
    lane = lax.broadcasted_iota(jnp.int32, (m_rows, LANES), 1)
    first = lane < HEAD_DIM
    scores = []
    for p, qp in enumerate(q_pairs):
        zero = jnp.zeros_like(qp)
        q2 = jnp.concatenate([jnp.where(first, qp, zero), jnp.where(first, zero, qp)], axis=0)
        s_list = []
        for i, k in enumerate(k_lists[p]):
            b0, b1 = bias_fn(p, 0, i), bias_fn(p, 1, i)
            s_list.append((_dot_nt(q2, k), None if b0 is None else jnp.concatenate([b0, b1], axis=0)))
        scores.append(s_list)
    outs = _softmax_pv_staged(scores, v_lists)
    return [jnp.where(first, o[:m_rows], o[m_rows:]) for o in outs]


def _attn_back_kernel(q_ref, *refs):
    nk = BAND // QBLK + 1
    k_refs, v_refs = refs[:nk], refs[nk:2 * nk]
    (sz_ref, bias_ref, qm_ref, mk_ref, mv_ref, szm_ref, x_ref, b_ref, ga_ref, gb_ref, gm_ref,
     wa_ref, wb_ref, wm_ref, wo_ref, y_ref, k_buf, v_buf, bias_buf, a_buf, m_buf) = refs[2 * nk:]
    i = pl.program_id(1)

    @pl.when(i <= BAND // QBLK)
    def _():
        col = lax.broadcasted_iota(jnp.int32, (1, BIAS_W), 1)
        for c in range(QBLK // CHUNK):
            before_start = jnp.where(col + c * CHUNK < (BAND // QBLK - i) * QBLK, NEG, 0.0)
            for h in range(H_A):
                bias_buf[c, h] = bias_ref[h] + before_start

    for j, (k_ref, v_ref) in enumerate(zip(k_refs, v_refs)):
        k_buf[j * QBLK:(j + 1) * QBLK, :] = k_ref[0]
        v_buf[j * QBLK:(j + 1) * QBLK, :] = v_ref[0]

    lanes_m = [slice(p * LANES, (p + 1) * LANES) for p in range(H_M // 2)]
    mk = [[mk_ref[0, :, lanes].astype(BF16)] for lanes in lanes_m]
    mv = [[mv_ref[0, :, lanes].astype(BF16)] for lanes in lanes_m]
    blocks = [(slice(r, r + MEM_ROWS), p) for r in range(0, QBLK, MEM_ROWS) for p in range(len(lanes_m))]
    outs = _head_pairs_attention([qm_ref[0, rows, lanes_m[p]] for rows, p in blocks],
                                 [mk[p] for _, p in blocks], [mv[p] for _, p in blocks], lambda n, e, _: None)
    for (rows, p), o in zip(blocks, outs):
        m_buf[rows, lanes_m[p]] = (o * szm_ref[0, rows, lanes_m[p]].astype(F32)).astype(BF16)

    pairs = [slice(p * LANES, (p + 1) * LANES) for p in range(H_A // 2)]
    for c in range(QBLK // CHUNK):
        rows = slice(c * CHUNK, (c + 1) * CHUNK)
        win = slice(c * CHUNK, c * CHUNK + WINDOW)
        outs = _head_pairs_attention(
            [q_ref[0, rows, lanes] for lanes in pairs],
            [[k_buf[win, lanes]] for lanes in pairs], [[v_buf[win, lanes]] for lanes in pairs],
            lambda p, e, _: bias_buf[c, 2 * p + e, :, :WINDOW])
        for lanes, o in zip(pairs, outs):
            a_buf[rows, lanes] = (o * sz_ref[0, rows, lanes].astype(F32)).astype(BF16)

    u = (ga_ref[0].astype(F32) * _dot(a_buf[...], wa_ref[...])
         + gb_ref[0].astype(F32) * _dot(b_ref[0], wb_ref[...])
         + gm_ref[0].astype(F32) * _dot(m_buf[...], wm_ref[...]))
    y_ref[0] = x_ref[0] + _dot(u.astype(BF16), wo_ref[...])


def _attn_back(acts, gates, bias, mk, mv, x, wts):
    b, t, _ = x.shape
    nb = BAND // QBLK
    n = t // QBLK
    assert BAND % QBLK == 0 and t % QBLK == 0 and QBLK % MEM_ROWS == 0

    def group(table, name, d=0):
        off, w = table[name]
        assert off % w == 0
        return pl.BlockSpec((1, QBLK, w), lambda bi, i: (bi, jnp.maximum(i - d, 0), off // w))

    act = functools.partial(group, _ACT_COLS)
    gate = functools.partial(group, _GATE_COLS)
    mem = pl.BlockSpec((1, N_MEM, W_M), lambda bi, i: (bi, 0, 0))
    return pl.pallas_call(
        _attn_back_kernel,
        grid=(b, n),
        in_specs=[act("qa")] + [act("ka", nb - j) for j in range(nb + 1)] + [act("va", nb - j) for j in range(nb + 1)]
                 + [act("sza"), _full((H_A, CHUNK, BIAS_W)), act("qm"), mem, mem, act("szm"),
                    pl.BlockSpec((1, QBLK, D_MODEL), lambda bi, i: (bi, i, 0)), act("out_b"),
                    gate("a"), gate("b"), gate("m"),
                    _full((W_A, D_MODEL)), _full((W_BV, D_MODEL)), _full((W_M, D_MODEL)),
                    _full((D_MODEL, D_MODEL))],
        out_specs=pl.BlockSpec((1, QBLK, D_MODEL), lambda bi, i: (bi, i, 0)),
        out_shape=jax.ShapeDtypeStruct((b, t, D_MODEL), F32),
        scratch_shapes=[pltpu.VMEM((KSPAN, W_A), BF16), pltpu.VMEM((KSPAN, W_A), BF16),
                        pltpu.VMEM((QBLK // CHUNK, H_A, CHUNK, BIAS_W), F32),
                        pltpu.VMEM((QBLK, W_A), BF16), pltpu.VMEM((QBLK, W_M), BF16)],
        compiler_params=_params("parallel", "arbitrary"),
        name="attn_back",
    )(*([acts] * (2 * nb + 4)), bias, acts, mk, mv, acts, x, acts, gates, gates, gates,
      wts["w_up_a"], wts["w_up_b"], wts["w_up_m"], wts["w_out"])


STEP_STREAMS = 8


def _attn_step_kernel(q_ref, kn_ref, vn_ref, kp_ref, vp_ref, sz_ref, bias_ref, o_ref):
    s_len = q_ref.shape[1]
    for b in range(STEP_STREAMS):
        scores, values = [], []
        for h in range(H_A):
            lanes = slice(h * HEAD_DIM, (h + 1) * HEAD_DIM)
            q = q_ref[b, :, lanes]
            scores.append([(_dot(q, kp_ref[b, h].astype(BF16)), bias_ref[h, :s_len, :BAND]),
                           (_dot_nt(q, kn_ref[b, :, lanes]), bias_ref[h, :s_len, BAND:BAND + s_len])])
            values.append([functools.partial(_dot_nt, b=vp_ref[b, h].astype(BF16)), vn_ref[b, :, lanes]])
        o = jnp.concatenate(_softmax_pv_staged(scores, values), axis=-1)
        o_ref[b] = (o * sz_ref[b].astype(F32)).astype(BF16)


def _attn_step(q, k_new, v_new, k_past_t, v_past_t, sz, bias):
    b, s_len, _ = q.shape
    assert k_past_t.shape[1:] == (H_A, HEAD_DIM, BAND) and s_len <= CHUNK and b % STEP_STREAMS == 0
    new = pl.BlockSpec((STEP_STREAMS, s_len, W_A), lambda i: (i, 0, 0))
    past = pl.BlockSpec((STEP_STREAMS, H_A, HEAD_DIM, BAND), lambda i: (i, 0, 0, 0))
    return pl.pallas_call(
        _attn_step_kernel,
        grid=(b // STEP_STREAMS,),
        in_specs=[new, new, new, past, past, new, _full((H_A, CHUNK, BIAS_W))],
        out_specs=new,
        out_shape=jax.ShapeDtypeStruct((b, s_len, W_A), BF16),
        compiler_params=_params("parallel"),
        name="attn_step",
    )(q, k_new, v_new, k_past_t, v_past_t, sz, bias)


def _mem_kv_kernel(mem_ref, gmem_ref, w_ref, gkm_ref, mk_ref, mv_ref):
    h = _rms_rows(mem_ref[...], gmem_ref[...]).astype(BF16)
    kv = _dot(h, w_ref[...])
    mk_ref[...] = _head_rms(kv[:, :W_M], gkm_ref[...], _head_mean_matrix(W_M, HEAD_DIM))
    mv_ref[...] = kv[:, W_M:]


def _mem_kv(mem, wts):
    b = mem.shape[0]
    rows = b * N_MEM
    out = pl.BlockSpec((rows, W_M), lambda i: (0, 0))
    mk, mv = pl.pallas_call(
        _mem_kv_kernel,
        grid=(1,),
        in_specs=[pl.BlockSpec((rows, D_MODEL), lambda i: (0, 0)), _full((1, D_MODEL)),
                  _full((D_MODEL, 2 * W_M)), _full((1, W_M))],
        out_specs=[out, out],
        out_shape=[jax.ShapeDtypeStruct((rows, W_M), F32)] * 2,
        compiler_params=_params("arbitrary"),
        name="mem_kv",
    )(mem.reshape(rows, D_MODEL), wts["g_mem"], wts["w_mem_kv"], wts["g_km"])
    return mk.reshape(b, N_MEM, W_M), mv.reshape(b, N_MEM, W_M)


MEM_ROWS = 128
MEM_STREAMS = 16


def _attn_mem_step_kernel(q_ref, mk_ref, mv_ref, sz_ref, o_ref):
    for b in range(q_ref.shape[0]):
        scores, values = [], []
        for h in range(H_M):
            lanes = slice(h * HEAD_DIM, (h + 1) * HEAD_DIM)
            scores.append([(_dot(q_ref[b, :, lanes], mk_ref[b, h].astype(BF16)), None)])
            values.append([functools.partial(_dot_nt, b=mv_ref[b, h].astype(BF16))])
        o = jnp.concatenate(_softmax_pv_staged(scores, values), axis=-1)
        o_ref[b] = (o * sz_ref[b].astype(F32)).astype(BF16)


def _attn_mem_step(q, mk_t, mv_t, sz):
    b, t, _ = q.shape
    nb = min(b, MEM_STREAMS)
    assert b % nb == 0 and mk_t.shape[1:] == (H_M, HEAD_DIM, N_MEM)
    rows = pl.BlockSpec((nb, t, W_M), lambda i: (i, 0, 0))
    mem = pl.BlockSpec((nb, H_M, HEAD_DIM, N_MEM), lambda i: (i, 0, 0, 0))
    return pl.pallas_call(
        _attn_mem_step_kernel,
        grid=(b // nb,),
        in_specs=[rows, mem, mem, rows],
        out_specs=rows,
        out_shape=jax.ShapeDtypeStruct((b, t, W_M), BF16),
        compiler_params=_params("parallel"),
        name="attn_mem_step",
    )(q, mk_t, mv_t, sz)


def _gla_constants(c, group):
    r = lax.broadcasted_iota(jnp.int32, (c, c), 0)
    s = lax.broadcasted_iota(jnp.int32, (c, c), 1)
    causal = (s <= r) & (r // group == s // group)
    head_of_k = lax.broadcasted_iota(jnp.int32, (W_BK, W_BV), 0) // DK_B
    head_of_v = lax.broadcasted_iota(jnp.int32, (W_BK, W_BV), 1) // DV_B
    head_of_row = lax.broadcasted_iota(jnp.int32, (H_B, W_BK, c), 1) // DK_B
    head = lax.broadcasted_iota(jnp.int32, (H_B, W_BK, c), 0)
    return causal.astype(BF16), (head_of_k == head_of_v).astype(BF16), (head_of_row == head).astype(BF16)


def _gla_intra(items, group, between=lambda: None):
    n = len(items)
    qs, ks, vs, v16s = ([it[j] for it in items] for j in range(4))
    c = qs[0].shape[0]
    las = [it[4] * LOG2E for it in items]
    causal16, spread, head_rows = _gla_constants(c, group)
    head_rows = [head_rows[h] for h in range(H_B)]
    cums = [_dot_exact_lhs(causal16, la) for la in las]
    cumxs = [cum - la for cum, la in zip(cums, las)]
    r_ck = lax.broadcasted_iota(jnp.int32, (c, W_BK), 0)

    os = [_dot((q * k).astype(BF16), spread) * v for q, k, v in zip(qs, ks, vs)]

    def shifted(x, j):
        return pltpu.roll(x.reshape(c // SUBLANES, SUBLANES, W_BK), j % SUBLANES, 1).reshape(c, W_BK)

    def block_edges(cum, cumx, half):
        if half >= SUBLANES:
            nb = c // half
            first = jnp.broadcast_to(cumx.reshape(nb, half, W_BK)[:, 0:1, :], (nb, half, W_BK))
            last = jnp.broadcast_to(cum.reshape(nb, half, W_BK)[:, half - 1:half, :], (nb, half, W_BK))
            return first.reshape(c, W_BK), last.reshape(c, W_BK)
        r_in = r_ck % half
        first, last = cumx, cum
        for j in range(1, half):
            first = jnp.where(r_in == j, shifted(cumx, j), first)
            last = jnp.where(r_in == half - 1 - j, shifted(cum, -j), last)
        return first, last

    t_xor_s = lax.broadcasted_iota(jnp.int32, (c, H_B * c), 0) ^ (lax.broadcasted_iota(jnp.int32, (c, H_B * c), 1) % c)
    atts = [jnp.zeros((c, H_B * c), F32) for _ in range(n)]
    half = 1
    while half < group:
        edges = [block_edges(cum, cumx, half) for cum, cumx in zip(cums, cumxs)]
        odd = (r_ck & half) != 0
        qts = [jnp.where(odd, q * jnp.exp2(cum - first), 0.0).astype(BF16)
               for q, cum, (first, _) in zip(qs, cums, edges)]
        kts = [jnp.where(odd, 0.0, k * jnp.exp2(last - cum)).T.astype(BF16)
               for k, cum, (_, last) in zip(ks, cums, edges)]
        prods = [_dot(qt, jnp.concatenate([kt * m for m in head_rows], axis=1)) for qt, kt in zip(qts, kts)]
        if 2 * half < c:
            prods = [jnp.where(t_xor_s < 2 * half, a, 0.0) for a in prods]
        atts = [att + a for att, a in zip(atts, prods)]
        between()
        half *= 2
    if group > 1:
        att16s = [att.astype(BF16) for att in atts]
        os = [o + jnp.concatenate([_dot(att16[:, h * c:(h + 1) * c], v16[:, h * DV_B:(h + 1) * DV_B])
                                   for h in range(H_B)], axis=-1)
              for o, att16, v16 in zip(os, att16s, v16s)]
    return list(zip(os, cums))


def _gla_state_step(q, k, v16, cum, s_old):
    c = q.shape[0]
    q_in = (q * jnp.exp2(cum)).astype(BF16)
    k_out_t = (k * jnp.exp2(cum[c - 1:c, :] - cum)).T.astype(BF16)
    keep = jnp.exp2(cum.T[:, c - 1:c])
    kd = lambda h: slice(h * DK_B, (h + 1) * DK_B)
    vd = lambda h: slice(h * DV_B, (h + 1) * DV_B)
    o = jnp.concatenate([_dot(q_in[:, kd(h)], s_old[kd(h)].astype(BF16)) for h in range(H_B)], axis=-1)
    s_new = jnp.concatenate([s_old[kd(h)] * keep[kd(h)] + _dot(k_out_t[kd(h)], v16[:, vd(h)])
                             for h in range(H_B)], axis=0)
    return o, s_new


def _gla_finish(o, g_ref, sz):
    outs = [_rms_rows(o[:, h * DV_B:(h + 1) * DV_B], g_ref[:, h * DV_B:(h + 1) * DV_B]) for h in range(H_B)]
    return (jnp.concatenate(outs, axis=-1) * sz.astype(F32)).astype(BF16)


def _gla_step_kernel(q_ref, k_ref, v_ref, la_ref, sz_ref, g_ref, s0_ref, o_ref, s_ref, *, t):
    c = GLA_C
    n = c // t
    q, k, la = q_ref[...].astype(F32), k_ref[...].astype(F32), la_ref[...]
    v16 = v_ref[...]
    (o, cum), = _gla_intra([(q, k, v16.astype(F32), v16, la)], t)

    q_in = (q * jnp.exp2(cum)).astype(BF16)
    cum_t, k_t = cum.T, k.T
    stream_of_col = lax.broadcasted_iota(jnp.int32, (W_BK, c), 1) // t
    inter = []
    for s in range(n):
        rows = slice(s * t, (s + 1) * t)
        last = cum_t[:, (s + 1) * t - 1:(s + 1) * t]
        k_out_t = (k_t * jnp.exp2(jnp.where(stream_of_col == s, last - cum_t, NEG))).astype(BF16)
        keep = jnp.exp2(last)
        o_s = []
        for h in range(H_B):
            ks = slice(h * DK_B, (h + 1) * DK_B)
            vs = slice(h * DV_B, (h + 1) * DV_B)
            s_old = s0_ref[s, h]
            o_s.append(_dot(q_in[rows, ks], s_old.astype(BF16)))
            s_ref[s, h] = s_old * keep[ks] + _dot(k_out_t[ks], v16[:, vs])
        inter.append(jnp.concatenate(o_s, axis=-1))
    o_ref[...] = _gla_finish(o + jnp.concatenate(inter, axis=0), g_ref, sz_ref[...])


def _gla_step(q, k, v, la, sz, g, s0):
    b, t, _ = q.shape
    n = GLA_C // t
    assert GLA_C % t == 0 and t % (2 * SUBLANES) == 0 and b % n == 0
    flat = lambda z: z.reshape(b * t, z.shape[-1])
    blk = lambda w: pl.BlockSpec((GLA_C, w), lambda i: (i, 0))
    state = pl.BlockSpec((n, H_B, DK_B, DV_B), lambda i: (i, 0, 0, 0))
    o, s_new = pl.pallas_call(
        functools.partial(_gla_step_kernel, t=t),
        grid=(b // n,),
        in_specs=[blk(W_BK), blk(W_BK), blk(W_BV), blk(W_BK), blk(W_BV),
                  pl.BlockSpec((1, W_BV), lambda i: (0, 0)), state],
        out_specs=[blk(W_BV), state],
        out_shape=[jax.ShapeDtypeStruct((b * t, W_BV), BF16),
                   jax.ShapeDtypeStruct((b, H_B, DK_B, DV_B), F32)],
        compiler_params=_params("parallel"),
        name="gla_step",
    )(flat(q), flat(k), flat(v), flat(la), flat(sz), g, s0)
    return o.reshape(b, t, W_BV), s_new


def _prep_weights(l, norm_in, w_in, g_qa, g_ka, w_gate2, b_gate, g_gla_out, g_mem, w_mem_kv, g_qm, g_km,
                  w_up_a, w_up_b, w_up_m, w_out):
    tile = lambda gain, n: jnp.tile(gain, n).reshape(1, -1)
    halved = jnp.concatenate([jnp.full((size,), 0.5 if i in _HALVED_GROUPS else 1.0, F32)
                              for i, size in enumerate(IN_SIZES)])
    return {
        "norm_in": norm_in[l].reshape(1, D_MODEL),
        "w_in_t": (w_in[l].T * halved[:, None]).astype(BF16),
        "w_gate2": w_gate2[l].astype(BF16),
        "b_gate": b_gate[l].reshape(1, W_BK),
        "g_qa": tile(g_qa[l], H_A), "g_ka": tile(g_ka[l], H_A), "g_qm": tile(g_qm[l], H_M),
        "g_km": tile(g_km[l], H_M), "g_gla": tile(g_gla_out[l], H_B),
        "g_mem": g_mem[l].reshape(1, D_MODEL),
        "w_mem_kv": w_mem_kv[l].astype(BF16),
        "w_up_a": w_up_a[l].astype(BF16), "w_up_b": w_up_b[l].astype(BF16),
        "w_up_m": w_up_m[l].astype(BF16), "w_out": w_out[l].astype(BF16),
    }


def _layer_long(x, wts, bias, mk, mv, s0):
    b, t, _ = x.shape
    keep = min(BAND, t)
    r3 = lambda z: z.reshape(b, t, z.shape[-1])
    heads = lambda z: jnp.transpose(z.reshape(b, H_A, HEAD_DIM, keep), (0, 3, 1, 2))
    ka, va, s_new, acts, gates = _front(x.reshape(b * t, D_MODEL), wts, keep, t, keep, s0)
    y = _attn_back(r3(acts), r3(gates), bias, mk, mv, x, wts)
    return y, heads(ka), heads(va), s_new


def _layer_short(x, wts, attend_a, attend_m, s0, tm):
    b, t, _ = x.shape
    keep = min(BAND, t)
    x2d = x.reshape(b * t, D_MODEL)
    r3 = lambda z: z.reshape(b, t, z.shape[-1])
    flat = lambda z: z.reshape(b * t, z.shape[-1])
    heads = lambda z: z.reshape(b, keep, H_A, HEAD_DIM)
    (qa, ka, va, ka16, va16, sza, qb, kb, vb, la, szb, qm, szm, ga, gb, gm) = _front(x2d, wts, tm, t, keep)
    out_b, s_new = _gla_step(r3(qb), r3(kb), r3(vb), r3(la), r3(szb), wts["g_gla"], s0)
    out_a = attend_a(r3(qa), r3(ka16), r3(va16), r3(sza))
    out_m = attend_m(r3(qm), r3(szm))
    y = _back(x2d, flat(out_a), flat(out_b), flat(out_m), ga, gb, gm, wts, tm)
    return y.reshape(b, t, D_MODEL), heads(ka), heads(va), s_new


def kernel(x_prompt, x_sample, mem_prompt, cache_a_k, cache_a_v, state_gla, cache_mem_k, cache_mem_v,
           norm_in, w_in, g_qa, g_ka, rel_bias, w_gate2, b_gate, g_gla_out, g_mem, w_mem_kv, g_qm, g_km,
           w_up_a, w_up_b, w_up_m, w_out):
    depth = w_in.shape[0]
    xp, xs = x_prompt, x_sample
    bp, tp, _ = xp.shape
    bs, ts, _ = xs.shape
    akp, avp, sgp, mkp, mvp, aks, avs, sgs = [], [], [], [], [], [], [], []
    for l in range(depth):
        wts = _prep_weights(l, norm_in, w_in, g_qa, g_ka, w_gate2, b_gate, g_gla_out, g_mem, w_mem_kv,
                            g_qm, g_km, w_up_a, w_up_b, w_up_m, w_out)
        bias = _bias_tile(rel_bias[l])
        mk, mv = _mem_kv(mem_prompt, wts)
        s0 = jnp.zeros((bp, H_B, DK_B, DV_B), state_gla.dtype)
        xp, ka, va, sp = _layer_long(xp, wts, bias, mk, mv, s0)
        akp.append(ka)
        avp.append(va)
        sgp.append(sp)
        mkp.append(mk.reshape(bp, N_MEM, H_M, HEAD_DIM))
        mvp.append(mv.reshape(bp, N_MEM, H_M, HEAD_DIM))
        rows_last = lambda z: jnp.transpose(z, (0, 2, 3, 1))
        past_k, past_v = rows_last(cache_a_k[l]), rows_last(cache_a_v[l])
        mem_k, mem_v = rows_last(cache_mem_k[l]), rows_last(cache_mem_v[l])
        attend_s = lambda q, k, v, sz: _attn_step(q, k, v, past_k, past_v, sz, bias)
        attend_ms = lambda q, sz: _attn_mem_step(q, mem_k, mem_v, sz)
        xs, ka_s, va_s, ss = _layer_short(xs, wts, attend_s, attend_ms, state_gla[l], 512)
        aks.append(ka_s)
        avs.append(va_s)
        sgs.append(ss)
    return (xp, xs, jnp.stack(akp), jnp.stack(avp), jnp.stack(sgp), jnp.stack(mkp), jnp.stack(mvp),
            jnp.stack(aks), jnp.stack(avs), jnp.stack(sgs))
```
